```python
import math
import jax, jax.numpy as jnp
from jax import lax
import numpy as np

D_MODEL = 1024
BATCH = 8
SEQ = 8192
DEPTH = 4

GRID_W = 64
CTX_LEN = 256
HEAD_DIM = 64
N_GROUPS = 4
GROUP_W = D_MODEL // N_GROUPS
HEADS_A = GROUP_W // HEAD_DIM
KV_A = HEADS_A // 2
HEADS_B = GROUP_W // HEAD_DIM
KV_B = HEADS_B // 2
D_HYENA = GROUP_W
D_FNET = GROUP_W
FNET_GROUP_DIM = 64
FNET_GROUPS = D_FNET // FNET_GROUP_DIM
WINDOW = 128
BLOCK = 128
ROPE_THETA = 10000.0
HYENA_BANDS = 8
HYENA_EMB = 2 * HYENA_BANDS + 1
HYENA_HIDDEN = 64
HYENA_FAST_DECAY = 0.3
HYENA_SLOW_DECAY = 1.5
HYENA_TARGET = 1e-2
D_FF = ((8 * D_MODEL // 3 + 63) // 64) * 64
PROJ_SIZES = (HEADS_A * HEAD_DIM, KV_A * HEAD_DIM, KV_A * HEAD_DIM,
              HEADS_B * HEAD_DIM, KV_B * HEAD_DIM, KV_B * HEAD_DIM,
              3 * D_HYENA, D_FNET)
D_PROJ = sum(PROJ_SIZES)
DEEPNORM_ALPHA = (2 * DEPTH) ** 0.25
DEEPNORM_BETA = (8 * DEPTH) ** -0.25
LN_EPS = 1e-6
NEG_INF = -1e30

kernel_name = "hybrid_dit_parallel_groups"


def layer_norm(x, g=None, b=None):
    xf = x.astype(jnp.float32)
    mu = jnp.mean(xf, axis=-1, keepdims=True)
    var = jnp.mean(jnp.square(xf - mu), axis=-1, keepdims=True)
    y = (xf - mu) * lax.rsqrt(var + LN_EPS)
    if g is not None:
        y = y * g.astype(jnp.float32) + b.astype(jnp.float32)
    return y.astype(x.dtype)


def rms_norm(x, g):
    xf = x.astype(jnp.float32)
    y = xf * lax.rsqrt(jnp.mean(jnp.square(xf), axis=-1, keepdims=True) + LN_EPS)
    return (y * g.astype(jnp.float32)).astype(x.dtype)


def modulate(h, shift, scale):
    return h * (1.0 + scale) + shift


def axial_rope_tables(L):
    rows = L // GRID_W
    row = jnp.broadcast_to(jnp.arange(rows)[:, None], (rows, GRID_W)).reshape(-1).astype(jnp.float32)
    col = jnp.broadcast_to(jnp.arange(GRID_W)[None, :], (rows, GRID_W)).reshape(-1).astype(jnp.float32)
    half = HEAD_DIM // 2
    inv = ROPE_THETA ** (-jnp.arange(0, half, 2, dtype=jnp.float32) / half)
    ang = jnp.concatenate([row[:, None] * inv, col[:, None] * inv], axis=-1)
    return jnp.cos(ang), jnp.sin(ang)


def apply_axial_rope(x, cos, sin):
    B, L, H, _ = x.shape
    q = HEAD_DIM // 4
    xs = x.astype(jnp.float32).reshape(B, L, H, 2, 2, q)
    c = cos.reshape(L, 1, 2, q)
    s = sin.reshape(L, 1, 2, q)
    x1, x2 = xs[..., 0, :], xs[..., 1, :]
    out = jnp.stack([x1 * c - x2 * s, x2 * c + x1 * s], axis=-2)
    return out.reshape(B, L, H, HEAD_DIM).astype(x.dtype)


def split_projection(z):
    B, L, _ = z.shape
    idx = np.cumsum(PROJ_SIZES)[:-1].tolist()
    qa, ka, va, qb, kb, vb, zh, zf = jnp.split(z, idx, axis=-1)
    hd = lambda t, n: t.reshape(B, L, n, HEAD_DIM)
    return (hd(qa, HEADS_A), hd(ka, KV_A), hd(va, KV_A),
            hd(qb, HEADS_B), hd(kb, KV_B), hd(vb, KV_B), zh, zf)


def dense_attention(q, k, v, sink):
    B, C, H, _ = q.shape
    KV = k.shape[2]
    G = H // KV
    qg = q.reshape(B, C, KV, G, HEAD_DIM)
    s = jnp.einsum('bqhgd,bkhd->bhgqk', qg, k).astype(jnp.float32) * (HEAD_DIM ** -0.5)
    if sink is not None:
        s_sink = jnp.broadcast_to(sink.astype(jnp.float32).reshape(1, KV, G, 1, 1), s.shape[:-1] + (1,))
        s = jnp.concatenate([s, s_sink], axis=-1)
    p = jax.nn.softmax(s, axis=-1)[..., :k.shape[1]].astype(v.dtype)
    o = jnp.einsum('bhgqk,bkhd->bqhgd', p, v)
    return o.reshape(B, C, H * HEAD_DIM)


def banded_window_attention(q, k, v, kc, vc, sink):
    B, L, H, _ = q.shape
    KV = k.shape[2]
    G = H // KV
    C = kc.shape[1]
    nb = L // BLOCK
    qb = q.reshape(B, nb, BLOCK, KV, G, HEAD_DIM)

    def band(t):
        tb = t.reshape(B, nb, BLOCK, KV, HEAD_DIM)
        tp = jnp.pad(tb, ((0, 0), (1, 1), (0, 0), (0, 0), (0, 0)))
        return jnp.concatenate([tp[:, :-2], tp[:, 1:-1], tp[:, 2:]], axis=2)

    kb, vb = band(k), band(v)
    scale = HEAD_DIM ** -0.5
    s_loc = jnp.einsum('bnqhgd,bnkhd->bnhgqk', qb, kb).astype(jnp.float32) * scale
    qi = jnp.arange(BLOCK)[:, None]
    kj = jnp.arange(3 * BLOCK)[None, :] - BLOCK
    in_window = jnp.abs(kj - qi) <= WINDOW
    kabs = jnp.arange(nb)[:, None] * BLOCK + kj
    in_range = (kabs >= 0) & (kabs < L)
    mask = in_window[None] & in_range[:, None, :]
    s_loc = jnp.where(mask[None, :, None, None], s_loc, NEG_INF)
    s_ctx = jnp.einsum('bnqhgd,bchd->bnhgqc', qb, kc).astype(jnp.float32) * scale
    s_sink = jnp.broadcast_to(sink.astype(jnp.float32).reshape(1, 1, KV, G, 1, 1), s_loc.shape[:-1] + (1,))
    p = jax.nn.softmax(jnp.concatenate([s_loc, s_ctx, s_sink], axis=-1), axis=-1).astype(v.dtype)
    o = (jnp.einsum('bnhgqk,bnkhd->bnqhgd', p[..., :3 * BLOCK], vb)
         + jnp.einsum('bnhgqc,bchd->bnqhgd', p[..., 3 * BLOCK:3 * BLOCK + C], vc))
    return o.reshape(B, L, H * HEAD_DIM)


def blockwise_full_attention(q, k, v, kc, vc):
    B, L, H, _ = q.shape
    KV = k.shape[2]
    G = H // KV
    nb = L // BLOCK
    qb = q.reshape(B, nb, BLOCK, KV, G, HEAD_DIM).swapaxes(0, 1)
    k_all = jnp.concatenate([k, kc], axis=1)
    v_all = jnp.concatenate([v, vc], axis=1)

    def one_block(q_blk):
        s = jnp.einsum('bqhgd,bkhd->bhgqk', q_blk, k_all).astype(jnp.float32) * (HEAD_DIM ** -0.5)
        p = jax.nn.softmax(s, axis=-1).astype(v_all.dtype)
        return jnp.einsum('bhgqk,bkhd->bqhgd', p, v_all)

    o = lax.map(one_block, qb)
    return o.swapaxes(0, 1).reshape(B, L, H * HEAD_DIM)


def dwconv3(u, w, b):
    up = jnp.pad(u, ((0, 0), (1, 1), (0, 0)))
    return up[:, :-2] * w[0] + up[:, 1:-1] * w[1] + up[:, 2:] * w[2] + b


def hyena_filters(L, w1, b1, freq, w2, b2, w3):
    f32 = jnp.float32
    t = jnp.arange(L, dtype=f32)
    t_norm = t / max(L - 1, 1)
    bands = jnp.linspace(1e-4, HYENA_BANDS - 1, HYENA_BANDS, dtype=f32)
    ang = 2.0 * math.pi * t[:, None] * bands[None, :] / L
    z = jnp.concatenate([t_norm[:, None], jnp.cos(ang), -jnp.sin(ang)], axis=-1)
    fr = freq.astype(f32)
    h = jnp.sin(fr * (z @ w1.astype(f32) + b1.astype(f32)))
    h = jnp.sin(fr * (h @ w2.astype(f32) + b2.astype(f32)))
    h = (h @ w3.astype(f32)).reshape(L, 2, D_HYENA)
    min_decay = math.log(HYENA_TARGET) / HYENA_SLOW_DECAY
    max_decay = math.log(HYENA_TARGET) / HYENA_FAST_DECAY
    deltas = jnp.abs(jnp.linspace(min_decay, max_decay, D_HYENA, dtype=f32))
    decay = jnp.exp(-t_norm[:, None] * deltas[None, :])
    h = h * decay[:, None, :]
    return h[:, 0], h[:, 1]


def bidir_long_conv(u, h_fwd, h_bwd):
    B, L, D = u.shape
    k = jnp.concatenate([h_fwd.at[0].add(h_bwd[0]), jnp.zeros((1, D), jnp.float32), h_bwd[1:][::-1]], axis=0)
    U = jnp.fft.rfft(u.astype(jnp.float32), n=2 * L, axis=1)
    K = jnp.fft.rfft(k, axis=0)
    y = jnp.fft.irfft(U * K[None], n=2 * L, axis=1)[:, :L]
    return y.astype(u.dtype)


def hyena_mixer(z, conv_w, conv_b, f_w1, f_b1, f_freq, f_w2, f_b2, f_w3, skip):
    z = dwconv3(z, conv_w, conv_b)
    x0, x1, v = jnp.split(z, 3, axis=-1)
    h_fwd, h_bwd = hyena_filters(z.shape[1], f_w1, f_b1, f_freq, f_w2, f_b2, f_w3)
    v = v * x1
    v = bidir_long_conv(v, h_fwd, h_bwd) + skip * v
    return v * x0


def fnet_mixer(u, w, b):
    B, L, _ = u.shape
    ug = u.astype(jnp.float32).reshape(B, L, FNET_GROUPS, FNET_GROUP_DIM)
    f = jnp.real(jnp.fft.fftn(ug, axes=(1, 3), norm='ortho'))
    return f.reshape(B, L, D_FNET).astype(u.dtype) @ w + b


def merge_groups(oa, ob, oc, od, g):
    o = jnp.stack([oa, ob, oc, od], axis=-2)
    of = o.astype(jnp.float32)
    of = of * lax.rsqrt(jnp.mean(jnp.square(of), axis=-1, keepdims=True) + LN_EPS)
    o = of.reshape(o.shape[:-2] + (N_GROUPS * GROUP_W,)) * g.astype(jnp.float32)
    return o.astype(oa.dtype)


def conv_ffn(h, w_up, b_up, conv_w, conv_b, w_down, b_down):
    u = dwconv3(h @ w_up + b_up, conv_w, conv_b)
    a, g = jnp.split(u, 2, axis=-1)
    return (jax.nn.silu(a) * g) @ w_down + b_down


def setup_inputs(seed: int = 0) -> dict:
    key = jax.random.key(seed)
    ks = iter(jax.random.split(key, 64))
    D = D_MODEL

    def nrm(shape, scale):
        return jax.random.normal(next(ks), shape, jnp.float32) * scale

    return {
        'x': nrm((BATCH, SEQ, D), 1.0),
        'c': nrm((BATCH, D), 1.0),
        'ctx': nrm((BATCH, CTX_LEN, D), 1.0),
        'c_ctx': nrm((D,), 1.0),
        'w_ada': nrm((DEPTH, D, 6 * D), 0.5 * D ** -0.5),
        'b_ada': nrm((DEPTH, 6 * D), 0.02),
        'w_in': nrm((DEPTH, D, D_PROJ), D ** -0.5),
        'sink_a': nrm((DEPTH, HEADS_A), 0.5),
        'q_norm_g': 1.0 + nrm((DEPTH, HEAD_DIM), 0.02),
        'k_norm_g': 1.0 + nrm((DEPTH, HEAD_DIM), 0.02),
        'hy_conv_w': nrm((DEPTH, 3, 3 * D_HYENA), 3 ** -0.5),
        'hy_conv_b': nrm((DEPTH, 3 * D_HYENA), 0.02),
        'hy_f_w1': nrm((DEPTH, HYENA_EMB, HYENA_HIDDEN), HYENA_EMB ** -0.5),
        'hy_f_b1': nrm((DEPTH, HYENA_HIDDEN), 0.02),
        'hy_f_freq': 1.0 + nrm((DEPTH, HYENA_HIDDEN), 0.02),
        'hy_f_w2': nrm((DEPTH, HYENA_HIDDEN, HYENA_HIDDEN), HYENA_HIDDEN ** -0.5),
        'hy_f_b2': nrm((DEPTH, HYENA_HIDDEN), 0.02),
        'hy_f_w3': nrm((DEPTH, HYENA_HIDDEN, 2 * D_HYENA), HYENA_HIDDEN ** -0.5),
        'hy_skip': nrm((DEPTH, D_HYENA), 1.0),
        'fnet_w': nrm((DEPTH, D_FNET, D_FNET), D_FNET ** -0.5),
        'fnet_b': nrm((DEPTH, D_FNET), 0.02),
        'out_norm_g': 1.0 + nrm((DEPTH, D), 0.02),
        'w_out': nrm((DEPTH, D, D), DEEPNORM_BETA * D ** -0.5),
        'b_out': nrm((DEPTH, D), 0.02),
        'ln1_g': 1.0 + nrm((DEPTH, D), 0.02),
        'ln1_b': nrm((DEPTH, D), 0.02),
        'ffn_w_up': nrm((DEPTH, D, 2 * D_FF), D ** -0.5),
        'ffn_b_up': nrm((DEPTH, 2 * D_FF), 0.02),
        'ffn_conv_w': nrm((DEPTH, 3, 2 * D_FF), 3 ** -0.5),
        'ffn_conv_b': nrm((DEPTH, 2 * D_FF), 0.02),
        'ffn_w_down': nrm((DEPTH, D_FF, D), DEEPNORM_BETA * D_FF ** -0.5),
        'ffn_b_down': nrm((DEPTH, D), 0.02),
        'ln2_g': 1.0 + nrm((DEPTH, D), 0.02),
        'ln2_b': nrm((DEPTH, D), 0.02),
    }


def reference(x, c, ctx, c_ctx, w_ada, b_ada, w_in, sink_a, q_norm_g, k_norm_g,
              hy_conv_w, hy_conv_b, hy_f_w1, hy_f_b1, hy_f_freq, hy_f_w2, hy_f_b2, hy_f_w3, hy_skip,
              fnet_w, fnet_b, out_norm_g, w_out, b_out, ln1_g, ln1_b,
              ffn_w_up, ffn_b_up, ffn_conv_w, ffn_conv_b, ffn_w_down, ffn_b_down, ln2_g, ln2_b):
    L = x.shape[1]
    cos, sin = axial_rope_tables(L)
    for l in range(DEPTH):
        mod = jax.nn.silu(c) @ w_ada[l] + b_ada[l]
        mod_c = jax.nn.silu(c_ctx) @ w_ada[l] + b_ada[l]
        sh1, sc1, g1, sh2, sc2, g2 = jnp.split(mod[:, None, :], 6, axis=-1)
        csh1, csc1, cg1, csh2, csc2, cg2 = jnp.split(mod_c, 6, axis=-1)
        hy_p = (hy_conv_w[l], hy_conv_b[l], hy_f_w1[l], hy_f_b1[l], hy_f_freq[l],
                hy_f_w2[l], hy_f_b2[l], hy_f_w3[l], hy_skip[l])
        ffn_p = (ffn_w_up[l], ffn_b_up[l], ffn_conv_w[l], ffn_conv_b[l], ffn_w_down[l], ffn_b_down[l])

        hc = modulate(layer_norm(ctx), csh1, csc1)
        cqa, cka, cva, cqb, ckb, cvb, czh, czf = split_projection(hc @ w_in[l])
        ckb = rms_norm(ckb, k_norm_g[l])

        h = modulate(layer_norm(x), sh1, sc1)
        qa, ka, va, qb, kb, vb, zh, zf = split_projection(h @ w_in[l])
        oa = banded_window_attention(apply_axial_rope(qa, cos, sin), apply_axial_rope(ka, cos, sin),
                                     va, cka, cva, sink_a[l])
        qb = apply_axial_rope(rms_norm(qb, q_norm_g[l]), cos, sin)
        kb = apply_axial_rope(rms_norm(kb, k_norm_g[l]), cos, sin)
        ob = blockwise_full_attention(qb, kb, vb, ckb, cvb)
        oc = hyena_mixer(zh, *hy_p)
        od = fnet_mixer(zf, fnet_w[l], fnet_b[l])
        y = merge_groups(oa, ob, oc, od, out_norm_g[l]) @ w_out[l] + b_out[l]
        x = layer_norm(DEEPNORM_ALPHA * x + g1 * y, ln1_g[l], ln1_b[l])

        h2 = modulate(layer_norm(x), sh2, sc2)
        x = layer_norm(DEEPNORM_ALPHA * x + g2 * conv_ffn(h2, *ffn_p), ln2_g[l], ln2_b[l])

        if l < DEPTH - 1:
            oac = dense_attention(cqa, cka, cva, sink_a[l])
            obc = dense_attention(rms_norm(cqb, q_norm_g[l]), ckb, cvb, None)
            occ = hyena_mixer(czh, *hy_p)
            odc = fnet_mixer(czf, fnet_w[l], fnet_b[l])
            yc = merge_groups(oac, obc, occ, odc, out_norm_g[l]) @ w_out[l] + b_out[l]
            ctx = layer_norm(DEEPNORM_ALPHA * ctx + cg1 * yc, ln1_g[l], ln1_b[l])
            hc2 = modulate(layer_norm(ctx), csh2, csc2)
            ctx = layer_norm(DEEPNORM_ALPHA * ctx + cg2 * conv_ffn(hc2, *ffn_p), ln2_g[l], ln2_b[l])
    return x
```

```python
import functools
import math

import jax
import jax.numpy as jnp
from jax import lax
from jax.experimental import pallas as pl
from jax.experimental.pallas import tpu as pltpu

F32 = jnp.float32
BF16 = jnp.bfloat16

HEAD_DIM = 64
GROUP_W = 256
N_HEADS = 4
N_KV = 2
WINDOW = 128
GRID_W = 64
ROPE_THETA = 10000.0
FNET_GROUP_DIM = 64
HYENA_BANDS = 8
HYENA_FAST_DECAY = 0.3
HYENA_SLOW_DECAY = 1.5
HYENA_TARGET = 1e-2
LN_EPS = 1e-6
NEG_INF = -1e30
FFT_N2 = 128
FFN_CHUNK = 256
HALO = 16
VMEM_LIMIT = 56 * 1024 * 1024


def _cp(sem, vmem=None):
    return pltpu.CompilerParams(dimension_semantics=sem, vmem_limit_bytes=vmem)


def _layer_norm(x):
    mu = jnp.mean(x, axis=-1, keepdims=True)
    xc = x - mu
    var = jnp.mean(xc * xc, axis=-1, keepdims=True)
    return xc * lax.rsqrt(var + LN_EPS)


def _dot(a, b):
    return jnp.dot(a, b, preferred_element_type=F32)


def _ada_kernel(c_ref, w_ref, b_ref, o_ref):
    c = c_ref[...]
    s = (c * jax.nn.sigmoid(c)).astype(BF16)
    o_ref[0] = _dot(s, w_ref[0].astype(BF16)) + b_ref[0]


def _ada(cc, w_ada, b_ada):
    depth, d, n = w_ada.shape
    r = cc.shape[0]
    tn = 1536
    return pl.pallas_call(
        _ada_kernel,
        grid=(depth, n // tn),
        in_specs=[pl.BlockSpec((r, d), lambda l, j: (0, 0)),
                  pl.BlockSpec((1, d, tn), lambda l, j: (l, 0, j)),
                  pl.BlockSpec((1, 1, tn), lambda l, j: (l, 0, j))],
        out_specs=pl.BlockSpec((1, r, tn), lambda l, j: (l, 0, j)),
        out_shape=jax.ShapeDtypeStruct((depth, r, n), F32),
        compiler_params=_cp(("parallel", "parallel"), VMEM_LIMIT),
        name="ada",
    )(cc, w_ada, b_ada.reshape(depth, 1, n))


def _swap_halves(z):
    w = z.shape[1]
    lane = lax.broadcasted_iota(jnp.int32, z.shape, 1)
    return jnp.where(lane % 32 < 16, pltpu.roll(z, w - 16, 1), pltpu.roll(z, 16, 1))


def _proj_kernel(x_ref, sh_ref, sc_ref, w_ref, cos_ref, sin_ref, bd_ref, qg_ref, kg_ref,
                 qa_ref, ka_ref, va_ref, qb_ref, kb_ref, vb_ref, zh_ref, zf_ref):
    x = x_ref[0]
    h = (_layer_norm(x) * (1.0 + sc_ref[0]) + sh_ref[0]).astype(BF16)
    cos1, sin1 = cos_ref[...], sin_ref[...]
    cos2 = jnp.concatenate([cos1, cos1], axis=1)
    sin2 = jnp.concatenate([sin1, sin1], axis=1)

    def proj(lo, width):
        return _dot(h, w_ref[:, lo:lo + width])

    def rope(z):
        if z.shape[1] == 128:
            return z * cos1 + _swap_halves(z) * sin1
        return z * cos2 + _swap_halves(z) * sin2

    def head_norm(z, g):
        w = z.shape[1]
        ms = _dot((z * z).astype(BF16), bd_ref[:w, :w])
        return z * lax.rsqrt(ms + LN_EPS) * g

    def put_heads(ref, z):
        for hh in range(z.shape[1] // HEAD_DIM):
            ref[0, hh] = z[:, hh * HEAD_DIM:(hh + 1) * HEAD_DIM].astype(BF16)

    scale = HEAD_DIM ** -0.5
    put_heads(qa_ref, rope(proj(0, 256)) * scale)
    put_heads(ka_ref, rope(proj(256, 128)))
    put_heads(va_ref, proj(384, 128))
    put_heads(qb_ref, rope(head_norm(proj(512, 256), qg_ref[...])) * scale)
    put_heads(kb_ref, rope(head_norm(proj(768, 128), kg_ref[...])))
    put_heads(vb_ref, proj(896, 128))
    zh_ref[0] = proj(1024, 768).astype(BF16)
    zf_ref[0] = proj(1792, 256).astype(BF16)


def _proj(x, sh, sc, w_in, cos, sin, bd, qg, kg):
    b, l, d = x.shape
    tm = min(l, 512)
    heads = lambda n: jax.ShapeDtypeStruct((b, n, l, HEAD_DIM), BF16)
    hspec = lambda n: pl.BlockSpec((1, n, tm, HEAD_DIM), lambda t, bb: (bb, 0, t, 0))
    tok = lambda w: pl.BlockSpec((1, tm, w), lambda t, bb: (bb, t, 0))
    vec = lambda w: pl.BlockSpec((1, 1, w), lambda t, bb: (bb, 0, 0))
    full = lambda a: pl.BlockSpec(a.shape, lambda t, bb: (0,) * a.ndim)
    tab = pl.BlockSpec((tm, 128), lambda t, bb: (t, 0))
    return pl.pallas_call(
        _proj_kernel,
        grid=(l // tm, b),
        in_specs=[tok(d), vec(d), vec(d), full(w_in), tab, tab, full(bd), full(qg), full(kg)],
        out_specs=[hspec(4), hspec(2), hspec(2), hspec(4), hspec(2), hspec(2), tok(768), tok(256)],
        out_shape=[heads(4), heads(2), heads(2), heads(4), heads(2), heads(2),
                   jax.ShapeDtypeStruct((b, l, 768), BF16), jax.ShapeDtypeStruct((b, l, 256), BF16)],
        compiler_params=_cp(("parallel", "parallel"), VMEM_LIMIT),
        name="proj",
    )(x, sh, sc, w_in, cos, sin, bd, qg, kg)


def _attn_kernel(sink_ref, q_ref, k_ref, v_ref, o_ref, m_ref, l_ref, acc_ref, *,
                 banded, has_sink, tq, tk, n_lat, n_k):
    j, qi, ki = pl.program_id(1), pl.program_id(2), pl.program_id(3)
    row = lax.broadcasted_iota(jnp.int32, (2 * tq, 1), 0)

    @pl.when(ki == 0)
    def _():
        if has_sink:
            m_ref[...] = jnp.where(row < tq, sink_ref[2 * j], sink_ref[2 * j + 1])
            l_ref[...] = jnp.ones_like(l_ref)
        else:
            m_ref[...] = jnp.full_like(m_ref, NEG_INF)
            l_ref[...] = jnp.zeros_like(l_ref)
        acc_ref[...] = jnp.zeros_like(acc_ref)

    q = q_ref[0].reshape(2 * tq, HEAD_DIM)
    s = lax.dot_general(q, k_ref[0, 0], (((1,), (1,)), ((), ())), preferred_element_type=F32)
    if banded:
        qpos = qi * tq + row % tq
        kpos = (qi - 1 + ki) * tk + lax.broadcasted_iota(jnp.int32, (1, tk), 1)
        ok = (jnp.abs(kpos - qpos) <= WINDOW) & (kpos >= 0) & (kpos < n_lat * tk)
        s = jnp.where(ok | (ki == n_k - 1), s, NEG_INF)
    m_prev = m_ref[...]
    m_new = jnp.maximum(m_prev, jnp.max(s, axis=-1, keepdims=True))
    alpha = jnp.exp(m_prev - m_new)
    p = jnp.exp(s - m_new)
    l_ref[...] = alpha * l_ref[...] + jnp.sum(p, axis=-1, keepdims=True)
    acc_ref[...] = alpha * acc_ref[...] + _dot(p.astype(BF16), v_ref[0, 0])
    m_ref[...] = m_new

    @pl.when(ki == n_k - 1)
    def _():
        o = acc_ref[...] / l_ref[...]
        o_ref[0] = jnp.concatenate([o[:tq], o[tq:]], axis=1).astype(BF16)


def _attention(q, k, v, sink, *, banded, n_lat_keys=0):
    b, _, lq, _ = q.shape
    lk = k.shape[2]
    tq = tk = 256
    has_sink = sink is not None
    if not has_sink:
        sink = jnp.zeros((N_HEADS,), F32)
    if banded:
        n_lat = n_lat_keys // tk
        assert lk == n_lat_keys + tk and WINDOW <= tk and lq == n_lat_keys
        n_k = 4
        kmap = lambda bb, j, qi, ki: (bb, j, jnp.where(ki == 3, n_lat, jnp.clip(qi - 1 + ki, 0, n_lat - 1)), 0)
    else:
        n_lat = 0
        n_k = lk // tk
        kmap = lambda bb, j, qi, ki: (bb, j, ki, 0)
    kern = functools.partial(_attn_kernel, banded=banded, has_sink=has_sink, tq=tq, tk=tk,
                             n_lat=n_lat, n_k=n_k)
    return pl.pallas_call(
        kern,
        grid=(b, N_KV, lq // tq, n_k),
        in_specs=[pl.BlockSpec(memory_space=pltpu.SMEM),
                  pl.BlockSpec((1, 2, tq, HEAD_DIM), lambda bb, j, qi, ki: (bb, j, qi, 0)),
                  pl.BlockSpec((1, 1, tk, HEAD_DIM), kmap),
                  pl.BlockSpec((1, 1, tk, HEAD_DIM), kmap)],
        out_specs=pl.BlockSpec((1, tq, 128), lambda bb, j, qi, ki: (bb, qi, j)),
        out_shape=jax.ShapeDtypeStruct((b, lq, GROUP_W), BF16),
        scratch_shapes=[pltpu.VMEM((2 * tq, 1), F32), pltpu.VMEM((2 * tq, 1), F32),
                        pltpu.VMEM((2 * tq, HEAD_DIM), F32)],
        compiler_params=_cp(("parallel", "parallel", "parallel", "arbitrary")),
        name="attn_band" if banded else "attn_full",
    )(sink, q, k, v)


def _shift_rows(z, prev_row, next_row):
    n = z.shape[0]
    row = lax.broadcasted_iota(jnp.int32, (n, 1), 0)
    dn = jnp.where(row == 0, prev_row, pltpu.roll(z, 1, 0))
    up = jnp.where(row == n - 1, next_row, pltpu.roll(z, n - 1, 0))
    return dn, up


def _hy_gate_kernel(z_ref, zp_ref, zn_ref, w_ref, b_ref, vx_ref, x0_ref, *, n_tiles):
    i = pl.program_id(1)
    z = z_ref[0].astype(F32)
    prev_row = zp_ref[0].astype(F32)[HALO - 1:HALO] * (i > 0).astype(F32)
    next_row = zn_ref[0].astype(F32)[0:1] * (i < n_tiles - 1).astype(F32)
    dn, up = _shift_rows(z, prev_row, next_row)
    w = w_ref[...]
    u = dn * w[0:1] + z * w[1:2] + up * w[2:3] + b_ref[...]
    x0_ref[0] = u[:, :GROUP_W].astype(BF16)
    vx_ref[0] = (u[:, 2 * GROUP_W:] * u[:, GROUP_W:2 * GROUP_W]).astype(BF16)


def _halo_specs(tl, l, w):
    nb = l // HALO
    per = tl // HALO
    return (pl.BlockSpec((1, tl, w), lambda bb, i: (bb, i, 0)),
            pl.BlockSpec((1, HALO, w), lambda bb, i: (bb, jnp.maximum(i * per - 1, 0), 0)),
            pl.BlockSpec((1, HALO, w), lambda bb, i: (bb, jnp.minimum((i + 1) * per, nb - 1), 0)))


def _hy_gate(zh, conv_w, conv_b):
    b, l, w = zh.shape
    tl = min(l, 512)
    out = jax.ShapeDtypeStruct((b, l, GROUP_W), BF16)
    ospec = pl.BlockSpec((1, tl, GROUP_W), lambda bb, i: (bb, i, 0))
    return pl.pallas_call(
        functools.partial(_hy_gate_kernel, n_tiles=l // tl),
        grid=(b, l // tl),
        in_specs=[*_halo_specs(tl, l, w),
                  pl.BlockSpec((3, w), lambda bb, i: (0, 0)),
                  pl.BlockSpec((1, w), lambda bb, i: (0, 0))],
        out_specs=[ospec, ospec],
        out_shape=[out, out],
        compiler_params=_cp(("parallel", "parallel")),
        name="hy_gate",
    )(zh, zh, zh, conv_w, conv_b.reshape(1, w))


def _hy_filter_kernel(fl_ref, w1_ref, b1_ref, fr_ref, w2_ref, b2_ref, w3_ref, dl_ref, k_ref, *, l, tr):
    m = pl.program_id(1) * tr + lax.broadcasted_iota(jnp.int32, (tr, 1), 0)
    t = jnp.where(m < l, m, 2 * l - m).astype(F32)
    tn = t / max(l - 1, 1)
    lane = lax.broadcasted_iota(jnp.int32, (1, 128), 1)
    phase = t * fl_ref[...]
    feat = jnp.where(lane == 0, tn,
                     jnp.where(lane <= HYENA_BANDS, jnp.cos(phase),
                               jnp.where(lane <= 2 * HYENA_BANDS, -jnp.sin(phase), 0.0)))
    hp = functools.partial(jnp.dot, precision=lax.Precision.HIGHEST, preferred_element_type=F32)
    fr = fr_ref[0]
    h = jnp.sin(fr * (hp(feat, w1_ref[0]) + b1_ref[0]))
    h = jnp.sin(fr * (hp(h, w2_ref[0]) + b2_ref[0]))
    h = hp(h, w3_ref[0])
    dec = jnp.exp(-tn * dl_ref[...])
    hf, hb = h[:, :GROUP_W] * dec, h[:, GROUP_W:] * dec
    k = jnp.where(m < l, hf, hb)
    k = jnp.where(m == 0, hf + hb, k)
    k_ref[0] = jnp.where(m == l, 0.0, k).astype(BF16)


def _hy_filter(l, w1, b1, fr, w2, b2, w3):
    depth, emb, hid = w1.shape
    tr = min(2 * l, 1024)
    bands = jnp.linspace(1e-4, HYENA_BANDS - 1, HYENA_BANDS, dtype=F32) * (2.0 * math.pi / l)
    fl = jnp.zeros((1, 128), F32).at[0, 1:1 + HYENA_BANDS].set(bands).at[0, 1 + HYENA_BANDS:emb].set(bands)
    w1p = jnp.zeros((depth, 128, hid), F32).at[:, :emb].set(w1)
    min_decay = math.log(HYENA_TARGET) / HYENA_SLOW_DECAY
    max_decay = math.log(HYENA_TARGET) / HYENA_FAST_DECAY
    deltas = jnp.abs(jnp.linspace(min_decay, max_decay, GROUP_W, dtype=F32)).reshape(1, GROUP_W)
    lay = lambda a: pl.BlockSpec((1,) + a.shape[1:], lambda d, i: (d,) + (0,) * (a.ndim - 1))
    fix = lambda a: pl.BlockSpec(a.shape, lambda d, i: (0,) * a.ndim)
    b1, fr, b2 = (a.reshape(depth, 1, hid) for a in (b1, fr, b2))
    return pl.pallas_call(
        functools.partial(_hy_filter_kernel, l=l, tr=tr),
        grid=(depth, 2 * l // tr),
        in_specs=[fix(fl), lay(w1p), lay(b1), lay(fr), lay(w2), lay(b2), lay(w3), fix(deltas)],
        out_specs=pl.BlockSpec((1, tr, GROUP_W), lambda d, i: (d, i, 0)),
        out_shape=jax.ShapeDtypeStruct((depth, 2 * l, GROUP_W), BF16),
        compiler_params=_cp(("parallel", "parallel")),
        name="hy_filter",
    )(fl, w1p, b1, fr, w2, b2, w3, deltas)


def _rowmm_kernel(f_ref, x_ref, o_ref):
    o_ref[0] = _dot(f_ref[...], x_ref[0]).astype(o_ref.dtype)


def _rowmm(f, xv, tl=8192):
    b, k, n = xv.shape
    m = f.shape[0]
    tl = min(tl, n)
    return pl.pallas_call(
        _rowmm_kernel,
        grid=(b, n // tl),
        in_specs=[pl.BlockSpec((m, k), lambda bb, i: (0, 0)),
                  pl.BlockSpec((1, k, tl), lambda bb, i: (bb, 0, i))],
        out_specs=pl.BlockSpec((1, m, tl), lambda bb, i: (bb, 0, i)),
        out_shape=jax.ShapeDtypeStruct((b, m, n), BF16),
        compiler_params=_cp(("parallel", "parallel"), VMEM_LIMIT),
        name="rowmm",
    )(f, xv)


def _hy_spec_kernel(a_ref, mf_ref, k_ref):
    x = a_ref[0, :, 0].reshape(2 * FFT_N2, GROUP_W)
    k_ref[0, 0] = _dot(mf_ref[0], x)


def _hy_mid_kernel(a_ref, mf_ref, mi_ref, k_ref, o_ref):
    n2 = FFT_N2
    x = a_ref[0, :, 0].reshape(2 * n2, GROUP_W)
    y = _dot(mf_ref[0], x)
    k = k_ref[0, 0]
    yr, yi, kr, ki = y[:n2], y[n2:], k[:n2], k[n2:]
    z = jnp.concatenate([yr * kr - yi * ki, yr * ki + yi * kr], axis=0).astype(BF16)
    o_ref[0, :, 0] = _dot(mi_ref[0], z).astype(BF16).reshape(2, n2, GROUP_W)


def _hy_spectrum(a, mf):
    depth, _, n1, n2, w = a.shape
    return pl.pallas_call(
        _hy_spec_kernel,
        grid=(n1, depth),
        in_specs=[pl.BlockSpec((1, 2, 1, n2, w), lambda k1, d: (d, 0, k1, 0, 0)),
                  pl.BlockSpec((1, 2 * n2, 2 * n2), lambda k1, d: (k1, 0, 0))],
        out_specs=pl.BlockSpec((1, 1, 2 * n2, w), lambda k1, d: (d, k1, 0, 0)),
        out_shape=jax.ShapeDtypeStruct((depth, n1, 2 * n2, w), F32),
        compiler_params=_cp(("parallel", "parallel")),
        name="hy_spectrum",
    )(a, mf)


def _hy_mid(a, mf, mi, kspec, layer):
    b, _, n1, n2, w = a.shape
    blk = pl.BlockSpec((1, 2, 1, n2, w), lambda k1, bb: (bb, 0, k1, 0, 0))
    mat = pl.BlockSpec((1, 2 * n2, 2 * n2), lambda k1, bb: (k1, 0, 0))
    return pl.pallas_call(
        _hy_mid_kernel,
        grid=(n1, b),
        in_specs=[blk, mat, mat,
                  pl.BlockSpec((1, 1, 2 * n2, w), lambda k1, bb: (layer, k1, 0, 0))],
        out_specs=blk,
        out_shape=jax.ShapeDtypeStruct(a.shape, BF16),
        compiler_params=_cp(("parallel", "parallel")),
        name="hy_mid",
    )(a, mf, mi, kspec)


def _hy_out_kernel(g_ref, b_ref, vx_ref, x0_ref, skip_ref, o_ref):
    y = _dot(g_ref[...], b_ref[0])
    vx = vx_ref[0].astype(F32)
    o_ref[0] = ((y + skip_ref[...] * vx) * x0_ref[0].astype(F32)).astype(BF16)


def _hy_out(g, bo, vx, x0, skip_t, tl=8192):
    b, k, n = bo.shape
    m = g.shape[0]
    tl = min(tl, n)
    sig = pl.BlockSpec((1, m, tl), lambda bb, i: (bb, 0, i))
    return pl.pallas_call(
        _hy_out_kernel,
        grid=(b, n // tl),
        in_specs=[pl.BlockSpec((m, k), lambda bb, i: (0, 0)),
                  pl.BlockSpec((1, k, tl), lambda bb, i: (bb, 0, i)),
                  sig, sig, pl.BlockSpec((1, tl), lambda bb, i: (0, 0))],
        out_specs=sig,
        out_shape=jax.ShapeDtypeStruct((b, m, n), BF16),
        compiler_params=_cp(("parallel", "parallel"), VMEM_LIMIT),
        name="hy_out",
    )(g, bo, vx, x0, skip_t)


def _hy_small_kernel(vx_ref, x0_ref, k_ref, fd_ref, ff_ref, gd_ref, skip_ref, o_ref):
    nf = fd_ref.shape[0] // 2
    vx = vx_ref[0]
    u = _dot(fd_ref[...], vx)
    k = _dot(ff_ref[...], k_ref[0])
    ur, ui, kr, ki = u[:nf], u[nf:], k[:nf], k[nf:]
    z = jnp.concatenate([ur * kr - ui * ki, ur * ki + ui * kr], axis=0).astype(BF16)
    y = _dot(gd_ref[...], z)
    o_ref[0] = ((y + skip_ref[...] * vx.astype(F32)) * x0_ref[0].astype(F32)).astype(BF16)


def _hy_small(vx, x0, kfilt, layer, fd, ff, gd, skip):
    b, l, w = vx.shape
    sig = pl.BlockSpec((1, l, w), lambda bb: (bb, 0, 0))
    full = lambda a: pl.BlockSpec(a.shape, lambda bb: (0,) * a.ndim)
    return pl.pallas_call(
        _hy_small_kernel,
        grid=(b,),
        in_specs=[sig, sig, pl.BlockSpec((1, 2 * l, w), lambda bb: (layer, 0, 0)),
                  full(fd), full(ff), full(gd), full(skip)],
        out_specs=sig,
        out_shape=jax.ShapeDtypeStruct((b, l, w), BF16),
        compiler_params=_cp(("parallel",)),
        name="hy_small",
    )(vx, x0, kfilt, fd, ff, gd, skip)


def _fn_first_kernel(z_ref, m_ref, o_ref):
    n2 = z_ref.shape[1]
    x = z_ref[0].reshape(n2 * 8, GROUP_W)
    o_ref[0] = _dot(m_ref[0], x).astype(BF16).reshape(2, 8, n2, GROUP_W)


def _fn_first(zf, mats):
    b, n2, n1, w = zf.shape
    return pl.pallas_call(
        _fn_first_kernel,
        grid=(n1 // 8, b),
        in_specs=[pl.BlockSpec((1, n2, 8, w), lambda i, bb: (bb, 0, i, 0)),
                  pl.BlockSpec((1, 16 * n2, 8 * n2), lambda i, bb: (i, 0, 0))],
        out_specs=pl.BlockSpec((1, 2, 8, n2, w), lambda i, bb: (bb, 0, i, 0, 0)),
        out_shape=jax.ShapeDtypeStruct((b, 2, n1, n2, w), BF16),
        compiler_params=_cp(("parallel", "parallel"), VMEM_LIMIT),
        name="fn_first",
    )(zf, mats)


def _fn_last_kernel(ar_ref, ai_ref, c_ref, s_ref, w_ref, b_ref, o_ref, *, scale):
    t = (_dot(ar_ref[0, 0], c_ref[...]) + _dot(ai_ref[0, 0], s_ref[...])) * scale
    o_ref[0] = (_dot(t.astype(BF16), w_ref[...]) + b_ref[...]).astype(BF16)


def _fn_last(a, cbd, sbd, w, bias, scale):
    b, _, l, c = a.shape
    tm = min(l, 1024)
    full = lambda z: pl.BlockSpec(z.shape, lambda bb, i: (0,) * z.ndim)
    return pl.pallas_call(
        functools.partial(_fn_last_kernel, scale=scale),
        grid=(b, l // tm),
        in_specs=[pl.BlockSpec((1, 1, tm, c), lambda bb, i: (bb, 0, i, 0)),
                  pl.BlockSpec((1, 1, tm, c), lambda bb, i: (bb, 1, i, 0)),
                  full(cbd), full(sbd), full(w), full(bias)],
        out_specs=pl.BlockSpec((1, tm, c), lambda bb, i: (bb, i, 0)),
        out_shape=jax.ShapeDtypeStruct((b, l, c), BF16),
        compiler_params=_cp(("parallel", "parallel")),
        name="fn_last",
    )(a, a, cbd, sbd, w, bias)


def _fn_small_kernel(u_ref, cl_ref, sl_ref, c_ref, s_ref, w_ref, b_ref, o_ref, *, scale):
    u = u_ref[0]
    uc = _dot(u, c_ref[...]).astype(BF16)
    us = _dot(u, s_ref[...]).astype(BF16)
    f = (_dot(cl_ref[...], uc) - _dot(sl_ref[...], us)) * scale
    o_ref[0] = (_dot(f.astype(BF16), w_ref[...]) + b_ref[...]).astype(BF16)


def _fn_small(zf, cl, sl, cbd, sbd, w, bias, scale):
    b, l, c = zf.shape
    sig = pl.BlockSpec((1, l, c), lambda bb: (bb, 0, 0))
    full = lambda z: pl.BlockSpec(z.shape, lambda bb: (0,) * z.ndim)
    return pl.pallas_call(
        functools.partial(_fn_small_kernel, scale=scale),
        grid=(b,),
        in_specs=[sig, full(cl), full(sl), full(cbd), full(sbd), full(w), full(bias)],
        out_specs=sig,
        out_shape=jax.ShapeDtypeStruct((b, l, c), BF16),
        compiler_params=_cp(("parallel",)),
        name="fn_small",
    )(zf, cl, sl, cbd, sbd, w, bias)


def _merge_kernel(oa_ref, ob_ref, oc_ref, od_ref, x_ref, g1_ref, gn_ref, w_ref, b_ref,
                  lng_ref, lnb_ref, o_ref, *, alpha):
    parts = []
    for idx, ref in enumerate((oa_ref, ob_ref, oc_ref, od_ref)):
        v = ref[0].astype(F32)
        ms = jnp.mean(v * v, axis=-1, keepdims=True)
        gain = gn_ref[:, idx * GROUP_W:(idx + 1) * GROUP_W]
        parts.append((v * lax.rsqrt(ms + LN_EPS) * gain).astype(BF16))
    y = _dot(jnp.concatenate(parts, axis=1), w_ref[...]) + b_ref[...]
    r = alpha * x_ref[0] + g1_ref[0] * y
    o_ref[0] = _layer_norm(r) * lng_ref[...] + lnb_ref[...]


def _merge(oa, ob, oc, od, x, g1, gn, w_out, b_out, ln_g, ln_b, alpha):
    b, l, d = x.shape
    tm = min(l, 512)
    grp = pl.BlockSpec((1, tm, GROUP_W), lambda bb, i: (bb, i, 0))
    tok = pl.BlockSpec((1, tm, d), lambda bb, i: (bb, i, 0))
    full = lambda a: pl.BlockSpec(a.shape, lambda bb, i: (0,) * a.ndim)
    return pl.pallas_call(
        functools.partial(_merge_kernel, alpha=alpha),
        grid=(b, l // tm),
        in_specs=[grp, grp, grp, grp, tok, pl.BlockSpec((1, 1, d), lambda bb, i: (bb, 0, 0)),
                  full(gn), full(w_out), full(b_out), full(ln_g), full(ln_b)],
        out_specs=tok,
        out_shape=jax.ShapeDtypeStruct((b, l, d), F32),
        compiler_params=_cp(("parallel", "parallel"), VMEM_LIMIT),
        name="merge",
    )(oa, ob, oc, od, x, g1, gn, w_out, b_out, ln_g, ln_b)


def _ffn_kernel(x_ref, xp_ref, xn_ref, sh_ref, sc_ref, g_ref, wup_ref, vec_ref, wd_ref, bd_ref,
                lng_ref, lnb_ref, o_ref, acc_ref, *, alpha, tm, n_chunks, n_tiles):
    i = pl.program_id(1)
    sh, sc = sh_ref[0], sc_ref[0]

    def hmod(v):
        return (_layer_norm(v) * (1.0 + sc) + sh).astype(BF16)

    x = x_ref[0]
    hext = jnp.concatenate([hmod(xp_ref[0]), hmod(x), hmod(xn_ref[0])], axis=0)
    rows = lax.broadcasted_iota(jnp.int32, (tm + 2 * HALO, 1), 0)
    valid = ((rows >= HALO) | (i > 0)) & ((rows < tm + HALO) | (i < n_tiles - 1))
    acc_ref[...] = jnp.zeros_like(acc_ref)

    def conv_branch(idx):
        vec = vec_ref[idx]
        u = jnp.where(valid, _dot(hext, wup_ref[idx]) + vec[0:1], 0.0)
        dn = pltpu.roll(u, 1, 0)[HALO:HALO + tm]
        up = pltpu.roll(u, tm + 2 * HALO - 1, 0)[HALO:HALO + tm]
        return dn * vec[1:2] + u[HALO:HALO + tm] * vec[2:3] + up * vec[3:4] + vec[4:5]

    def body(c, carry):
        a = conv_branch(c)
        g = conv_branch(n_chunks + c)
        act = (a * jax.nn.sigmoid(a) * g).astype(BF16)
        acc_ref[...] += _dot(act, wd_ref[c])
        return carry

    lax.fori_loop(0, n_chunks, body, 0)
    r = alpha * x + g_ref[0] * (acc_ref[...] + bd_ref[...])
    o_ref[0] = _layer_norm(r) * lng_ref[...] + lnb_ref[...]


def _ffn(x, sh, sc, g2, wup, vec, wd, b_down, ln_g, ln_b, alpha):
    b, l, d = x.shape
    tm = min(l, 512)
    n_chunks = wd.shape[0]
    main, prev, nxt = _halo_specs(tm, l, d)
    mod = pl.BlockSpec((1, 1, d), lambda bb, i: (bb, 0, 0))
    full = lambda a: pl.BlockSpec(a.shape, lambda bb, i: (0,) * a.ndim)
    return pl.pallas_call(
        functools.partial(_ffn_kernel, alpha=alpha, tm=tm, n_chunks=n_chunks, n_tiles=l // tm),
        grid=(b, l // tm),
        in_specs=[main, prev, nxt, mod, mod, mod, full(wup), full(vec), full(wd),
                  full(b_down), full(ln_g), full(ln_b)],
        out_specs=main,
        out_shape=jax.ShapeDtypeStruct((b, l, d), F32),
        scratch_shapes=[pltpu.VMEM((tm, d), F32)],
        compiler_params=_cp(("parallel", "parallel"), VMEM_LIMIT),
        name="ffn",
    )(x, x, x, sh, sc, g2, wup, vec, wd, b_down, ln_g, ln_b)


def _cos_sin(idx, n):
    ang = (idx % n).astype(F32) * (2.0 * math.pi / n)
    return jnp.cos(ang), jnp.sin(ang)


def _cplx_rows(re, im):
    return jnp.concatenate([re, im], axis=-2)


def _cplx_block(re, im):
    return jnp.concatenate([jnp.concatenate([re, -im], axis=-1), jnp.concatenate([im, re], axis=-1)], axis=-2)


def _ar(n):
    return jnp.arange(n, dtype=jnp.int32)


def _hyena_tables(l):
    n, n2 = 2 * l, FFT_N2
    n1 = n // n2
    c, s = _cos_sin(_ar(n1)[:, None] * _ar(n1)[None, :], n1)
    f_full = _cplx_rows(c, -s).astype(BF16)
    g = (jnp.concatenate([c, -s], axis=1)[:n1 // 2] / n).astype(BF16)
    k1, k2, m2 = _ar(n1)[:, None, None], _ar(n2)[None, :, None], _ar(n2)[None, None, :]
    c, s = _cos_sin(m2 * k1 + n1 * m2 * k2, n)
    mf = _cplx_block(c, -s).astype(BF16)
    return dict(f_half=f_full[:, :n1 // 2], f_full=f_full, g=g, mf=mf, mi=jnp.swapaxes(mf, 1, 2))


def _fnet_tables(l):
    n2 = FFT_N2
    n1 = l // n2
    nblk = n1 // 8
    blk, j = _ar(nblk)[:, None, None, None], _ar(8)[None, :, None, None]
    k2, m2 = _ar(n2)[None, None, :, None], _ar(n2)[None, None, None, :]
    c, s = _cos_sin(n1 * m2 * k2 + (8 * blk + j) * k2, l)
    eye = jnp.eye(8, dtype=F32)
    emb = lambda t: jnp.einsum('bjkn,ji->bjkni', t, eye).reshape(nblk, 8 * n2, n2 * 8)
    first = jnp.concatenate([emb(c), emb(-s)], axis=1).astype(BF16)
    c, s = _cos_sin(_ar(n1)[:, None] * _ar(n1)[None, :], n1)
    second = _cplx_block(c, -s).astype(BF16)
    return dict(first=first, second=second)


def _dense_tables(l):
    n = 2 * l
    c, s = _cos_sin(_ar(n)[:, None] * _ar(n)[None, :], n)
    ff = _cplx_rows(c, -s).astype(BF16)
    gd = (jnp.concatenate([c, -s], axis=1)[:l] / n).astype(BF16)
    cl, sl = _cos_sin(_ar(l)[:, None] * _ar(l)[None, :], l)
    return dict(fd=ff[:, :l], ff=ff, gd=gd, cl=cl.astype(BF16), sl=sl.astype(BF16))


def _rope_tables(l):
    rows = l // GRID_W
    row = jnp.broadcast_to(jnp.arange(rows)[:, None], (rows, GRID_W)).reshape(-1).astype(F32)
    col = jnp.broadcast_to(jnp.arange(GRID_W)[None, :], (rows, GRID_W)).reshape(-1).astype(F32)
    half = HEAD_DIM // 2
    inv = ROPE_THETA ** (-jnp.arange(0, half, 2, dtype=F32) / half)
    ar, ac = row[:, None] * inv, col[:, None] * inv
    cos = jnp.concatenate([jnp.cos(ar), jnp.cos(ar), jnp.cos(ac), jnp.cos(ac)], axis=1)
    sin = jnp.concatenate([-jnp.sin(ar), jnp.sin(ar), -jnp.sin(ac), jnp.sin(ac)], axis=1)
    return jnp.tile(cos, (1, 2)), jnp.tile(sin, (1, 2))


def kernel(x, c, ctx, c_ctx, w_ada, b_ada, w_in, sink_a, q_norm_g, k_norm_g, hy_conv_w, hy_conv_b,
           hy_f_w1, hy_f_b1, hy_f_freq, hy_f_w2, hy_f_b2, hy_f_w3, hy_skip, fnet_w, fnet_b,
           out_norm_g, w_out, b_out, ln1_g, ln1_b, ffn_w_up, ffn_b_up, ffn_conv_w, ffn_conv_b,
           ffn_w_down, ffn_b_down, ln2_g, ln2_b):
    bsz, l, d = x.shape
    lc = ctx.shape[1]
    depth = w_ada.shape[0]
    d_ff = ffn_w_down.shape[1]
    alpha = (2 * depth) ** 0.25
    n2 = FFT_N2
    n1h, n1f = 2 * l // n2, l // n2
    lanes = n2 * GROUP_W

    ht, ft, dt = _hyena_tables(l), _fnet_tables(l), _dense_tables(lc)
    cos_l, sin_l = _rope_tables(l)
    cos_c, sin_c = jnp.ones((lc, 128), F32), jnp.zeros((lc, 128), F32)
    jj = _ar(GROUP_W)
    same = (jj[:, None] // FNET_GROUP_DIM) == (jj[None, :] // FNET_GROUP_DIM)
    bd_mean = (same.astype(F32) / HEAD_DIM).astype(BF16)
    cg, sg = _cos_sin(jj[:, None] * jj[None, :], FNET_GROUP_DIM)
    cbd, sbd = jnp.where(same, cg, 0.0).astype(BF16), jnp.where(same, sg, 0.0).astype(BF16)

    n_rows = -(-(bsz + 1) // 8) * 8
    cc = jnp.zeros((n_rows, d), F32).at[:bsz].set(c).at[bsz].set(c_ctx)
    mod = _ada(cc, w_ada, b_ada)

    k_lat = _hy_filter(l, hy_f_w1, hy_f_b1, hy_f_freq, hy_f_w2, hy_f_b2, hy_f_w3)
    k_ctx = _hy_filter(lc, hy_f_w1, hy_f_b1, hy_f_freq, hy_f_w2, hy_f_b2, hy_f_w3)
    a_k = _rowmm(ht['f_full'], k_lat.reshape(depth, n1h, lanes))
    k_spec = _hy_spectrum(a_k.reshape(depth, 2, n1h, n2, GROUP_W), ht['mf'])

    n_chunks = -(-d_ff // FFN_CHUNK)
    pad = n_chunks * FFN_CHUNK - d_ff

    def chunked(v):
        v = v.reshape(v.shape[:-1] + (2, d_ff))
        v = jnp.pad(v, [(0, 0)] * (v.ndim - 1) + [(0, pad)])
        v = v.reshape(v.shape[:-2] + (2 * n_chunks, FFN_CHUNK))
        return jnp.moveaxis(v, -2, 0)

    for layer in range(depth):
        m6 = mod[layer].reshape(n_rows, 6, d)
        lat = [m6[:bsz, i][:, None, :] for i in range(6)]
        con = [jnp.broadcast_to(m6[bsz, i][None, None, :], (bsz, 1, d)) for i in range(6)]
        w_in_l = w_in[layer].astype(BF16)
        qg = jnp.tile(q_norm_g[layer], N_HEADS).reshape(1, -1)
        kg = jnp.tile(k_norm_g[layer], N_KV).reshape(1, -1)
        w_out_l = w_out[layer].astype(BF16)
        gn, b_out_l = out_norm_g[layer].reshape(1, d), b_out[layer].reshape(1, d)
        l1g, l1b = ln1_g[layer].reshape(1, d), ln1_b[layer].reshape(1, d)
        l2g, l2b = ln2_g[layer].reshape(1, d), ln2_b[layer].reshape(1, d)
        fw, fb = fnet_w[layer].astype(BF16), fnet_b[layer].reshape(1, GROUP_W)
        skip = hy_skip[layer].reshape(1, GROUP_W)
        wup = chunked(ffn_w_up[layer]).astype(BF16)
        vec = chunked(jnp.concatenate([ffn_b_up[layer][None], ffn_conv_w[layer], ffn_conv_b[layer][None],
                                       jnp.zeros((3, 2 * d_ff), F32)], axis=0))
        wd = jnp.pad(ffn_w_down[layer], ((0, pad), (0, 0))).reshape(n_chunks, FFN_CHUNK, d).astype(BF16)
        bdn = ffn_b_down[layer].reshape(1, d)
        last = layer == depth - 1

        cqa, cka, cva, cqb, ckb, cvb, czh, czf = _proj(ctx, con[0], con[1], w_in_l, cos_c, sin_c,
                                                       bd_mean, qg, kg)
        qa, ka, va, qb, kb, vb, zh, zf = _proj(x, lat[0], lat[1], w_in_l, cos_l, sin_l, bd_mean, qg, kg)
        cat = lambda a, b_: jnp.concatenate([a, b_], axis=2)
        oa = _attention(qa, cat(ka, cka), cat(va, cva), sink_a[layer], banded=True, n_lat_keys=l)
        ob = _attention(qb, cat(kb, ckb), cat(vb, cvb), None, banded=False)

        vx, x0 = _hy_gate(zh, hy_conv_w[layer], hy_conv_b[layer])
        a1 = _rowmm(ht['f_half'], vx.reshape(bsz, n1h // 2, lanes))
        bo = _hy_mid(a1.reshape(bsz, 2, n1h, n2, GROUP_W), ht['mf'], ht['mi'], k_spec, layer)
        oc = _hy_out(ht['g'], bo.reshape(bsz, 2 * n1h, lanes), vx.reshape(bsz, n1h // 2, lanes),
                     x0.reshape(bsz, n1h // 2, lanes), jnp.tile(skip, (1, n2))).reshape(bsz, l, GROUP_W)

        f1 = _fn_first(zf.reshape(bsz, n2, n1f, GROUP_W), ft['first'])
        f2 = _rowmm(ft['second'], f1.reshape(bsz, 2 * n1f, lanes))
        od = _fn_last(f2.reshape(bsz, 2, l, GROUP_W), cbd, sbd, fw, fb, (l * FNET_GROUP_DIM) ** -0.5)

        x = _merge(oa, ob, oc, od, x, lat[2], gn, w_out_l, b_out_l, l1g, l1b, alpha)
        x = _ffn(x, lat[3], lat[4], lat[5], wup, vec, wd, bdn, l2g, l2b, alpha)

        if not last:
            oac = _attention(cqa, cka, cva, sink_a[layer], banded=False)
            obc = _attention(cqb, ckb, cvb, None, banded=False)
            cvx, cx0 = _hy_gate(czh, hy_conv_w[layer], hy_conv_b[layer])
            occ = _hy_small(cvx, cx0, k_ctx, layer, dt['fd'], dt['ff'], dt['gd'], skip)
            odc = _fn_small(czf, dt['cl'], dt['sl'], cbd, sbd, fw, fb, (lc * FNET_GROUP_DIM) ** -0.5)
            ctx = _merge(oac, obc, occ, odc, ctx, con[2], gn, w_out_l, b_out_l, l1g, l1b, alpha)
            ctx = _ffn(ctx, con[3], con[4], con[5], wup, vec, wd, bdn, l2g, l2b, alpha)
    return x
```

```python
import functools
import math

import jax
import jax.numpy as jnp
from jax import lax
from jax.experimental import pallas as pl
from jax.experimental.pallas import tpu as pltpu

F32 = jnp.float32
BF16 = jnp.bfloat16

HEAD_DIM = 64
GROUP_W = 256
N_HEADS = 4
N_KV = 2
WINDOW = 128
GRID_W = 64
ROPE_THETA = 10000.0
FNET_GROUP_DIM = 64
HYENA_BANDS = 8
HYENA_FAST_DECAY = 0.3
HYENA_SLOW_DECAY = 1.5
HYENA_TARGET = 1e-2
LN_EPS = 1e-6
NEG_INF = -1e30
FFT_N2 = 128
FFN_CHUNK = 256
HALO = 16
VMEM_LIMIT = 56 * 1024 * 1024


def _cp(sem, vmem=None):
    return pltpu.CompilerParams(dimension_semantics=sem, vmem_limit_bytes=vmem)


def _layer_norm(x):
    mu = jnp.mean(x, axis=-1, keepdims=True)
    xc = x - mu
    var = jnp.mean(xc * xc, axis=-1, keepdims=True)
    return xc * lax.rsqrt(var + LN_EPS)


def _dot(a, b):
    return jnp.dot(a, b, preferred_element_type=F32)


def _ada_kernel(c_ref, w_ref, b_ref, o_ref):
    c = c_ref[...]
    s = (c * jax.nn.sigmoid(c)).astype(BF16)
    o_ref[0] = _dot(s, w_ref[0].astype(BF16)) + b_ref[0]


def _ada(cc, w_ada, b_ada):
    depth, d, n = w_ada.shape
    r = cc.shape[0]
    tn = 1536
    return pl.pallas_call(
        _ada_kernel,
        grid=(depth, n // tn),
        in_specs=[pl.BlockSpec((r, d), lambda l, j: (0, 0)),
                  pl.BlockSpec((1, d, tn), lambda l, j: (l, 0, j)),
                  pl.BlockSpec((1, 1, tn), lambda l, j: (l, 0, j))],
        out_specs=pl.BlockSpec((1, r, tn), lambda l, j: (l, 0, j)),
        out_shape=jax.ShapeDtypeStruct((depth, r, n), F32),
        compiler_params=_cp(("parallel", "parallel"), VMEM_LIMIT),
        name="ada",
    )(cc, w_ada, b_ada.reshape(depth, 1, n))


def _swap_halves(z):
    w = z.shape[1]
    lane = lax.broadcasted_iota(jnp.int32, z.shape, 1)
    return jnp.where(lane % 32 < 16, pltpu.roll(z, w - 16, 1), pltpu.roll(z, 16, 1))


def _proj_kernel(x_ref, sh_ref, sc_ref, w_ref, cos_ref, sin_ref, bd_ref, qg_ref, kg_ref,
                 qa_ref, ka_ref, va_ref, qb_ref, kb_ref, vb_ref, zh_ref, zf_ref):
    x = x_ref[0]
    h = (_layer_norm(x) * (1.0 + sc_ref[0]) + sh_ref[0]).astype(BF16)
    cos1, sin1 = cos_ref[...], sin_ref[...]
    cos2 = jnp.concatenate([cos1, cos1], axis=1)
    sin2 = jnp.concatenate([sin1, sin1], axis=1)

    def proj(lo, width):
        return _dot(h, w_ref[:, lo:lo + width])

    def rope(z):
        if z.shape[1] == 128:
            return z * cos1 + _swap_halves(z) * sin1
        return z * cos2 + _swap_halves(z) * sin2

    def head_norm(z, g):
        w = z.shape[1]
        ms = _dot((z * z).astype(BF16), bd_ref[:w, :w])
        return z * lax.rsqrt(ms + LN_EPS) * g

    def put_heads(ref, z):
        for hh in range(z.shape[1] // HEAD_DIM):
            ref[0, hh] = z[:, hh * HEAD_DIM:(hh + 1) * HEAD_DIM].astype(BF16)

    scale = HEAD_DIM ** -0.5
    put_heads(qa_ref, rope(proj(0, 256)) * scale)
    put_heads(ka_ref, rope(proj(256, 128)))
    put_heads(va_ref, proj(384, 128))
    put_heads(qb_ref, rope(head_norm(proj(512, 256), qg_ref[...])) * scale)
    put_heads(kb_ref, rope(head_norm(proj(768, 128), kg_ref[...])))
    put_heads(vb_ref, proj(896, 128))
    zh_ref[0] = proj(1024, 768).astype(BF16)
    zf_ref[0] = proj(1792, 256).astype(BF16)


def _proj(x, sh, sc, w_in, cos, sin, bd, qg, kg):
    b, l, d = x.shape
    tm = min(l, 512)
    heads = lambda n: jax.ShapeDtypeStruct((b, n, l, HEAD_DIM), BF16)
    hspec = lambda n: pl.BlockSpec((1, n, tm, HEAD_DIM), lambda t, bb: (bb, 0, t, 0))
    tok = lambda w: pl.BlockSpec((1, tm, w), lambda t, bb: (bb, t, 0))
    vec = lambda w: pl.BlockSpec((1, 1, w), lambda t, bb: (bb, 0, 0))
    full = lambda a: pl.BlockSpec(a.shape, lambda t, bb: (0,) * a.ndim)
    tab = pl.BlockSpec((tm, 128), lambda t, bb: (t, 0))
    return pl.pallas_call(
        _proj_kernel,
        grid=(l // tm, b),
        in_specs=[tok(d), vec(d), vec(d), full(w_in), tab, tab, full(bd), full(qg), full(kg)],
        out_specs=[hspec(4), hspec(2), hspec(2), hspec(4), hspec(2), hspec(2), tok(768), tok(256)],
        out_shape=[heads(4), heads(2), heads(2), heads(4), heads(2), heads(2),
                   jax.ShapeDtypeStruct((b, l, 768), BF16), jax.ShapeDtypeStruct((b, l, 256), BF16)],
        compiler_params=_cp(("parallel", "parallel"), VMEM_LIMIT),
        name="proj",
    )(x, sh, sc, w_in, cos, sin, bd, qg, kg)


_NT = (((1,), (1,)), ((), ()))


def _sink_rows(sink_ref, j, tq):
    row = lax.broadcasted_iota(jnp.int32, (2 * tq, 1), 0)
    return jnp.where(row < tq, sink_ref[2 * j], sink_ref[2 * j + 1])


def _attn_finish(o_ref, acc, denom, tq):
    o = acc[:, :HEAD_DIM] / denom
    o_ref[0] = jnp.concatenate([o[:tq], o[tq:]], axis=1).astype(BF16)


def _attn_full_kernel(sink_ref, q_ref, k_ref, v_ref, o_ref, m_ref, acc_ref, s0_ref, s1_ref, *,
                      has_sink, tq, tk, n_chunks):
    j = pl.program_id(1)
    q = q_ref[0].reshape(2 * tq, HEAD_DIM)
    lane = lax.broadcasted_iota(jnp.int32, acc_ref.shape, 1)
    if has_sink:
        m_ref[...] = _sink_rows(sink_ref, j, tq)
        acc_ref[...] = jnp.where(lane == HEAD_DIM, 1.0, 0.0)
    else:
        m_ref[...] = jnp.full_like(m_ref, NEG_INF)
        acc_ref[...] = jnp.zeros_like(acc_ref)

    def scores(c, dst):
        off = pl.multiple_of(c * tk, tk)
        dst[...] = lax.dot_general(q, k_ref[0, 0, pl.ds(off, tk), :], _NT, preferred_element_type=F32)

    def update(c, src):
        off = pl.multiple_of(c * tk, tk)
        s = src[...]
        m_prev = m_ref[...]
        m_new = jnp.maximum(m_prev, jnp.max(s, axis=-1, keepdims=True))
        p = jnp.exp((s - m_new).astype(BF16))
        acc_ref[...] = jnp.exp(m_prev - m_new) * acc_ref[...] + _dot(p, v_ref[0, 0, pl.ds(off, tk), :])
        m_ref[...] = m_new

    scores(0, s0_ref)

    def body(i, carry):
        c = 2 * i
        scores(c + 1, s1_ref)
        update(c, s0_ref)
        scores(c + 2, s0_ref)
        update(c + 1, s1_ref)
        return carry

    n_pairs = (n_chunks - 1) // 2
    lax.fori_loop(0, n_pairs, body, 0)
    c = 2 * n_pairs
    if c + 1 < n_chunks:
        scores(c + 1, s1_ref)
    update(c, s0_ref)
    if c + 1 < n_chunks:
        update(c + 1, s1_ref)
    acc = acc_ref[...]
    _attn_finish(o_ref, acc, acc[:, HEAD_DIM:HEAD_DIM + 1], tq)


def _attn_band_kernel(sink_ref, q_ref, k_ref, v_ref, o_ref, *, tq, l_lat, l_ctx):
    j, qi = pl.program_id(1), pl.program_id(2)
    span = tq + 2 * WINDOW
    start = pl.multiple_of(jnp.clip(qi * tq - WINDOW, 0, l_lat - span), WINDOW)
    q = q_ref[0].reshape(2 * tq, HEAD_DIM)
    k = jnp.concatenate([k_ref[0, 0, pl.ds(start, span), :], k_ref[0, 0, pl.ds(l_lat, l_ctx), :]], axis=0)
    v = jnp.concatenate([v_ref[0, 0, pl.ds(start, span), :], v_ref[0, 0, pl.ds(l_lat, l_ctx), :]], axis=0)
    s = lax.dot_general(q, k, _NT, preferred_element_type=F32)
    row = lax.broadcasted_iota(jnp.int32, (2 * tq, 1), 0)
    col = lax.broadcasted_iota(jnp.int32, (1, span + l_ctx), 1)
    ok = (jnp.abs(start + col - (qi * tq + row % tq)) <= WINDOW) | (col >= span)
    s = jnp.where(ok, s, NEG_INF)
    sink = _sink_rows(sink_ref, j, tq)
    m = jnp.maximum(sink, jnp.max(s, axis=-1, keepdims=True))
    acc = _dot(jnp.exp((s - m).astype(BF16)), v)
    _attn_finish(o_ref, acc, acc[:, HEAD_DIM:HEAD_DIM + 1] + jnp.exp(sink - m), tq)


def _attention(q, k, v, sink, *, banded, n_lat_keys=0):
    b, _, lq, _ = q.shape
    lk = k.shape[2]
    tq = 256
    has_sink = sink is not None
    if not has_sink:
        sink = jnp.zeros((N_HEADS,), F32)
    v = jnp.concatenate([v, jnp.ones_like(v[..., :1]), jnp.zeros_like(v[..., 1:])], axis=-1)
    scratch = []
    if banded:
        assert lq == n_lat_keys and n_lat_keys >= tq + 2 * WINDOW and tq % WINDOW == 0
        kern = functools.partial(_attn_band_kernel, tq=tq, l_lat=n_lat_keys, l_ctx=lk - n_lat_keys)
    else:
        tk = next(t for t in (768, 512, 256) if lk % t == 0)
        kern = functools.partial(_attn_full_kernel, has_sink=has_sink, tq=tq, tk=tk, n_chunks=lk // tk)
        scratch = [pltpu.VMEM((2 * tq, 1), F32), pltpu.VMEM((2 * tq, 2 * HEAD_DIM), F32),
                   pltpu.VMEM((2 * tq, tk), F32), pltpu.VMEM((2 * tq, tk), F32)]
    return pl.pallas_call(
        kern,
        grid=(b, N_KV, lq // tq),
        in_specs=[pl.BlockSpec(memory_space=pltpu.SMEM),
                  pl.BlockSpec((1, 2, tq, HEAD_DIM), lambda bb, j, qi: (bb, j, qi, 0)),
                  pl.BlockSpec((1, 1, lk, HEAD_DIM), lambda bb, j, qi: (bb, j, 0, 0)),
                  pl.BlockSpec((1, 1, lk, 2 * HEAD_DIM), lambda bb, j, qi: (bb, j, 0, 0))],
        out_specs=pl.BlockSpec((1, tq, 128), lambda bb, j, qi: (bb, qi, j)),
        out_shape=jax.ShapeDtypeStruct((b, lq, GROUP_W), BF16),
        scratch_shapes=scratch,
        compiler_params=_cp(("parallel", "parallel", "parallel"), VMEM_LIMIT),
        name="attn_band" if banded else "attn_full",
    )(sink, q, k, v)


def _shift_rows(z, prev_row, next_row):
    n = z.shape[0]
    row = lax.broadcasted_iota(jnp.int32, (n, 1), 0)
    dn = jnp.where(row == 0, prev_row, pltpu.roll(z, 1, 0))
    up = jnp.where(row == n - 1, next_row, pltpu.roll(z, n - 1, 0))
    return dn, up


def _hy_gate_kernel(z_ref, zp_ref, zn_ref, w_ref, b_ref, vx_ref, x0_ref, *, n_tiles):
    i = pl.program_id(1)
    z = z_ref[0].astype(F32)
    prev_row = zp_ref[0].astype(F32)[HALO - 1:HALO] * (i > 0).astype(F32)
    next_row = zn_ref[0].astype(F32)[0:1] * (i < n_tiles - 1).astype(F32)
    dn, up = _shift_rows(z, prev_row, next_row)
    w = w_ref[...]
    u = dn * w[0:1] + z * w[1:2] + up * w[2:3] + b_ref[...]
    x0_ref[0] = u[:, :GROUP_W].astype(BF16)
    vx_ref[0] = (u[:, 2 * GROUP_W:] * u[:, GROUP_W:2 * GROUP_W]).astype(BF16)


def _halo_specs(tl, l, w):
    nb = l // HALO
    per = tl // HALO
    return (pl.BlockSpec((1, tl, w), lambda bb, i: (bb, i, 0)),
            pl.BlockSpec((1, HALO, w), lambda bb, i: (bb, jnp.maximum(i * per - 1, 0), 0)),
            pl.BlockSpec((1, HALO, w), lambda bb, i: (bb, jnp.minimum((i + 1) * per, nb - 1), 0)))


def _hy_gate(zh, conv_w, conv_b):
    b, l, w = zh.shape
    tl = min(l, 512)
    out = jax.ShapeDtypeStruct((b, l, GROUP_W), BF16)
    ospec = pl.BlockSpec((1, tl, GROUP_W), lambda bb, i: (bb, i, 0))
    return pl.pallas_call(
        functools.partial(_hy_gate_kernel, n_tiles=l // tl),
        grid=(b, l // tl),
        in_specs=[*_halo_specs(tl, l, w),
                  pl.BlockSpec((3, w), lambda bb, i: (0, 0)),
                  pl.BlockSpec((1, w), lambda bb, i: (0, 0))],
        out_specs=[ospec, ospec],
        out_shape=[out, out],
        compiler_params=_cp(("parallel", "parallel")),
        name="hy_gate",
    )(zh, zh, zh, conv_w, conv_b.reshape(1, w))


def _hy_filter_kernel(fl_ref, w1_ref, b1_ref, fr_ref, w2_ref, b2_ref, w3_ref, dl_ref, k_ref, *, l, tr):
    m = pl.program_id(1) * tr + lax.broadcasted_iota(jnp.int32, (tr, 1), 0)
    t = jnp.where(m < l, m, 2 * l - m).astype(F32)
    tn = t / max(l - 1, 1)
    lane = lax.broadcasted_iota(jnp.int32, (1, 128), 1)
    phase = t * fl_ref[...]
    feat = jnp.where(lane == 0, tn,
                     jnp.where(lane <= HYENA_BANDS, jnp.cos(phase),
                               jnp.where(lane <= 2 * HYENA_BANDS, -jnp.sin(phase), 0.0)))
    hp = functools.partial(jnp.dot, precision=lax.Precision.HIGHEST, preferred_element_type=F32)
    fr = fr_ref[0]
    h = jnp.sin(fr * (hp(feat, w1_ref[0]) + b1_ref[0]))
    h = jnp.sin(fr * (hp(h, w2_ref[0]) + b2_ref[0]))
    h = hp(h, w3_ref[0])
    dec = jnp.exp(-tn * dl_ref[...])
    hf, hb = h[:, :GROUP_W] * dec, h[:, GROUP_W:] * dec
    k = jnp.where(m < l, hf, hb)
    k = jnp.where(m == 0, hf + hb, k)
    k_ref[0] = jnp.where(m == l, 0.0, k).astype(BF16)


def _hy_filter(l, w1, b1, fr, w2, b2, w3):
    depth, emb, hid = w1.shape
    tr = min(2 * l, 1024)
    bands = jnp.linspace(1e-4, HYENA_BANDS - 1, HYENA_BANDS, dtype=F32) * (2.0 * math.pi / l)
    fl = jnp.zeros((1, 128), F32).at[0, 1:1 + HYENA_BANDS].set(bands).at[0, 1 + HYENA_BANDS:emb].set(bands)
    w1p = jnp.zeros((depth, 128, hid), F32).at[:, :emb].set(w1)
    min_decay = math.log(HYENA_TARGET) / HYENA_SLOW_DECAY
    max_decay = math.log(HYENA_TARGET) / HYENA_FAST_DECAY
    deltas = jnp.abs(jnp.linspace(min_decay, max_decay, GROUP_W, dtype=F32)).reshape(1, GROUP_W)
    lay = lambda a: pl.BlockSpec((1,) + a.shape[1:], lambda d, i: (d,) + (0,) * (a.ndim - 1))
    fix = lambda a: pl.BlockSpec(a.shape, lambda d, i: (0,) * a.ndim)
    b1, fr, b2 = (a.reshape(depth, 1, hid) for a in (b1, fr, b2))
    return pl.pallas_call(
        functools.partial(_hy_filter_kernel, l=l, tr=tr),
        grid=(depth, 2 * l // tr),
        in_specs=[fix(fl), lay(w1p), lay(b1), lay(fr), lay(w2), lay(b2), lay(w3), fix(deltas)],
        out_specs=pl.BlockSpec((1, tr, GROUP_W), lambda d, i: (d, i, 0)),
        out_shape=jax.ShapeDtypeStruct((depth, 2 * l, GROUP_W), BF16),
        compiler_params=_cp(("parallel", "parallel")),
        name="hy_filter",
    )(fl, w1p, b1, fr, w2, b2, w3, deltas)


def _rowmm_kernel(f_ref, x_ref, o_ref):
    o_ref[0] = _dot(f_ref[...], x_ref[0]).astype(o_ref.dtype)


def _rowmm(f, xv, tl=8192):
    b, k, n = xv.shape
    m = f.shape[0]
    tl = min(tl, n)
    return pl.pallas_call(
        _rowmm_kernel,
        grid=(b, n // tl),
        in_specs=[pl.BlockSpec((m, k), lambda bb, i: (0, 0)),
                  pl.BlockSpec((1, k, tl), lambda bb, i: (bb, 0, i))],
        out_specs=pl.BlockSpec((1, m, tl), lambda bb, i: (bb, 0, i)),
        out_shape=jax.ShapeDtypeStruct((b, m, n), BF16),
        compiler_params=_cp(("parallel", "parallel"), VMEM_LIMIT),
        name="rowmm",
    )(f, xv)


def _hy_spec_kernel(a_ref, mf_ref, k_ref):
    x = a_ref[0, :, 0].reshape(2 * FFT_N2, GROUP_W)
    k_ref[0, 0] = _dot(mf_ref[0], x)


def _hy_mid_kernel(a_ref, mf_ref, mi_ref, k_ref, o_ref):
    n2 = FFT_N2
    x = a_ref[0, :, 0].reshape(2 * n2, GROUP_W)
    y = _dot(mf_ref[0], x)
    k = k_ref[0, 0]
    yr, yi, kr, ki = y[:n2], y[n2:], k[:n2], k[n2:]
    z = jnp.concatenate([yr * kr - yi * ki, yr * ki + yi * kr], axis=0).astype(BF16)
    o_ref[0, :, 0] = _dot(mi_ref[0], z).astype(BF16).reshape(2, n2, GROUP_W)


def _hy_spectrum(a, mf):
    depth, _, n1, n2, w = a.shape
    return pl.pallas_call(
        _hy_spec_kernel,
        grid=(n1, depth),
        in_specs=[pl.BlockSpec((1, 2, 1, n2, w), lambda k1, d: (d, 0, k1, 0, 0)),
                  pl.BlockSpec((1, 2 * n2, 2 * n2), lambda k1, d: (k1, 0, 0))],
        out_specs=pl.BlockSpec((1, 1, 2 * n2, w), lambda k1, d: (d, k1, 0, 0)),
        out_shape=jax.ShapeDtypeStruct((depth, n1, 2 * n2, w), F32),
        compiler_params=_cp(("parallel", "parallel")),
        name="hy_spectrum",
    )(a, mf)


def _hy_mid(a, mf, mi, kspec, layer):
    b, _, n1, n2, w = a.shape
    blk = pl.BlockSpec((1, 2, 1, n2, w), lambda k1, bb: (bb, 0, k1, 0, 0))
    mat = pl.BlockSpec((1, 2 * n2, 2 * n2), lambda k1, bb: (k1, 0, 0))
    return pl.pallas_call(
        _hy_mid_kernel,
        grid=(n1, b),
        in_specs=[blk, mat, mat,
                  pl.BlockSpec((1, 1, 2 * n2, w), lambda k1, bb: (layer, k1, 0, 0))],
        out_specs=blk,
        out_shape=jax.ShapeDtypeStruct(a.shape, BF16),
        compiler_params=_cp(("parallel", "parallel")),
        name="hy_mid",
    )(a, mf, mi, kspec)


def _hy_out_kernel(g_ref, b_ref, vx_ref, x0_ref, skip_ref, o_ref):
    y = _dot(g_ref[...], b_ref[0])
    vx = vx_ref[0].astype(F32)
    o_ref[0] = ((y + skip_ref[...] * vx) * x0_ref[0].astype(F32)).astype(BF16)


def _hy_out(g, bo, vx, x0, skip_t, tl=8192):
    b, k, n = bo.shape
    m = g.shape[0]
    tl = min(tl, n)
    sig = pl.BlockSpec((1, m, tl), lambda bb, i: (bb, 0, i))
    return pl.pallas_call(
        _hy_out_kernel,
        grid=(b, n // tl),
        in_specs=[pl.BlockSpec((m, k), lambda bb, i: (0, 0)),
                  pl.BlockSpec((1, k, tl), lambda bb, i: (bb, 0, i)),
                  sig, sig, pl.BlockSpec((1, tl), lambda bb, i: (0, 0))],
        out_specs=sig,
        out_shape=jax.ShapeDtypeStruct((b, m, n), BF16),
        compiler_params=_cp(("parallel", "parallel"), VMEM_LIMIT),
        name="hy_out",
    )(g, bo, vx, x0, skip_t)


def _hy_small_kernel(vx_ref, x0_ref, k_ref, fd_ref, ff_ref, gd_ref, skip_ref, o_ref):
    nf = fd_ref.shape[0] // 2
    vx = vx_ref[0]
    u = _dot(fd_ref[...], vx)
    k = _dot(ff_ref[...], k_ref[0])
    ur, ui, kr, ki = u[:nf], u[nf:], k[:nf], k[nf:]
    z = jnp.concatenate([ur * kr - ui * ki, ur * ki + ui * kr], axis=0).astype(BF16)
    y = _dot(gd_ref[...], z)
    o_ref[0] = ((y + skip_ref[...] * vx.astype(F32)) * x0_ref[0].astype(F32)).astype(BF16)


def _hy_small(vx, x0, kfilt, layer, fd, ff, gd, skip):
    b, l, w = vx.shape
    sig = pl.BlockSpec((1, l, w), lambda bb: (bb, 0, 0))
    full = lambda a: pl.BlockSpec(a.shape, lambda bb: (0,) * a.ndim)
    return pl.pallas_call(
        _hy_small_kernel,
        grid=(b,),
        in_specs=[sig, sig, pl.BlockSpec((1, 2 * l, w), lambda bb: (layer, 0, 0)),
                  full(fd), full(ff), full(gd), full(skip)],
        out_specs=sig,
        out_shape=jax.ShapeDtypeStruct((b, l, w), BF16),
        compiler_params=_cp(("parallel",)),
        name="hy_small",
    )(vx, x0, kfilt, fd, ff, gd, skip)


def _fn_first_kernel(z_ref, m_ref, o_ref):
    n2 = z_ref.shape[1]
    x = z_ref[0].reshape(n2 * 8, GROUP_W)
    o_ref[0] = _dot(m_ref[0], x).astype(BF16).reshape(2, 8, n2, GROUP_W)


def _fn_first(zf, mats):
    b, n2, n1, w = zf.shape
    return pl.pallas_call(
        _fn_first_kernel,
        grid=(n1 // 8, b),
        in_specs=[pl.BlockSpec((1, n2, 8, w), lambda i, bb: (bb, 0, i, 0)),
                  pl.BlockSpec((1, 16 * n2, 8 * n2), lambda i, bb: (i, 0, 0))],
        out_specs=pl.BlockSpec((1, 2, 8, n2, w), lambda i, bb: (bb, 0, i, 0, 0)),
        out_shape=jax.ShapeDtypeStruct((b, 2, n1, n2, w), BF16),
        compiler_params=_cp(("parallel", "parallel"), VMEM_LIMIT),
        name="fn_first",
    )(zf, mats)


def _fn_last_kernel(ar_ref, ai_ref, c_ref, s_ref, w_ref, b_ref, o_ref, *, scale):
    t = (_dot(ar_ref[0, 0], c_ref[...]) + _dot(ai_ref[0, 0], s_ref[...])) * scale
    o_ref[0] = (_dot(t.astype(BF16), w_ref[...]) + b_ref[...]).astype(BF16)


def _fn_last(a, cbd, sbd, w, bias, scale):
    b, _, l, c = a.shape
    tm = min(l, 1024)
    full = lambda z: pl.BlockSpec(z.shape, lambda bb, i: (0,) * z.ndim)
    return pl.pallas_call(
        functools.partial(_fn_last_kernel, scale=scale),
        grid=(b, l // tm),
        in_specs=[pl.BlockSpec((1, 1, tm, c), lambda bb, i: (bb, 0, i, 0)),
                  pl.BlockSpec((1, 1, tm, c), lambda bb, i: (bb, 1, i, 0)),
                  full(cbd), full(sbd), full(w), full(bias)],
        out_specs=pl.BlockSpec((1, tm, c), lambda bb, i: (bb, i, 0)),
        out_shape=jax.ShapeDtypeStruct((b, l, c), BF16),
        compiler_params=_cp(("parallel", "parallel")),
        name="fn_last",
    )(a, a, cbd, sbd, w, bias)


def _fn_small_kernel(u_ref, cl_ref, sl_ref, c_ref, s_ref, w_ref, b_ref, o_ref, *, scale):
    u = u_ref[0]
    uc = _dot(u, c_ref[...]).astype(BF16)
    us = _dot(u, s_ref[...]).astype(BF16)
    f = (_dot(cl_ref[...], uc) - _dot(sl_ref[...], us)) * scale
    o_ref[0] = (_dot(f.astype(BF16), w_ref[...]) + b_ref[...]).astype(BF16)


def _fn_small(zf, cl, sl, cbd, sbd, w, bias, scale):
    b, l, c = zf.shape
    sig = pl.BlockSpec((1, l, c), lambda bb: (bb, 0, 0))
    full = lambda z: pl.BlockSpec(z.shape, lambda bb: (0,) * z.ndim)
    return pl.pallas_call(
        functools.partial(_fn_small_kernel, scale=scale),
        grid=(b,),
        in_specs=[sig, full(cl), full(sl), full(cbd), full(sbd), full(w), full(bias)],
        out_specs=sig,
        out_shape=jax.ShapeDtypeStruct((b, l, c), BF16),
        compiler_params=_cp(("parallel",)),
        name="fn_small",
    )(zf, cl, sl, cbd, sbd, w, bias)


def _merge_kernel(oa_ref, ob_ref, oc_ref, od_ref, x_ref, g1_ref, gn_ref, w_ref, b_ref,
                  lng_ref, lnb_ref, o_ref, *, alpha):
    parts = []
    for idx, ref in enumerate((oa_ref, ob_ref, oc_ref, od_ref)):
        v = ref[0].astype(F32)
        ms = jnp.mean(v * v, axis=-1, keepdims=True)
        gain = gn_ref[:, idx * GROUP_W:(idx + 1) * GROUP_W]
        parts.append((v * lax.rsqrt(ms + LN_EPS) * gain).astype(BF16))
    y = _dot(jnp.concatenate(parts, axis=1), w_ref[...]) + b_ref[...]
    r = alpha * x_ref[0] + g1_ref[0] * y
    o_ref[0] = _layer_norm(r) * lng_ref[...] + lnb_ref[...]


def _merge(oa, ob, oc, od, x, g1, gn, w_out, b_out, ln_g, ln_b, alpha):
    b, l, d = x.shape
    tm = min(l, 512)
    grp = pl.BlockSpec((1, tm, GROUP_W), lambda bb, i: (bb, i, 0))
    tok = pl.BlockSpec((1, tm, d), lambda bb, i: (bb, i, 0))
    full = lambda a: pl.BlockSpec(a.shape, lambda bb, i: (0,) * a.ndim)
    return pl.pallas_call(
        functools.partial(_merge_kernel, alpha=alpha),
        grid=(b, l // tm),
        in_specs=[grp, grp, grp, grp, tok, pl.BlockSpec((1, 1, d), lambda bb, i: (bb, 0, 0)),
                  full(gn), full(w_out), full(b_out), full(ln_g), full(ln_b)],
        out_specs=tok,
        out_shape=jax.ShapeDtypeStruct((b, l, d), F32),
        compiler_params=_cp(("parallel", "parallel"), VMEM_LIMIT),
        name="merge",
    )(oa, ob, oc, od, x, g1, gn, w_out, b_out, ln_g, ln_b)


def _ffn_kernel(x_ref, xp_ref, xn_ref, sh_ref, sc_ref, g_ref, wup_ref, vec_ref, wd_ref, bd_ref,
                lng_ref, lnb_ref, o_ref, acc_ref, *, alpha, tm, n_chunks, n_tiles):
    i = pl.program_id(1)
    sh, sc = sh_ref[0], sc_ref[0]

    def hmod(v):
        return (_layer_norm(v) * (1.0 + sc) + sh).astype(BF16)

    x = x_ref[0]
    hext = jnp.concatenate([hmod(xp_ref[0]), hmod(x), hmod(xn_ref[0])], axis=0)
    rows = lax.broadcasted_iota(jnp.int32, (tm + 2 * HALO, 1), 0)
    valid = ((rows >= HALO) | (i > 0)) & ((rows < tm + HALO) | (i < n_tiles - 1))
    acc_ref[...] = jnp.zeros_like(acc_ref)

    def conv_branch(idx):
        vec = vec_ref[idx]
        u = jnp.where(valid, _dot(hext, wup_ref[idx]) + vec[0:1], 0.0)
        dn = pltpu.roll(u, 1, 0)[HALO:HALO + tm]
        up = pltpu.roll(u, tm + 2 * HALO - 1, 0)[HALO:HALO + tm]
        return dn * vec[1:2] + u[HALO:HALO + tm] * vec[2:3] + up * vec[3:4] + vec[4:5]

    def body(c, carry):
        a = conv_branch(c)
        g = conv_branch(n_chunks + c)
        act = (a * jax.nn.sigmoid(a) * g).astype(BF16)
        acc_ref[...] += _dot(act, wd_ref[c])
        return carry

    lax.fori_loop(0, n_chunks, body, 0)
    r = alpha * x + g_ref[0] * (acc_ref[...] + bd_ref[...])
    o_ref[0] = _layer_norm(r) * lng_ref[...] + lnb_ref[...]


def _ffn(x, sh, sc, g2, wup, vec, wd, b_down, ln_g, ln_b, alpha):
    b, l, d = x.shape
    tm = min(l, 512)
    n_chunks = wd.shape[0]
    main, prev, nxt = _halo_specs(tm, l, d)
    mod = pl.BlockSpec((1, 1, d), lambda bb, i: (bb, 0, 0))
    full = lambda a: pl.BlockSpec(a.shape, lambda bb, i: (0,) * a.ndim)
    return pl.pallas_call(
        functools.partial(_ffn_kernel, alpha=alpha, tm=tm, n_chunks=n_chunks, n_tiles=l // tm),
        grid=(b, l // tm),
        in_specs=[main, prev, nxt, mod, mod, mod, full(wup), full(vec), full(wd),
                  full(b_down), full(ln_g), full(ln_b)],
        out_specs=main,
        out_shape=jax.ShapeDtypeStruct((b, l, d), F32),
        scratch_shapes=[pltpu.VMEM((tm, d), F32)],
        compiler_params=_cp(("parallel", "parallel"), VMEM_LIMIT),
        name="ffn",
    )(x, x, x, sh, sc, g2, wup, vec, wd, b_down, ln_g, ln_b)


def _cos_sin(idx, n):
    ang = (idx % n).astype(F32) * (2.0 * math.pi / n)
    return jnp.cos(ang), jnp.sin(ang)


def _cplx_rows(re, im):
    return jnp.concatenate([re, im], axis=-2)


def _cplx_block(re, im):
    return jnp.concatenate([jnp.concatenate([re, -im], axis=-1), jnp.concatenate([im, re], axis=-1)], axis=-2)


def _ar(n):
    return jnp.arange(n, dtype=jnp.int32)


def _hyena_tables(l):
    n, n2 = 2 * l, FFT_N2
    n1 = n // n2
    c, s = _cos_sin(_ar(n1)[:, None] * _ar(n1)[None, :], n1)
    f_full = _cplx_rows(c, -s).astype(BF16)
    g = (jnp.concatenate([c, -s], axis=1)[:n1 // 2] / n).astype(BF16)
    k1, k2, m2 = _ar(n1)[:, None, None], _ar(n2)[None, :, None], _ar(n2)[None, None, :]
    c, s = _cos_sin(m2 * k1 + n1 * m2 * k2, n)
    mf = _cplx_block(c, -s).astype(BF16)
    return dict(f_half=f_full[:, :n1 // 2], f_full=f_full, g=g, mf=mf, mi=jnp.swapaxes(mf, 1, 2))


def _fnet_tables(l):
    n2 = FFT_N2
    n1 = l // n2
    nblk = n1 // 8
    blk, j = _ar(nblk)[:, None, None, None], _ar(8)[None, :, None, None]
    k2, m2 = _ar(n2)[None, None, :, None], _ar(n2)[None, None, None, :]
    c, s = _cos_sin(n1 * m2 * k2 + (8 * blk + j) * k2, l)
    eye = jnp.eye(8, dtype=F32)
    emb = lambda t: jnp.einsum('bjkn,ji->bjkni', t, eye).reshape(nblk, 8 * n2, n2 * 8)
    first = jnp.concatenate([emb(c), emb(-s)], axis=1).astype(BF16)
    c, s = _cos_sin(_ar(n1)[:, None] * _ar(n1)[None, :], n1)
    second = _cplx_block(c, -s).astype(BF16)
    return dict(first=first, second=second)


def _dense_tables(l):
    n = 2 * l
    c, s = _cos_sin(_ar(n)[:, None] * _ar(n)[None, :], n)
    ff = _cplx_rows(c, -s).astype(BF16)
    gd = (jnp.concatenate([c, -s], axis=1)[:l] / n).astype(BF16)
    cl, sl = _cos_sin(_ar(l)[:, None] * _ar(l)[None, :], l)
    return dict(fd=ff[:, :l], ff=ff, gd=gd, cl=cl.astype(BF16), sl=sl.astype(BF16))


def _rope_tables(l):
    rows = l // GRID_W
    row = jnp.broadcast_to(jnp.arange(rows)[:, None], (rows, GRID_W)).reshape(-1).astype(F32)
    col = jnp.broadcast_to(jnp.arange(GRID_W)[None, :], (rows, GRID_W)).reshape(-1).astype(F32)
    half = HEAD_DIM // 2
    inv = ROPE_THETA ** (-jnp.arange(0, half, 2, dtype=F32) / half)
    ar, ac = row[:, None] * inv, col[:, None] * inv
    cos = jnp.concatenate([jnp.cos(ar), jnp.cos(ar), jnp.cos(ac), jnp.cos(ac)], axis=1)
    sin = jnp.concatenate([-jnp.sin(ar), jnp.sin(ar), -jnp.sin(ac), jnp.sin(ac)], axis=1)
    return jnp.tile(cos, (1, 2)), jnp.tile(sin, (1, 2))


def kernel(x, c, ctx, c_ctx, w_ada, b_ada, w_in, sink_a, q_norm_g, k_norm_g, hy_conv_w, hy_conv_b,
           hy_f_w1, hy_f_b1, hy_f_freq, hy_f_w2, hy_f_b2, hy_f_w3, hy_skip, fnet_w, fnet_b,
           out_norm_g, w_out, b_out, ln1_g, ln1_b, ffn_w_up, ffn_b_up, ffn_conv_w, ffn_conv_b,
           ffn_w_down, ffn_b_down, ln2_g, ln2_b):
    bsz, l, d = x.shape
    lc = ctx.shape[1]
    depth = w_ada.shape[0]
    d_ff = ffn_w_down.shape[1]
    alpha = (2 * depth) ** 0.25
    n2 = FFT_N2
    n1h, n1f = 2 * l // n2, l // n2
    lanes = n2 * GROUP_W

    ht, ft, dt = _hyena_tables(l), _fnet_tables(l), _dense_tables(lc)
    cos_l, sin_l = _rope_tables(l)
    cos_c, sin_c = jnp.ones((lc, 128), F32), jnp.zeros((lc, 128), F32)
    jj = _ar(GROUP_W)
    same = (jj[:, None] // FNET_GROUP_DIM) == (jj[None, :] // FNET_GROUP_DIM)
    bd_mean = (same.astype(F32) / HEAD_DIM).astype(BF16)
    cg, sg = _cos_sin(jj[:, None] * jj[None, :], FNET_GROUP_DIM)
    cbd, sbd = jnp.where(same, cg, 0.0).astype(BF16), jnp.where(same, sg, 0.0).astype(BF16)

    n_rows = -(-(bsz + 1) // 8) * 8
    cc = jnp.zeros((n_rows, d), F32).at[:bsz].set(c).at[bsz].set(c_ctx)
    mod = _ada(cc, w_ada, b_ada)

    k_lat = _hy_filter(l, hy_f_w1, hy_f_b1, hy_f_freq, hy_f_w2, hy_f_b2, hy_f_w3)
    k_ctx = _hy_filter(lc, hy_f_w1, hy_f_b1, hy_f_freq, hy_f_w2, hy_f_b2, hy_f_w3)
    a_k = _rowmm(ht['f_full'], k_lat.reshape(depth, n1h, lanes))
    k_spec = _hy_spectrum(a_k.reshape(depth, 2, n1h, n2, GROUP_W), ht['mf'])

    n_chunks = -(-d_ff // FFN_CHUNK)
    pad = n_chunks * FFN_CHUNK - d_ff

    def chunked(v):
        v = v.reshape(v.shape[:-1] + (2, d_ff))
        v = jnp.pad(v, [(0, 0)] * (v.ndim - 1) + [(0, pad)])
        v = v.reshape(v.shape[:-2] + (2 * n_chunks, FFN_CHUNK))
        return jnp.moveaxis(v, -2, 0)

    for layer in range(depth):
        m6 = mod[layer].reshape(n_rows, 6, d)
        lat = [m6[:bsz, i][:, None, :] for i in range(6)]
        con = [jnp.broadcast_to(m6[bsz, i][None, None, :], (bsz, 1, d)) for i in range(6)]
        w_in_l = w_in[layer].astype(BF16)
        qg = jnp.tile(q_norm_g[layer], N_HEADS).reshape(1, -1)
        kg = jnp.tile(k_norm_g[layer], N_KV).reshape(1, -1)
        w_out_l = w_out[layer].astype(BF16)
        gn, b_out_l = out_norm_g[layer].reshape(1, d), b_out[layer].reshape(1, d)
        l1g, l1b = ln1_g[layer].reshape(1, d), ln1_b[layer].reshape(1, d)
        l2g, l2b = ln2_g[layer].reshape(1, d), ln2_b[layer].reshape(1, d)
        fw, fb = fnet_w[layer].astype(BF16), fnet_b[layer].reshape(1, GROUP_W)
        skip = hy_skip[layer].reshape(1, GROUP_W)
        wup = chunked(ffn_w_up[layer]).astype(BF16)
        vec = chunked(jnp.concatenate([ffn_b_up[layer][None], ffn_conv_w[layer], ffn_conv_b[layer][None],
                                       jnp.zeros((3, 2 * d_ff), F32)], axis=0))
        wd = jnp.pad(ffn_w_down[layer], ((0, pad), (0, 0))).reshape(n_chunks, FFN_CHUNK, d).astype(BF16)
        bdn = ffn_b_down[layer].reshape(1, d)
        last = layer == depth - 1

        cqa, cka, cva, cqb, ckb, cvb, czh, czf = _proj(ctx, con[0], con[1], w_in_l, cos_c, sin_c,
                                                       bd_mean, qg, kg)
        qa, ka, va, qb, kb, vb, zh, zf = _proj(x, lat[0], lat[1], w_in_l, cos_l, sin_l, bd_mean, qg, kg)
        cat = lambda a, b_: jnp.concatenate([a, b_], axis=2)
        oa = _attention(qa, cat(ka, cka), cat(va, cva), sink_a[layer], banded=True, n_lat_keys=l)
        ob = _attention(qb, cat(kb, ckb), cat(vb, cvb), None, banded=False)

        vx, x0 = _hy_gate(zh, hy_conv_w[layer], hy_conv_b[layer])
        a1 = _rowmm(ht['f_half'], vx.reshape(bsz, n1h // 2, lanes))
        bo = _hy_mid(a1.reshape(bsz, 2, n1h, n2, GROUP_W), ht['mf'], ht['mi'], k_spec, layer)
        oc = _hy_out(ht['g'], bo.reshape(bsz, 2 * n1h, lanes), vx.reshape(bsz, n1h // 2, lanes),
                     x0.reshape(bsz, n1h // 2, lanes), jnp.tile(skip, (1, n2))).reshape(bsz, l, GROUP_W)

        f1 = _fn_first(zf.reshape(bsz, n2, n1f, GROUP_W), ft['first'])
        f2 = _rowmm(ft['second'], f1.reshape(bsz, 2 * n1f, lanes))
        od = _fn_last(f2.reshape(bsz, 2, l, GROUP_W), cbd, sbd, fw, fb, (l * FNET_GROUP_DIM) ** -0.5)

        x = _merge(oa, ob, oc, od, x, lat[2], gn, w_out_l, b_out_l, l1g, l1b, alpha)
        x = _ffn(x, lat[3], lat[4], lat[5], wup, vec, wd, bdn, l2g, l2b, alpha)

        if not last:
            oac = _attention(cqa, cka, cva, sink_a[layer], banded=False)
            obc = _attention(cqb, ckb, cvb, None, banded=False)
            cvx, cx0 = _hy_gate(czh, hy_conv_w[layer], hy_conv_b[layer])
            occ = _hy_small(cvx, cx0, k_ctx, layer, dt['fd'], dt['ff'], dt['gd'], skip)
            odc = _fn_small(czf, dt['cl'], dt['sl'], cbd, sbd, fw, fb, (lc * FNET_GROUP_DIM) ** -0.5)
            ctx = _merge(oac, obc, occ, odc, ctx, con[2], gn, w_out_l, b_out_l, l1g, l1b, alpha)
            ctx = _ffn(ctx, con[3], con[4], con[5], wup, vec, wd, bdn, l2g, l2b, alpha)
    return x
```

```python
import functools
import math

import jax
import jax.numpy as jnp
from jax import lax
from jax.experimental import pallas as pl
from jax.experimental.pallas import tpu as pltpu

F32 = jnp.float32
BF16 = jnp.bfloat16

HEAD_DIM = 64
GROUP_W = 256
N_HEADS = 4
N_KV = 2
WINDOW = 128
GRID_W = 64
ROPE_THETA = 10000.0
FNET_GROUP_DIM = 64
HYENA_BANDS = 8
HYENA_FAST_DECAY = 0.3
HYENA_SLOW_DECAY = 1.5
HYENA_TARGET = 1e-2
LN_EPS = 1e-6
NEG_INF = -1e30
FFT_N2 = 128
FFN_CHUNK = 256
HALO = 16
VMEM_LIMIT = 56 * 1024 * 1024


def _cp(sem, vmem=None):
    return pltpu.CompilerParams(dimension_semantics=sem, vmem_limit_bytes=vmem)


def _layer_norm(x):
    mu = jnp.mean(x, axis=-1, keepdims=True)
    xc = x - mu
    var = jnp.mean(xc * xc, axis=-1, keepdims=True)
    return xc * lax.rsqrt(var + LN_EPS)


def _dot(a, b):
    return jnp.dot(a, b, preferred_element_type=F32)


def _ada_kernel(c_ref, w_ref, b_ref, o_ref):
    c = c_ref[...]
    s = (c * jax.nn.sigmoid(c)).astype(BF16)
    o_ref[0] = _dot(s, w_ref[0].astype(BF16)) + b_ref[0]


def _ada(cc, w_ada, b_ada):
    depth, d, n = w_ada.shape
    r = cc.shape[0]
    tn = 1536
    return pl.pallas_call(
        _ada_kernel,
        grid=(depth, n // tn),
        in_specs=[pl.BlockSpec((r, d), lambda l, j: (0, 0)),
                  pl.BlockSpec((1, d, tn), lambda l, j: (l, 0, j)),
                  pl.BlockSpec((1, 1, tn), lambda l, j: (l, 0, j))],
        out_specs=pl.BlockSpec((1, r, tn), lambda l, j: (l, 0, j)),
        out_shape=jax.ShapeDtypeStruct((depth, r, n), F32),
        compiler_params=_cp(("parallel", "parallel"), VMEM_LIMIT),
        name="ada",
    )(cc, w_ada, b_ada.reshape(depth, 1, n))


def _swap_halves(z):
    w = z.shape[1]
    lane = lax.broadcasted_iota(jnp.int32, z.shape, 1)
    return jnp.where(lane % 32 < 16, pltpu.roll(z, w - 16, 1), pltpu.roll(z, 16, 1))


def _proj_kernel(x_ref, sh_ref, sc_ref, w_ref, cos_ref, sin_ref, bd_ref, qg_ref, kg_ref,
                 qa_ref, ka_ref, va_ref, qb_ref, kb_ref, vb_ref, zh_ref, zf_ref):
    x = x_ref[0]
    h = (_layer_norm(x) * (1.0 + sc_ref[0]) + sh_ref[0]).astype(BF16)
    cos1, sin1 = cos_ref[...], sin_ref[...]
    cos2 = jnp.concatenate([cos1, cos1], axis=1)
    sin2 = jnp.concatenate([sin1, sin1], axis=1)

    def proj(lo, width):
        return _dot(h, w_ref[:, lo:lo + width])

    def rope(z):
        if z.shape[1] == 128:
            return z * cos1 + _swap_halves(z) * sin1
        return z * cos2 + _swap_halves(z) * sin2

    def head_norm(z, g):
        w = z.shape[1]
        ms = _dot((z * z).astype(BF16), bd_ref[:w, :w])
        return z * lax.rsqrt(ms + LN_EPS) * g

    def put_heads(ref, z):
        for hh in range(z.shape[1] // HEAD_DIM):
            ref[0, hh] = z[:, hh * HEAD_DIM:(hh + 1) * HEAD_DIM].astype(BF16)

    scale = HEAD_DIM ** -0.5
    put_heads(qa_ref, rope(proj(0, 256)) * scale)
    put_heads(ka_ref, rope(proj(256, 128)))
    put_heads(va_ref, proj(384, 128))
    put_heads(qb_ref, rope(head_norm(proj(512, 256), qg_ref[...])) * scale)
    put_heads(kb_ref, rope(head_norm(proj(768, 128), kg_ref[...])))
    put_heads(vb_ref, proj(896, 128))
    zh_ref[0] = proj(1024, 768).astype(BF16)
    zf_ref[0] = proj(1792, 256).astype(BF16)


def _proj(x, sh, sc, w_in, cos, sin, bd, qg, kg):
    b, l, d = x.shape
    tm = min(l, 512)
    heads = lambda n: jax.ShapeDtypeStruct((b, n, l, HEAD_DIM), BF16)
    hspec = lambda n: pl.BlockSpec((1, n, tm, HEAD_DIM), lambda t, bb: (bb, 0, t, 0))
    tok = lambda w: pl.BlockSpec((1, tm, w), lambda t, bb: (bb, t, 0))
    vec = lambda w: pl.BlockSpec((1, 1, w), lambda t, bb: (bb, 0, 0))
    full = lambda a: pl.BlockSpec(a.shape, lambda t, bb: (0,) * a.ndim)
    tab = pl.BlockSpec((tm, 128), lambda t, bb: (t, 0))
    return pl.pallas_call(
        _proj_kernel,
        grid=(l // tm, b),
        in_specs=[tok(d), vec(d), vec(d), full(w_in), tab, tab, full(bd), full(qg), full(kg)],
        out_specs=[hspec(4), hspec(2), hspec(2), hspec(4), hspec(2), hspec(2), tok(768), tok(256)],
        out_shape=[heads(4), heads(2), heads(2), heads(4), heads(2), heads(2),
                   jax.ShapeDtypeStruct((b, l, 768), BF16), jax.ShapeDtypeStruct((b, l, 256), BF16)],
        compiler_params=_cp(("parallel", "parallel"), VMEM_LIMIT),
        name="proj",
    )(x, sh, sc, w_in, cos, sin, bd, qg, kg)


_NT = (((1,), (1,)), ((), ()))


def _sink_rows(sink_ref, j, tq):
    row = lax.broadcasted_iota(jnp.int32, (2 * tq, 1), 0)
    return jnp.where(row < tq, sink_ref[2 * j], sink_ref[2 * j + 1])


def _attn_finish(o_ref, acc, denom, tq):
    o = acc[:, :HEAD_DIM] / denom
    o_ref[0] = jnp.concatenate([o[:tq], o[tq:]], axis=1).astype(BF16)


def _attn_full_kernel(sink_ref, q_ref, k_ref, v_ref, o_ref, m_ref, acc_ref, s0_ref, s1_ref, *,
                      has_sink, tq, tk, n_t, n_chunks):
    j = pl.program_id(1)
    lane = lax.broadcasted_iota(jnp.int32, acc_ref.shape[1:], 1)
    for tile in range(n_t):
        if has_sink:
            m_ref[tile] = _sink_rows(sink_ref, j, tq)
            acc_ref[tile] = jnp.where(lane == HEAD_DIM, 1.0, 0.0)
        else:
            m_ref[tile] = jnp.full(m_ref.shape[1:], NEG_INF, F32)
            acc_ref[tile] = jnp.zeros(acc_ref.shape[1:], F32)

    def chunk(ref, t):
        c = t // n_t
        off = c * tk if isinstance(c, int) else pl.multiple_of(c * tk, tk)
        return ref[0, 0, pl.ds(off, tk), :]

    def scores(t, tile, dst):
        q = q_ref[0, :, tile * tq:(tile + 1) * tq, :].reshape(2 * tq, HEAD_DIM)
        dst[...] = lax.dot_general(q, chunk(k_ref, t), _NT, preferred_element_type=F32)

    def update(t, tile, src):
        s = src[...]
        m_prev = m_ref[tile]
        m_new = jnp.maximum(m_prev, jnp.max(s, axis=-1, keepdims=True))
        p = jnp.exp((s - m_new).astype(BF16))
        acc_ref[tile] = jnp.exp(m_prev - m_new) * acc_ref[tile] + _dot(p, chunk(v_ref, t))
        m_ref[tile] = m_new

    total = n_t * n_chunks
    odd_tile = 1 % n_t
    scores(0, 0, s0_ref)

    def body(i, carry):
        t = 2 * i
        scores(t + 1, odd_tile, s1_ref)
        update(t, 0, s0_ref)
        scores(t + 2, 0, s0_ref)
        update(t + 1, odd_tile, s1_ref)
        return carry

    n_pairs = (total - 1) // 2
    lax.fori_loop(0, n_pairs, body, 0, unroll=2)
    t = 2 * n_pairs
    if t + 1 < total:
        scores(t + 1, odd_tile, s1_ref)
    update(t, 0, s0_ref)
    if t + 1 < total:
        update(t + 1, odd_tile, s1_ref)
    for tile in range(n_t):
        acc = acc_ref[tile]
        o = acc[:, :HEAD_DIM] * (1.0 / acc[:, HEAD_DIM:HEAD_DIM + 1])
        o_ref[0, tile * tq:(tile + 1) * tq, :] = jnp.concatenate([o[:tq], o[tq:]], axis=1).astype(BF16)


def _attn_band_kernel(sink_ref, q_ref, k_ref, v_ref, o_ref, *, tq, l_lat, l_ctx):
    j, qi = pl.program_id(1), pl.program_id(2)
    span = tq + 2 * WINDOW
    start = pl.multiple_of(jnp.clip(qi * tq - WINDOW, 0, l_lat - span), WINDOW)
    q = q_ref[0].reshape(2 * tq, HEAD_DIM)
    k = jnp.concatenate([k_ref[0, 0, pl.ds(start, span), :], k_ref[0, 0, pl.ds(l_lat, l_ctx), :]], axis=0)
    v = jnp.concatenate([v_ref[0, 0, pl.ds(start, span), :], v_ref[0, 0, pl.ds(l_lat, l_ctx), :]], axis=0)
    s = lax.dot_general(q, k, _NT, preferred_element_type=F32)
    row = lax.broadcasted_iota(jnp.int32, (2 * tq, 1), 0)
    col = lax.broadcasted_iota(jnp.int32, (1, span + l_ctx), 1)
    ok = (jnp.abs(start + col - (qi * tq + row % tq)) <= WINDOW) | (col >= span)
    s = jnp.where(ok, s, NEG_INF)
    sink = _sink_rows(sink_ref, j, tq)
    m = jnp.maximum(sink, jnp.max(s, axis=-1, keepdims=True))
    acc = _dot(jnp.exp((s - m).astype(BF16)), v)
    _attn_finish(o_ref, acc, acc[:, HEAD_DIM:HEAD_DIM + 1] + jnp.exp(sink - m), tq)


def _attention(q, k, v, sink, *, banded, n_lat_keys=0):
    b, _, lq, _ = q.shape
    lk = k.shape[2]
    tq = 256
    has_sink = sink is not None
    if not has_sink:
        sink = jnp.zeros((N_HEADS,), F32)
    v = jnp.concatenate([v, jnp.ones_like(v[..., :1]), jnp.zeros_like(v[..., 1:])], axis=-1)
    smem = pl.BlockSpec(memory_space=pltpu.SMEM)
    out_shape = jax.ShapeDtypeStruct((b, lq, GROUP_W), BF16)
    if banded:
        assert lq == n_lat_keys and n_lat_keys >= tq + 2 * WINDOW and tq % WINDOW == 0
        return pl.pallas_call(
            functools.partial(_attn_band_kernel, tq=tq, l_lat=n_lat_keys, l_ctx=lk - n_lat_keys),
            grid=(b, N_KV, lq // tq),
            in_specs=[smem,
                      pl.BlockSpec((1, 2, tq, HEAD_DIM), lambda bb, j, qi: (bb, j, qi, 0)),
                      pl.BlockSpec((1, 1, lk, HEAD_DIM), lambda bb, j, qi: (bb, j, 0, 0)),
                      pl.BlockSpec((1, 1, lk, 2 * HEAD_DIM), lambda bb, j, qi: (bb, j, 0, 0))],
            out_specs=pl.BlockSpec((1, tq, 128), lambda bb, j, qi: (bb, qi, j)),
            out_shape=out_shape,
            compiler_params=_cp(("parallel", "parallel", "parallel"), VMEM_LIMIT),
            name="attn_band",
        )(sink, q, k, v)
    tk = next(t for t in (768, 512, 256) if lk % t == 0)
    n_t = 2 if lq % (2 * tq) == 0 else 1
    kern = functools.partial(_attn_full_kernel, has_sink=has_sink, tq=tq, tk=tk, n_t=n_t, n_chunks=lk // tk)
    return pl.pallas_call(
        kern,
        grid=(b, N_KV, lq // (n_t * tq)),
        in_specs=[smem,
                  pl.BlockSpec((1, 2, n_t * tq, HEAD_DIM), lambda bb, j, qi: (bb, j, qi, 0)),
                  pl.BlockSpec((1, 1, lk, HEAD_DIM), lambda bb, j, qi: (bb, j, 0, 0)),
                  pl.BlockSpec((1, 1, lk, 2 * HEAD_DIM), lambda bb, j, qi: (bb, j, 0, 0))],
        out_specs=pl.BlockSpec((1, n_t * tq, 128), lambda bb, j, qi: (bb, qi, j)),
        out_shape=out_shape,
        scratch_shapes=[pltpu.VMEM((n_t, 2 * tq, 1), F32), pltpu.VMEM((n_t, 2 * tq, 2 * HEAD_DIM), F32),
                        pltpu.VMEM((2 * tq, tk), F32), pltpu.VMEM((2 * tq, tk), F32)],
        compiler_params=_cp(("parallel", "parallel", "parallel"), VMEM_LIMIT),
        name="attn_full",
    )(sink, q, k, v)


def _shift_rows(z, prev_row, next_row):
    n = z.shape[0]
    row = lax.broadcasted_iota(jnp.int32, (n, 1), 0)
    dn = jnp.where(row == 0, prev_row, pltpu.roll(z, 1, 0))
    up = jnp.where(row == n - 1, next_row, pltpu.roll(z, n - 1, 0))
    return dn, up


def _hy_gate_kernel(z_ref, zp_ref, zn_ref, w_ref, b_ref, vx_ref, x0_ref, *, n_tiles):
    i = pl.program_id(1)
    z = z_ref[0].astype(F32)
    prev_row = zp_ref[0].astype(F32)[HALO - 1:HALO] * (i > 0).astype(F32)
    next_row = zn_ref[0].astype(F32)[0:1] * (i < n_tiles - 1).astype(F32)
    dn, up = _shift_rows(z, prev_row, next_row)
    w = w_ref[...]
    u = dn * w[0:1] + z * w[1:2] + up * w[2:3] + b_ref[...]
    x0_ref[0] = u[:, :GROUP_W].astype(BF16)
    vx_ref[0] = (u[:, 2 * GROUP_W:] * u[:, GROUP_W:2 * GROUP_W]).astype(BF16)


def _halo_specs(tl, l, w):
    nb = l // HALO
    per = tl // HALO
    return (pl.BlockSpec((1, tl, w), lambda bb, i: (bb, i, 0)),
            pl.BlockSpec((1, HALO, w), lambda bb, i: (bb, jnp.maximum(i * per - 1, 0), 0)),
            pl.BlockSpec((1, HALO, w), lambda bb, i: (bb, jnp.minimum((i + 1) * per, nb - 1), 0)))


def _hy_gate(zh, conv_w, conv_b):
    b, l, w = zh.shape
    tl = min(l, 512)
    out = jax.ShapeDtypeStruct((b, l, GROUP_W), BF16)
    ospec = pl.BlockSpec((1, tl, GROUP_W), lambda bb, i: (bb, i, 0))
    return pl.pallas_call(
        functools.partial(_hy_gate_kernel, n_tiles=l // tl),
        grid=(b, l // tl),
        in_specs=[*_halo_specs(tl, l, w),
                  pl.BlockSpec((3, w), lambda bb, i: (0, 0)),
                  pl.BlockSpec((1, w), lambda bb, i: (0, 0))],
        out_specs=[ospec, ospec],
        out_shape=[out, out],
        compiler_params=_cp(("parallel", "parallel")),
        name="hy_gate",
    )(zh, zh, zh, conv_w, conv_b.reshape(1, w))


def _hy_filter_kernel(fl_ref, w1_ref, b1_ref, fr_ref, w2_ref, b2_ref, w3_ref, dl_ref, k_ref, *, l, tr):
    m = pl.program_id(1) * tr + lax.broadcasted_iota(jnp.int32, (tr, 1), 0)
    t = jnp.where(m < l, m, 2 * l - m).astype(F32)
    tn = t / max(l - 1, 1)
    lane = lax.broadcasted_iota(jnp.int32, (1, 128), 1)
    phase = t * fl_ref[...]
    feat = jnp.where(lane == 0, tn,
                     jnp.where(lane <= HYENA_BANDS, jnp.cos(phase),
                               jnp.where(lane <= 2 * HYENA_BANDS, -jnp.sin(phase), 0.0)))
    hp = functools.partial(jnp.dot, precision=lax.Precision.HIGHEST, preferred_element_type=F32)
    fr = fr_ref[0]
    h = jnp.sin(fr * (hp(feat, w1_ref[0]) + b1_ref[0]))
    h = jnp.sin(fr * (hp(h, w2_ref[0]) + b2_ref[0]))
    h = hp(h, w3_ref[0])
    dec = jnp.exp(-tn * dl_ref[...])
    hf, hb = h[:, :GROUP_W] * dec, h[:, GROUP_W:] * dec
    k = jnp.where(m < l, hf, hb)
    k = jnp.where(m == 0, hf + hb, k)
    k_ref[0] = jnp.where(m == l, 0.0, k).astype(BF16)


def _hy_filter(l, w1, b1, fr, w2, b2, w3):
    depth, emb, hid = w1.shape
    tr = min(2 * l, 1024)
    bands = jnp.linspace(1e-4, HYENA_BANDS - 1, HYENA_BANDS, dtype=F32) * (2.0 * math.pi / l)
    fl = jnp.zeros((1, 128), F32).at[0, 1:1 + HYENA_BANDS].set(bands).at[0, 1 + HYENA_BANDS:emb].set(bands)
    w1p = jnp.zeros((depth, 128, hid), F32).at[:, :emb].set(w1)
    min_decay = math.log(HYENA_TARGET) / HYENA_SLOW_DECAY
    max_decay = math.log(HYENA_TARGET) / HYENA_FAST_DECAY
    deltas = jnp.abs(jnp.linspace(min_decay, max_decay, GROUP_W, dtype=F32)).reshape(1, GROUP_W)
    lay = lambda a: pl.BlockSpec((1,) + a.shape[1:], lambda d, i: (d,) + (0,) * (a.ndim - 1))
    fix = lambda a: pl.BlockSpec(a.shape, lambda d, i: (0,) * a.ndim)
    b1, fr, b2 = (a.reshape(depth, 1, hid) for a in (b1, fr, b2))
    return pl.pallas_call(
        functools.partial(_hy_filter_kernel, l=l, tr=tr),
        grid=(depth, 2 * l // tr),
        in_specs=[fix(fl), lay(w1p), lay(b1), lay(fr), lay(w2), lay(b2), lay(w3), fix(deltas)],
        out_specs=pl.BlockSpec((1, tr, GROUP_W), lambda d, i: (d, i, 0)),
        out_shape=jax.ShapeDtypeStruct((depth, 2 * l, GROUP_W), BF16),
        compiler_params=_cp(("parallel", "parallel")),
        name="hy_filter",
    )(fl, w1p, b1, fr, w2, b2, w3, deltas)


def _rowmm_kernel(f_ref, x_ref, o_ref):
    o_ref[0] = _dot(f_ref[...], x_ref[0]).astype(o_ref.dtype)


def _rowmm(f, xv, tl=8192):
    b, k, n = xv.shape
    m = f.shape[0]
    tl = min(tl, n)
    return pl.pallas_call(
        _rowmm_kernel,
        grid=(b, n // tl),
        in_specs=[pl.BlockSpec((m, k), lambda bb, i: (0, 0)),
                  pl.BlockSpec((1, k, tl), lambda bb, i: (bb, 0, i))],
        out_specs=pl.BlockSpec((1, m, tl), lambda bb, i: (bb, 0, i)),
        out_shape=jax.ShapeDtypeStruct((b, m, n), BF16),
        compiler_params=_cp(("parallel", "parallel"), VMEM_LIMIT),
        name="rowmm",
    )(f, xv)


def _hy_spec_kernel(a_ref, mf_ref, k_ref):
    x = a_ref[0, :, 0].reshape(2 * FFT_N2, GROUP_W)
    k_ref[0, 0] = _dot(mf_ref[0], x)


def _hy_mid_kernel(a_ref, mf_ref, mi_ref, k_ref, o_ref):
    n2 = FFT_N2
    k = k_ref[0, 0]
    kr, ki = k[:n2], k[n2:]
    for bb in range(a_ref.shape[0]):
        x = a_ref[bb, :, 0].reshape(2 * n2, GROUP_W)
        y = _dot(mf_ref[0], x)
        yr, yi = y[:n2], y[n2:]
        z = jnp.concatenate([yr * kr - yi * ki, yr * ki + yi * kr], axis=0).astype(BF16)
        o_ref[bb, :, 0] = _dot(mi_ref[0], z).astype(BF16).reshape(2, n2, GROUP_W)


def _hy_spectrum(a, mf):
    depth, _, n1, n2, w = a.shape
    return pl.pallas_call(
        _hy_spec_kernel,
        grid=(n1, depth),
        in_specs=[pl.BlockSpec((1, 2, 1, n2, w), lambda k1, d: (d, 0, k1, 0, 0)),
                  pl.BlockSpec((1, 2 * n2, 2 * n2), lambda k1, d: (k1, 0, 0))],
        out_specs=pl.BlockSpec((1, 1, 2 * n2, w), lambda k1, d: (d, k1, 0, 0)),
        out_shape=jax.ShapeDtypeStruct((depth, n1, 2 * n2, w), F32),
        compiler_params=_cp(("parallel", "parallel")),
        name="hy_spectrum",
    )(a, mf)


def _hy_mid(a, mf, mi, kspec, layer):
    b, _, n1, n2, w = a.shape
    blk = pl.BlockSpec((b, 2, 1, n2, w), lambda k1: (0, 0, k1, 0, 0))
    mat = pl.BlockSpec((1, 2 * n2, 2 * n2), lambda k1: (k1, 0, 0))
    return pl.pallas_call(
        _hy_mid_kernel,
        grid=(n1,),
        in_specs=[blk, mat, mat,
                  pl.BlockSpec((1, 1, 2 * n2, w), lambda k1: (layer, k1, 0, 0))],
        out_specs=blk,
        out_shape=jax.ShapeDtypeStruct(a.shape, BF16),
        compiler_params=_cp(("parallel",)),
        name="hy_mid",
    )(a, mf, mi, kspec)


def _hy_out_kernel(g_ref, b_ref, vx_ref, x0_ref, skip_ref, o_ref):
    y = _dot(g_ref[...], b_ref[0])
    vx = vx_ref[0].astype(F32)
    o_ref[0] = ((y + skip_ref[...] * vx) * x0_ref[0].astype(F32)).astype(BF16)


def _hy_out(g, bo, vx, x0, skip_t, tl=8192):
    b, k, n = bo.shape
    m = g.shape[0]
    tl = min(tl, n)
    sig = pl.BlockSpec((1, m, tl), lambda bb, i: (bb, 0, i))
    return pl.pallas_call(
        _hy_out_kernel,
        grid=(b, n // tl),
        in_specs=[pl.BlockSpec((m, k), lambda bb, i: (0, 0)),
                  pl.BlockSpec((1, k, tl), lambda bb, i: (bb, 0, i)),
                  sig, sig, pl.BlockSpec((1, tl), lambda bb, i: (0, 0))],
        out_specs=sig,
        out_shape=jax.ShapeDtypeStruct((b, m, n), BF16),
        compiler_params=_cp(("parallel", "parallel"), VMEM_LIMIT),
        name="hy_out",
    )(g, bo, vx, x0, skip_t)


def _hy_small_kernel(vx_ref, x0_ref, k_ref, fd_ref, ff_ref, gd_ref, skip_ref, o_ref):
    nf = fd_ref.shape[0] // 2
    vx = vx_ref[0]
    u = _dot(fd_ref[...], vx)
    k = _dot(ff_ref[...], k_ref[0])
    ur, ui, kr, ki = u[:nf], u[nf:], k[:nf], k[nf:]
    z = jnp.concatenate([ur * kr - ui * ki, ur * ki + ui * kr], axis=0).astype(BF16)
    y = _dot(gd_ref[...], z)
    o_ref[0] = ((y + skip_ref[...] * vx.astype(F32)) * x0_ref[0].astype(F32)).astype(BF16)


def _hy_small(vx, x0, kfilt, layer, fd, ff, gd, skip):
    b, l, w = vx.shape
    sig = pl.BlockSpec((1, l, w), lambda bb: (bb, 0, 0))
    full = lambda a: pl.BlockSpec(a.shape, lambda bb: (0,) * a.ndim)
    return pl.pallas_call(
        _hy_small_kernel,
        grid=(b,),
        in_specs=[sig, sig, pl.BlockSpec((1, 2 * l, w), lambda bb: (layer, 0, 0)),
                  full(fd), full(ff), full(gd), full(skip)],
        out_specs=sig,
        out_shape=jax.ShapeDtypeStruct((b, l, w), BF16),
        compiler_params=_cp(("parallel",)),
        name="hy_small",
    )(vx, x0, kfilt, fd, ff, gd, skip)


def _fn_first_kernel(z_ref, m_ref, o_ref):
    n2 = z_ref.shape[1]
    x = z_ref[0].reshape(n2 * 8, GROUP_W)
    o_ref[0] = _dot(m_ref[0], x).astype(BF16).reshape(2, 8, n2, GROUP_W)


def _fn_first(zf, mats):
    b, n2, n1, w = zf.shape
    return pl.pallas_call(
        _fn_first_kernel,
        grid=(n1 // 8, b),
        in_specs=[pl.BlockSpec((1, n2, 8, w), lambda i, bb: (bb, 0, i, 0)),
                  pl.BlockSpec((1, 16 * n2, 8 * n2), lambda i, bb: (i, 0, 0))],
        out_specs=pl.BlockSpec((1, 2, 8, n2, w), lambda i, bb: (bb, 0, i, 0, 0)),
        out_shape=jax.ShapeDtypeStruct((b, 2, n1, n2, w), BF16),
        compiler_params=_cp(("parallel", "parallel"), VMEM_LIMIT),
        name="fn_first",
    )(zf, mats)


def _fn_last_kernel(ar_ref, ai_ref, c_ref, s_ref, w_ref, b_ref, o_ref, *, scale):
    t = (_dot(ar_ref[0, 0], c_ref[...]) + _dot(ai_ref[0, 0], s_ref[...])) * scale
    o_ref[0] = (_dot(t.astype(BF16), w_ref[...]) + b_ref[...]).astype(BF16)


def _fn_last(a, cbd, sbd, w, bias, scale):
    b, _, l, c = a.shape
    tm = min(l, 1024)
    full = lambda z: pl.BlockSpec(z.shape, lambda bb, i: (0,) * z.ndim)
    return pl.pallas_call(
        functools.partial(_fn_last_kernel, scale=scale),
        grid=(b, l // tm),
        in_specs=[pl.BlockSpec((1, 1, tm, c), lambda bb, i: (bb, 0, i, 0)),
                  pl.BlockSpec((1, 1, tm, c), lambda bb, i: (bb, 1, i, 0)),
                  full(cbd), full(sbd), full(w), full(bias)],
        out_specs=pl.BlockSpec((1, tm, c), lambda bb, i: (bb, i, 0)),
        out_shape=jax.ShapeDtypeStruct((b, l, c), BF16),
        compiler_params=_cp(("parallel", "parallel")),
        name="fn_last",
    )(a, a, cbd, sbd, w, bias)


def _fn_small_kernel(u_ref, cl_ref, sl_ref, c_ref, s_ref, w_ref, b_ref, o_ref, *, scale):
    u = u_ref[0]
    uc = _dot(u, c_ref[...]).astype(BF16)
    us = _dot(u, s_ref[...]).astype(BF16)
    f = (_dot(cl_ref[...], uc) - _dot(sl_ref[...], us)) * scale
    o_ref[0] = (_dot(f.astype(BF16), w_ref[...]) + b_ref[...]).astype(BF16)


def _fn_small(zf, cl, sl, cbd, sbd, w, bias, scale):
    b, l, c = zf.shape
    sig = pl.BlockSpec((1, l, c), lambda bb: (bb, 0, 0))
    full = lambda z: pl.BlockSpec(z.shape, lambda bb: (0,) * z.ndim)
    return pl.pallas_call(
        functools.partial(_fn_small_kernel, scale=scale),
        grid=(b,),
        in_specs=[sig, full(cl), full(sl), full(cbd), full(sbd), full(w), full(bias)],
        out_specs=sig,
        out_shape=jax.ShapeDtypeStruct((b, l, c), BF16),
        compiler_params=_cp(("parallel",)),
        name="fn_small",
    )(zf, cl, sl, cbd, sbd, w, bias)


def _merge_kernel(oa_ref, ob_ref, oc_ref, od_ref, x_ref, g1_ref, gn_ref, w_ref, b_ref,
                  lng_ref, lnb_ref, o_ref, *, alpha):
    parts = []
    for idx, ref in enumerate((oa_ref, ob_ref, oc_ref, od_ref)):
        v = ref[0].astype(F32)
        ms = jnp.mean(v * v, axis=-1, keepdims=True)
        gain = gn_ref[:, idx * GROUP_W:(idx + 1) * GROUP_W]
        parts.append((v * lax.rsqrt(ms + LN_EPS) * gain).astype(BF16))
    y = _dot(jnp.concatenate(parts, axis=1), w_ref[...]) + b_ref[...]
    r = alpha * x_ref[0] + g1_ref[0] * y
    o_ref[0] = _layer_norm(r) * lng_ref[...] + lnb_ref[...]


def _merge(oa, ob, oc, od, x, g1, gn, w_out, b_out, ln_g, ln_b, alpha):
    b, l, d = x.shape
    tm = min(l, 512)
    grp = pl.BlockSpec((1, tm, GROUP_W), lambda bb, i: (bb, i, 0))
    tok = pl.BlockSpec((1, tm, d), lambda bb, i: (bb, i, 0))
    full = lambda a: pl.BlockSpec(a.shape, lambda bb, i: (0,) * a.ndim)
    return pl.pallas_call(
        functools.partial(_merge_kernel, alpha=alpha),
        grid=(b, l // tm),
        in_specs=[grp, grp, grp, grp, tok, pl.BlockSpec((1, 1, d), lambda bb, i: (bb, 0, 0)),
                  full(gn), full(w_out), full(b_out), full(ln_g), full(ln_b)],
        out_specs=tok,
        out_shape=jax.ShapeDtypeStruct((b, l, d), F32),
        compiler_params=_cp(("parallel", "parallel"), VMEM_LIMIT),
        name="merge",
    )(oa, ob, oc, od, x, g1, gn, w_out, b_out, ln_g, ln_b)


def _ffn_kernel(x_ref, xp_ref, xn_ref, sh_ref, sc_ref, g_ref, wup_ref, vec_ref, wd_ref, bd_ref,
                lng_ref, lnb_ref, o_ref, h_ref, acc_ref, u0_ref, u1_ref, *, alpha, tm, n_chunks, n_tiles):
    i = pl.program_id(1)
    sh, sc = sh_ref[0], sc_ref[0]
    cw = FFN_CHUNK

    def hmod(v):
        return (_layer_norm(v) * (1.0 + sc) + sh).astype(BF16)

    h_ref[0:HALO] = hmod(xp_ref[0])
    h_ref[HALO:HALO + tm] = hmod(x_ref[0])
    h_ref[HALO + tm:] = hmod(xn_ref[0])
    acc_ref[...] = jnp.zeros_like(acc_ref)
    row8 = lax.broadcasted_iota(jnp.int32, (8, 1), 0)
    pad_top = (row8 == 7) & (i == 0)
    pad_bot = (row8 == 0) & (i == n_tiles - 1)

    def up(c, dst):
        dst[...] = _dot(h_ref[...], wup_ref[c])

    def down(c, src):
        vec = vec_ref[c]
        bias, w0, w1, w2 = vec[0:1], vec[1:2], vec[2:3], vec[3:4]
        src[HALO - 8:HALO] = jnp.where(pad_top, -bias, src[HALO - 8:HALO])
        src[HALO + tm:HALO + tm + 8] = jnp.where(pad_bot, -bias, src[HALO + tm:HALO + tm + 8])
        conv = (src[HALO - 1:HALO - 1 + tm] * w0 + src[HALO:HALO + tm] * w1 + src[HALO + 1:HALO + 1 + tm] * w2
                + (vec[4:5] + bias * (w0 + w1 + w2)))
        a, g = conv[:, :cw], conv[:, cw:]
        act = (a * jax.nn.sigmoid(a) * g).astype(BF16)
        acc_ref[...] += _dot(act, wd_ref[c])

    up(0, u0_ref)

    def body(p, carry):
        c = 2 * p
        up(c + 1, u1_ref)
        down(c, u0_ref)
        up(c + 2, u0_ref)
        down(c + 1, u1_ref)
        return carry

    n_pairs = (n_chunks - 1) // 2
    lax.fori_loop(0, n_pairs, body, 0)
    c = 2 * n_pairs
    if c + 1 < n_chunks:
        up(c + 1, u1_ref)
    down(c, u0_ref)
    if c + 1 < n_chunks:
        down(c + 1, u1_ref)
    r = alpha * x_ref[0] + g_ref[0] * (acc_ref[...] + bd_ref[...])
    o_ref[0] = _layer_norm(r) * lng_ref[...] + lnb_ref[...]


def _ffn(x, sh, sc, g2, wup, vec, wd, b_down, ln_g, ln_b, alpha):
    b, l, d = x.shape
    tm = min(l, 512)
    n_chunks = wd.shape[0]
    main, prev, nxt = _halo_specs(tm, l, d)
    mod = pl.BlockSpec((1, 1, d), lambda bb, i: (bb, 0, 0))
    full = lambda a: pl.BlockSpec(a.shape, lambda bb, i: (0,) * a.ndim)
    return pl.pallas_call(
        functools.partial(_ffn_kernel, alpha=alpha, tm=tm, n_chunks=n_chunks, n_tiles=l // tm),
        grid=(b, l // tm),
        in_specs=[main, prev, nxt, mod, mod, mod, full(wup), full(vec), full(wd),
                  full(b_down), full(ln_g), full(ln_b)],
        out_specs=main,
        out_shape=jax.ShapeDtypeStruct((b, l, d), F32),
        scratch_shapes=[pltpu.VMEM((tm + 2 * HALO, d), BF16), pltpu.VMEM((tm, d), F32),
                        pltpu.VMEM((tm + 2 * HALO, 2 * FFN_CHUNK), F32),
                        pltpu.VMEM((tm + 2 * HALO, 2 * FFN_CHUNK), F32)],
        compiler_params=_cp(("parallel", "parallel"), VMEM_LIMIT),
        name="ffn",
    )(x, x, x, sh, sc, g2, wup, vec, wd, b_down, ln_g, ln_b)


def _cos_sin(idx, n):
    ang = (idx % n).astype(F32) * (2.0 * math.pi / n)
    return jnp.cos(ang), jnp.sin(ang)


def _cplx_rows(re, im):
    return jnp.concatenate([re, im], axis=-2)


def _cplx_block(re, im):
    return jnp.concatenate([jnp.concatenate([re, -im], axis=-1), jnp.concatenate([im, re], axis=-1)], axis=-2)


def _ar(n):
    return jnp.arange(n, dtype=jnp.int32)


def _hyena_tables(l):
    n, n2 = 2 * l, FFT_N2
    n1 = n // n2
    c, s = _cos_sin(_ar(n1)[:, None] * _ar(n1)[None, :], n1)
    f_full = _cplx_rows(c, -s).astype(BF16)
    g = (jnp.concatenate([c, -s], axis=1)[:n1 // 2] / n).astype(BF16)
    k1, k2, m2 = _ar(n1)[:, None, None], _ar(n2)[None, :, None], _ar(n2)[None, None, :]
    c, s = _cos_sin(m2 * k1 + n1 * m2 * k2, n)
    mf = _cplx_block(c, -s).astype(BF16)
    return dict(f_half=f_full[:, :n1 // 2], f_full=f_full, g=g, mf=mf, mi=jnp.swapaxes(mf, 1, 2))


def _fnet_tables(l):
    n2 = FFT_N2
    n1 = l // n2
    nblk = n1 // 8
    blk, j = _ar(nblk)[:, None, None, None], _ar(8)[None, :, None, None]
    k2, m2 = _ar(n2)[None, None, :, None], _ar(n2)[None, None, None, :]
    c, s = _cos_sin(n1 * m2 * k2 + (8 * blk + j) * k2, l)
    eye = jnp.eye(8, dtype=F32)
    emb = lambda t: jnp.einsum('bjkn,ji->bjkni', t, eye).reshape(nblk, 8 * n2, n2 * 8)
    first = jnp.concatenate([emb(c), emb(-s)], axis=1).astype(BF16)
    c, s = _cos_sin(_ar(n1)[:, None] * _ar(n1)[None, :], n1)
    second = _cplx_block(c, -s).astype(BF16)
    return dict(first=first, second=second)


def _dense_tables(l):
    n = 2 * l
    c, s = _cos_sin(_ar(n)[:, None] * _ar(n)[None, :], n)
    ff = _cplx_rows(c, -s).astype(BF16)
    gd = (jnp.concatenate([c, -s], axis=1)[:l] / n).astype(BF16)
    cl, sl = _cos_sin(_ar(l)[:, None] * _ar(l)[None, :], l)
    return dict(fd=ff[:, :l], ff=ff, gd=gd, cl=cl.astype(BF16), sl=sl.astype(BF16))


def _rope_tables(l):
    rows = l // GRID_W
    row = jnp.broadcast_to(jnp.arange(rows)[:, None], (rows, GRID_W)).reshape(-1).astype(F32)
    col = jnp.broadcast_to(jnp.arange(GRID_W)[None, :], (rows, GRID_W)).reshape(-1).astype(F32)
    half = HEAD_DIM // 2
    inv = ROPE_THETA ** (-jnp.arange(0, half, 2, dtype=F32) / half)
    ar, ac = row[:, None] * inv, col[:, None] * inv
    cos = jnp.concatenate([jnp.cos(ar), jnp.cos(ar), jnp.cos(ac), jnp.cos(ac)], axis=1)
    sin = jnp.concatenate([-jnp.sin(ar), jnp.sin(ar), -jnp.sin(ac), jnp.sin(ac)], axis=1)
    return jnp.tile(cos, (1, 2)), jnp.tile(sin, (1, 2))


def kernel(x, c, ctx, c_ctx, w_ada, b_ada, w_in, sink_a, q_norm_g, k_norm_g, hy_conv_w, hy_conv_b,
           hy_f_w1, hy_f_b1, hy_f_freq, hy_f_w2, hy_f_b2, hy_f_w3, hy_skip, fnet_w, fnet_b,
           out_norm_g, w_out, b_out, ln1_g, ln1_b, ffn_w_up, ffn_b_up, ffn_conv_w, ffn_conv_b,
           ffn_w_down, ffn_b_down, ln2_g, ln2_b):
    bsz, l, d = x.shape
    lc = ctx.shape[1]
    depth = w_ada.shape[0]
    d_ff = ffn_w_down.shape[1]
    alpha = (2 * depth) ** 0.25
    n2 = FFT_N2
    n1h, n1f = 2 * l // n2, l // n2
    lanes = n2 * GROUP_W

    ht, ft, dt = _hyena_tables(l), _fnet_tables(l), _dense_tables(lc)
    cos_l, sin_l = _rope_tables(l)
    cos_c, sin_c = jnp.ones((lc, 128), F32), jnp.zeros((lc, 128), F32)
    jj = _ar(GROUP_W)
    same = (jj[:, None] // FNET_GROUP_DIM) == (jj[None, :] // FNET_GROUP_DIM)
    bd_mean = (same.astype(F32) / HEAD_DIM).astype(BF16)
    cg, sg = _cos_sin(jj[:, None] * jj[None, :], FNET_GROUP_DIM)
    cbd, sbd = jnp.where(same, cg, 0.0).astype(BF16), jnp.where(same, sg, 0.0).astype(BF16)

    n_rows = -(-(bsz + 1) // 8) * 8
    cc = jnp.zeros((n_rows, d), F32).at[:bsz].set(c).at[bsz].set(c_ctx)
    mod = _ada(cc, w_ada, b_ada)

    k_lat = _hy_filter(l, hy_f_w1, hy_f_b1, hy_f_freq, hy_f_w2, hy_f_b2, hy_f_w3)
    k_ctx = _hy_filter(lc, hy_f_w1, hy_f_b1, hy_f_freq, hy_f_w2, hy_f_b2, hy_f_w3)
    a_k = _rowmm(ht['f_full'], k_lat.reshape(depth, n1h, lanes))
    k_spec = _hy_spectrum(a_k.reshape(depth, 2, n1h, n2, GROUP_W), ht['mf'])

    n_chunks = -(-d_ff // FFN_CHUNK)
    pad = n_chunks * FFN_CHUNK - d_ff

    def chunked(v):
        v = v.reshape(v.shape[:-1] + (2, d_ff))
        v = jnp.pad(v, [(0, 0)] * (v.ndim - 1) + [(0, pad)])
        v = v.reshape(v.shape[:-2] + (2, n_chunks, FFN_CHUNK))
        v = jnp.swapaxes(v, -3, -2).reshape(v.shape[:-3] + (n_chunks, 2 * FFN_CHUNK))
        return jnp.moveaxis(v, -2, 0)

    for layer in range(depth):
        m6 = mod[layer].reshape(n_rows, 6, d)
        lat = [m6[:bsz, i][:, None, :] for i in range(6)]
        con = [jnp.broadcast_to(m6[bsz, i][None, None, :], (bsz, 1, d)) for i in range(6)]
        w_in_l = w_in[layer].astype(BF16)
        qg = jnp.tile(q_norm_g[layer], N_HEADS).reshape(1, -1)
        kg = jnp.tile(k_norm_g[layer], N_KV).reshape(1, -1)
        w_out_l = w_out[layer].astype(BF16)
        gn, b_out_l = out_norm_g[layer].reshape(1, d), b_out[layer].reshape(1, d)
        l1g, l1b = ln1_g[layer].reshape(1, d), ln1_b[layer].reshape(1, d)
        l2g, l2b = ln2_g[layer].reshape(1, d), ln2_b[layer].reshape(1, d)
        fw, fb = fnet_w[layer].astype(BF16), fnet_b[layer].reshape(1, GROUP_W)
        skip = hy_skip[layer].reshape(1, GROUP_W)
        wup = chunked(ffn_w_up[layer]).astype(BF16)
        vec = chunked(jnp.concatenate([ffn_b_up[layer][None], ffn_conv_w[layer], ffn_conv_b[layer][None],
                                       jnp.zeros((3, 2 * d_ff), F32)], axis=0))
        wd = jnp.pad(ffn_w_down[layer], ((0, pad), (0, 0))).reshape(n_chunks, FFN_CHUNK, d).astype(BF16)
        bdn = ffn_b_down[layer].reshape(1, d)
        last = layer == depth - 1

        cqa, cka, cva, cqb, ckb, cvb, czh, czf = _proj(ctx, con[0], con[1], w_in_l, cos_c, sin_c,
                                                       bd_mean, qg, kg)
        qa, ka, va, qb, kb, vb, zh, zf = _proj(x, lat[0], lat[1], w_in_l, cos_l, sin_l, bd_mean, qg, kg)
        cat = lambda a, b_: jnp.concatenate([a, b_], axis=2)
        oa = _attention(qa, cat(ka, cka), cat(va, cva), sink_a[layer], banded=True, n_lat_keys=l)
        ob = _attention(qb, cat(kb, ckb), cat(vb, cvb), None, banded=False)

        vx, x0 = _hy_gate(zh, hy_conv_w[layer], hy_conv_b[layer])
        a1 = _rowmm(ht['f_half'], vx.reshape(bsz, n1h // 2, lanes))
        bo = _hy_mid(a1.reshape(bsz, 2, n1h, n2, GROUP_W), ht['mf'], ht['mi'], k_spec, layer)
        oc = _hy_out(ht['g'], bo.reshape(bsz, 2 * n1h, lanes), vx.reshape(bsz, n1h // 2, lanes),
                     x0.reshape(bsz, n1h // 2, lanes), jnp.tile(skip, (1, n2))).reshape(bsz, l, GROUP_W)

        f1 = _fn_first(zf.reshape(bsz, n2, n1f, GROUP_W), ft['first'])
        f2 = _rowmm(ft['second'], f1.reshape(bsz, 2 * n1f, lanes))
        od = _fn_last(f2.reshape(bsz, 2, l, GROUP_W), cbd, sbd, fw, fb, (l * FNET_GROUP_DIM) ** -0.5)

        x = _merge(oa, ob, oc, od, x, lat[2], gn, w_out_l, b_out_l, l1g, l1b, alpha)
        x = _ffn(x, lat[3], lat[4], lat[5], wup, vec, wd, bdn, l2g, l2b, alpha)

        if not last:
            oac = _attention(cqa, cka, cva, sink_a[layer], banded=False)
            obc = _attention(cqb, ckb, cvb, None, banded=False)
            cvx, cx0 = _hy_gate(czh, hy_conv_w[layer], hy_conv_b[layer])
            occ = _hy_small(cvx, cx0, k_ctx, layer, dt['fd'], dt['ff'], dt['gd'], skip)
            odc = _fn_small(czf, dt['cl'], dt['sl'], cbd, sbd, fw, fb, (lc * FNET_GROUP_DIM) ** -0.5)
            ctx = _merge(oac, obc, occ, odc, ctx, con[2], gn, w_out_l, b_out_l, l1g, l1b, alpha)
            ctx = _ffn(ctx, con[3], con[4], con[5], wup, vec, wd, bdn, l2g, l2b, alpha)
    return x
```

```python
import functools
import math

import jax
import jax.numpy as jnp
from jax import lax
from jax.experimental import pallas as pl
from jax.experimental.pallas import tpu as pltpu

F32 = jnp.float32
BF16 = jnp.bfloat16

HEAD_DIM = 64
GROUP_W = 256
N_HEADS = 4
N_KV = 2
WINDOW = 128
GRID_W = 64
ROPE_THETA = 10000.0
FNET_GROUP_DIM = 64
HYENA_BANDS = 8
HYENA_FAST_DECAY = 0.3
HYENA_SLOW_DECAY = 1.5
HYENA_TARGET = 1e-2
LN_EPS = 1e-6
NEG_INF = -1e30
FFT_N2 = 128
FFN_CHUNK = 256
HALO = 16
VMEM_LIMIT = 56 * 1024 * 1024


def _cp(sem, vmem=None):
    return pltpu.CompilerParams(dimension_semantics=sem, vmem_limit_bytes=vmem)


def _layer_norm(x):
    mu = jnp.mean(x, axis=-1, keepdims=True)
    xc = x - mu
    var = jnp.mean(xc * xc, axis=-1, keepdims=True)
    return xc * lax.rsqrt(var + LN_EPS)


def _dot(a, b):
    return jnp.dot(a, b, preferred_element_type=F32)


def _ada_kernel(c_ref, w_ref, b_ref, o_ref):
    c = c_ref[...]
    s = (c * jax.nn.sigmoid(c)).astype(BF16)
    o_ref[0] = _dot(s, w_ref[0].astype(BF16)) + b_ref[0]


def _ada(cc, w_ada, b_ada):
    depth, d, n = w_ada.shape
    r = cc.shape[0]
    tn = 1536
    return pl.pallas_call(
        _ada_kernel,
        grid=(depth, n // tn),
        in_specs=[pl.BlockSpec((r, d), lambda l, j: (0, 0)),
                  pl.BlockSpec((1, d, tn), lambda l, j: (l, 0, j)),
                  pl.BlockSpec((1, 1, tn), lambda l, j: (l, 0, j))],
        out_specs=pl.BlockSpec((1, r, tn), lambda l, j: (l, 0, j)),
        out_shape=jax.ShapeDtypeStruct((depth, r, n), F32),
        compiler_params=_cp(("parallel", "parallel"), VMEM_LIMIT),
        name="ada",
    )(cc, w_ada, b_ada.reshape(depth, 1, n))


def _swap_halves(z):
    w = z.shape[1]
    lane = lax.broadcasted_iota(jnp.int32, z.shape, 1)
    return jnp.where(lane % 32 < 16, pltpu.roll(z, w - 16, 1), pltpu.roll(z, 16, 1))


def _proj_kernel(x_ref, sh_ref, sc_ref, w_ref, cos_ref, sin_ref, bd_ref, qg_ref, kg_ref,
                 qa_ref, ka_ref, va_ref, qb_ref, kb_ref, vb_ref, zh_ref, zf_ref):
    x = x_ref[0]
    h = (_layer_norm(x) * (1.0 + sc_ref[0]) + sh_ref[0]).astype(BF16)
    cos1, sin1 = cos_ref[...], sin_ref[...]
    cos2 = jnp.concatenate([cos1, cos1], axis=1)
    sin2 = jnp.concatenate([sin1, sin1], axis=1)

    def proj(lo, width):
        return _dot(h, w_ref[:, lo:lo + width])

    def rope(z):
        if z.shape[1] == 128:
            return z * cos1 + _swap_halves(z) * sin1
        return z * cos2 + _swap_halves(z) * sin2

    def head_norm(z, g):
        w = z.shape[1]
        ms = _dot((z * z).astype(BF16), bd_ref[:w, :w])
        return z * lax.rsqrt(ms + LN_EPS) * g

    def put_heads(ref, z):
        for hh in range(z.shape[1] // HEAD_DIM):
            ref[0, hh] = z[:, hh * HEAD_DIM:(hh + 1) * HEAD_DIM].astype(BF16)

    scale = HEAD_DIM ** -0.5
    put_heads(qa_ref, rope(proj(0, 256)) * scale)
    put_heads(ka_ref, rope(proj(256, 128)))
    put_heads(va_ref, proj(384, 128))
    put_heads(qb_ref, rope(head_norm(proj(512, 256), qg_ref[...])) * scale)
    put_heads(kb_ref, rope(head_norm(proj(768, 128), kg_ref[...])))
    put_heads(vb_ref, proj(896, 128))
    zh_ref[0] = proj(1024, 768).astype(BF16)
    zf_ref[0] = proj(1792, 256).astype(BF16)


def _proj(x, sh, sc, w_in, cos, sin, bd, qg, kg):
    b, l, d = x.shape
    tm = min(l, 512)
    heads = lambda n: jax.ShapeDtypeStruct((b, n, l, HEAD_DIM), BF16)
    hspec = lambda n: pl.BlockSpec((1, n, tm, HEAD_DIM), lambda t, bb: (bb, 0, t, 0))
    tok = lambda w: pl.BlockSpec((1, tm, w), lambda t, bb: (bb, t, 0))
    vec = lambda w: pl.BlockSpec((1, 1, w), lambda t, bb: (bb, 0, 0))
    full = lambda a: pl.BlockSpec(a.shape, lambda t, bb: (0,) * a.ndim)
    tab = pl.BlockSpec((tm, 128), lambda t, bb: (t, 0))
    return pl.pallas_call(
        _proj_kernel,
        grid=(l // tm, b),
        in_specs=[tok(d), vec(d), vec(d), full(w_in), tab, tab, full(bd), full(qg), full(kg)],
        out_specs=[hspec(4), hspec(2), hspec(2), hspec(4), hspec(2), hspec(2), tok(768), tok(256)],
        out_shape=[heads(4), heads(2), heads(2), heads(4), heads(2), heads(2),
                   jax.ShapeDtypeStruct((b, l, 768), BF16), jax.ShapeDtypeStruct((b, l, 256), BF16)],
        compiler_params=_cp(("parallel", "parallel"), VMEM_LIMIT),
        name="proj",
    )(x, sh, sc, w_in, cos, sin, bd, qg, kg)


_NT = (((1,), (1,)), ((), ()))


def _sink_rows(sink_ref, j, tq):
    row = lax.broadcasted_iota(jnp.int32, (2 * tq, 1), 0)
    return jnp.where(row < tq, sink_ref[2 * j], sink_ref[2 * j + 1])


def _attn_finish(o_ref, acc, denom, tq):
    o = acc[:, :HEAD_DIM] / denom
    o_ref[0] = jnp.concatenate([o[:tq], o[tq:]], axis=1).astype(BF16)


def _attn_full_kernel(sink_ref, q_ref, k_ref, v_ref, o_ref, m_ref, acc_ref, s0_ref, s1_ref, *,
                      has_sink, tq, tk, n_t, n_chunks):
    j = pl.program_id(1)
    lane = lax.broadcasted_iota(jnp.int32, acc_ref.shape[1:], 1)
    for tile in range(n_t):
        if has_sink:
            m_ref[tile] = _sink_rows(sink_ref, j, tq)
            acc_ref[tile] = jnp.where(lane == HEAD_DIM, 1.0, 0.0)
        else:
            m_ref[tile] = jnp.full(m_ref.shape[1:], NEG_INF, F32)
            acc_ref[tile] = jnp.zeros(acc_ref.shape[1:], F32)

    def chunk(ref, t):
        c = t // n_t
        off = c * tk if isinstance(c, int) else pl.multiple_of(c * tk, tk)
        return ref[0, 0, pl.ds(off, tk), :]

    def scores(t, tile, dst):
        q = q_ref[0, :, tile * tq:(tile + 1) * tq, :].reshape(2 * tq, HEAD_DIM)
        dst[...] = lax.dot_general(q, chunk(k_ref, t), _NT, preferred_element_type=F32)

    def update(t, tile, src):
        s = src[...]
        m_prev = m_ref[tile]
        m_new = jnp.maximum(m_prev, jnp.max(s, axis=-1, keepdims=True))
        p = jnp.exp((s - m_new).astype(BF16))
        acc_ref[tile] = jnp.exp(m_prev - m_new) * acc_ref[tile] + _dot(p, chunk(v_ref, t))
        m_ref[tile] = m_new

    total = n_t * n_chunks
    odd_tile = 1 % n_t
    scores(0, 0, s0_ref)

    def body(i, carry):
        t = 2 * i
        scores(t + 1, odd_tile, s1_ref)
        update(t, 0, s0_ref)
        scores(t + 2, 0, s0_ref)
        update(t + 1, odd_tile, s1_ref)
        return carry

    n_pairs = (total - 1) // 2
    lax.fori_loop(0, n_pairs, body, 0, unroll=2)
    t = 2 * n_pairs
    if t + 1 < total:
        scores(t + 1, odd_tile, s1_ref)
    update(t, 0, s0_ref)
    if t + 1 < total:
        update(t + 1, odd_tile, s1_ref)
    for tile in range(n_t):
        acc = acc_ref[tile]
        o = acc[:, :HEAD_DIM] * (1.0 / acc[:, HEAD_DIM:HEAD_DIM + 1])
        o_ref[0, tile * tq:(tile + 1) * tq, :] = jnp.concatenate([o[:tq], o[tq:]], axis=1).astype(BF16)


def _band_bias(tq):
    span = tq + 2 * WINDOW
    r = (_ar(2 * tq) % tq)[None, :, None]
    col = _ar(span)[None, None, :]
    shift = (_ar(3) * WINDOW)[:, None, None]
    return jnp.where(jnp.abs(col - shift - r) <= WINDOW, 0.0, NEG_INF).astype(F32)


def _attn_band_kernel(sink_ref, q_ref, k_ref, v_ref, bias_ref, o_ref, *, tq, n_t, n_tiles, l_lat, l_ctx):
    j, g = pl.program_id(1), pl.program_id(2)
    span = tq + 2 * WINDOW
    sink = _sink_rows(sink_ref, j, tq)
    kc, vc = k_ref[0, 0, pl.ds(l_lat, l_ctx), :], v_ref[0, 0, pl.ds(l_lat, l_ctx), :]
    for tile in range(n_t):
        qi = g * n_t + tile
        start = pl.multiple_of(jnp.clip(qi * tq - WINDOW, 0, l_lat - span), WINDOW)
        kind = jnp.where(qi == 0, 0, jnp.where(qi == n_tiles - 1, 2, 1))
        q = q_ref[0, :, tile * tq:(tile + 1) * tq, :].reshape(2 * tq, HEAD_DIM)
        s_lat = (lax.dot_general(q, k_ref[0, 0, pl.ds(start, span), :], _NT, preferred_element_type=F32)
                 + bias_ref[kind])
        s_ctx = lax.dot_general(q, kc, _NT, preferred_element_type=F32)
        m = jnp.maximum(sink, jnp.maximum(jnp.max(s_lat, axis=-1, keepdims=True),
                                          jnp.max(s_ctx, axis=-1, keepdims=True)))
        acc = (_dot(jnp.exp((s_lat - m).astype(BF16)), v_ref[0, 0, pl.ds(start, span), :])
               + _dot(jnp.exp((s_ctx - m).astype(BF16)), vc))
        o = acc[:, :HEAD_DIM] * (1.0 / (acc[:, HEAD_DIM:HEAD_DIM + 1] + jnp.exp(sink - m)))
        o_ref[0, tile * tq:(tile + 1) * tq, :] = jnp.concatenate([o[:tq], o[tq:]], axis=1).astype(BF16)


def _attention(q, k, v, sink, *, banded, n_lat_keys=0):
    b, _, lq, _ = q.shape
    lk = k.shape[2]
    tq = 256
    has_sink = sink is not None
    if not has_sink:
        sink = jnp.zeros((N_HEADS,), F32)
    v = jnp.concatenate([v, jnp.ones_like(v[..., :1]), jnp.zeros_like(v[..., 1:])], axis=-1)
    smem = pl.BlockSpec(memory_space=pltpu.SMEM)
    out_shape = jax.ShapeDtypeStruct((b, lq, GROUP_W), BF16)
    if banded:
        assert lq == n_lat_keys and n_lat_keys >= tq + 2 * WINDOW and tq % WINDOW == 0
        n_tiles = lq // tq
        n_t = next(t for t in (4, 2, 1) if n_tiles % t == 0)
        bias = _band_bias(tq)
        return pl.pallas_call(
            functools.partial(_attn_band_kernel, tq=tq, n_t=n_t, n_tiles=n_tiles, l_lat=n_lat_keys,
                              l_ctx=lk - n_lat_keys),
            grid=(b, N_KV, n_tiles // n_t),
            in_specs=[smem,
                      pl.BlockSpec((1, 2, n_t * tq, HEAD_DIM), lambda bb, j, g: (bb, j, g, 0)),
                      pl.BlockSpec((1, 1, lk, HEAD_DIM), lambda bb, j, g: (bb, j, 0, 0)),
                      pl.BlockSpec((1, 1, lk, 2 * HEAD_DIM), lambda bb, j, g: (bb, j, 0, 0)),
                      pl.BlockSpec(bias.shape, lambda bb, j, g: (0, 0, 0))],
            out_specs=pl.BlockSpec((1, n_t * tq, 128), lambda bb, j, g: (bb, g, j)),
            out_shape=out_shape,
            compiler_params=_cp(("parallel", "parallel", "parallel"), VMEM_LIMIT),
            name="attn_band",
        )(sink, q, k, v, bias)
    tk = next(t for t in (768, 512, 256) if lk % t == 0)
    n_t = 2 if lq % (2 * tq) == 0 else 1
    kern = functools.partial(_attn_full_kernel, has_sink=has_sink, tq=tq, tk=tk, n_t=n_t, n_chunks=lk // tk)
    return pl.pallas_call(
        kern,
        grid=(b, N_KV, lq // (n_t * tq)),
        in_specs=[smem,
                  pl.BlockSpec((1, 2, n_t * tq, HEAD_DIM), lambda bb, j, qi: (bb, j, qi, 0)),
                  pl.BlockSpec((1, 1, lk, HEAD_DIM), lambda bb, j, qi: (bb, j, 0, 0)),
                  pl.BlockSpec((1, 1, lk, 2 * HEAD_DIM), lambda bb, j, qi: (bb, j, 0, 0))],
        out_specs=pl.BlockSpec((1, n_t * tq, 128), lambda bb, j, qi: (bb, qi, j)),
        out_shape=out_shape,
        scratch_shapes=[pltpu.VMEM((n_t, 2 * tq, 1), F32), pltpu.VMEM((n_t, 2 * tq, 2 * HEAD_DIM), F32),
                        pltpu.VMEM((2 * tq, tk), F32), pltpu.VMEM((2 * tq, tk), F32)],
        compiler_params=_cp(("parallel", "parallel", "parallel"), VMEM_LIMIT),
        name="attn_full",
    )(sink, q, k, v)


def _shift_rows(z, prev_row, next_row):
    n = z.shape[0]
    row = lax.broadcasted_iota(jnp.int32, (n, 1), 0)
    dn = jnp.where(row == 0, prev_row, pltpu.roll(z, 1, 0))
    up = jnp.where(row == n - 1, next_row, pltpu.roll(z, n - 1, 0))
    return dn, up


def _hy_gate_kernel(z_ref, zp_ref, zn_ref, w_ref, b_ref, vx_ref, x0_ref, *, n_tiles):
    i = pl.program_id(1)
    z = z_ref[0].astype(F32)
    prev_row = zp_ref[0].astype(F32)[HALO - 1:HALO] * (i > 0).astype(F32)
    next_row = zn_ref[0].astype(F32)[0:1] * (i < n_tiles - 1).astype(F32)
    dn, up = _shift_rows(z, prev_row, next_row)
    w = w_ref[...]
    u = dn * w[0:1] + z * w[1:2] + up * w[2:3] + b_ref[...]
    x0_ref[0] = u[:, :GROUP_W].astype(BF16)
    vx_ref[0] = (u[:, 2 * GROUP_W:] * u[:, GROUP_W:2 * GROUP_W]).astype(BF16)


def _halo_specs(tl, l, w):
    nb = l // HALO
    per = tl // HALO
    return (pl.BlockSpec((1, tl, w), lambda bb, i: (bb, i, 0)),
            pl.BlockSpec((1, HALO, w), lambda bb, i: (bb, jnp.maximum(i * per - 1, 0), 0)),
            pl.BlockSpec((1, HALO, w), lambda bb, i: (bb, jnp.minimum((i + 1) * per, nb - 1), 0)))


def _hy_gate(zh, conv_w, conv_b):
    b, l, w = zh.shape
    tl = min(l, 512)
    out = jax.ShapeDtypeStruct((b, l, GROUP_W), BF16)
    ospec = pl.BlockSpec((1, tl, GROUP_W), lambda bb, i: (bb, i, 0))
    return pl.pallas_call(
        functools.partial(_hy_gate_kernel, n_tiles=l // tl),
        grid=(b, l // tl),
        in_specs=[*_halo_specs(tl, l, w),
                  pl.BlockSpec((3, w), lambda bb, i: (0, 0)),
                  pl.BlockSpec((1, w), lambda bb, i: (0, 0))],
        out_specs=[ospec, ospec],
        out_shape=[out, out],
        compiler_params=_cp(("parallel", "parallel")),
        name="hy_gate",
    )(zh, zh, zh, conv_w, conv_b.reshape(1, w))


def _hy_filter_kernel(fl_ref, w1_ref, b1_ref, fr_ref, w2_ref, b2_ref, w3_ref, dl_ref, k_ref, *, l, tr):
    m = pl.program_id(1) * tr + lax.broadcasted_iota(jnp.int32, (tr, 1), 0)
    t = jnp.where(m < l, m, 2 * l - m).astype(F32)
    tn = t / max(l - 1, 1)
    lane = lax.broadcasted_iota(jnp.int32, (1, 128), 1)
    phase = t * fl_ref[...]
    feat = jnp.where(lane == 0, tn,
                     jnp.where(lane <= HYENA_BANDS, jnp.cos(phase),
                               jnp.where(lane <= 2 * HYENA_BANDS, -jnp.sin(phase), 0.0)))
    hp = functools.partial(jnp.dot, precision=lax.Precision.HIGHEST, preferred_element_type=F32)
    fr = fr_ref[0]
    h = jnp.sin(fr * (hp(feat, w1_ref[0]) + b1_ref[0]))
    h = jnp.sin(fr * (hp(h, w2_ref[0]) + b2_ref[0]))
    h = hp(h, w3_ref[0])
    dec = jnp.exp(-tn * dl_ref[...])
    hf, hb = h[:, :GROUP_W] * dec, h[:, GROUP_W:] * dec
    k = jnp.where(m < l, hf, hb)
    k = jnp.where(m == 0, hf + hb, k)
    k_ref[0] = jnp.where(m == l, 0.0, k).astype(BF16)


def _hy_filter(l, w1, b1, fr, w2, b2, w3):
    depth, emb, hid = w1.shape
    tr = min(2 * l, 1024)
    bands = jnp.linspace(1e-4, HYENA_BANDS - 1, HYENA_BANDS, dtype=F32) * (2.0 * math.pi / l)
    fl = jnp.zeros((1, 128), F32).at[0, 1:1 + HYENA_BANDS].set(bands).at[0, 1 + HYENA_BANDS:emb].set(bands)
    w1p = jnp.zeros((depth, 128, hid), F32).at[:, :emb].set(w1)
    min_decay = math.log(HYENA_TARGET) / HYENA_SLOW_DECAY
    max_decay = math.log(HYENA_TARGET) / HYENA_FAST_DECAY
    deltas = jnp.abs(jnp.linspace(min_decay, max_decay, GROUP_W, dtype=F32)).reshape(1, GROUP_W)
    lay = lambda a: pl.BlockSpec((1,) + a.shape[1:], lambda d, i: (d,) + (0,) * (a.ndim - 1))
    fix = lambda a: pl.BlockSpec(a.shape, lambda d, i: (0,) * a.ndim)
    b1, fr, b2 = (a.reshape(depth, 1, hid) for a in (b1, fr, b2))
    return pl.pallas_call(
        functools.partial(_hy_filter_kernel, l=l, tr=tr),
        grid=(depth, 2 * l // tr),
        in_specs=[fix(fl), lay(w1p), lay(b1), lay(fr), lay(w2), lay(b2), lay(w3), fix(deltas)],
        out_specs=pl.BlockSpec((1, tr, GROUP_W), lambda d, i: (d, i, 0)),
        out_shape=jax.ShapeDtypeStruct((depth, 2 * l, GROUP_W), BF16),
        compiler_params=_cp(("parallel", "parallel")),
        name="hy_filter",
    )(fl, w1p, b1, fr, w2, b2, w3, deltas)


ROW_TILE = 8


def _lead_kron(f):
    return jnp.kron(f.astype(F32), jnp.eye(ROW_TILE, dtype=F32)).astype(BF16)


def _lead_mm_kernel(f_ref, x_ref, o_ref):
    k, j, c = x_ref.shape[1:]
    y = _dot(f_ref[...], x_ref[0].reshape(k * j, c))
    o_ref[0] = y.reshape(-1, j, c).astype(o_ref.dtype)


def _lead_mm(fk, x):
    b, k, n, c = x.shape
    m = fk.shape[0] // ROW_TILE
    return pl.pallas_call(
        _lead_mm_kernel,
        grid=(b, n // ROW_TILE),
        in_specs=[pl.BlockSpec(fk.shape, lambda bb, i: (0, 0)),
                  pl.BlockSpec((1, k, ROW_TILE, c), lambda bb, i: (bb, 0, i, 0))],
        out_specs=pl.BlockSpec((1, m, ROW_TILE, c), lambda bb, i: (bb, 0, i, 0)),
        out_shape=jax.ShapeDtypeStruct((b, m, n, c), BF16),
        compiler_params=_cp(("parallel", "parallel"), VMEM_LIMIT),
        name="lead_mm",
    )(fk, x)


def _hy_spec_kernel(a_ref, mf_ref, k_ref):
    x = a_ref[0, :, 0].reshape(2 * FFT_N2, GROUP_W)
    k_ref[0, 0] = _dot(mf_ref[0], x)


def _hy_mid_kernel(a_ref, mf_ref, mi_ref, k_ref, o_ref):
    n2 = FFT_N2
    k = k_ref[0, 0]
    kr, ki = k[:n2], k[n2:]
    for bb in range(a_ref.shape[0]):
        x = a_ref[bb, :, 0].reshape(2 * n2, GROUP_W)
        y = _dot(mf_ref[0], x)
        yr, yi = y[:n2], y[n2:]
        z = jnp.concatenate([yr * kr - yi * ki, yr * ki + yi * kr], axis=0).astype(BF16)
        o_ref[bb, :, 0] = _dot(mi_ref[0], z).astype(BF16).reshape(2, n2, GROUP_W)


def _hy_spectrum(a, mf):
    depth, _, n1, n2, w = a.shape
    return pl.pallas_call(
        _hy_spec_kernel,
        grid=(n1, depth),
        in_specs=[pl.BlockSpec((1, 2, 1, n2, w), lambda k1, d: (d, 0, k1, 0, 0)),
                  pl.BlockSpec((1, 2 * n2, 2 * n2), lambda k1, d: (k1, 0, 0))],
        out_specs=pl.BlockSpec((1, 1, 2 * n2, w), lambda k1, d: (d, k1, 0, 0)),
        out_shape=jax.ShapeDtypeStruct((depth, n1, 2 * n2, w), F32),
        compiler_params=_cp(("parallel", "parallel")),
        name="hy_spectrum",
    )(a, mf)


def _hy_mid(a, mf, mi, kspec, layer):
    b, _, n1, n2, w = a.shape
    blk = pl.BlockSpec((b, 2, 1, n2, w), lambda k1: (0, 0, k1, 0, 0))
    mat = pl.BlockSpec((1, 2 * n2, 2 * n2), lambda k1: (k1, 0, 0))
    return pl.pallas_call(
        _hy_mid_kernel,
        grid=(n1,),
        in_specs=[blk, mat, mat,
                  pl.BlockSpec((1, 1, 2 * n2, w), lambda k1: (layer, k1, 0, 0))],
        out_specs=blk,
        out_shape=jax.ShapeDtypeStruct(a.shape, BF16),
        compiler_params=_cp(("parallel",)),
        name="hy_mid",
    )(a, mf, mi, kspec)


def _hy_out_kernel(g_ref, b_ref, vx_ref, x0_ref, skip_ref, o_ref):
    k, j, c = b_ref.shape[1:]
    y = _dot(g_ref[...], b_ref[0].reshape(k * j, c)).reshape(-1, j, c)
    vx = vx_ref[0].astype(F32)
    o_ref[0] = ((y + skip_ref[...] * vx) * x0_ref[0].astype(F32)).astype(BF16)


def _hy_out(gk, bo, vx, x0, skip):
    b, k, n, c = bo.shape
    m = gk.shape[0] // ROW_TILE
    sig = pl.BlockSpec((1, m, ROW_TILE, c), lambda bb, i: (bb, 0, i, 0))
    return pl.pallas_call(
        _hy_out_kernel,
        grid=(b, n // ROW_TILE),
        in_specs=[pl.BlockSpec(gk.shape, lambda bb, i: (0, 0)),
                  pl.BlockSpec((1, k, ROW_TILE, c), lambda bb, i: (bb, 0, i, 0)),
                  sig, sig, pl.BlockSpec((1, c), lambda bb, i: (0, 0))],
        out_specs=sig,
        out_shape=jax.ShapeDtypeStruct((b, m, n, c), BF16),
        compiler_params=_cp(("parallel", "parallel"), VMEM_LIMIT),
        name="hy_out",
    )(gk, bo, vx, x0, skip)


def _hy_small_kernel(vx_ref, x0_ref, k_ref, fd_ref, ff_ref, gd_ref, skip_ref, o_ref):
    nf = fd_ref.shape[0] // 2
    vx = vx_ref[0]
    u = _dot(fd_ref[...], vx)
    k = _dot(ff_ref[...], k_ref[0])
    ur, ui, kr, ki = u[:nf], u[nf:], k[:nf], k[nf:]
    z = jnp.concatenate([ur * kr - ui * ki, ur * ki + ui * kr], axis=0).astype(BF16)
    y = _dot(gd_ref[...], z)
    o_ref[0] = ((y + skip_ref[...] * vx.astype(F32)) * x0_ref[0].astype(F32)).astype(BF16)


def _hy_small(vx, x0, kfilt, layer, fd, ff, gd, skip):
    b, l, w = vx.shape
    sig = pl.BlockSpec((1, l, w), lambda bb: (bb, 0, 0))
    full = lambda a: pl.BlockSpec(a.shape, lambda bb: (0,) * a.ndim)
    return pl.pallas_call(
        _hy_small_kernel,
        grid=(b,),
        in_specs=[sig, sig, pl.BlockSpec((1, 2 * l, w), lambda bb: (layer, 0, 0)),
                  full(fd), full(ff), full(gd), full(skip)],
        out_specs=sig,
        out_shape=jax.ShapeDtypeStruct((b, l, w), BF16),
        compiler_params=_cp(("parallel",)),
        name="hy_small",
    )(vx, x0, kfilt, fd, ff, gd, skip)


def _fn_first_kernel(z_ref, m_ref, o_ref):
    n2 = z_ref.shape[1]
    x = z_ref[0].reshape(n2 * 8, GROUP_W)
    o_ref[0] = _dot(m_ref[0], x).astype(BF16).reshape(2, 8, n2, GROUP_W)


def _fn_first(zf, mats):
    b, n2, n1, w = zf.shape
    return pl.pallas_call(
        _fn_first_kernel,
        grid=(n1 // 8, b),
        in_specs=[pl.BlockSpec((1, n2, 8, w), lambda i, bb: (bb, 0, i, 0)),
                  pl.BlockSpec((1, 16 * n2, 8 * n2), lambda i, bb: (i, 0, 0))],
        out_specs=pl.BlockSpec((1, 2, 8, n2, w), lambda i, bb: (bb, 0, i, 0, 0)),
        out_shape=jax.ShapeDtypeStruct((b, 2, n1, n2, w), BF16),
        compiler_params=_cp(("parallel", "parallel"), VMEM_LIMIT),
        name="fn_first",
    )(zf, mats)


def _fn_last_kernel(ar_ref, ai_ref, c_ref, s_ref, w_ref, b_ref, o_ref, *, scale):
    t = (_dot(ar_ref[0, 0], c_ref[...]) + _dot(ai_ref[0, 0], s_ref[...])) * scale
    o_ref[0] = (_dot(t.astype(BF16), w_ref[...]) + b_ref[...]).astype(BF16)


def _fn_last(a, cbd, sbd, w, bias, scale):
    b, _, l, c = a.shape
    tm = min(l, 1024)
    full = lambda z: pl.BlockSpec(z.shape, lambda bb, i: (0,) * z.ndim)
    return pl.pallas_call(
        functools.partial(_fn_last_kernel, scale=scale),
        grid=(b, l // tm),
        in_specs=[pl.BlockSpec((1, 1, tm, c), lambda bb, i: (bb, 0, i, 0)),
                  pl.BlockSpec((1, 1, tm, c), lambda bb, i: (bb, 1, i, 0)),
                  full(cbd), full(sbd), full(w), full(bias)],
        out_specs=pl.BlockSpec((1, tm, c), lambda bb, i: (bb, i, 0)),
        out_shape=jax.ShapeDtypeStruct((b, l, c), BF16),
        compiler_params=_cp(("parallel", "parallel")),
        name="fn_last",
    )(a, a, cbd, sbd, w, bias)


def _fn_small_kernel(u_ref, cl_ref, sl_ref, c_ref, s_ref, w_ref, b_ref, o_ref, *, scale):
    u = u_ref[0]
    uc = _dot(u, c_ref[...]).astype(BF16)
    us = _dot(u, s_ref[...]).astype(BF16)
    f = (_dot(cl_ref[...], uc) - _dot(sl_ref[...], us)) * scale
    o_ref[0] = (_dot(f.astype(BF16), w_ref[...]) + b_ref[...]).astype(BF16)


def _fn_small(zf, cl, sl, cbd, sbd, w, bias, scale):
    b, l, c = zf.shape
    sig = pl.BlockSpec((1, l, c), lambda bb: (bb, 0, 0))
    full = lambda z: pl.BlockSpec(z.shape, lambda bb: (0,) * z.ndim)
    return pl.pallas_call(
        functools.partial(_fn_small_kernel, scale=scale),
        grid=(b,),
        in_specs=[sig, full(cl), full(sl), full(cbd), full(sbd), full(w), full(bias)],
        out_specs=sig,
        out_shape=jax.ShapeDtypeStruct((b, l, c), BF16),
        compiler_params=_cp(("parallel",)),
        name="fn_small",
    )(zf, cl, sl, cbd, sbd, w, bias)


def _merge_kernel(oa_ref, ob_ref, oc_ref, od_ref, x_ref, g1_ref, gn_ref, w_ref, b_ref,
                  lng_ref, lnb_ref, o_ref, *, alpha):
    parts = []
    for idx, ref in enumerate((oa_ref, ob_ref, oc_ref, od_ref)):
        v = ref[0].astype(F32)
        ms = jnp.mean(v * v, axis=-1, keepdims=True)
        gain = gn_ref[:, idx * GROUP_W:(idx + 1) * GROUP_W]
        parts.append((v * lax.rsqrt(ms + LN_EPS) * gain).astype(BF16))
    y = _dot(jnp.concatenate(parts, axis=1), w_ref[...]) + b_ref[...]
    r = alpha * x_ref[0] + g1_ref[0] * y
    o_ref[0] = _layer_norm(r) * lng_ref[...] + lnb_ref[...]


def _merge(oa, ob, oc, od, x, g1, gn, w_out, b_out, ln_g, ln_b, alpha):
    b, l, d = x.shape
    tm = min(l, 512)
    grp = pl.BlockSpec((1, tm, GROUP_W), lambda bb, i: (bb, i, 0))
    tok = pl.BlockSpec((1, tm, d), lambda bb, i: (bb, i, 0))
    full = lambda a: pl.BlockSpec(a.shape, lambda bb, i: (0,) * a.ndim)
    return pl.pallas_call(
        functools.partial(_merge_kernel, alpha=alpha),
        grid=(b, l // tm),
        in_specs=[grp, grp, grp, grp, tok, pl.BlockSpec((1, 1, d), lambda bb, i: (bb, 0, 0)),
                  full(gn), full(w_out), full(b_out), full(ln_g), full(ln_b)],
        out_specs=tok,
        out_shape=jax.ShapeDtypeStruct((b, l, d), F32),
        compiler_params=_cp(("parallel", "parallel"), VMEM_LIMIT),
        name="merge",
    )(oa, ob, oc, od, x, g1, gn, w_out, b_out, ln_g, ln_b)


def _ffn_kernel(x_ref, xp_ref, xn_ref, sh_ref, sc_ref, g_ref, wup_ref, vec_ref, wd_ref, bd_ref,
                lng_ref, lnb_ref, o_ref, h_ref, acc_ref, u0_ref, u1_ref, *, alpha, tm, n_chunks, n_tiles):
    i = pl.program_id(1)
    sh, sc = sh_ref[0], sc_ref[0]
    cw = FFN_CHUNK

    def hmod(v):
        return (_layer_norm(v) * (1.0 + sc) + sh).astype(BF16)

    h_ref[0:HALO] = hmod(xp_ref[0])
    h_ref[HALO:HALO + tm] = hmod(x_ref[0])
    h_ref[HALO + tm:] = hmod(xn_ref[0])
    acc_ref[...] = jnp.zeros_like(acc_ref)
    row8 = lax.broadcasted_iota(jnp.int32, (8, 1), 0)
    pad_top = (row8 == 7) & (i == 0)
    pad_bot = (row8 == 0) & (i == n_tiles - 1)

    def up(c, dst):
        dst[...] = _dot(h_ref[...], wup_ref[c])

    def down(c, src):
        vec = vec_ref[c]
        bias, w0, w1, w2 = vec[0:1], vec[1:2], vec[2:3], vec[3:4]
        src[HALO - 8:HALO] = jnp.where(pad_top, -bias, src[HALO - 8:HALO])
        src[HALO + tm:HALO + tm + 8] = jnp.where(pad_bot, -bias, src[HALO + tm:HALO + tm + 8])
        conv = (src[HALO - 1:HALO - 1 + tm] * w0 + src[HALO:HALO + tm] * w1 + src[HALO + 1:HALO + 1 + tm] * w2
                + (vec[4:5] + bias * (w0 + w1 + w2)))
        a, g = conv[:, :cw], conv[:, cw:]
        act = (a * jax.nn.sigmoid(a) * g).astype(BF16)
        acc_ref[...] += _dot(act, wd_ref[c])

    up(0, u0_ref)

    def body(p, carry):
        c = 2 * p
        up(c + 1, u1_ref)
        down(c, u0_ref)
        up(c + 2, u0_ref)
        down(c + 1, u1_ref)
        return carry

    n_pairs = (n_chunks - 1) // 2
    lax.fori_loop(0, n_pairs, body, 0)
    c = 2 * n_pairs
    if c + 1 < n_chunks:
        up(c + 1, u1_ref)
    down(c, u0_ref)
    if c + 1 < n_chunks:
        down(c + 1, u1_ref)
    r = alpha * x_ref[0] + g_ref[0] * (acc_ref[...] + bd_ref[...])
    o_ref[0] = _layer_norm(r) * lng_ref[...] + lnb_ref[...]


def _ffn(x, sh, sc, g2, wup, vec, wd, b_down, ln_g, ln_b, alpha):
    b, l, d = x.shape
    tm = min(l, 512)
    n_chunks = wd.shape[0]
    main, prev, nxt = _halo_specs(tm, l, d)
    mod = pl.BlockSpec((1, 1, d), lambda bb, i: (bb, 0, 0))
    full = lambda a: pl.BlockSpec(a.shape, lambda bb, i: (0,) * a.ndim)
    return pl.pallas_call(
        functools.partial(_ffn_kernel, alpha=alpha, tm=tm, n_chunks=n_chunks, n_tiles=l // tm),
        grid=(b, l // tm),
        in_specs=[main, prev, nxt, mod, mod, mod, full(wup), full(vec), full(wd),
                  full(b_down), full(ln_g), full(ln_b)],
        out_specs=main,
        out_shape=jax.ShapeDtypeStruct((b, l, d), F32),
        scratch_shapes=[pltpu.VMEM((tm + 2 * HALO, d), BF16), pltpu.VMEM((tm, d), F32),
                        pltpu.VMEM((tm + 2 * HALO, 2 * FFN_CHUNK), F32),
                        pltpu.VMEM((tm + 2 * HALO, 2 * FFN_CHUNK), F32)],
        compiler_params=_cp(("parallel", "parallel"), VMEM_LIMIT),
        name="ffn",
    )(x, x, x, sh, sc, g2, wup, vec, wd, b_down, ln_g, ln_b)


def _cos_sin(idx, n):
    ang = (idx % n).astype(F32) * (2.0 * math.pi / n)
    return jnp.cos(ang), jnp.sin(ang)


def _cplx_rows(re, im):
    return jnp.concatenate([re, im], axis=-2)


def _cplx_block(re, im):
    return jnp.concatenate([jnp.concatenate([re, -im], axis=-1), jnp.concatenate([im, re], axis=-1)], axis=-2)


def _ar(n):
    return jnp.arange(n, dtype=jnp.int32)


def _hyena_tables(l):
    n, n2 = 2 * l, FFT_N2
    n1 = n // n2
    c, s = _cos_sin(_ar(n1)[:, None] * _ar(n1)[None, :], n1)
    f_full = _cplx_rows(c, -s).astype(BF16)
    g = (jnp.concatenate([c, -s], axis=1)[:n1 // 2] / n).astype(BF16)
    k1, k2, m2 = _ar(n1)[:, None, None], _ar(n2)[None, :, None], _ar(n2)[None, None, :]
    c, s = _cos_sin(m2 * k1 + n1 * m2 * k2, n)
    mf = _cplx_block(c, -s).astype(BF16)
    return dict(f_half=_lead_kron(f_full[:, :n1 // 2]), f_full=_lead_kron(f_full), g=_lead_kron(g),
                mf=mf, mi=jnp.swapaxes(mf, 1, 2))


def _fnet_tables(l):
    n2 = FFT_N2
    n1 = l // n2
    nblk = n1 // 8
    blk, j = _ar(nblk)[:, None, None, None], _ar(8)[None, :, None, None]
    k2, m2 = _ar(n2)[None, None, :, None], _ar(n2)[None, None, None, :]
    c, s = _cos_sin(n1 * m2 * k2 + (8 * blk + j) * k2, l)
    eye = jnp.eye(8, dtype=F32)
    emb = lambda t: jnp.einsum('bjkn,ji->bjkni', t, eye).reshape(nblk, 8 * n2, n2 * 8)
    first = jnp.concatenate([emb(c), emb(-s)], axis=1).astype(BF16)
    c, s = _cos_sin(_ar(n1)[:, None] * _ar(n1)[None, :], n1)
    second = _cplx_block(c, -s).astype(BF16)
    return dict(first=first, second=_lead_kron(second))


def _dense_tables(l):
    n = 2 * l
    c, s = _cos_sin(_ar(n)[:, None] * _ar(n)[None, :], n)
    ff = _cplx_rows(c, -s).astype(BF16)
    gd = (jnp.concatenate([c, -s], axis=1)[:l] / n).astype(BF16)
    cl, sl = _cos_sin(_ar(l)[:, None] * _ar(l)[None, :], l)
    return dict(fd=ff[:, :l], ff=ff, gd=gd, cl=cl.astype(BF16), sl=sl.astype(BF16))


def _rope_tables(l):
    rows = l // GRID_W
    row = jnp.broadcast_to(jnp.arange(rows)[:, None], (rows, GRID_W)).reshape(-1).astype(F32)
    col = jnp.broadcast_to(jnp.arange(GRID_W)[None, :], (rows, GRID_W)).reshape(-1).astype(F32)
    half = HEAD_DIM // 2
    inv = ROPE_THETA ** (-jnp.arange(0, half, 2, dtype=F32) / half)
    ar, ac = row[:, None] * inv, col[:, None] * inv
    cos = jnp.concatenate([jnp.cos(ar), jnp.cos(ar), jnp.cos(ac), jnp.cos(ac)], axis=1)
    sin = jnp.concatenate([-jnp.sin(ar), jnp.sin(ar), -jnp.sin(ac), jnp.sin(ac)], axis=1)
    return jnp.tile(cos, (1, 2)), jnp.tile(sin, (1, 2))


def kernel(x, c, ctx, c_ctx, w_ada, b_ada, w_in, sink_a, q_norm_g, k_norm_g, hy_conv_w, hy_conv_b,
           hy_f_w1, hy_f_b1, hy_f_freq, hy_f_w2, hy_f_b2, hy_f_w3, hy_skip, fnet_w, fnet_b,
           out_norm_g, w_out, b_out, ln1_g, ln1_b, ffn_w_up, ffn_b_up, ffn_conv_w, ffn_conv_b,
           ffn_w_down, ffn_b_down, ln2_g, ln2_b):
    bsz, l, d = x.shape
    lc = ctx.shape[1]
    depth = w_ada.shape[0]
    d_ff = ffn_w_down.shape[1]
    alpha = (2 * depth) ** 0.25
    n2 = FFT_N2
    n1h, n1f = 2 * l // n2, l // n2
    lanes = n2 * GROUP_W

    ht, ft, dt = _hyena_tables(l), _fnet_tables(l), _dense_tables(lc)
    cos_l, sin_l = _rope_tables(l)
    cos_c, sin_c = jnp.ones((lc, 128), F32), jnp.zeros((lc, 128), F32)
    jj = _ar(GROUP_W)
    same = (jj[:, None] // FNET_GROUP_DIM) == (jj[None, :] // FNET_GROUP_DIM)
    bd_mean = (same.astype(F32) / HEAD_DIM).astype(BF16)
    cg, sg = _cos_sin(jj[:, None] * jj[None, :], FNET_GROUP_DIM)
    cbd, sbd = jnp.where(same, cg, 0.0).astype(BF16), jnp.where(same, sg, 0.0).astype(BF16)

    n_rows = -(-(bsz + 1) // 8) * 8
    cc = jnp.zeros((n_rows, d), F32).at[:bsz].set(c).at[bsz].set(c_ctx)
    mod = _ada(cc, w_ada, b_ada)

    k_lat = _hy_filter(l, hy_f_w1, hy_f_b1, hy_f_freq, hy_f_w2, hy_f_b2, hy_f_w3)
    k_ctx = _hy_filter(lc, hy_f_w1, hy_f_b1, hy_f_freq, hy_f_w2, hy_f_b2, hy_f_w3)
    a_k = _lead_mm(ht['f_full'], k_lat.reshape(depth, n1h, n2, GROUP_W))
    k_spec = _hy_spectrum(a_k.reshape(depth, 2, n1h, n2, GROUP_W), ht['mf'])

    n_chunks = -(-d_ff // FFN_CHUNK)
    pad = n_chunks * FFN_CHUNK - d_ff

    def chunked(v):
        v = v.reshape(v.shape[:-1] + (2, d_ff))
        v = jnp.pad(v, [(0, 0)] * (v.ndim - 1) + [(0, pad)])
        v = v.reshape(v.shape[:-2] + (2, n_chunks, FFN_CHUNK))
        v = jnp.swapaxes(v, -3, -2).reshape(v.shape[:-3] + (n_chunks, 2 * FFN_CHUNK))
        return jnp.moveaxis(v, -2, 0)

    for layer in range(depth):
        m6 = mod[layer].reshape(n_rows, 6, d)
        lat = [m6[:bsz, i][:, None, :] for i in range(6)]
        con = [jnp.broadcast_to(m6[bsz, i][None, None, :], (bsz, 1, d)) for i in range(6)]
        w_in_l = w_in[layer].astype(BF16)
        qg = jnp.tile(q_norm_g[layer], N_HEADS).reshape(1, -1)
        kg = jnp.tile(k_norm_g[layer], N_KV).reshape(1, -1)
        w_out_l = w_out[layer].astype(BF16)
        gn, b_out_l = out_norm_g[layer].reshape(1, d), b_out[layer].reshape(1, d)
        l1g, l1b = ln1_g[layer].reshape(1, d), ln1_b[layer].reshape(1, d)
        l2g, l2b = ln2_g[layer].reshape(1, d), ln2_b[layer].reshape(1, d)
        fw, fb = fnet_w[layer].astype(BF16), fnet_b[layer].reshape(1, GROUP_W)
        skip = hy_skip[layer].reshape(1, GROUP_W)
        wup = chunked(ffn_w_up[layer]).astype(BF16)
        vec = chunked(jnp.concatenate([ffn_b_up[layer][None], ffn_conv_w[layer], ffn_conv_b[layer][None],
                                       jnp.zeros((3, 2 * d_ff), F32)], axis=0))
        wd = jnp.pad(ffn_w_down[layer], ((0, pad), (0, 0))).reshape(n_chunks, FFN_CHUNK, d).astype(BF16)
        bdn = ffn_b_down[layer].reshape(1, d)
        last = layer == depth - 1

        cqa, cka, cva, cqb, ckb, cvb, czh, czf = _proj(ctx, con[0], con[1], w_in_l, cos_c, sin_c,
                                                       bd_mean, qg, kg)
        qa, ka, va, qb, kb, vb, zh, zf = _proj(x, lat[0], lat[1], w_in_l, cos_l, sin_l, bd_mean, qg, kg)
        cat = lambda a, b_: jnp.concatenate([a, b_], axis=2)
        oa = _attention(qa, cat(ka, cka), cat(va, cva), sink_a[layer], banded=True, n_lat_keys=l)
        ob = _attention(qb, cat(kb, ckb), cat(vb, cvb), None, banded=False)

        vx, x0 = _hy_gate(zh, hy_conv_w[layer], hy_conv_b[layer])
        vx4, x04 = (a.reshape(bsz, n1h // 2, n2, GROUP_W) for a in (vx, x0))
        a1 = _lead_mm(ht['f_half'], vx4)
        bo = _hy_mid(a1.reshape(bsz, 2, n1h, n2, GROUP_W), ht['mf'], ht['mi'], k_spec, layer)
        oc = _hy_out(ht['g'], bo.reshape(bsz, 2 * n1h, n2, GROUP_W), vx4, x04, skip).reshape(bsz, l, GROUP_W)

        f1 = _fn_first(zf.reshape(bsz, n2, n1f, GROUP_W), ft['first'])
        f2 = _lead_mm(ft['second'], f1.reshape(bsz, 2 * n1f, n2, GROUP_W))
        od = _fn_last(f2.reshape(bsz, 2, l, GROUP_W), cbd, sbd, fw, fb, (l * FNET_GROUP_DIM) ** -0.5)

        x = _merge(oa, ob, oc, od, x, lat[2], gn, w_out_l, b_out_l, l1g, l1b, alpha)
        x = _ffn(x, lat[3], lat[4], lat[5], wup, vec, wd, bdn, l2g, l2b, alpha)

        if not last:
            oac = _attention(cqa, cka, cva, sink_a[layer], banded=False)
            obc = _attention(cqb, ckb, cvb, None, banded=False)
            cvx, cx0 = _hy_gate(czh, hy_conv_w[layer], hy_conv_b[layer])
            occ = _hy_small(cvx, cx0, k_ctx, layer, dt['fd'], dt['ff'], dt['gd'], skip)
            odc = _fn_small(czf, dt['cl'], dt['sl'], cbd, sbd, fw, fb, (lc * FNET_GROUP_DIM) ** -0.5)
            ctx = _merge(oac, obc, occ, odc, ctx, con[2], gn, w_out_l, b_out_l, l1g, l1b, alpha)
            ctx = _ffn(ctx, con[3], con[4], con[5], wup, vec, wd, bdn, l2g, l2b, alpha)
    return x
```

```python
import functools
import math

import jax
import jax.numpy as jnp
from jax import lax
from jax.experimental import pallas as pl
from jax.experimental.pallas import tpu as pltpu

F32 = jnp.float32
BF16 = jnp.bfloat16

HEAD_DIM = 64
GROUP_W = 256
N_HEADS = 4
N_KV = 2
WINDOW = 128
GRID_W = 64
ROPE_THETA = 10000.0
FNET_GROUP_DIM = 64
HYENA_BANDS = 8
HYENA_FAST_DECAY = 0.3
HYENA_SLOW_DECAY = 1.5
HYENA_TARGET = 1e-2
LN_EPS = 1e-6
NEG_INF = -1e30
FFT_N2 = 128
FFN_CHUNK = 256
HALO = 16
VMEM_LIMIT = 56 * 1024 * 1024


def _cp(sem, vmem=None):
    return pltpu.CompilerParams(dimension_semantics=sem, vmem_limit_bytes=vmem)


def _layer_norm(x):
    mu = jnp.mean(x, axis=-1, keepdims=True)
    xc = x - mu
    var = jnp.mean(xc * xc, axis=-1, keepdims=True)
    return xc * lax.rsqrt(var + LN_EPS)


def _dot(a, b):
    return jnp.dot(a, b, preferred_element_type=F32)


def _ada_kernel(c_ref, w_ref, b_ref, o_ref):
    c = c_ref[...]
    s = (c * jax.nn.sigmoid(c)).astype(BF16)
    o_ref[0] = _dot(s, w_ref[0].astype(BF16)) + b_ref[0]


def _ada(cc, w_ada, b_ada):
    depth, d, n = w_ada.shape
    r = cc.shape[0]
    tn = 1536
    return pl.pallas_call(
        _ada_kernel,
        grid=(depth, n // tn),
        in_specs=[pl.BlockSpec((r, d), lambda l, j: (0, 0)),
                  pl.BlockSpec((1, d, tn), lambda l, j: (l, 0, j)),
                  pl.BlockSpec((1, 1, tn), lambda l, j: (l, 0, j))],
        out_specs=pl.BlockSpec((1, r, tn), lambda l, j: (l, 0, j)),
        out_shape=jax.ShapeDtypeStruct((depth, r, n), F32),
        compiler_params=_cp(("parallel", "parallel"), VMEM_LIMIT),
        name="ada",
    )(cc, w_ada, b_ada.reshape(depth, 1, n))


def _swap_halves(z):
    w = z.shape[1]
    lane = lax.broadcasted_iota(jnp.int32, z.shape, 1)
    return jnp.where(lane % 32 < 16, pltpu.roll(z, w - 16, 1), pltpu.roll(z, 16, 1))


def _proj_kernel(x_ref, sh_ref, sc_ref, w_ref, cos_ref, sin_ref, bd_ref, qg_ref, kg_ref,
                 qa_ref, ka_ref, va_ref, qb_ref, kb_ref, vb_ref, zh_ref, zf_ref, *, n_parts):
    tm = x_ref.shape[1]
    rows = tm // n_parts
    scale = HEAD_DIM ** -0.5
    for part in range(n_parts):
        rs = slice(part * rows, (part + 1) * rows)
        h = (_layer_norm(x_ref[0, rs]) * (1.0 + sc_ref[0]) + sh_ref[0]).astype(BF16)
        cos1, sin1 = cos_ref[rs], sin_ref[rs]
        cos2 = jnp.concatenate([cos1, cos1], axis=1)
        sin2 = jnp.concatenate([sin1, sin1], axis=1)

        def proj(lo, width):
            return _dot(h, w_ref[:, lo:lo + width])

        def rope(z):
            if z.shape[1] == 128:
                return z * cos1 + _swap_halves(z) * sin1
            return z * cos2 + _swap_halves(z) * sin2

        def head_norm(z, g):
            w = z.shape[1]
            ms = _dot((z * z).astype(BF16), bd_ref[:w, :w])
            return z * lax.rsqrt(ms + LN_EPS) * g

        def put_heads(ref, z):
            for hh in range(z.shape[1] // HEAD_DIM):
                ref[0, hh, rs] = z[:, hh * HEAD_DIM:(hh + 1) * HEAD_DIM].astype(BF16)

        put_heads(qa_ref, rope(proj(0, 256)) * scale)
        kv = proj(256, 256)
        put_heads(ka_ref, rope(kv[:, :128]))
        put_heads(va_ref, kv[:, 128:])
        put_heads(qb_ref, rope(head_norm(proj(512, 256), qg_ref[...])) * scale)
        kv = proj(768, 256)
        put_heads(kb_ref, rope(head_norm(kv[:, :128], kg_ref[...])))
        put_heads(vb_ref, kv[:, 128:])
        zh_ref[0, rs] = proj(1024, 768).astype(BF16)
        zf_ref[0, rs] = proj(1792, 256).astype(BF16)


def _proj(x, sh, sc, w_in, cos, sin, bd, qg, kg):
    b, l, d = x.shape
    tm = min(l, 512)
    heads = lambda n: jax.ShapeDtypeStruct((b, n, l, HEAD_DIM), BF16)
    hspec = lambda n: pl.BlockSpec((1, n, tm, HEAD_DIM), lambda t, bb: (bb, 0, t, 0))
    tok = lambda w: pl.BlockSpec((1, tm, w), lambda t, bb: (bb, t, 0))
    vec = lambda w: pl.BlockSpec((1, 1, w), lambda t, bb: (bb, 0, 0))
    full = lambda a: pl.BlockSpec(a.shape, lambda t, bb: (0,) * a.ndim)
    tab = pl.BlockSpec((tm, 128), lambda t, bb: (t, 0))
    return pl.pallas_call(
        functools.partial(_proj_kernel, n_parts=1),
        grid=(l // tm, b),
        in_specs=[tok(d), vec(d), vec(d), full(w_in), tab, tab, full(bd), full(qg), full(kg)],
        out_specs=[hspec(4), hspec(2), hspec(2), hspec(4), hspec(2), hspec(2), tok(768), tok(256)],
        out_shape=[heads(4), heads(2), heads(2), heads(4), heads(2), heads(2),
                   jax.ShapeDtypeStruct((b, l, 768), BF16), jax.ShapeDtypeStruct((b, l, 256), BF16)],
        compiler_params=_cp(("parallel", "parallel"), VMEM_LIMIT),
        name="proj",
    )(x, sh, sc, w_in, cos, sin, bd, qg, kg)


_NT = (((1,), (1,)), ((), ()))


def _row_max(*parts):
    tiles = [p[:, i:i + 128] for p in parts for i in range(0, p.shape[1], 128)]
    m = tiles[0]
    for t in tiles[1:]:
        m = jnp.maximum(m, t)
    return jnp.max(m, axis=-1, keepdims=True)


def _sink_rows(sink_ref, j, tq):
    row = lax.broadcasted_iota(jnp.int32, (2 * tq, 1), 0)
    return jnp.where(row < tq, sink_ref[2 * j], sink_ref[2 * j + 1])


def _attn_finish(o_ref, acc, denom, tq):
    o = acc[:, :HEAD_DIM] / denom
    o_ref[0] = jnp.concatenate([o[:tq], o[tq:]], axis=1).astype(BF16)


def _attn_full_kernel(sink_ref, q_ref, k_ref, v_ref, o_ref, m_ref, acc_ref, s0_ref, s1_ref, *,
                      has_sink, tq, tk, n_t, n_chunks):
    j = pl.program_id(1)
    lane = lax.broadcasted_iota(jnp.int32, acc_ref.shape[1:], 1)
    for tile in range(n_t):
        if has_sink:
            m_ref[tile] = _sink_rows(sink_ref, j, tq)
            acc_ref[tile] = jnp.where(lane == HEAD_DIM, 1.0, 0.0)
        else:
            m_ref[tile] = jnp.full(m_ref.shape[1:], NEG_INF, F32)
            acc_ref[tile] = jnp.zeros(acc_ref.shape[1:], F32)

    def chunk(ref, t):
        c = t // n_t
        off = c * tk if isinstance(c, int) else pl.multiple_of(c * tk, tk)
        return ref[0, 0, pl.ds(off, tk), :]

    def scores(t, tile, dst):
        q = q_ref[0, :, tile * tq:(tile + 1) * tq, :].reshape(2 * tq, HEAD_DIM)
        dst[...] = lax.dot_general(q, chunk(k_ref, t), _NT, preferred_element_type=F32)

    def update(t, tile, src):
        s = src[...]
        m_prev = m_ref[tile]
        m_new = jnp.maximum(m_prev, _row_max(s))
        p = jnp.exp((s - m_new).astype(BF16))
        acc_ref[tile] = jnp.exp(m_prev - m_new) * acc_ref[tile] + _dot(p, chunk(v_ref, t))
        m_ref[tile] = m_new

    total = n_t * n_chunks
    odd_tile = 1 % n_t
    scores(0, 0, s0_ref)

    def body(i, carry):
        t = 2 * i
        scores(t + 1, odd_tile, s1_ref)
        update(t, 0, s0_ref)
        scores(t + 2, 0, s0_ref)
        update(t + 1, odd_tile, s1_ref)
        return carry

    n_pairs = (total - 1) // 2
    lax.fori_loop(0, n_pairs, body, 0, unroll=2)
    t = 2 * n_pairs
    if t + 1 < total:
        scores(t + 1, odd_tile, s1_ref)
    update(t, 0, s0_ref)
    if t + 1 < total:
        update(t + 1, odd_tile, s1_ref)
    for tile in range(n_t):
        acc = acc_ref[tile]
        o = acc[:, :HEAD_DIM] * (1.0 / acc[:, HEAD_DIM:HEAD_DIM + 1])
        o_ref[0, tile * tq:(tile + 1) * tq, :] = jnp.concatenate([o[:tq], o[tq:]], axis=1).astype(BF16)


def _band_bias(tq):
    span = tq + 2 * WINDOW
    r = (_ar(2 * tq) % tq)[None, :, None]
    col = _ar(span)[None, None, :]
    shift = (_ar(3) * WINDOW)[:, None, None]
    return jnp.where(jnp.abs(col - shift - r) <= WINDOW, 0.0, NEG_INF).astype(F32)


def _attn_band_kernel(sink_ref, q_ref, k_ref, v_ref, bias_ref, o_ref, *, tq, n_t, n_tiles, l_lat, l_ctx):
    j, g = pl.program_id(1), pl.program_id(2)
    span = tq + 2 * WINDOW
    sink = _sink_rows(sink_ref, j, tq)
    kc, vc = k_ref[0, 0, pl.ds(l_lat, l_ctx), :], v_ref[0, 0, pl.ds(l_lat, l_ctx), :]
    for tile in range(n_t):
        qi = g * n_t + tile
        start = pl.multiple_of(jnp.clip(qi * tq - WINDOW, 0, l_lat - span), WINDOW)
        kind = jnp.where(qi == 0, 0, jnp.where(qi == n_tiles - 1, 2, 1))
        q = q_ref[0, :, tile * tq:(tile + 1) * tq, :].reshape(2 * tq, HEAD_DIM)
        s_lat = (lax.dot_general(q, k_ref[0, 0, pl.ds(start, span), :], _NT, preferred_element_type=F32)
                 + bias_ref[kind])
        s_ctx = lax.dot_general(q, kc, _NT, preferred_element_type=F32)
        m = jnp.maximum(sink, _row_max(s_lat, s_ctx))
        acc = (_dot(jnp.exp((s_lat - m).astype(BF16)), v_ref[0, 0, pl.ds(start, span), :])
               + _dot(jnp.exp((s_ctx - m).astype(BF16)), vc))
        o = acc[:, :HEAD_DIM] * (1.0 / (acc[:, HEAD_DIM:HEAD_DIM + 1] + jnp.exp(sink - m)))
        o_ref[0, tile * tq:(tile + 1) * tq, :] = jnp.concatenate([o[:tq], o[tq:]], axis=1).astype(BF16)


def _attention(q, k, v, sink, *, banded, n_lat_keys=0):
    b, _, lq, _ = q.shape
    lk = k.shape[2]
    tq = 256
    has_sink = sink is not None
    if not has_sink:
        sink = jnp.zeros((N_HEADS,), F32)
    v = jnp.concatenate([v, jnp.ones_like(v[..., :1]), jnp.zeros_like(v[..., 1:])], axis=-1)
    smem = pl.BlockSpec(memory_space=pltpu.SMEM)
    out_shape = jax.ShapeDtypeStruct((b, lq, GROUP_W), BF16)
    if banded:
        assert lq == n_lat_keys and n_lat_keys >= tq + 2 * WINDOW and tq % WINDOW == 0
        n_tiles = lq // tq
        n_t = next(t for t in (4, 2, 1) if n_tiles % t == 0)
        bias = _band_bias(tq)
        return pl.pallas_call(
            functools.partial(_attn_band_kernel, tq=tq, n_t=n_t, n_tiles=n_tiles, l_lat=n_lat_keys,
                              l_ctx=lk - n_lat_keys),
            grid=(b, N_KV, n_tiles // n_t),
            in_specs=[smem,
                      pl.BlockSpec((1, 2, n_t * tq, HEAD_DIM), lambda bb, j, g: (bb, j, g, 0)),
                      pl.BlockSpec((1, 1, lk, HEAD_DIM), lambda bb, j, g: (bb, j, 0, 0)),
                      pl.BlockSpec((1, 1, lk, 2 * HEAD_DIM), lambda bb, j, g: (bb, j, 0, 0)),
                      pl.BlockSpec(bias.shape, lambda bb, j, g: (0, 0, 0))],
            out_specs=pl.BlockSpec((1, n_t * tq, 128), lambda bb, j, g: (bb, g, j)),
            out_shape=out_shape,
            compiler_params=_cp(("parallel", "parallel", "parallel"), VMEM_LIMIT),
            name="attn_band",
        )(sink, q, k, v, bias)
    tk = next(t for t in (768, 512, 256) if lk % t == 0)
    n_t = 2 if lq % (2 * tq) == 0 else 1
    kern = functools.partial(_attn_full_kernel, has_sink=has_sink, tq=tq, tk=tk, n_t=n_t, n_chunks=lk // tk)
    return pl.pallas_call(
        kern,
        grid=(b, N_KV, lq // (n_t * tq)),
        in_specs=[smem,
                  pl.BlockSpec((1, 2, n_t * tq, HEAD_DIM), lambda bb, j, qi: (bb, j, qi, 0)),
                  pl.BlockSpec((1, 1, lk, HEAD_DIM), lambda bb, j, qi: (bb, j, 0, 0)),
                  pl.BlockSpec((1, 1, lk, 2 * HEAD_DIM), lambda bb, j, qi: (bb, j, 0, 0))],
        out_specs=pl.BlockSpec((1, n_t * tq, 128), lambda bb, j, qi: (bb, qi, j)),
        out_shape=out_shape,
        scratch_shapes=[pltpu.VMEM((n_t, 2 * tq, 1), F32), pltpu.VMEM((n_t, 2 * tq, 2 * HEAD_DIM), F32),
                        pltpu.VMEM((2 * tq, tk), F32), pltpu.VMEM((2 * tq, tk), F32)],
        compiler_params=_cp(("parallel", "parallel", "parallel"), VMEM_LIMIT),
        name="attn_full",
    )(sink, q, k, v)


def _shift_rows(z, prev_row, next_row):
    n = z.shape[0]
    row = lax.broadcasted_iota(jnp.int32, (n, 1), 0)
    dn = jnp.where(row == 0, prev_row, pltpu.roll(z, 1, 0))
    up = jnp.where(row == n - 1, next_row, pltpu.roll(z, n - 1, 0))
    return dn, up


def _hy_gate_kernel(z_ref, zp_ref, zn_ref, w_ref, b_ref, vx_ref, x0_ref, *, n_tiles):
    i = pl.program_id(1)
    z = z_ref[0].astype(F32)
    prev_row = zp_ref[0].astype(F32)[HALO - 1:HALO] * (i > 0).astype(F32)
    next_row = zn_ref[0].astype(F32)[0:1] * (i < n_tiles - 1).astype(F32)
    dn, up = _shift_rows(z, prev_row, next_row)
    w = w_ref[...]
    u = dn * w[0:1] + z * w[1:2] + up * w[2:3] + b_ref[...]
    x0_ref[0] = u[:, :GROUP_W].astype(BF16)
    vx_ref[0] = (u[:, 2 * GROUP_W:] * u[:, GROUP_W:2 * GROUP_W]).astype(BF16)


def _halo_specs(tl, l, w):
    nb = l // HALO
    per = tl // HALO
    return (pl.BlockSpec((1, tl, w), lambda bb, i: (bb, i, 0)),
            pl.BlockSpec((1, HALO, w), lambda bb, i: (bb, jnp.maximum(i * per - 1, 0), 0)),
            pl.BlockSpec((1, HALO, w), lambda bb, i: (bb, jnp.minimum((i + 1) * per, nb - 1), 0)))


def _hy_gate(zh, conv_w, conv_b):
    b, l, w = zh.shape
    tl = min(l, 512)
    out = jax.ShapeDtypeStruct((b, l, GROUP_W), BF16)
    ospec = pl.BlockSpec((1, tl, GROUP_W), lambda bb, i: (bb, i, 0))
    return pl.pallas_call(
        functools.partial(_hy_gate_kernel, n_tiles=l // tl),
        grid=(b, l // tl),
        in_specs=[*_halo_specs(tl, l, w),
                  pl.BlockSpec((3, w), lambda bb, i: (0, 0)),
                  pl.BlockSpec((1, w), lambda bb, i: (0, 0))],
        out_specs=[ospec, ospec],
        out_shape=[out, out],
        compiler_params=_cp(("parallel", "parallel")),
        name="hy_gate",
    )(zh, zh, zh, conv_w, conv_b.reshape(1, w))


def _hy_filter_kernel(fc_ref, w1_ref, b1_ref, fr_ref, w2_ref, b2_ref, w3_ref, dl_ref, k_ref, *, l, tr):
    base = pl.program_id(1) * tr

    def position(m):
        return jnp.where(m < l, m, 2 * l - m).astype(F32)

    m_row = base + lax.broadcasted_iota(jnp.int32, (1, tr), 1)
    t_row = position(m_row)
    feat_id = lax.broadcasted_iota(jnp.int32, (fc_ref.shape[0], 1), 0)
    phase = fc_ref[...] * t_row
    feat = jnp.where(feat_id == 0, t_row / max(l - 1, 1),
                     jnp.where(feat_id <= HYENA_BANDS, jnp.cos(phase),
                               jnp.where(feat_id <= 2 * HYENA_BANDS, -jnp.sin(phase), 0.0)))
    hp = functools.partial(jnp.dot, precision=lax.Precision.HIGHEST, preferred_element_type=F32)
    fr = fr_ref[0]
    h = jnp.sin(fr * (hp(w1_ref[0], feat) + b1_ref[0]))
    h = jnp.sin(fr * (hp(w2_ref[0], h) + b2_ref[0]))
    h = hp(h.T, w3_ref[0])
    m = base + lax.broadcasted_iota(jnp.int32, (tr, 1), 0)
    dec = jnp.exp(-(position(m) / max(l - 1, 1)) * dl_ref[...])
    hf, hb = h[:, :GROUP_W] * dec, h[:, GROUP_W:] * dec
    k = jnp.where(m < l, hf, hb)
    k = jnp.where(m == 0, hf + hb, k)
    k_ref[0] = jnp.where(m == l, 0.0, k).astype(BF16)


def _hy_filter(l, w1, b1, fr, w2, b2, w3):
    depth, emb, hid = w1.shape
    tr = min(2 * l, 2048)
    n_feat = -(-emb // 8) * 8
    bands = jnp.linspace(1e-4, HYENA_BANDS - 1, HYENA_BANDS, dtype=F32) * (2.0 * math.pi / l)
    fc = jnp.zeros((n_feat, 1), F32).at[1:1 + HYENA_BANDS, 0].set(bands).at[1 + HYENA_BANDS:emb, 0].set(bands)
    w1t = jnp.zeros((depth, hid, n_feat), F32).at[:, :, :emb].set(jnp.swapaxes(w1, 1, 2))
    w2t = jnp.swapaxes(w2, 1, 2)
    min_decay = math.log(HYENA_TARGET) / HYENA_SLOW_DECAY
    max_decay = math.log(HYENA_TARGET) / HYENA_FAST_DECAY
    deltas = jnp.abs(jnp.linspace(min_decay, max_decay, GROUP_W, dtype=F32)).reshape(1, GROUP_W)
    lay = lambda a: pl.BlockSpec((1,) + a.shape[1:], lambda d, i: (d,) + (0,) * (a.ndim - 1))
    fix = lambda a: pl.BlockSpec(a.shape, lambda d, i: (0,) * a.ndim)
    b1, fr, b2 = (a.reshape(depth, hid, 1) for a in (b1, fr, b2))
    return pl.pallas_call(
        functools.partial(_hy_filter_kernel, l=l, tr=tr),
        grid=(depth, 2 * l // tr),
        in_specs=[fix(fc), lay(w1t), lay(b1), lay(fr), lay(w2t), lay(b2), lay(w3), fix(deltas)],
        out_specs=pl.BlockSpec((1, tr, GROUP_W), lambda d, i: (d, i, 0)),
        out_shape=jax.ShapeDtypeStruct((depth, 2 * l, GROUP_W), BF16),
        compiler_params=_cp(("parallel", "parallel")),
        name="hy_filter",
    )(fc, w1t, b1, fr, w2t, b2, w3, deltas)


ROW_TILE = 8


def _lead_kron(f):
    return jnp.kron(f.astype(F32), jnp.eye(ROW_TILE, dtype=F32)).astype(BF16)


def _lead_mm_kernel(f_ref, x_ref, o_ref):
    k, j, c = x_ref.shape[1:]
    y = _dot(f_ref[...], x_ref[0].reshape(k * j, c))
    o_ref[0] = y.reshape(-1, j, c).astype(o_ref.dtype)


def _lead_mm(fk, x):
    b, k, n, c = x.shape
    m = fk.shape[0] // ROW_TILE
    return pl.pallas_call(
        _lead_mm_kernel,
        grid=(b, n // ROW_TILE),
        in_specs=[pl.BlockSpec(fk.shape, lambda bb, i: (0, 0)),
                  pl.BlockSpec((1, k, ROW_TILE, c), lambda bb, i: (bb, 0, i, 0))],
        out_specs=pl.BlockSpec((1, m, ROW_TILE, c), lambda bb, i: (bb, 0, i, 0)),
        out_shape=jax.ShapeDtypeStruct((b, m, n, c), BF16),
        compiler_params=_cp(("parallel", "parallel"), VMEM_LIMIT),
        name="lead_mm",
    )(fk, x)


def _hy_spec_kernel(a_ref, mf_ref, k_ref):
    x = a_ref[0, :, 0].reshape(2 * FFT_N2, GROUP_W)
    k_ref[0, 0] = _dot(mf_ref[0], x)


def _hy_mid_kernel(a_ref, mf_ref, mi_ref, k_ref, o_ref):
    n2 = FFT_N2
    k = k_ref[0, 0]
    kr, ki = k[:n2], k[n2:]
    for bb in range(a_ref.shape[0]):
        x = a_ref[bb, :, 0].reshape(2 * n2, GROUP_W)
        y = _dot(mf_ref[0], x)
        yr, yi = y[:n2], y[n2:]
        z = jnp.concatenate([yr * kr - yi * ki, yr * ki + yi * kr], axis=0).astype(BF16)
        o_ref[bb, :, 0] = _dot(mi_ref[0], z).astype(BF16).reshape(2, n2, GROUP_W)


def _hy_spectrum(a, mf):
    depth, _, n1, n2, w = a.shape
    return pl.pallas_call(
        _hy_spec_kernel,
        grid=(n1, depth),
        in_specs=[pl.BlockSpec((1, 2, 1, n2, w), lambda k1, d: (d, 0, k1, 0, 0)),
                  pl.BlockSpec((1, 2 * n2, 2 * n2), lambda k1, d: (k1, 0, 0))],
        out_specs=pl.BlockSpec((1, 1, 2 * n2, w), lambda k1, d: (d, k1, 0, 0)),
        out_shape=jax.ShapeDtypeStruct((depth, n1, 2 * n2, w), F32),
        compiler_params=_cp(("parallel", "parallel")),
        name="hy_spectrum",
    )(a, mf)


def _hy_mid(a, mf, mi, kspec, layer):
    b, _, n1, n2, w = a.shape
    blk = pl.BlockSpec((b, 2, 1, n2, w), lambda k1: (0, 0, k1, 0, 0))
    mat = pl.BlockSpec((1, 2 * n2, 2 * n2), lambda k1: (k1, 0, 0))
    return pl.pallas_call(
        _hy_mid_kernel,
        grid=(n1,),
        in_specs=[blk, mat, mat,
                  pl.BlockSpec((1, 1, 2 * n2, w), lambda k1: (layer, k1, 0, 0))],
        out_specs=blk,
        out_shape=jax.ShapeDtypeStruct(a.shape, BF16),
        compiler_params=_cp(("parallel",)),
        name="hy_mid",
    )(a, mf, mi, kspec)


def _hy_out_kernel(g_ref, b_ref, vx_ref, x0_ref, skip_ref, o_ref):
    k, j, c = b_ref.shape[1:]
    y = _dot(g_ref[...], b_ref[0].reshape(k * j, c)).reshape(-1, j, c)
    vx = vx_ref[0].astype(F32)
    o_ref[0] = ((y + skip_ref[...] * vx) * x0_ref[0].astype(F32)).astype(BF16)


def _hy_out(gk, bo, vx, x0, skip):
    b, k, n, c = bo.shape
    m = gk.shape[0] // ROW_TILE
    sig = pl.BlockSpec((1, m, ROW_TILE, c), lambda bb, i: (bb, 0, i, 0))
    return pl.pallas_call(
        _hy_out_kernel,
        grid=(b, n // ROW_TILE),
        in_specs=[pl.BlockSpec(gk.shape, lambda bb, i: (0, 0)),
                  pl.BlockSpec((1, k, ROW_TILE, c), lambda bb, i: (bb, 0, i, 0)),
                  sig, sig, pl.BlockSpec((1, c), lambda bb, i: (0, 0))],
        out_specs=sig,
        out_shape=jax.ShapeDtypeStruct((b, m, n, c), BF16),
        compiler_params=_cp(("parallel", "parallel"), VMEM_LIMIT),
        name="hy_out",
    )(gk, bo, vx, x0, skip)


def _hy_small_kernel(vx_ref, x0_ref, k_ref, fd_ref, ff_ref, gd_ref, skip_ref, o_ref):
    nf = fd_ref.shape[0] // 2
    vx = vx_ref[0]
    u = _dot(fd_ref[...], vx)
    k = _dot(ff_ref[...], k_ref[0])
    ur, ui, kr, ki = u[:nf], u[nf:], k[:nf], k[nf:]
    z = jnp.concatenate([ur * kr - ui * ki, ur * ki + ui * kr], axis=0).astype(BF16)
    y = _dot(gd_ref[...], z)
    o_ref[0] = ((y + skip_ref[...] * vx.astype(F32)) * x0_ref[0].astype(F32)).astype(BF16)


def _hy_small(vx, x0, kfilt, layer, fd, ff, gd, skip):
    b, l, w = vx.shape
    sig = pl.BlockSpec((1, l, w), lambda bb: (bb, 0, 0))
    full = lambda a: pl.BlockSpec(a.shape, lambda bb: (0,) * a.ndim)
    return pl.pallas_call(
        _hy_small_kernel,
        grid=(b,),
        in_specs=[sig, sig, pl.BlockSpec((1, 2 * l, w), lambda bb: (layer, 0, 0)),
                  full(fd), full(ff), full(gd), full(skip)],
        out_specs=sig,
        out_shape=jax.ShapeDtypeStruct((b, l, w), BF16),
        compiler_params=_cp(("parallel",)),
        name="hy_small",
    )(vx, x0, kfilt, fd, ff, gd, skip)


def _fn_first_kernel(z_ref, m_ref, o_ref):
    n2 = z_ref.shape[1]
    x = z_ref[0].reshape(n2 * 8, GROUP_W)
    o_ref[0] = _dot(m_ref[0], x).astype(BF16).reshape(2, 8, n2, GROUP_W)


def _fn_first(zf, mats):
    b, n2, n1, w = zf.shape
    return pl.pallas_call(
        _fn_first_kernel,
        grid=(n1 // 8, b),
        in_specs=[pl.BlockSpec((1, n2, 8, w), lambda i, bb: (bb, 0, i, 0)),
                  pl.BlockSpec((1, 16 * n2, 8 * n2), lambda i, bb: (i, 0, 0))],
        out_specs=pl.BlockSpec((1, 2, 8, n2, w), lambda i, bb: (bb, 0, i, 0, 0)),
        out_shape=jax.ShapeDtypeStruct((b, 2, n1, n2, w), BF16),
        compiler_params=_cp(("parallel", "parallel"), VMEM_LIMIT),
        name="fn_first",
    )(zf, mats)


def _fn_last_kernel(ar_ref, ai_ref, c_ref, s_ref, w_ref, b_ref, o_ref, *, scale):
    t = (_dot(ar_ref[0, 0], c_ref[...]) + _dot(ai_ref[0, 0], s_ref[...])) * scale
    o_ref[0] = (_dot(t.astype(BF16), w_ref[...]) + b_ref[...]).astype(BF16)


def _fn_last(a, cbd, sbd, w, bias, scale):
    b, _, l, c = a.shape
    tm = min(l, 1024)
    full = lambda z: pl.BlockSpec(z.shape, lambda bb, i: (0,) * z.ndim)
    return pl.pallas_call(
        functools.partial(_fn_last_kernel, scale=scale),
        grid=(b, l // tm),
        in_specs=[pl.BlockSpec((1, 1, tm, c), lambda bb, i: (bb, 0, i, 0)),
                  pl.BlockSpec((1, 1, tm, c), lambda bb, i: (bb, 1, i, 0)),
                  full(cbd), full(sbd), full(w), full(bias)],
        out_specs=pl.BlockSpec((1, tm, c), lambda bb, i: (bb, i, 0)),
        out_shape=jax.ShapeDtypeStruct((b, l, c), BF16),
        compiler_params=_cp(("parallel", "parallel")),
        name="fn_last",
    )(a, a, cbd, sbd, w, bias)


def _fn_small_kernel(u_ref, cl_ref, sl_ref, c_ref, s_ref, w_ref, b_ref, o_ref, *, scale):
    u = u_ref[0]
    uc = _dot(u, c_ref[...]).astype(BF16)
    us = _dot(u, s_ref[...]).astype(BF16)
    f = (_dot(cl_ref[...], uc) - _dot(sl_ref[...], us)) * scale
    o_ref[0] = (_dot(f.astype(BF16), w_ref[...]) + b_ref[...]).astype(BF16)


def _fn_small(zf, cl, sl, cbd, sbd, w, bias, scale):
    b, l, c = zf.shape
    sig = pl.BlockSpec((1, l, c), lambda bb: (bb, 0, 0))
    full = lambda z: pl.BlockSpec(z.shape, lambda bb: (0,) * z.ndim)
    return pl.pallas_call(
        functools.partial(_fn_small_kernel, scale=scale),
        grid=(b,),
        in_specs=[sig, full(cl), full(sl), full(cbd), full(sbd), full(w), full(bias)],
        out_specs=sig,
        out_shape=jax.ShapeDtypeStruct((b, l, c), BF16),
        compiler_params=_cp(("parallel",)),
        name="fn_small",
    )(zf, cl, sl, cbd, sbd, w, bias)


def _merge_kernel(oa_ref, ob_ref, oc_ref, od_ref, x_ref, g1_ref, gn_ref, w_ref, b_ref,
                  lng_ref, lnb_ref, o_ref, *, alpha):
    parts = []
    for idx, ref in enumerate((oa_ref, ob_ref, oc_ref, od_ref)):
        v = ref[0].astype(F32)
        ms = jnp.mean(v * v, axis=-1, keepdims=True)
        gain = gn_ref[:, idx * GROUP_W:(idx + 1) * GROUP_W]
        parts.append((v * lax.rsqrt(ms + LN_EPS) * gain).astype(BF16))
    y = _dot(jnp.concatenate(parts, axis=1), w_ref[...]) + b_ref[...]
    r = alpha * x_ref[0] + g1_ref[0] * y
    o_ref[0] = _layer_norm(r) * lng_ref[...] + lnb_ref[...]


def _merge(oa, ob, oc, od, x, g1, gn, w_out, b_out, ln_g, ln_b, alpha):
    b, l, d = x.shape
    tm = min(l, 512)
    grp = pl.BlockSpec((1, tm, GROUP_W), lambda bb, i: (bb, i, 0))
    tok = pl.BlockSpec((1, tm, d), lambda bb, i: (bb, i, 0))
    full = lambda a: pl.BlockSpec(a.shape, lambda bb, i: (0,) * a.ndim)
    return pl.pallas_call(
        functools.partial(_merge_kernel, alpha=alpha),
        grid=(b, l // tm),
        in_specs=[grp, grp, grp, grp, tok, pl.BlockSpec((1, 1, d), lambda bb, i: (bb, 0, 0)),
                  full(gn), full(w_out), full(b_out), full(ln_g), full(ln_b)],
        out_specs=tok,
        out_shape=jax.ShapeDtypeStruct((b, l, d), F32),
        compiler_params=_cp(("parallel", "parallel"), VMEM_LIMIT),
        name="merge",
    )(oa, ob, oc, od, x, g1, gn, w_out, b_out, ln_g, ln_b)


def _ffn_kernel(x_ref, xp_ref, xn_ref, sh_ref, sc_ref, g_ref, wup_ref, vec_ref, wd_ref, bd_ref,
                lng_ref, lnb_ref, o_ref, h_ref, acc_ref, u0_ref, u1_ref, *, alpha, tm, n_chunks, n_tiles):
    i = pl.program_id(1)
    sh, sc = sh_ref[0], sc_ref[0]
    cw = FFN_CHUNK

    def hmod(v):
        return (_layer_norm(v) * (1.0 + sc) + sh).astype(BF16)

    h_ref[0:HALO] = hmod(xp_ref[0])
    h_ref[HALO:HALO + tm] = hmod(x_ref[0])
    h_ref[HALO + tm:] = hmod(xn_ref[0])
    acc_ref[...] = jnp.zeros_like(acc_ref)
    row8 = lax.broadcasted_iota(jnp.int32, (8, 1), 0)
    pad_top = (row8 == 7) & (i == 0)
    pad_bot = (row8 == 0) & (i == n_tiles - 1)

    def up(c, dst):
        dst[...] = _dot(h_ref[...], wup_ref[c])

    def down(c, src):
        vec = vec_ref[c]
        bias, w0, w1, w2 = vec[0:1], vec[1:2], vec[2:3], vec[3:4]
        src[HALO - 8:HALO] = jnp.where(pad_top, -bias, src[HALO - 8:HALO])
        src[HALO + tm:HALO + tm + 8] = jnp.where(pad_bot, -bias, src[HALO + tm:HALO + tm + 8])
        conv = (src[HALO - 1:HALO - 1 + tm] * w0 + src[HALO:HALO + tm] * w1 + src[HALO + 1:HALO + 1 + tm] * w2
                + (vec[4:5] + bias * (w0 + w1 + w2)))
        a, g = conv[:, :cw], conv[:, cw:]
        act = (a * jax.nn.sigmoid(a) * g).astype(BF16)
        acc_ref[...] += _dot(act, wd_ref[c])

    up(0, u0_ref)

    def body(p, carry):
        c = 2 * p
        up(c + 1, u1_ref)
        down(c, u0_ref)
        up(c + 2, u0_ref)
        down(c + 1, u1_ref)
        return carry

    n_pairs = (n_chunks - 1) // 2
    lax.fori_loop(0, n_pairs, body, 0)
    c = 2 * n_pairs
    if c + 1 < n_chunks:
        up(c + 1, u1_ref)
    down(c, u0_ref)
    if c + 1 < n_chunks:
        down(c + 1, u1_ref)
    r = alpha * x_ref[0] + g_ref[0] * (acc_ref[...] + bd_ref[...])
    o_ref[0] = _layer_norm(r) * lng_ref[...] + lnb_ref[...]


def _ffn(x, sh, sc, g2, wup, vec, wd, b_down, ln_g, ln_b, alpha):
    b, l, d = x.shape
    tm = min(l, 512)
    n_chunks = wd.shape[0]
    main, prev, nxt = _halo_specs(tm, l, d)
    mod = pl.BlockSpec((1, 1, d), lambda bb, i: (bb, 0, 0))
    full = lambda a: pl.BlockSpec(a.shape, lambda bb, i: (0,) * a.ndim)
    return pl.pallas_call(
        functools.partial(_ffn_kernel, alpha=alpha, tm=tm, n_chunks=n_chunks, n_tiles=l // tm),
        grid=(b, l // tm),
        in_specs=[main, prev, nxt, mod, mod, mod, full(wup), full(vec), full(wd),
                  full(b_down), full(ln_g), full(ln_b)],
        out_specs=main,
        out_shape=jax.ShapeDtypeStruct((b, l, d), F32),
        scratch_shapes=[pltpu.VMEM((tm + 2 * HALO, d), BF16), pltpu.VMEM((tm, d), F32),
                        pltpu.VMEM((tm + 2 * HALO, 2 * FFN_CHUNK), F32),
                        pltpu.VMEM((tm + 2 * HALO, 2 * FFN_CHUNK), F32)],
        compiler_params=_cp(("parallel", "parallel"), VMEM_LIMIT),
        name="ffn",
    )(x, x, x, sh, sc, g2, wup, vec, wd, b_down, ln_g, ln_b)


def _cos_sin(idx, n):
    ang = (idx % n).astype(F32) * (2.0 * math.pi / n)
    return jnp.cos(ang), jnp.sin(ang)


def _cplx_rows(re, im):
    return jnp.concatenate([re, im], axis=-2)


def _cplx_block(re, im):
    return jnp.concatenate([jnp.concatenate([re, -im], axis=-1), jnp.concatenate([im, re], axis=-1)], axis=-2)


def _ar(n):
    return jnp.arange(n, dtype=jnp.int32)


def _hyena_tables(l):
    n, n2 = 2 * l, FFT_N2
    n1 = n // n2
    c, s = _cos_sin(_ar(n1)[:, None] * _ar(n1)[None, :], n1)
    f_full = _cplx_rows(c, -s).astype(BF16)
    g = (jnp.concatenate([c, -s], axis=1)[:n1 // 2] / n).astype(BF16)
    k1, k2, m2 = _ar(n1)[:, None, None], _ar(n2)[None, :, None], _ar(n2)[None, None, :]
    c, s = _cos_sin(m2 * k1 + n1 * m2 * k2, n)
    mf = _cplx_block(c, -s).astype(BF16)
    return dict(f_half=_lead_kron(f_full[:, :n1 // 2]), f_full=_lead_kron(f_full), g=_lead_kron(g),
                mf=mf, mi=jnp.swapaxes(mf, 1, 2))


def _fnet_tables(l):
    n2 = FFT_N2
    n1 = l // n2
    nblk = n1 // 8
    blk, j = _ar(nblk)[:, None, None, None], _ar(8)[None, :, None, None]
    k2, m2 = _ar(n2)[None, None, :, None], _ar(n2)[None, None, None, :]
    c, s = _cos_sin(n1 * m2 * k2 + (8 * blk + j) * k2, l)
    eye = jnp.eye(8, dtype=F32)
    emb = lambda t: jnp.einsum('bjkn,ji->bjkni', t, eye).reshape(nblk, 8 * n2, n2 * 8)
    first = jnp.concatenate([emb(c), emb(-s)], axis=1).astype(BF16)
    c, s = _cos_sin(_ar(n1)[:, None] * _ar(n1)[None, :], n1)
    second = _cplx_block(c, -s).astype(BF16)
    return dict(first=first, second=_lead_kron(second))


def _dense_tables(l):
    n = 2 * l
    c, s = _cos_sin(_ar(n)[:, None] * _ar(n)[None, :], n)
    ff = _cplx_rows(c, -s).astype(BF16)
    gd = (jnp.concatenate([c, -s], axis=1)[:l] / n).astype(BF16)
    cl, sl = _cos_sin(_ar(l)[:, None] * _ar(l)[None, :], l)
    return dict(fd=ff[:, :l], ff=ff, gd=gd, cl=cl.astype(BF16), sl=sl.astype(BF16))


def _rope_tables(l):
    rows = l // GRID_W
    row = jnp.broadcast_to(jnp.arange(rows)[:, None], (rows, GRID_W)).reshape(-1).astype(F32)
    col = jnp.broadcast_to(jnp.arange(GRID_W)[None, :], (rows, GRID_W)).reshape(-1).astype(F32)
    half = HEAD_DIM // 2
    inv = ROPE_THETA ** (-jnp.arange(0, half, 2, dtype=F32) / half)
    ar, ac = row[:, None] * inv, col[:, None] * inv
    cos = jnp.concatenate([jnp.cos(ar), jnp.cos(ar), jnp.cos(ac), jnp.cos(ac)], axis=1)
    sin = jnp.concatenate([-jnp.sin(ar), jnp.sin(ar), -jnp.sin(ac), jnp.sin(ac)], axis=1)
    return jnp.tile(cos, (1, 2)), jnp.tile(sin, (1, 2))


def kernel(x, c, ctx, c_ctx, w_ada, b_ada, w_in, sink_a, q_norm_g, k_norm_g, hy_conv_w, hy_conv_b,
           hy_f_w1, hy_f_b1, hy_f_freq, hy_f_w2, hy_f_b2, hy_f_w3, hy_skip, fnet_w, fnet_b,
           out_norm_g, w_out, b_out, ln1_g, ln1_b, ffn_w_up, ffn_b_up, ffn_conv_w, ffn_conv_b,
           ffn_w_down, ffn_b_down, ln2_g, ln2_b):
    bsz, l, d = x.shape
    lc = ctx.shape[1]
    depth = w_ada.shape[0]
    d_ff = ffn_w_down.shape[1]
    alpha = (2 * depth) ** 0.25
    n2 = FFT_N2
    n1h, n1f = 2 * l // n2, l // n2
    lanes = n2 * GROUP_W

    ht, ft, dt = _hyena_tables(l), _fnet_tables(l), _dense_tables(lc)
    cos_l, sin_l = _rope_tables(l)
    cos_c, sin_c = jnp.ones((lc, 128), F32), jnp.zeros((lc, 128), F32)
    jj = _ar(GROUP_W)
    same = (jj[:, None] // FNET_GROUP_DIM) == (jj[None, :] // FNET_GROUP_DIM)
    bd_mean = (same.astype(F32) / HEAD_DIM).astype(BF16)
    cg, sg = _cos_sin(jj[:, None] * jj[None, :], FNET_GROUP_DIM)
    cbd, sbd = jnp.where(same, cg, 0.0).astype(BF16), jnp.where(same, sg, 0.0).astype(BF16)

    n_rows = -(-(bsz + 1) // 8) * 8
    cc = jnp.zeros((n_rows, d), F32).at[:bsz].set(c).at[bsz].set(c_ctx)
    mod = _ada(cc, w_ada, b_ada)

    k_lat = _hy_filter(l, hy_f_w1, hy_f_b1, hy_f_freq, hy_f_w2, hy_f_b2, hy_f_w3)
    k_ctx = _hy_filter(lc, hy_f_w1, hy_f_b1, hy_f_freq, hy_f_w2, hy_f_b2, hy_f_w3)
    a_k = _lead_mm(ht['f_full'], k_lat.reshape(depth, n1h, n2, GROUP_W))
    k_spec = _hy_spectrum(a_k.reshape(depth, 2, n1h, n2, GROUP_W), ht['mf'])

    n_chunks = -(-d_ff // FFN_CHUNK)
    pad = n_chunks * FFN_CHUNK - d_ff

    def chunked(v):
        v = v.reshape(v.shape[:-1] + (2, d_ff))
        v = jnp.pad(v, [(0, 0)] * (v.ndim - 1) + [(0, pad)])
        v = v.reshape(v.shape[:-2] + (2, n_chunks, FFN_CHUNK))
        v = jnp.swapaxes(v, -3, -2).reshape(v.shape[:-3] + (n_chunks, 2 * FFN_CHUNK))
        return jnp.moveaxis(v, -2, 0)

    for layer in range(depth):
        m6 = mod[layer].reshape(n_rows, 6, d)
        lat = [m6[:bsz, i][:, None, :] for i in range(6)]
        con = [jnp.broadcast_to(m6[bsz, i][None, None, :], (bsz, 1, d)) for i in range(6)]
        w_in_l = w_in[layer].astype(BF16)
        qg = jnp.tile(q_norm_g[layer], N_HEADS).reshape(1, -1)
        kg = jnp.tile(k_norm_g[layer], N_KV).reshape(1, -1)
        w_out_l = w_out[layer].astype(BF16)
        gn, b_out_l = out_norm_g[layer].reshape(1, d), b_out[layer].reshape(1, d)
        l1g, l1b = ln1_g[layer].reshape(1, d), ln1_b[layer].reshape(1, d)
        l2g, l2b = ln2_g[layer].reshape(1, d), ln2_b[layer].reshape(1, d)
        fw, fb = fnet_w[layer].astype(BF16), fnet_b[layer].reshape(1, GROUP_W)
        skip = hy_skip[layer].reshape(1, GROUP_W)
        wup = chunked(ffn_w_up[layer]).astype(BF16)
        vec = chunked(jnp.concatenate([ffn_b_up[layer][None], ffn_conv_w[layer], ffn_conv_b[layer][None],
                                       jnp.zeros((3, 2 * d_ff), F32)], axis=0))
        wd = jnp.pad(ffn_w_down[layer], ((0, pad), (0, 0))).reshape(n_chunks, FFN_CHUNK, d).astype(BF16)
        bdn = ffn_b_down[layer].reshape(1, d)
        last = layer == depth - 1

        cqa, cka, cva, cqb, ckb, cvb, czh, czf = _proj(ctx, con[0], con[1], w_in_l, cos_c, sin_c,
                                                       bd_mean, qg, kg)
        qa, ka, va, qb, kb, vb, zh, zf = _proj(x, lat[0], lat[1], w_in_l, cos_l, sin_l, bd_mean, qg, kg)
        cat = lambda a, b_: jnp.concatenate([a, b_], axis=2)
        oa = _attention(qa, cat(ka, cka), cat(va, cva), sink_a[layer], banded=True, n_lat_keys=l)
        ob = _attention(qb, cat(kb, ckb), cat(vb, cvb), None, banded=False)

        vx, x0 = _hy_gate(zh, hy_conv_w[layer], hy_conv_b[layer])
        vx4, x04 = (a.reshape(bsz, n1h // 2, n2, GROUP_W) for a in (vx, x0))
        a1 = _lead_mm(ht['f_half'], vx4)
        bo = _hy_mid(a1.reshape(bsz, 2, n1h, n2, GROUP_W), ht['mf'], ht['mi'], k_spec, layer)
        oc = _hy_out(ht['g'], bo.reshape(bsz, 2 * n1h, n2, GROUP_W), vx4, x04, skip).reshape(bsz, l, GROUP_W)

        f1 = _fn_first(zf.reshape(bsz, n2, n1f, GROUP_W), ft['first'])
        f2 = _lead_mm(ft['second'], f1.reshape(bsz, 2 * n1f, n2, GROUP_W))
        od = _fn_last(f2.reshape(bsz, 2, l, GROUP_W), cbd, sbd, fw, fb, (l * FNET_GROUP_DIM) ** -0.5)

        x = _merge(oa, ob, oc, od, x, lat[2], gn, w_out_l, b_out_l, l1g, l1b, alpha)
        x = _ffn(x, lat[3], lat[4], lat[5], wup, vec, wd, bdn, l2g, l2b, alpha)

        if not last:
            oac = _attention(cqa, cka, cva, sink_a[layer], banded=False)
            obc = _attention(cqb, ckb, cvb, None, banded=False)
            cvx, cx0 = _hy_gate(czh, hy_conv_w[layer], hy_conv_b[layer])
            occ = _hy_small(cvx, cx0, k_ctx, layer, dt['fd'], dt['ff'], dt['gd'], skip)
            odc = _fn_small(czf, dt['cl'], dt['sl'], cbd, sbd, fw, fb, (lc * FNET_GROUP_DIM) ** -0.5)
            ctx = _merge(oac, obc, occ, odc, ctx, con[2], gn, w_out_l, b_out_l, l1g, l1b, alpha)
            ctx = _ffn(ctx, con[3], con[4], con[5], wup, vec, wd, bdn, l2g, l2b, alpha)
    return x
```

```python
import functools
import math

import jax
import jax.numpy as jnp
from jax import lax
from jax.experimental import pallas as pl
from jax.experimental.pallas import tpu as pltpu

F32 = jnp.float32
BF16 = jnp.bfloat16

HEAD_DIM = 64
GROUP_W = 256
N_HEADS = 4
N_KV = 2
WINDOW = 128
GRID_W = 64
ROPE_THETA = 10000.0
FNET_GROUP_DIM = 64
HYENA_BANDS = 8
HYENA_FAST_DECAY = 0.3
HYENA_SLOW_DECAY = 1.5
HYENA_TARGET = 1e-2
LN_EPS = 1e-6
NEG_INF = -1e30
FFT_N2 = 128
FFN_CHUNK = 256
HALO = 16
VMEM_LIMIT = 56 * 1024 * 1024


def _cp(sem, vmem=None):
    return pltpu.CompilerParams(dimension_semantics=sem, vmem_limit_bytes=vmem)


def _layer_norm(x):
    mu = jnp.mean(x, axis=-1, keepdims=True)
    xc = x - mu
    var = jnp.mean(xc * xc, axis=-1, keepdims=True)
    return xc * lax.rsqrt(var + LN_EPS)


def _dot(a, b):
    return jnp.dot(a, b, preferred_element_type=F32)


def _ada_kernel(c_ref, w_ref, b_ref, o_ref):
    c = c_ref[...]
    s = (c * jax.nn.sigmoid(c)).astype(BF16)
    o_ref[0] = _dot(s, w_ref[0].astype(BF16)) + b_ref[0]


def _ada(cc, w_ada, b_ada):
    depth, d, n = w_ada.shape
    r = cc.shape[0]
    tn = 1536
    return pl.pallas_call(
        _ada_kernel,
        grid=(depth, n // tn),
        in_specs=[pl.BlockSpec((r, d), lambda l, j: (0, 0)),
                  pl.BlockSpec((1, d, tn), lambda l, j: (l, 0, j)),
                  pl.BlockSpec((1, 1, tn), lambda l, j: (l, 0, j))],
        out_specs=pl.BlockSpec((1, r, tn), lambda l, j: (l, 0, j)),
        out_shape=jax.ShapeDtypeStruct((depth, r, n), F32),
        compiler_params=_cp(("parallel", "parallel"), VMEM_LIMIT),
        name="ada",
    )(cc, w_ada, b_ada.reshape(depth, 1, n))


def _swap_halves(z):
    w = z.shape[1]
    lane = lax.broadcasted_iota(jnp.int32, z.shape, 1)
    return jnp.where(lane % 32 < 16, pltpu.roll(z, w - 16, 1), pltpu.roll(z, 16, 1))


def _proj_kernel(x_ref, sh_ref, sc_ref, w_ref, cos_ref, sin_ref, bd_ref, qg_ref, kg_ref,
                 qa_ref, ka_ref, va_ref, qb_ref, kb_ref, vb_ref, zh_ref, zf_ref, *, n_parts):
    tm = x_ref.shape[1]
    rows = tm // n_parts
    scale = HEAD_DIM ** -0.5
    for part in range(n_parts):
        rs = slice(part * rows, (part + 1) * rows)
        h = (_layer_norm(x_ref[0, rs]) * (1.0 + sc_ref[0]) + sh_ref[0]).astype(BF16)
        cos1, sin1 = cos_ref[rs], sin_ref[rs]
        cos2 = jnp.concatenate([cos1, cos1], axis=1)
        sin2 = jnp.concatenate([sin1, sin1], axis=1)

        def proj(lo, width):
            return _dot(h, w_ref[:, lo:lo + width])

        def rope(z):
            if z.shape[1] == 128:
                return z * cos1 + _swap_halves(z) * sin1
            return z * cos2 + _swap_halves(z) * sin2

        def head_norm(z, g):
            w = z.shape[1]
            ms = _dot((z * z).astype(BF16), bd_ref[:w, :w])
            return z * lax.rsqrt(ms + LN_EPS) * g

        def put_heads(ref, z):
            for hh in range(z.shape[1] // HEAD_DIM):
                ref[0, hh, rs] = z[:, hh * HEAD_DIM:(hh + 1) * HEAD_DIM].astype(BF16)

        def put_values(ref, z):
            lane = lax.broadcasted_iota(jnp.int32, (z.shape[0], HEAD_DIM), 1)
            ones_col = jnp.where(lane == 0, 1.0, 0.0)
            for hh in range(z.shape[1] // HEAD_DIM):
                ref[0, hh, rs] = jnp.concatenate([z[:, hh * HEAD_DIM:(hh + 1) * HEAD_DIM], ones_col],
                                                 axis=1).astype(BF16)

        put_heads(qa_ref, rope(proj(0, 256)) * scale)
        kv = proj(256, 256)
        put_heads(ka_ref, rope(kv[:, :128]))
        put_values(va_ref, kv[:, 128:])
        put_heads(qb_ref, rope(head_norm(proj(512, 256), qg_ref[...])) * scale)
        kv = proj(768, 256)
        put_heads(kb_ref, rope(head_norm(kv[:, :128], kg_ref[...])))
        put_values(vb_ref, kv[:, 128:])
        zh_ref[0, rs] = proj(1024, 768).astype(BF16)
        zf_ref[0, rs] = proj(1792, 256).astype(BF16)


def _proj(x, sh, sc, w_in, cos, sin, bd, qg, kg):
    b, l, d = x.shape
    tm = min(l, 512)
    heads = lambda n, w=HEAD_DIM: jax.ShapeDtypeStruct((b, n, l, w), BF16)
    hspec = lambda n, w=HEAD_DIM: pl.BlockSpec((1, n, tm, w), lambda t, bb: (bb, 0, t, 0))
    tok = lambda w: pl.BlockSpec((1, tm, w), lambda t, bb: (bb, t, 0))
    vec = lambda w: pl.BlockSpec((1, 1, w), lambda t, bb: (bb, 0, 0))
    full = lambda a: pl.BlockSpec(a.shape, lambda t, bb: (0,) * a.ndim)
    tab = pl.BlockSpec((tm, 128), lambda t, bb: (t, 0))
    return pl.pallas_call(
        functools.partial(_proj_kernel, n_parts=1),
        grid=(l // tm, b),
        in_specs=[tok(d), vec(d), vec(d), full(w_in), tab, tab, full(bd), full(qg), full(kg)],
        out_specs=[hspec(4), hspec(2), hspec(2, 128), hspec(4), hspec(2), hspec(2, 128), tok(768), tok(256)],
        out_shape=[heads(4), heads(2), heads(2, 128), heads(4), heads(2), heads(2, 128),
                   jax.ShapeDtypeStruct((b, l, 768), BF16), jax.ShapeDtypeStruct((b, l, 256), BF16)],
        compiler_params=_cp(("parallel", "parallel"), VMEM_LIMIT),
        name="proj",
    )(x, sh, sc, w_in, cos, sin, bd, qg, kg)


_NT = (((1,), (1,)), ((), ()))


def _row_max(*parts):
    tiles = [p[:, i:i + 128] for p in parts for i in range(0, p.shape[1], 128)]
    m = tiles[0]
    for t in tiles[1:]:
        m = jnp.maximum(m, t)
    return jnp.max(m, axis=-1, keepdims=True)


def _sink_rows(sink_ref, j, tq):
    row = lax.broadcasted_iota(jnp.int32, (2 * tq, 1), 0)
    return jnp.where(row < tq, sink_ref[2 * j], sink_ref[2 * j + 1])


def _attn_finish(o_ref, acc, denom, tq):
    o = acc[:, :HEAD_DIM] / denom
    o_ref[0] = jnp.concatenate([o[:tq], o[tq:]], axis=1).astype(BF16)


def _attn_full_kernel(*refs, has_sink, has_ctx, tq, tk, n_t, n_chunks):
    if has_ctx:
        sink_ref, q_ref, k_ref, v_ref, kc_ref, vc_ref, o_ref, m_ref, acc_ref, s0_ref, s1_ref = refs
        l_ctx = kc_ref.shape[2]
    else:
        sink_ref, q_ref, k_ref, v_ref, o_ref, m_ref, acc_ref, s0_ref, s1_ref = refs
    bufs = (s0_ref, s1_ref)
    j = pl.program_id(1)
    lane = lax.broadcasted_iota(jnp.int32, acc_ref.shape[1:], 1)
    for tile in range(n_t):
        if has_sink:
            m_ref[tile] = _sink_rows(sink_ref, j, tq)
            acc_ref[tile] = jnp.where(lane == HEAD_DIM, 1.0, 0.0)
        else:
            m_ref[tile] = jnp.full(m_ref.shape[1:], NEG_INF, F32)
            acc_ref[tile] = jnp.zeros(acc_ref.shape[1:], F32)

    def chunk(ref, t):
        c = t // n_t
        off = c * tk if isinstance(c, int) else pl.multiple_of(c * tk, tk)
        return ref[0, 0, pl.ds(off, tk), :]

    def scores(t, tile, dst, ctx=False):
        q = q_ref[0, :, tile * tq:(tile + 1) * tq, :].reshape(2 * tq, HEAD_DIM)
        if ctx:
            dst[:, :l_ctx] = lax.dot_general(q, kc_ref[0, 0], _NT, preferred_element_type=F32)
        else:
            dst[...] = lax.dot_general(q, chunk(k_ref, t), _NT, preferred_element_type=F32)

    def update(t, tile, src, ctx=False):
        s = src[:, :l_ctx] if ctx else src[...]
        v = vc_ref[0, 0] if ctx else chunk(v_ref, t)
        m_prev = m_ref[tile]
        m_new = jnp.maximum(m_prev, _row_max(s))
        p = jnp.exp((s - m_new).astype(BF16))
        acc_ref[tile] = jnp.exp(m_prev - m_new) * acc_ref[tile] + _dot(p, v)
        m_ref[tile] = m_new

    total = n_t * n_chunks
    odd_tile = 1 % n_t
    scores(0, 0, s0_ref)

    def body(i, carry):
        t = 2 * i
        scores(t + 1, odd_tile, s1_ref)
        update(t, 0, s0_ref)
        scores(t + 2, 0, s0_ref)
        update(t + 1, odd_tile, s1_ref)
        return carry

    n_pairs = (total - 1) // 2
    lax.fori_loop(0, n_pairs, body, 0, unroll=2)
    first = 2 * n_pairs
    items = [(t, t % n_t, False) for t in range(first, total)]
    if has_ctx:
        items += [(None, tile, True) for tile in range(n_t)]
    for idx, (t, tile, ctx) in enumerate(items):
        if idx + 1 < len(items):
            nt, ntile, nctx = items[idx + 1]
            scores(nt, ntile, bufs[(first + idx + 1) % 2], nctx)
        update(t, tile, bufs[(first + idx) % 2], ctx)
    for tile in range(n_t):
        acc = acc_ref[tile]
        o = acc[:, :HEAD_DIM] * (1.0 / acc[:, HEAD_DIM:HEAD_DIM + 1])
        o_ref[0, tile * tq:(tile + 1) * tq, :] = jnp.concatenate([o[:tq], o[tq:]], axis=1).astype(BF16)


def _band_bias(tq):
    span = tq + 2 * WINDOW
    r = (_ar(2 * tq) % tq)[None, :, None]
    col = _ar(span)[None, None, :]
    shift = (_ar(3) * WINDOW)[:, None, None]
    return jnp.where(jnp.abs(col - shift - r) <= WINDOW, 0.0, NEG_INF).astype(F32)


def _attn_band_kernel(sink_ref, q_ref, k_ref, v_ref, kc_ref, vc_ref, bias_ref, o_ref, *,
                      tq, n_t, n_tiles, l_lat):
    j, g = pl.program_id(1), pl.program_id(2)
    span = tq + 2 * WINDOW
    sink = _sink_rows(sink_ref, j, tq)
    kc, vc = kc_ref[0, 0], vc_ref[0, 0]
    for tile in range(n_t):
        qi = g * n_t + tile
        start = pl.multiple_of(jnp.clip(qi * tq - WINDOW, 0, l_lat - span), WINDOW)
        kind = jnp.where(qi == 0, 0, jnp.where(qi == n_tiles - 1, 2, 1))
        q = q_ref[0, :, tile * tq:(tile + 1) * tq, :].reshape(2 * tq, HEAD_DIM)
        s_lat = (lax.dot_general(q, k_ref[0, 0, pl.ds(start, span), :], _NT, preferred_element_type=F32)
                 + bias_ref[kind])
        s_ctx = lax.dot_general(q, kc, _NT, preferred_element_type=F32)
        m = jnp.maximum(sink, _row_max(s_lat, s_ctx))
        acc = (_dot(jnp.exp((s_lat - m).astype(BF16)), v_ref[0, 0, pl.ds(start, span), :])
               + _dot(jnp.exp((s_ctx - m).astype(BF16)), vc))
        o = acc[:, :HEAD_DIM] * (1.0 / (acc[:, HEAD_DIM:HEAD_DIM + 1] + jnp.exp(sink - m)))
        o_ref[0, tile * tq:(tile + 1) * tq, :] = jnp.concatenate([o[:tq], o[tq:]], axis=1).astype(BF16)


def _attention(q, k, v, sink, *, banded, kc=None, vc=None):
    b, _, lq, _ = q.shape
    lk = k.shape[2]
    tq = 256
    has_sink = sink is not None
    if not has_sink:
        sink = jnp.zeros((N_HEADS,), F32)
    smem = pl.BlockSpec(memory_space=pltpu.SMEM)
    whole = lambda a: pl.BlockSpec((1, 1) + a.shape[2:], lambda bb, j, g: (bb, j, 0, 0))
    out_shape = jax.ShapeDtypeStruct((b, lq, GROUP_W), BF16)
    if banded:
        assert lq == lk and lk >= tq + 2 * WINDOW and tq % WINDOW == 0
        n_tiles = lq // tq
        n_t = next(t for t in (4, 2, 1) if n_tiles % t == 0)
        bias = _band_bias(tq)
        return pl.pallas_call(
            functools.partial(_attn_band_kernel, tq=tq, n_t=n_t, n_tiles=n_tiles, l_lat=lk),
            grid=(b, N_KV, n_tiles // n_t),
            in_specs=[smem,
                      pl.BlockSpec((1, 2, n_t * tq, HEAD_DIM), lambda bb, j, g: (bb, j, g, 0)),
                      whole(k), whole(v), whole(kc), whole(vc),
                      pl.BlockSpec(bias.shape, lambda bb, j, g: (0, 0, 0))],
            out_specs=pl.BlockSpec((1, n_t * tq, 128), lambda bb, j, g: (bb, g, j)),
            out_shape=out_shape,
            compiler_params=_cp(("parallel", "parallel", "parallel"), VMEM_LIMIT),
            name="attn_band",
        )(sink, q, k, v, kc, vc, bias)
    has_ctx = kc is not None
    tk = next(t for t in (1024, 768, 512, 256) if lk % t == 0)
    assert not has_ctx or kc.shape[2] <= tk
    n_t = 2 if lq % (2 * tq) == 0 else 1
    kern = functools.partial(_attn_full_kernel, has_sink=has_sink, has_ctx=has_ctx, tq=tq, tk=tk, n_t=n_t,
                             n_chunks=lk // tk)
    extra = (kc, vc) if has_ctx else ()
    return pl.pallas_call(
        kern,
        grid=(b, N_KV, lq // (n_t * tq)),
        in_specs=[smem,
                  pl.BlockSpec((1, 2, n_t * tq, HEAD_DIM), lambda bb, j, g: (bb, j, g, 0)),
                  whole(k), whole(v), *[whole(a) for a in extra]],
        out_specs=pl.BlockSpec((1, n_t * tq, 128), lambda bb, j, g: (bb, g, j)),
        out_shape=out_shape,
        scratch_shapes=[pltpu.VMEM((n_t, 2 * tq, 1), F32), pltpu.VMEM((n_t, 2 * tq, 2 * HEAD_DIM), F32),
                        pltpu.VMEM((2 * tq, tk), F32), pltpu.VMEM((2 * tq, tk), F32)],
        compiler_params=_cp(("parallel", "parallel", "parallel"), VMEM_LIMIT),
        name="attn_full",
    )(sink, q, k, v, *extra)


def _shift_rows(z, prev_row, next_row):
    n = z.shape[0]
    row = lax.broadcasted_iota(jnp.int32, (n, 1), 0)
    dn = jnp.where(row == 0, prev_row, pltpu.roll(z, 1, 0))
    up = jnp.where(row == n - 1, next_row, pltpu.roll(z, n - 1, 0))
    return dn, up


def _hy_gate_kernel(z_ref, zp_ref, zn_ref, w_ref, b_ref, vx_ref, x0_ref, *, n_tiles):
    i = pl.program_id(1)
    z = z_ref[0].astype(F32)
    prev_row = zp_ref[0].astype(F32)[HALO - 1:HALO] * (i > 0).astype(F32)
    next_row = zn_ref[0].astype(F32)[0:1] * (i < n_tiles - 1).astype(F32)
    dn, up = _shift_rows(z, prev_row, next_row)
    w = w_ref[...]
    u = dn * w[0:1] + z * w[1:2] + up * w[2:3] + b_ref[...]
    x0_ref[0] = u[:, :GROUP_W].astype(BF16)
    vx_ref[0] = (u[:, 2 * GROUP_W:] * u[:, GROUP_W:2 * GROUP_W]).astype(BF16)


def _halo_specs(tl, l, w):
    nb = l // HALO
    per = tl // HALO
    return (pl.BlockSpec((1, tl, w), lambda bb, i: (bb, i, 0)),
            pl.BlockSpec((1, HALO, w), lambda bb, i: (bb, jnp.maximum(i * per - 1, 0), 0)),
            pl.BlockSpec((1, HALO, w), lambda bb, i: (bb, jnp.minimum((i + 1) * per, nb - 1), 0)))


def _hy_gate(zh, conv_w, conv_b):
    b, l, w = zh.shape
    tl = min(l, 512)
    out = jax.ShapeDtypeStruct((b, l, GROUP_W), BF16)
    ospec = pl.BlockSpec((1, tl, GROUP_W), lambda bb, i: (bb, i, 0))
    return pl.pallas_call(
        functools.partial(_hy_gate_kernel, n_tiles=l // tl),
        grid=(b, l // tl),
        in_specs=[*_halo_specs(tl, l, w),
                  pl.BlockSpec((3, w), lambda bb, i: (0, 0)),
                  pl.BlockSpec((1, w), lambda bb, i: (0, 0))],
        out_specs=[ospec, ospec],
        out_shape=[out, out],
        compiler_params=_cp(("parallel", "parallel")),
        name="hy_gate",
    )(zh, zh, zh, conv_w, conv_b.reshape(1, w))


def _hy_filter_kernel(fc_ref, w1_ref, b1_ref, fr_ref, w2_ref, b2_ref, w3_ref, dl_ref, k_ref, *, l, tr):
    base = pl.program_id(1) * tr

    def position(m):
        return jnp.where(m < l, m, 2 * l - m).astype(F32)

    m_row = base + lax.broadcasted_iota(jnp.int32, (1, tr), 1)
    t_row = position(m_row)
    feat_id = lax.broadcasted_iota(jnp.int32, (fc_ref.shape[0], 1), 0)
    phase = fc_ref[...] * t_row
    feat = jnp.where(feat_id == 0, t_row / max(l - 1, 1),
                     jnp.where(feat_id <= HYENA_BANDS, jnp.cos(phase),
                               jnp.where(feat_id <= 2 * HYENA_BANDS, -jnp.sin(phase), 0.0)))
    hp = functools.partial(jnp.dot, precision=lax.Precision.HIGHEST, preferred_element_type=F32)
    fr = fr_ref[0]
    h = jnp.sin(fr * (hp(w1_ref[0], feat) + b1_ref[0]))
    h = jnp.sin(fr * (hp(w2_ref[0], h) + b2_ref[0]))
    h = hp(h.T, w3_ref[0])
    m = base + lax.broadcasted_iota(jnp.int32, (tr, 1), 0)
    dec = jnp.exp(-(position(m) / max(l - 1, 1)) * dl_ref[...])
    hf, hb = h[:, :GROUP_W] * dec, h[:, GROUP_W:] * dec
    k = jnp.where(m < l, hf, hb)
    k = jnp.where(m == 0, hf + hb, k)
    k_ref[0] = jnp.where(m == l, 0.0, k).astype(BF16)


def _hy_filter(l, w1, b1, fr, w2, b2, w3):
    depth, emb, hid = w1.shape
    tr = min(2 * l, 2048)
    n_feat = -(-emb // 8) * 8
    bands = jnp.linspace(1e-4, HYENA_BANDS - 1, HYENA_BANDS, dtype=F32) * (2.0 * math.pi / l)
    fc = jnp.zeros((n_feat, 1), F32).at[1:1 + HYENA_BANDS, 0].set(bands).at[1 + HYENA_BANDS:emb, 0].set(bands)
    w1t = jnp.zeros((depth, hid, n_feat), F32).at[:, :, :emb].set(jnp.swapaxes(w1, 1, 2))
    w2t = jnp.swapaxes(w2, 1, 2)
    min_decay = math.log(HYENA_TARGET) / HYENA_SLOW_DECAY
    max_decay = math.log(HYENA_TARGET) / HYENA_FAST_DECAY
    deltas = jnp.abs(jnp.linspace(min_decay, max_decay, GROUP_W, dtype=F32)).reshape(1, GROUP_W)
    lay = lambda a: pl.BlockSpec((1,) + a.shape[1:], lambda d, i: (d,) + (0,) * (a.ndim - 1))
    fix = lambda a: pl.BlockSpec(a.shape, lambda d, i: (0,) * a.ndim)
    b1, fr, b2 = (a.reshape(depth, hid, 1) for a in (b1, fr, b2))
    return pl.pallas_call(
        functools.partial(_hy_filter_kernel, l=l, tr=tr),
        grid=(depth, 2 * l // tr),
        in_specs=[fix(fc), lay(w1t), lay(b1), lay(fr), lay(w2t), lay(b2), lay(w3), fix(deltas)],
        out_specs=pl.BlockSpec((1, tr, GROUP_W), lambda d, i: (d, i, 0)),
        out_shape=jax.ShapeDtypeStruct((depth, 2 * l, GROUP_W), BF16),
        compiler_params=_cp(("parallel", "parallel")),
        name="hy_filter",
    )(fc, w1t, b1, fr, w2t, b2, w3, deltas)


ROW_TILE = 8


def _lead_kron(f):
    return jnp.kron(f.astype(F32), jnp.eye(ROW_TILE, dtype=F32)).astype(BF16)


def _lead_mm_kernel(f_ref, x_ref, o_ref):
    k, j, c = x_ref.shape[1:]
    y = _dot(f_ref[...], x_ref[0].reshape(k * j, c))
    o_ref[0] = y.reshape(-1, j, c).astype(o_ref.dtype)


def _lead_mm(fk, x):
    b, k, n, c = x.shape
    m = fk.shape[0] // ROW_TILE
    return pl.pallas_call(
        _lead_mm_kernel,
        grid=(b, n // ROW_TILE),
        in_specs=[pl.BlockSpec(fk.shape, lambda bb, i: (0, 0)),
                  pl.BlockSpec((1, k, ROW_TILE, c), lambda bb, i: (bb, 0, i, 0))],
        out_specs=pl.BlockSpec((1, m, ROW_TILE, c), lambda bb, i: (bb, 0, i, 0)),
        out_shape=jax.ShapeDtypeStruct((b, m, n, c), BF16),
        compiler_params=_cp(("parallel", "parallel"), VMEM_LIMIT),
        name="lead_mm",
    )(fk, x)


def _hy_spec_kernel(a_ref, mf_ref, k_ref):
    x = a_ref[0, :, 0].reshape(2 * FFT_N2, GROUP_W)
    k_ref[0, 0] = _dot(mf_ref[0], x)


def _hy_mid_kernel(a_ref, mf_ref, mi_ref, k_ref, o_ref):
    n2 = FFT_N2
    k = k_ref[0, 0]
    kr, ki = k[:n2], k[n2:]
    for bb in range(a_ref.shape[0]):
        x = a_ref[bb, :, 0].reshape(2 * n2, GROUP_W)
        y = _dot(mf_ref[0], x)
        yr, yi = y[:n2], y[n2:]
        z = jnp.concatenate([yr * kr - yi * ki, yr * ki + yi * kr], axis=0).astype(BF16)
        o_ref[bb, :, 0] = _dot(mi_ref[0], z).astype(BF16).reshape(2, n2, GROUP_W)


def _hy_spectrum(a, mf):
    depth, _, n1, n2, w = a.shape
    return pl.pallas_call(
        _hy_spec_kernel,
        grid=(n1, depth),
        in_specs=[pl.BlockSpec((1, 2, 1, n2, w), lambda k1, d: (d, 0, k1, 0, 0)),
                  pl.BlockSpec((1, 2 * n2, 2 * n2), lambda k1, d: (k1, 0, 0))],
        out_specs=pl.BlockSpec((1, 1, 2 * n2, w), lambda k1, d: (d, k1, 0, 0)),
        out_shape=jax.ShapeDtypeStruct((depth, n1, 2 * n2, w), F32),
        compiler_params=_cp(("parallel", "parallel")),
        name="hy_spectrum",
    )(a, mf)


def _hy_mid(a, mf, mi, kspec, layer):
    b, _, n1, n2, w = a.shape
    blk = pl.BlockSpec((b, 2, 1, n2, w), lambda k1: (0, 0, k1, 0, 0))
    mat = pl.BlockSpec((1, 2 * n2, 2 * n2), lambda k1: (k1, 0, 0))
    return pl.pallas_call(
        _hy_mid_kernel,
        grid=(n1,),
        in_specs=[blk, mat, mat,
                  pl.BlockSpec((1, 1, 2 * n2, w), lambda k1: (layer, k1, 0, 0))],
        out_specs=blk,
        out_shape=jax.ShapeDtypeStruct(a.shape, BF16),
        compiler_params=_cp(("parallel",)),
        name="hy_mid",
    )(a, mf, mi, kspec)


def _hy_out_kernel(g_ref, b_ref, vx_ref, x0_ref, skip_ref, o_ref):
    k, j, c = b_ref.shape[1:]
    y = _dot(g_ref[...], b_ref[0].reshape(k * j, c)).reshape(-1, j, c)
    vx = vx_ref[0].astype(F32)
    o_ref[0] = ((y + skip_ref[...] * vx) * x0_ref[0].astype(F32)).astype(BF16)


def _hy_out(gk, bo, vx, x0, skip):
    b, k, n, c = bo.shape
    m = gk.shape[0] // ROW_TILE
    sig = pl.BlockSpec((1, m, ROW_TILE, c), lambda bb, i: (bb, 0, i, 0))
    return pl.pallas_call(
        _hy_out_kernel,
        grid=(b, n // ROW_TILE),
        in_specs=[pl.BlockSpec(gk.shape, lambda bb, i: (0, 0)),
                  pl.BlockSpec((1, k, ROW_TILE, c), lambda bb, i: (bb, 0, i, 0)),
                  sig, sig, pl.BlockSpec((1, c), lambda bb, i: (0, 0))],
        out_specs=sig,
        out_shape=jax.ShapeDtypeStruct((b, m, n, c), BF16),
        compiler_params=_cp(("parallel", "parallel"), VMEM_LIMIT),
        name="hy_out",
    )(gk, bo, vx, x0, skip)


def _hy_small_kernel(vx_ref, x0_ref, k_ref, fd_ref, ff_ref, gd_ref, skip_ref, o_ref):
    nf = fd_ref.shape[0] // 2
    vx = vx_ref[0]
    u = _dot(fd_ref[...], vx)
    k = _dot(ff_ref[...], k_ref[0])
    ur, ui, kr, ki = u[:nf], u[nf:], k[:nf], k[nf:]
    z = jnp.concatenate([ur * kr - ui * ki, ur * ki + ui * kr], axis=0).astype(BF16)
    y = _dot(gd_ref[...], z)
    o_ref[0] = ((y + skip_ref[...] * vx.astype(F32)) * x0_ref[0].astype(F32)).astype(BF16)


def _hy_small(vx, x0, kfilt, layer, fd, ff, gd, skip):
    b, l, w = vx.shape
    sig = pl.BlockSpec((1, l, w), lambda bb: (bb, 0, 0))
    full = lambda a: pl.BlockSpec(a.shape, lambda bb: (0,) * a.ndim)
    return pl.pallas_call(
        _hy_small_kernel,
        grid=(b,),
        in_specs=[sig, sig, pl.BlockSpec((1, 2 * l, w), lambda bb: (layer, 0, 0)),
                  full(fd), full(ff), full(gd), full(skip)],
        out_specs=sig,
        out_shape=jax.ShapeDtypeStruct((b, l, w), BF16),
        compiler_params=_cp(("parallel",)),
        name="hy_small",
    )(vx, x0, kfilt, fd, ff, gd, skip)


def _fn_first_kernel(z_ref, m_ref, o_ref):
    n2 = z_ref.shape[1]
    x = z_ref[0].reshape(n2 * 8, GROUP_W)
    o_ref[0] = _dot(m_ref[0], x).astype(BF16).reshape(2, 8, n2, GROUP_W)


def _fn_first(zf, mats):
    b, n2, n1, w = zf.shape
    return pl.pallas_call(
        _fn_first_kernel,
        grid=(n1 // 8, b),
        in_specs=[pl.BlockSpec((1, n2, 8, w), lambda i, bb: (bb, 0, i, 0)),
                  pl.BlockSpec((1, 16 * n2, 8 * n2), lambda i, bb: (i, 0, 0))],
        out_specs=pl.BlockSpec((1, 2, 8, n2, w), lambda i, bb: (bb, 0, i, 0, 0)),
        out_shape=jax.ShapeDtypeStruct((b, 2, n1, n2, w), BF16),
        compiler_params=_cp(("parallel", "parallel"), VMEM_LIMIT),
        name="fn_first",
    )(zf, mats)


def _fn_last_kernel(ar_ref, ai_ref, c_ref, s_ref, w_ref, b_ref, o_ref, *, scale):
    t = (_dot(ar_ref[0, 0], c_ref[...]) + _dot(ai_ref[0, 0], s_ref[...])) * scale
    o_ref[0] = (_dot(t.astype(BF16), w_ref[...]) + b_ref[...]).astype(BF16)


def _fn_last(a, cbd, sbd, w, bias, scale):
    b, _, l, c = a.shape
    tm = min(l, 1024)
    full = lambda z: pl.BlockSpec(z.shape, lambda bb, i: (0,) * z.ndim)
    return pl.pallas_call(
        functools.partial(_fn_last_kernel, scale=scale),
        grid=(b, l // tm),
        in_specs=[pl.BlockSpec((1, 1, tm, c), lambda bb, i: (bb, 0, i, 0)),
                  pl.BlockSpec((1, 1, tm, c), lambda bb, i: (bb, 1, i, 0)),
                  full(cbd), full(sbd), full(w), full(bias)],
        out_specs=pl.BlockSpec((1, tm, c), lambda bb, i: (bb, i, 0)),
        out_shape=jax.ShapeDtypeStruct((b, l, c), BF16),
        compiler_params=_cp(("parallel", "parallel")),
        name="fn_last",
    )(a, a, cbd, sbd, w, bias)


def _fn_small_kernel(u_ref, cl_ref, sl_ref, c_ref, s_ref, w_ref, b_ref, o_ref, *, scale):
    u = u_ref[0]
    uc = _dot(u, c_ref[...]).astype(BF16)
    us = _dot(u, s_ref[...]).astype(BF16)
    f = (_dot(cl_ref[...], uc) - _dot(sl_ref[...], us)) * scale
    o_ref[0] = (_dot(f.astype(BF16), w_ref[...]) + b_ref[...]).astype(BF16)


def _fn_small(zf, cl, sl, cbd, sbd, w, bias, scale):
    b, l, c = zf.shape
    sig = pl.BlockSpec((1, l, c), lambda bb: (bb, 0, 0))
    full = lambda z: pl.BlockSpec(z.shape, lambda bb: (0,) * z.ndim)
    return pl.pallas_call(
        functools.partial(_fn_small_kernel, scale=scale),
        grid=(b,),
        in_specs=[sig, full(cl), full(sl), full(cbd), full(sbd), full(w), full(bias)],
        out_specs=sig,
        out_shape=jax.ShapeDtypeStruct((b, l, c), BF16),
        compiler_params=_cp(("parallel",)),
        name="fn_small",
    )(zf, cl, sl, cbd, sbd, w, bias)


def _merge_kernel(oa_ref, ob_ref, oc_ref, od_ref, x_ref, g1_ref, gn_ref, w_ref, b_ref,
                  lng_ref, lnb_ref, o_ref, *, alpha):
    parts = []
    for idx, ref in enumerate((oa_ref, ob_ref, oc_ref, od_ref)):
        v = ref[0].astype(F32)
        ms = jnp.mean(v * v, axis=-1, keepdims=True)
        gain = gn_ref[:, idx * GROUP_W:(idx + 1) * GROUP_W]
        parts.append((v * lax.rsqrt(ms + LN_EPS) * gain).astype(BF16))
    y = _dot(jnp.concatenate(parts, axis=1), w_ref[...]) + b_ref[...]
    r = alpha * x_ref[0] + g1_ref[0] * y
    o_ref[0] = _layer_norm(r) * lng_ref[...] + lnb_ref[...]


def _merge(oa, ob, oc, od, x, g1, gn, w_out, b_out, ln_g, ln_b, alpha):
    b, l, d = x.shape
    tm = min(l, 512)
    grp = pl.BlockSpec((1, tm, GROUP_W), lambda bb, i: (bb, i, 0))
    tok = pl.BlockSpec((1, tm, d), lambda bb, i: (bb, i, 0))
    full = lambda a: pl.BlockSpec(a.shape, lambda bb, i: (0,) * a.ndim)
    return pl.pallas_call(
        functools.partial(_merge_kernel, alpha=alpha),
        grid=(b, l // tm),
        in_specs=[grp, grp, grp, grp, tok, pl.BlockSpec((1, 1, d), lambda bb, i: (bb, 0, 0)),
                  full(gn), full(w_out), full(b_out), full(ln_g), full(ln_b)],
        out_specs=tok,
        out_shape=jax.ShapeDtypeStruct((b, l, d), F32),
        compiler_params=_cp(("parallel", "parallel"), VMEM_LIMIT),
        name="merge",
    )(oa, ob, oc, od, x, g1, gn, w_out, b_out, ln_g, ln_b)


def _ffn_kernel(x_ref, xp_ref, xn_ref, sh_ref, sc_ref, g_ref, wup_ref, vec_ref, wd_ref, bd_ref,
                lng_ref, lnb_ref, o_ref, h_ref, acc_ref, u0_ref, u1_ref, *, alpha, tm, n_chunks, n_tiles):
    i = pl.program_id(1)
    sh, sc = sh_ref[0], sc_ref[0]
    cw = FFN_CHUNK

    def hmod(v):
        return (_layer_norm(v) * (1.0 + sc) + sh).astype(BF16)

    h_ref[0:HALO] = hmod(xp_ref[0])
    h_ref[HALO:HALO + tm] = hmod(x_ref[0])
    h_ref[HALO + tm:] = hmod(xn_ref[0])
    acc_ref[...] = jnp.zeros_like(acc_ref)
    row8 = lax.broadcasted_iota(jnp.int32, (8, 1), 0)
    pad_top = (row8 == 7) & (i == 0)
    pad_bot = (row8 == 0) & (i == n_tiles - 1)

    def up(c, dst):
        dst[...] = _dot(h_ref[...], wup_ref[c])

    def down(c, src):
        vec = vec_ref[c]
        bias, w0, w1, w2 = vec[0:1], vec[1:2], vec[2:3], vec[3:4]
        src[HALO - 8:HALO] = jnp.where(pad_top, -bias, src[HALO - 8:HALO])
        src[HALO + tm:HALO + tm + 8] = jnp.where(pad_bot, -bias, src[HALO + tm:HALO + tm + 8])
        conv = (src[HALO - 1:HALO - 1 + tm] * w0 + src[HALO:HALO + tm] * w1 + src[HALO + 1:HALO + 1 + tm] * w2
                + (vec[4:5] + bias * (w0 + w1 + w2)))
        a, g = conv[:, :cw], conv[:, cw:]
        act = (a * jax.nn.sigmoid(a) * g).astype(BF16)
        acc_ref[...] += _dot(act, wd_ref[c])

    up(0, u0_ref)

    def body(p, carry):
        c = 2 * p
        up(c + 1, u1_ref)
        down(c, u0_ref)
        up(c + 2, u0_ref)
        down(c + 1, u1_ref)
        return carry

    n_pairs = (n_chunks - 1) // 2
    lax.fori_loop(0, n_pairs, body, 0)
    c = 2 * n_pairs
    if c + 1 < n_chunks:
        up(c + 1, u1_ref)
    down(c, u0_ref)
    if c + 1 < n_chunks:
        down(c + 1, u1_ref)
    r = alpha * x_ref[0] + g_ref[0] * (acc_ref[...] + bd_ref[...])
    o_ref[0] = _layer_norm(r) * lng_ref[...] + lnb_ref[...]


def _ffn(x, sh, sc, g2, wup, vec, wd, b_down, ln_g, ln_b, alpha):
    b, l, d = x.shape
    tm = min(l, 512)
    n_chunks = wd.shape[0]
    main, prev, nxt = _halo_specs(tm, l, d)
    mod = pl.BlockSpec((1, 1, d), lambda bb, i: (bb, 0, 0))
    full = lambda a: pl.BlockSpec(a.shape, lambda bb, i: (0,) * a.ndim)
    return pl.pallas_call(
        functools.partial(_ffn_kernel, alpha=alpha, tm=tm, n_chunks=n_chunks, n_tiles=l // tm),
        grid=(b, l // tm),
        in_specs=[main, prev, nxt, mod, mod, mod, full(wup), full(vec), full(wd),
                  full(b_down), full(ln_g), full(ln_b)],
        out_specs=main,
        out_shape=jax.ShapeDtypeStruct((b, l, d), F32),
        scratch_shapes=[pltpu.VMEM((tm + 2 * HALO, d), BF16), pltpu.VMEM((tm, d), F32),
                        pltpu.VMEM((tm + 2 * HALO, 2 * FFN_CHUNK), F32),
                        pltpu.VMEM((tm + 2 * HALO, 2 * FFN_CHUNK), F32)],
        compiler_params=_cp(("parallel", "parallel"), VMEM_LIMIT),
        name="ffn",
    )(x, x, x, sh, sc, g2, wup, vec, wd, b_down, ln_g, ln_b)


def _cos_sin(idx, n):
    ang = (idx % n).astype(F32) * (2.0 * math.pi / n)
    return jnp.cos(ang), jnp.sin(ang)


def _cplx_rows(re, im):
    return jnp.concatenate([re, im], axis=-2)


def _cplx_block(re, im):
    return jnp.concatenate([jnp.concatenate([re, -im], axis=-1), jnp.concatenate([im, re], axis=-1)], axis=-2)


def _ar(n):
    return jnp.arange(n, dtype=jnp.int32)


def _hyena_tables(l):
    n, n2 = 2 * l, FFT_N2
    n1 = n // n2
    c, s = _cos_sin(_ar(n1)[:, None] * _ar(n1)[None, :], n1)
    f_full = _cplx_rows(c, -s).astype(BF16)
    g = (jnp.concatenate([c, -s], axis=1)[:n1 // 2] / n).astype(BF16)
    k1, k2, m2 = _ar(n1)[:, None, None], _ar(n2)[None, :, None], _ar(n2)[None, None, :]
    c, s = _cos_sin(m2 * k1 + n1 * m2 * k2, n)
    mf = _cplx_block(c, -s).astype(BF16)
    return dict(f_half=_lead_kron(f_full[:, :n1 // 2]), f_full=_lead_kron(f_full), g=_lead_kron(g),
                mf=mf, mi=jnp.swapaxes(mf, 1, 2))


def _fnet_tables(l):
    n2 = FFT_N2
    n1 = l // n2
    nblk = n1 // 8
    blk, j = _ar(nblk)[:, None, None, None], _ar(8)[None, :, None, None]
    k2, m2 = _ar(n2)[None, None, :, None], _ar(n2)[None, None, None, :]
    c, s = _cos_sin(n1 * m2 * k2 + (8 * blk + j) * k2, l)
    eye = jnp.eye(8, dtype=F32)
    emb = lambda t: jnp.einsum('bjkn,ji->bjkni', t, eye).reshape(nblk, 8 * n2, n2 * 8)
    first = jnp.concatenate([emb(c), emb(-s)], axis=1).astype(BF16)
    c, s = _cos_sin(_ar(n1)[:, None] * _ar(n1)[None, :], n1)
    second = _cplx_block(c, -s).astype(BF16)
    return dict(first=first, second=_lead_kron(second))


def _dense_tables(l):
    n = 2 * l
    c, s = _cos_sin(_ar(n)[:, None] * _ar(n)[None, :], n)
    ff = _cplx_rows(c, -s).astype(BF16)
    gd = (jnp.concatenate([c, -s], axis=1)[:l] / n).astype(BF16)
    cl, sl = _cos_sin(_ar(l)[:, None] * _ar(l)[None, :], l)
    return dict(fd=ff[:, :l], ff=ff, gd=gd, cl=cl.astype(BF16), sl=sl.astype(BF16))


def _rope_tables(l):
    rows = l // GRID_W
    row = jnp.broadcast_to(jnp.arange(rows)[:, None], (rows, GRID_W)).reshape(-1).astype(F32)
    col = jnp.broadcast_to(jnp.arange(GRID_W)[None, :], (rows, GRID_W)).reshape(-1).astype(F32)
    half = HEAD_DIM // 2
    inv = ROPE_THETA ** (-jnp.arange(0, half, 2, dtype=F32) / half)
    ar, ac = row[:, None] * inv, col[:, None] * inv
    cos = jnp.concatenate([jnp.cos(ar), jnp.cos(ar), jnp.cos(ac), jnp.cos(ac)], axis=1)
    sin = jnp.concatenate([-jnp.sin(ar), jnp.sin(ar), -jnp.sin(ac), jnp.sin(ac)], axis=1)
    return jnp.tile(cos, (1, 2)), jnp.tile(sin, (1, 2))


def kernel(x, c, ctx, c_ctx, w_ada, b_ada, w_in, sink_a, q_norm_g, k_norm_g, hy_conv_w, hy_conv_b,
           hy_f_w1, hy_f_b1, hy_f_freq, hy_f_w2, hy_f_b2, hy_f_w3, hy_skip, fnet_w, fnet_b,
           out_norm_g, w_out, b_out, ln1_g, ln1_b, ffn_w_up, ffn_b_up, ffn_conv_w, ffn_conv_b,
           ffn_w_down, ffn_b_down, ln2_g, ln2_b):
    bsz, l, d = x.shape
    lc = ctx.shape[1]
    depth = w_ada.shape[0]
    d_ff = ffn_w_down.shape[1]
    alpha = (2 * depth) ** 0.25
    n2 = FFT_N2
    n1h, n1f = 2 * l // n2, l // n2
    lanes = n2 * GROUP_W

    ht, ft, dt = _hyena_tables(l), _fnet_tables(l), _dense_tables(lc)
    cos_l, sin_l = _rope_tables(l)
    cos_c, sin_c = jnp.ones((lc, 128), F32), jnp.zeros((lc, 128), F32)
    jj = _ar(GROUP_W)
    same = (jj[:, None] // FNET_GROUP_DIM) == (jj[None, :] // FNET_GROUP_DIM)
    bd_mean = (same.astype(F32) / HEAD_DIM).astype(BF16)
    cg, sg = _cos_sin(jj[:, None] * jj[None, :], FNET_GROUP_DIM)
    cbd, sbd = jnp.where(same, cg, 0.0).astype(BF16), jnp.where(same, sg, 0.0).astype(BF16)

    n_rows = -(-(bsz + 1) // 8) * 8
    cc = jnp.zeros((n_rows, d), F32).at[:bsz].set(c).at[bsz].set(c_ctx)
    mod = _ada(cc, w_ada, b_ada)

    k_lat = _hy_filter(l, hy_f_w1, hy_f_b1, hy_f_freq, hy_f_w2, hy_f_b2, hy_f_w3)
    k_ctx = _hy_filter(lc, hy_f_w1, hy_f_b1, hy_f_freq, hy_f_w2, hy_f_b2, hy_f_w3)
    a_k = _lead_mm(ht['f_full'], k_lat.reshape(depth, n1h, n2, GROUP_W))
    k_spec = _hy_spectrum(a_k.reshape(depth, 2, n1h, n2, GROUP_W), ht['mf'])

    n_chunks = -(-d_ff // FFN_CHUNK)
    pad = n_chunks * FFN_CHUNK - d_ff

    def chunked(v):
        v = v.reshape(v.shape[:-1] + (2, d_ff))
        v = jnp.pad(v, [(0, 0)] * (v.ndim - 1) + [(0, pad)])
        v = v.reshape(v.shape[:-2] + (2, n_chunks, FFN_CHUNK))
        v = jnp.swapaxes(v, -3, -2).reshape(v.shape[:-3] + (n_chunks, 2 * FFN_CHUNK))
        return jnp.moveaxis(v, -2, 0)

    for layer in range(depth):
        m6 = mod[layer].reshape(n_rows, 6, d)
        lat = [m6[:bsz, i][:, None, :] for i in range(6)]
        con = [jnp.broadcast_to(m6[bsz, i][None, None, :], (bsz, 1, d)) for i in range(6)]
        w_in_l = w_in[layer].astype(BF16)
        qg = jnp.tile(q_norm_g[layer], N_HEADS).reshape(1, -1)
        kg = jnp.tile(k_norm_g[layer], N_KV).reshape(1, -1)
        w_out_l = w_out[layer].astype(BF16)
        gn, b_out_l = out_norm_g[layer].reshape(1, d), b_out[layer].reshape(1, d)
        l1g, l1b = ln1_g[layer].reshape(1, d), ln1_b[layer].reshape(1, d)
        l2g, l2b = ln2_g[layer].reshape(1, d), ln2_b[layer].reshape(1, d)
        fw, fb = fnet_w[layer].astype(BF16), fnet_b[layer].reshape(1, GROUP_W)
        skip = hy_skip[layer].reshape(1, GROUP_W)
        wup = chunked(ffn_w_up[layer]).astype(BF16)
        vec = chunked(jnp.concatenate([ffn_b_up[layer][None], ffn_conv_w[layer], ffn_conv_b[layer][None],
                                       jnp.zeros((3, 2 * d_ff), F32)], axis=0))
        wd = jnp.pad(ffn_w_down[layer], ((0, pad), (0, 0))).reshape(n_chunks, FFN_CHUNK, d).astype(BF16)
        bdn = ffn_b_down[layer].reshape(1, d)
        last = layer == depth - 1

        cqa, cka, cva, cqb, ckb, cvb, czh, czf = _proj(ctx, con[0], con[1], w_in_l, cos_c, sin_c,
                                                       bd_mean, qg, kg)
        qa, ka, va, qb, kb, vb, zh, zf = _proj(x, lat[0], lat[1], w_in_l, cos_l, sin_l, bd_mean, qg, kg)
        oa = _attention(qa, ka, va, sink_a[layer], banded=True, kc=cka, vc=cva)
        ob = _attention(qb, kb, vb, None, banded=False, kc=ckb, vc=cvb)

        vx, x0 = _hy_gate(zh, hy_conv_w[layer], hy_conv_b[layer])
        vx4, x04 = (a.reshape(bsz, n1h // 2, n2, GROUP_W) for a in (vx, x0))
        a1 = _lead_mm(ht['f_half'], vx4)
        bo = _hy_mid(a1.reshape(bsz, 2, n1h, n2, GROUP_W), ht['mf'], ht['mi'], k_spec, layer)
        oc = _hy_out(ht['g'], bo.reshape(bsz, 2 * n1h, n2, GROUP_W), vx4, x04, skip).reshape(bsz, l, GROUP_W)

        f1 = _fn_first(zf.reshape(bsz, n2, n1f, GROUP_W), ft['first'])
        f2 = _lead_mm(ft['second'], f1.reshape(bsz, 2 * n1f, n2, GROUP_W))
        od = _fn_last(f2.reshape(bsz, 2, l, GROUP_W), cbd, sbd, fw, fb, (l * FNET_GROUP_DIM) ** -0.5)

        x = _merge(oa, ob, oc, od, x, lat[2], gn, w_out_l, b_out_l, l1g, l1b, alpha)
        x = _ffn(x, lat[3], lat[4], lat[5], wup, vec, wd, bdn, l2g, l2b, alpha)

        if not last:
            oac = _attention(cqa, cka, cva, sink_a[layer], banded=False)
            obc = _attention(cqb, ckb, cvb, None, banded=False)
            cvx, cx0 = _hy_gate(czh, hy_conv_w[layer], hy_conv_b[layer])
            occ = _hy_small(cvx, cx0, k_ctx, layer, dt['fd'], dt['ff'], dt['gd'], skip)
            odc = _fn_small(czf, dt['cl'], dt['sl'], cbd, sbd, fw, fb, (lc * FNET_GROUP_DIM) ** -0.5)
            ctx = _merge(oac, obc, occ, odc, ctx, con[2], gn, w_out_l, b_out_l, l1g, l1b, alpha)
            ctx = _ffn(ctx, con[3], con[4], con[5], wup, vec, wd, bdn, l2g, l2b, alpha)
    return x
```

```python
import functools
import math

import jax
import jax.numpy as jnp
from jax import lax
from jax.experimental import pallas as pl
from jax.experimental.pallas import tpu as pltpu

F32 = jnp.float32
BF16 = jnp.bfloat16

HEAD_DIM = 64
GROUP_W = 256
N_HEADS = 4
N_KV = 2
WINDOW = 128
GRID_W = 64
ROPE_THETA = 10000.0
FNET_GROUP_DIM = 64
HYENA_BANDS = 8
HYENA_FAST_DECAY = 0.3
HYENA_SLOW_DECAY = 1.5
HYENA_TARGET = 1e-2
LN_EPS = 1e-6
NEG_INF = -1e30
HYENA_N2 = 256
FNET_N2 = 128
FFN_CHUNK = 256
HALO = 16
VMEM_LIMIT = 56 * 1024 * 1024


def _cp(sem, vmem=None):
    return pltpu.CompilerParams(dimension_semantics=sem, vmem_limit_bytes=vmem)


def _layer_norm(x):
    mu = jnp.mean(x, axis=-1, keepdims=True)
    xc = x - mu
    var = jnp.mean(xc * xc, axis=-1, keepdims=True)
    return xc * lax.rsqrt(var + LN_EPS)


def _dot(a, b):
    return jnp.dot(a, b, preferred_element_type=F32)


def _ada_kernel(c_ref, w_ref, b_ref, o_ref):
    c = c_ref[...]
    s = (c * jax.nn.sigmoid(c)).astype(BF16)
    o_ref[0] = _dot(s, w_ref[0].astype(BF16)) + b_ref[0]


def _ada(cc, w_ada, b_ada):
    depth, d, n = w_ada.shape
    r = cc.shape[0]
    tn = 1536
    return pl.pallas_call(
        _ada_kernel,
        grid=(depth, n // tn),
        in_specs=[pl.BlockSpec((r, d), lambda l, j: (0, 0)),
                  pl.BlockSpec((1, d, tn), lambda l, j: (l, 0, j)),
                  pl.BlockSpec((1, 1, tn), lambda l, j: (l, 0, j))],
        out_specs=pl.BlockSpec((1, r, tn), lambda l, j: (l, 0, j)),
        out_shape=jax.ShapeDtypeStruct((depth, r, n), F32),
        compiler_params=_cp(("parallel", "parallel"), VMEM_LIMIT),
        name="ada",
    )(cc, w_ada, b_ada.reshape(depth, 1, n))


def _swap_halves(z):
    w = z.shape[1]
    lane = lax.broadcasted_iota(jnp.int32, z.shape, 1)
    return jnp.where(lane % 32 < 16, pltpu.roll(z, w - 16, 1), pltpu.roll(z, 16, 1))


def _proj_kernel(x_ref, sh_ref, sc_ref, w_ref, cos_ref, sin_ref, bd_ref, qg_ref, kg_ref,
                 qa_ref, ka_ref, va_ref, qb_ref, kb_ref, vb_ref, zh_ref, zf_ref, *, n_parts):
    tm = x_ref.shape[1]
    rows = tm // n_parts
    scale = HEAD_DIM ** -0.5
    for part in range(n_parts):
        rs = slice(part * rows, (part + 1) * rows)
        h = (_layer_norm(x_ref[0, rs]) * (1.0 + sc_ref[0]) + sh_ref[0]).astype(BF16)
        cos1, sin1 = cos_ref[rs], sin_ref[rs]
        cos2 = jnp.concatenate([cos1, cos1], axis=1)
        sin2 = jnp.concatenate([sin1, sin1], axis=1)

        def proj(lo, width):
            return _dot(h, w_ref[:, lo:lo + width])

        def rope(z):
            if z.shape[1] == 128:
                return z * cos1 + _swap_halves(z) * sin1
            return z * cos2 + _swap_halves(z) * sin2

        def head_norm(z, g):
            w = z.shape[1]
            ms = _dot((z * z).astype(BF16), bd_ref[:w, :w])
            return z * lax.rsqrt(ms + LN_EPS) * g

        def put_heads(ref, z):
            for hh in range(z.shape[1] // HEAD_DIM):
                ref[0, hh, rs] = z[:, hh * HEAD_DIM:(hh + 1) * HEAD_DIM].astype(BF16)

        def put_values(ref, z):
            lane = lax.broadcasted_iota(jnp.int32, (z.shape[0], HEAD_DIM), 1)
            ones_col = jnp.where(lane == 0, 1.0, 0.0)
            for hh in range(z.shape[1] // HEAD_DIM):
                ref[0, hh, rs] = jnp.concatenate([z[:, hh * HEAD_DIM:(hh + 1) * HEAD_DIM], ones_col],
                                                 axis=1).astype(BF16)

        put_heads(qa_ref, rope(proj(0, 256)) * scale)
        kv = proj(256, 256)
        put_heads(ka_ref, rope(kv[:, :128]))
        put_values(va_ref, kv[:, 128:])
        put_heads(qb_ref, rope(head_norm(proj(512, 256), qg_ref[...])) * scale)
        kv = proj(768, 256)
        put_heads(kb_ref, rope(head_norm(kv[:, :128], kg_ref[...])))
        put_values(vb_ref, kv[:, 128:])
        zh_ref[0, rs] = proj(1024, 768).astype(BF16)
        zf_ref[0, rs] = proj(1792, 256).astype(BF16)


def _proj(x, sh, sc, w_in, cos, sin, bd, qg, kg):
    b, l, d = x.shape
    tm = min(l, 512)
    heads = lambda n, w=HEAD_DIM: jax.ShapeDtypeStruct((b, n, l, w), BF16)
    hspec = lambda n, w=HEAD_DIM: pl.BlockSpec((1, n, tm, w), lambda t, bb: (bb, 0, t, 0))
    tok = lambda w: pl.BlockSpec((1, tm, w), lambda t, bb: (bb, t, 0))
    vec = lambda w: pl.BlockSpec((1, 1, w), lambda t, bb: (bb, 0, 0))
    full = lambda a: pl.BlockSpec(a.shape, lambda t, bb: (0,) * a.ndim)
    tab = pl.BlockSpec((tm, 128), lambda t, bb: (t, 0))
    return pl.pallas_call(
        functools.partial(_proj_kernel, n_parts=1),
        grid=(l // tm, b),
        in_specs=[tok(d), vec(d), vec(d), full(w_in), tab, tab, full(bd), full(qg), full(kg)],
        out_specs=[hspec(4), hspec(2), hspec(2, 128), hspec(4), hspec(2), hspec(2, 128), tok(768), tok(256)],
        out_shape=[heads(4), heads(2), heads(2, 128), heads(4), heads(2), heads(2, 128),
                   jax.ShapeDtypeStruct((b, l, 768), BF16), jax.ShapeDtypeStruct((b, l, 256), BF16)],
        compiler_params=_cp(("parallel", "parallel"), VMEM_LIMIT),
        name="proj",
    )(x, sh, sc, w_in, cos, sin, bd, qg, kg)


_NT = (((1,), (1,)), ((), ()))


def _row_max(*parts):
    tiles = [p[:, i:i + 128] for p in parts for i in range(0, p.shape[1], 128)]
    m = tiles[0]
    for t in tiles[1:]:
        m = jnp.maximum(m, t)
    return jnp.max(m, axis=-1, keepdims=True)


def _sink_rows(sink_ref, j, tq):
    row = lax.broadcasted_iota(jnp.int32, (2 * tq, 1), 0)
    return jnp.where(row < tq, sink_ref[2 * j], sink_ref[2 * j + 1])


def _attn_full_kernel(*refs, has_sink, has_ctx, tq, tk, n_t, n_chunks):
    if has_ctx:
        sink_ref, q_ref, k_ref, v_ref, kc_ref, vc_ref, o_ref, m_ref, acc_ref, s0_ref, s1_ref = refs
        l_ctx = kc_ref.shape[2]
    else:
        sink_ref, q_ref, k_ref, v_ref, o_ref, m_ref, acc_ref, s0_ref, s1_ref = refs
    bufs = (s0_ref, s1_ref)
    j = pl.program_id(1)
    lane = lax.broadcasted_iota(jnp.int32, acc_ref.shape[1:], 1)
    for tile in range(n_t):
        if has_sink:
            m_ref[tile] = _sink_rows(sink_ref, j, tq)
            acc_ref[tile] = jnp.where(lane == HEAD_DIM, 1.0, 0.0)
        else:
            m_ref[tile] = jnp.full(m_ref.shape[1:], NEG_INF, F32)
            acc_ref[tile] = jnp.zeros(acc_ref.shape[1:], F32)

    def chunk(ref, t):
        c = t // n_t
        off = c * tk if isinstance(c, int) else pl.multiple_of(c * tk, tk)
        return ref[0, 0, pl.ds(off, tk), :]

    def scores(t, tile, dst, ctx=False):
        q = q_ref[0, :, tile * tq:(tile + 1) * tq, :].reshape(2 * tq, HEAD_DIM)
        if ctx:
            dst[:, :l_ctx] = lax.dot_general(q, kc_ref[0, 0], _NT, preferred_element_type=F32)
        else:
            dst[...] = lax.dot_general(q, chunk(k_ref, t), _NT, preferred_element_type=F32)

    def update(t, tile, src, ctx=False):
        s = src[:, :l_ctx] if ctx else src[...]
        v = vc_ref[0, 0] if ctx else chunk(v_ref, t)
        m_prev = m_ref[tile]
        m_new = jnp.maximum(m_prev, _row_max(s))
        p = jnp.exp((s - m_new).astype(BF16))
        acc_ref[tile] = jnp.exp(m_prev - m_new) * acc_ref[tile] + _dot(p, v)
        m_ref[tile] = m_new

    total = n_t * n_chunks
    per_iter = max(n_t, 2)
    scores(0, 0, s0_ref)

    def body(i, carry):
        t = per_iter * i
        for r in range(per_iter):
            scores(t + r + 1, (r + 1) % n_t, bufs[(r + 1) % 2])
            update(t + r, r % n_t, bufs[r % 2])
        return carry

    n_iter = (total - 1) // per_iter
    lax.fori_loop(0, n_iter, body, 0, unroll=4 // per_iter)
    first = per_iter * n_iter
    items = [(t, t % n_t, False) for t in range(first, total)]
    if has_ctx:
        items += [(None, tile, True) for tile in range(n_t)]
    for idx, (t, tile, ctx) in enumerate(items):
        if idx + 1 < len(items):
            nt, ntile, nctx = items[idx + 1]
            scores(nt, ntile, bufs[(first + idx + 1) % 2], nctx)
        update(t, tile, bufs[(first + idx) % 2], ctx)
    for tile in range(n_t):
        acc = acc_ref[tile]
        o = acc[:, :HEAD_DIM] * (1.0 / acc[:, HEAD_DIM:HEAD_DIM + 1])
        o_ref[0, tile * tq:(tile + 1) * tq, :] = jnp.concatenate([o[:tq], o[tq:]], axis=1).astype(BF16)


def _band_bias(tq):
    span = tq + 2 * WINDOW
    r = (_ar(2 * tq) % tq)[None, :, None]
    col = _ar(span)[None, None, :]
    shift = (_ar(3) * WINDOW)[:, None, None]
    return jnp.where(jnp.abs(col - shift - r) <= WINDOW, 0.0, NEG_INF).astype(F32)


def _attn_band_kernel(sink_ref, q_ref, k_ref, v_ref, kc_ref, vc_ref, bias_ref, o_ref, *,
                      tq, n_t, n_tiles, l_lat):
    j, g = pl.program_id(1), pl.program_id(2)
    span = tq + 2 * WINDOW
    sink = _sink_rows(sink_ref, j, tq)
    kc, vc = kc_ref[0, 0], vc_ref[0, 0]
    for tile in range(n_t):
        qi = g * n_t + tile
        start = pl.multiple_of(jnp.clip(qi * tq - WINDOW, 0, l_lat - span), WINDOW)
        kind = jnp.where(qi == 0, 0, jnp.where(qi == n_tiles - 1, 2, 1))
        q = q_ref[0, :, tile * tq:(tile + 1) * tq, :].reshape(2 * tq, HEAD_DIM)
        s_lat = (lax.dot_general(q, k_ref[0, 0, pl.ds(start, span), :], _NT, preferred_element_type=F32)
                 + bias_ref[kind])
        s_ctx = lax.dot_general(q, kc, _NT, preferred_element_type=F32)
        m = jnp.maximum(sink, _row_max(s_lat, s_ctx))
        acc = (_dot(jnp.exp((s_lat - m).astype(BF16)), v_ref[0, 0, pl.ds(start, span), :])
               + _dot(jnp.exp((s_ctx - m).astype(BF16)), vc))
        o = acc[:, :HEAD_DIM] * (1.0 / (acc[:, HEAD_DIM:HEAD_DIM + 1] + jnp.exp(sink - m)))
        o_ref[0, tile * tq:(tile + 1) * tq, :] = jnp.concatenate([o[:tq], o[tq:]], axis=1).astype(BF16)


def _attention(q, k, v, sink, *, banded, kc=None, vc=None):
    b, _, lq, _ = q.shape
    lk = k.shape[2]
    tq = 256
    has_sink = sink is not None
    if not has_sink:
        sink = jnp.zeros((N_HEADS,), F32)
    smem = pl.BlockSpec(memory_space=pltpu.SMEM)
    whole = lambda a: pl.BlockSpec((1, 1) + a.shape[2:], lambda bb, j, g: (bb, j, 0, 0))
    out_shape = jax.ShapeDtypeStruct((b, lq, GROUP_W), BF16)
    if banded:
        assert lq == lk and lk >= tq + 2 * WINDOW and tq % WINDOW == 0
        n_tiles = lq // tq
        n_t = next(t for t in (4, 2, 1) if n_tiles % t == 0)
        bias = _band_bias(tq)
        return pl.pallas_call(
            functools.partial(_attn_band_kernel, tq=tq, n_t=n_t, n_tiles=n_tiles, l_lat=lk),
            grid=(b, N_KV, n_tiles // n_t),
            in_specs=[smem,
                      pl.BlockSpec((1, 2, n_t * tq, HEAD_DIM), lambda bb, j, g: (bb, j, g, 0)),
                      whole(k), whole(v), whole(kc), whole(vc),
                      pl.BlockSpec(bias.shape, lambda bb, j, g: (0, 0, 0))],
            out_specs=pl.BlockSpec((1, n_t * tq, 128), lambda bb, j, g: (bb, g, j)),
            out_shape=out_shape,
            compiler_params=_cp(("parallel", "parallel", "parallel"), VMEM_LIMIT),
            name="attn_band",
        )(sink, q, k, v, kc, vc, bias)
    has_ctx = kc is not None
    tk = next(t for t in (1024, 768, 512, 256) if lk % t == 0)
    assert not has_ctx or kc.shape[2] <= tk
    n_t = 2 if lq % (2 * tq) == 0 else 1
    kern = functools.partial(_attn_full_kernel, has_sink=has_sink, has_ctx=has_ctx, tq=tq, tk=tk, n_t=n_t,
                             n_chunks=lk // tk)
    extra = (kc, vc) if has_ctx else ()
    return pl.pallas_call(
        kern,
        grid=(b, N_KV, lq // (n_t * tq)),
        in_specs=[smem,
                  pl.BlockSpec((1, 2, n_t * tq, HEAD_DIM), lambda bb, j, g: (bb, j, g, 0)),
                  whole(k), whole(v), *[whole(a) for a in extra]],
        out_specs=pl.BlockSpec((1, n_t * tq, 128), lambda bb, j, g: (bb, g, j)),
        out_shape=out_shape,
        scratch_shapes=[pltpu.VMEM((n_t, 2 * tq, 1), F32), pltpu.VMEM((n_t, 2 * tq, 2 * HEAD_DIM), F32),
                        pltpu.VMEM((2 * tq, tk), F32), pltpu.VMEM((2 * tq, tk), F32)],
        compiler_params=_cp(("parallel", "parallel", "parallel"), VMEM_LIMIT),
        name="attn_full",
    )(sink, q, k, v, *extra)


def _shift_rows(z, prev_row, next_row):
    n = z.shape[0]
    row = lax.broadcasted_iota(jnp.int32, (n, 1), 0)
    dn = jnp.where(row == 0, prev_row, pltpu.roll(z, 1, 0))
    up = jnp.where(row == n - 1, next_row, pltpu.roll(z, n - 1, 0))
    return dn, up


def _hy_gate_kernel(z_ref, zp_ref, zn_ref, w_ref, b_ref, vx_ref, x0_ref, *, n_tiles):
    i = pl.program_id(1)
    z = z_ref[0].astype(F32)
    prev_row = zp_ref[0].astype(F32)[HALO - 1:HALO] * (i > 0).astype(F32)
    next_row = zn_ref[0].astype(F32)[0:1] * (i < n_tiles - 1).astype(F32)
    dn, up = _shift_rows(z, prev_row, next_row)
    w = w_ref[...]
    u = dn * w[0:1] + z * w[1:2] + up * w[2:3] + b_ref[...]
    x0_ref[0] = u[:, :GROUP_W].astype(BF16)
    vx_ref[0] = (u[:, 2 * GROUP_W:] * u[:, GROUP_W:2 * GROUP_W]).astype(BF16)


def _halo_specs(tl, l, w):
    nb = l // HALO
    per = tl // HALO
    return (pl.BlockSpec((1, tl, w), lambda bb, i: (bb, i, 0)),
            pl.BlockSpec((1, HALO, w), lambda bb, i: (bb, jnp.maximum(i * per - 1, 0), 0)),
            pl.BlockSpec((1, HALO, w), lambda bb, i: (bb, jnp.minimum((i + 1) * per, nb - 1), 0)))


def _hy_gate(zh, conv_w, conv_b):
    b, l, w = zh.shape
    tl = min(l, 512)
    out = jax.ShapeDtypeStruct((b, l, GROUP_W), BF16)
    ospec = pl.BlockSpec((1, tl, GROUP_W), lambda bb, i: (bb, i, 0))
    return pl.pallas_call(
        functools.partial(_hy_gate_kernel, n_tiles=l // tl),
        grid=(b, l // tl),
        in_specs=[*_halo_specs(tl, l, w),
                  pl.BlockSpec((3, w), lambda bb, i: (0, 0)),
                  pl.BlockSpec((1, w), lambda bb, i: (0, 0))],
        out_specs=[ospec, ospec],
        out_shape=[out, out],
        compiler_params=_cp(("parallel", "parallel")),
        name="hy_gate",
    )(zh, zh, zh, conv_w, conv_b.reshape(1, w))


def _hy_filter_kernel(fc_ref, w1_ref, b1_ref, fr_ref, w2_ref, b2_ref, w3_ref, dl_ref, k_ref, *, l, tr):
    base = pl.program_id(1) * tr

    def position(m):
        return jnp.where(m < l, m, 2 * l - m).astype(F32)

    m_row = base + lax.broadcasted_iota(jnp.int32, (1, tr), 1)
    t_row = position(m_row)
    feat_id = lax.broadcasted_iota(jnp.int32, (fc_ref.shape[0], 1), 0)
    phase = fc_ref[...] * t_row
    feat = jnp.where(feat_id == 0, t_row / max(l - 1, 1),
                     jnp.where(feat_id <= HYENA_BANDS, jnp.cos(phase),
                               jnp.where(feat_id <= 2 * HYENA_BANDS, -jnp.sin(phase), 0.0)))
    hp = functools.partial(jnp.dot, precision=lax.Precision.HIGHEST, preferred_element_type=F32)
    fr = fr_ref[0]
    h = jnp.sin(fr * (hp(w1_ref[0], feat) + b1_ref[0]))
    h = jnp.sin(fr * (hp(w2_ref[0], h) + b2_ref[0]))
    h = hp(h.T, w3_ref[0])
    m = base + lax.broadcasted_iota(jnp.int32, (tr, 1), 0)
    dec = jnp.exp(-(position(m) / max(l - 1, 1)) * dl_ref[...])
    hf, hb = h[:, :GROUP_W] * dec, h[:, GROUP_W:] * dec
    k = jnp.where(m < l, hf, hb)
    k = jnp.where(m == 0, hf + hb, k)
    k_ref[0] = jnp.where(m == l, 0.0, k).astype(BF16)


def _hy_filter(l, w1, b1, fr, w2, b2, w3):
    depth, emb, hid = w1.shape
    tr = min(2 * l, 2048)
    n_feat = -(-emb // 8) * 8
    bands = jnp.linspace(1e-4, HYENA_BANDS - 1, HYENA_BANDS, dtype=F32) * (2.0 * math.pi / l)
    fc = jnp.zeros((n_feat, 1), F32).at[1:1 + HYENA_BANDS, 0].set(bands).at[1 + HYENA_BANDS:emb, 0].set(bands)
    w1t = jnp.zeros((depth, hid, n_feat), F32).at[:, :, :emb].set(jnp.swapaxes(w1, 1, 2))
    w2t = jnp.swapaxes(w2, 1, 2)
    min_decay = math.log(HYENA_TARGET) / HYENA_SLOW_DECAY
    max_decay = math.log(HYENA_TARGET) / HYENA_FAST_DECAY
    deltas = jnp.abs(jnp.linspace(min_decay, max_decay, GROUP_W, dtype=F32)).reshape(1, GROUP_W)
    lay = lambda a: pl.BlockSpec((1,) + a.shape[1:], lambda d, i: (d,) + (0,) * (a.ndim - 1))
    fix = lambda a: pl.BlockSpec(a.shape, lambda d, i: (0,) * a.ndim)
    b1, fr, b2 = (a.reshape(depth, hid, 1) for a in (b1, fr, b2))
    return pl.pallas_call(
        functools.partial(_hy_filter_kernel, l=l, tr=tr),
        grid=(depth, 2 * l // tr),
        in_specs=[fix(fc), lay(w1t), lay(b1), lay(fr), lay(w2t), lay(b2), lay(w3), fix(deltas)],
        out_specs=pl.BlockSpec((1, tr, GROUP_W), lambda d, i: (d, i, 0)),
        out_shape=jax.ShapeDtypeStruct((depth, 2 * l, GROUP_W), BF16),
        compiler_params=_cp(("parallel", "parallel")),
        name="hy_filter",
    )(fc, w1t, b1, fr, w2t, b2, w3, deltas)


ROW_TILE = 8


def _lead_kron(f):
    return jnp.kron(f.astype(F32), jnp.eye(ROW_TILE, dtype=F32)).astype(BF16)


def _lead_rows(n):
    return next(r for r in (4 * ROW_TILE, 2 * ROW_TILE, ROW_TILE) if n % r == 0)


def _lead_mm_kernel(f_ref, x_ref, o_ref):
    k, rows, c = x_ref.shape[1:]
    for j in range(rows // ROW_TILE):
        rs = slice(j * ROW_TILE, (j + 1) * ROW_TILE)
        y = _dot(f_ref[...], x_ref[0, :, rs, :].reshape(k * ROW_TILE, c))
        o_ref[0, :, rs, :] = y.reshape(-1, ROW_TILE, c).astype(o_ref.dtype)


def _lead_mm(fk, x):
    b, k, n, c = x.shape
    m = fk.shape[0] // ROW_TILE
    rows = _lead_rows(n)
    return pl.pallas_call(
        _lead_mm_kernel,
        grid=(b, n // rows),
        in_specs=[pl.BlockSpec(fk.shape, lambda bb, i: (0, 0)),
                  pl.BlockSpec((1, k, rows, c), lambda bb, i: (bb, 0, i, 0))],
        out_specs=pl.BlockSpec((1, m, rows, c), lambda bb, i: (bb, 0, i, 0)),
        out_shape=jax.ShapeDtypeStruct((b, m, n, c), BF16),
        compiler_params=_cp(("parallel", "parallel"), VMEM_LIMIT),
        name="lead_mm",
    )(fk, x)


def _hy_spec_kernel(a_ref, mf_ref, k_ref):
    for d in range(a_ref.shape[0]):
        x = a_ref[d, :, 0].reshape(2 * a_ref.shape[3], GROUP_W)
        k_ref[d, 0] = _dot(mf_ref[0], x)


def _hy_mid_kernel(a_ref, mf_ref, mi_ref, k_ref, o_ref):
    n2 = a_ref.shape[3]
    k = k_ref[0, 0]
    kr, ki = k[:n2], k[n2:]
    for bb in range(a_ref.shape[0]):
        x = a_ref[bb, :, 0].reshape(2 * n2, GROUP_W)
        y = _dot(mf_ref[0], x)
        yr, yi = y[:n2], y[n2:]
        z = jnp.concatenate([yr * kr - yi * ki, yr * ki + yi * kr], axis=0).astype(BF16)
        o_ref[bb, :, 0] = _dot(mi_ref[0], z).astype(BF16).reshape(2, n2, GROUP_W)


def _hy_spectrum(a, mf):
    depth, _, n1, n2, w = a.shape
    return pl.pallas_call(
        _hy_spec_kernel,
        grid=(n1,),
        in_specs=[pl.BlockSpec((depth, 2, 1, n2, w), lambda k1: (0, 0, k1, 0, 0)),
                  pl.BlockSpec((1, 2 * n2, 2 * n2), lambda k1: (k1, 0, 0))],
        out_specs=pl.BlockSpec((depth, 1, 2 * n2, w), lambda k1: (0, k1, 0, 0)),
        out_shape=jax.ShapeDtypeStruct((depth, n1, 2 * n2, w), F32),
        compiler_params=_cp(("parallel",)),
        name="hy_spectrum",
    )(a, mf)


def _hy_mid(a, mf, mi, kspec, layer):
    b, _, n1, n2, w = a.shape
    blk = pl.BlockSpec((b, 2, 1, n2, w), lambda k1: (0, 0, k1, 0, 0))
    mat = pl.BlockSpec((1, 2 * n2, 2 * n2), lambda k1: (k1, 0, 0))
    return pl.pallas_call(
        _hy_mid_kernel,
        grid=(n1,),
        in_specs=[blk, mat, mat,
                  pl.BlockSpec((1, 1, 2 * n2, w), lambda k1: (layer, k1, 0, 0))],
        out_specs=blk,
        out_shape=jax.ShapeDtypeStruct(a.shape, BF16),
        compiler_params=_cp(("parallel",)),
        name="hy_mid",
    )(a, mf, mi, kspec)


def _hy_out_kernel(g_ref, b_ref, vx_ref, x0_ref, skip_ref, o_ref):
    k, rows, c = b_ref.shape[1:]
    for j in range(rows // ROW_TILE):
        rs = slice(j * ROW_TILE, (j + 1) * ROW_TILE)
        y = _dot(g_ref[...], b_ref[0, :, rs, :].reshape(k * ROW_TILE, c)).reshape(-1, ROW_TILE, c)
        vx = vx_ref[0, :, rs, :].astype(F32)
        o_ref[0, :, rs, :] = ((y + skip_ref[...] * vx) * x0_ref[0, :, rs, :].astype(F32)).astype(BF16)


def _hy_out(gk, bo, vx, x0, skip):
    b, k, n, c = bo.shape
    m = gk.shape[0] // ROW_TILE
    rows = _lead_rows(n)
    sig = pl.BlockSpec((1, m, rows, c), lambda bb, i: (bb, 0, i, 0))
    return pl.pallas_call(
        _hy_out_kernel,
        grid=(b, n // rows),
        in_specs=[pl.BlockSpec(gk.shape, lambda bb, i: (0, 0)),
                  pl.BlockSpec((1, k, rows, c), lambda bb, i: (bb, 0, i, 0)),
                  sig, sig, pl.BlockSpec((1, c), lambda bb, i: (0, 0))],
        out_specs=sig,
        out_shape=jax.ShapeDtypeStruct((b, m, n, c), BF16),
        compiler_params=_cp(("parallel", "parallel"), VMEM_LIMIT),
        name="hy_out",
    )(gk, bo, vx, x0, skip)


def _hy_small_kernel(vx_ref, x0_ref, k_ref, fd_ref, ff_ref, gd_ref, skip_ref, o_ref):
    nf = fd_ref.shape[0] // 2
    vx = vx_ref[0]
    u = _dot(fd_ref[...], vx)
    k = _dot(ff_ref[...], k_ref[0])
    ur, ui, kr, ki = u[:nf], u[nf:], k[:nf], k[nf:]
    z = jnp.concatenate([ur * kr - ui * ki, ur * ki + ui * kr], axis=0).astype(BF16)
    y = _dot(gd_ref[...], z)
    o_ref[0] = ((y + skip_ref[...] * vx.astype(F32)) * x0_ref[0].astype(F32)).astype(BF16)


def _hy_small(vx, x0, kfilt, layer, fd, ff, gd, skip):
    b, l, w = vx.shape
    sig = pl.BlockSpec((1, l, w), lambda bb: (bb, 0, 0))
    full = lambda a: pl.BlockSpec(a.shape, lambda bb: (0,) * a.ndim)
    return pl.pallas_call(
        _hy_small_kernel,
        grid=(b,),
        in_specs=[sig, sig, pl.BlockSpec((1, 2 * l, w), lambda bb: (layer, 0, 0)),
                  full(fd), full(ff), full(gd), full(skip)],
        out_specs=sig,
        out_shape=jax.ShapeDtypeStruct((b, l, w), BF16),
        compiler_params=_cp(("parallel",)),
        name="hy_small",
    )(vx, x0, kfilt, fd, ff, gd, skip)


def _fn_first_kernel(z_ref, m_ref, o_ref):
    n2 = z_ref.shape[1]
    x = z_ref[0].reshape(n2 * 8, GROUP_W)
    o_ref[0] = _dot(m_ref[0], x).astype(BF16).reshape(2, 8, n2, GROUP_W)


def _fn_first(zf, mats):
    b, n2, n1, w = zf.shape
    return pl.pallas_call(
        _fn_first_kernel,
        grid=(n1 // 8, b),
        in_specs=[pl.BlockSpec((1, n2, 8, w), lambda i, bb: (bb, 0, i, 0)),
                  pl.BlockSpec((1, 16 * n2, 8 * n2), lambda i, bb: (i, 0, 0))],
        out_specs=pl.BlockSpec((1, 2, 8, n2, w), lambda i, bb: (bb, 0, i, 0, 0)),
        out_shape=jax.ShapeDtypeStruct((b, 2, n1, n2, w), BF16),
        compiler_params=_cp(("parallel", "parallel"), VMEM_LIMIT),
        name="fn_first",
    )(zf, mats)


def _fn_small_kernel(u_ref, cl_ref, sl_ref, c_ref, s_ref, w_ref, b_ref, o_ref, *, scale):
    u = u_ref[0]
    uc = _dot(u, c_ref[...]).astype(BF16)
    us = _dot(u, s_ref[...]).astype(BF16)
    f = (_dot(cl_ref[...], uc) - _dot(sl_ref[...], us)) * scale
    o_ref[0] = (_dot(f.astype(BF16), w_ref[...]) + b_ref[...]).astype(BF16)


def _fn_small(zf, cl, sl, cbd, sbd, w, bias, scale):
    b, l, c = zf.shape
    sig = pl.BlockSpec((1, l, c), lambda bb: (bb, 0, 0))
    full = lambda z: pl.BlockSpec(z.shape, lambda bb: (0,) * z.ndim)
    return pl.pallas_call(
        functools.partial(_fn_small_kernel, scale=scale),
        grid=(b,),
        in_specs=[sig, full(cl), full(sl), full(cbd), full(sbd), full(w), full(bias)],
        out_specs=sig,
        out_shape=jax.ShapeDtypeStruct((b, l, c), BF16),
        compiler_params=_cp(("parallel",)),
        name="fn_small",
    )(zf, cl, sl, cbd, sbd, w, bias)


def _merge_kernel(*refs, alpha, fnet_scale):
    if fnet_scale is None:
        oa_ref, ob_ref, oc_ref, od_ref = refs[:4]
        od = od_ref[0].astype(F32)
        rest = refs[4:]
    else:
        oa_ref, ob_ref, oc_ref, ar_ref, ai_ref, c_ref, s_ref, fw_ref, fb_ref = refs[:9]
        t = (_dot(ar_ref[0, 0], c_ref[...]) + _dot(ai_ref[0, 0], s_ref[...])) * fnet_scale
        od = _dot(t.astype(BF16), fw_ref[...]) + fb_ref[...]
        rest = refs[9:]
    x_ref, g1_ref, gn_ref, w_ref, b_ref, lng_ref, lnb_ref, o_ref = rest
    parts = []
    for idx, v in enumerate((oa_ref[0].astype(F32), ob_ref[0].astype(F32), oc_ref[0].astype(F32), od)):
        ms = jnp.mean(v * v, axis=-1, keepdims=True)
        gain = gn_ref[:, idx * GROUP_W:(idx + 1) * GROUP_W]
        parts.append((v * lax.rsqrt(ms + LN_EPS) * gain).astype(BF16))
    y = _dot(jnp.concatenate(parts, axis=1), w_ref[...]) + b_ref[...]
    r = alpha * x_ref[0] + g1_ref[0] * y
    o_ref[0] = _layer_norm(r) * lng_ref[...] + lnb_ref[...]


def _merge(oa, ob, oc, od, x, g1, gn, w_out, b_out, ln_g, ln_b, alpha):
    b, l, d = x.shape
    tm = min(l, 512)
    grp = pl.BlockSpec((1, tm, GROUP_W), lambda bb, i: (bb, i, 0))
    tok = pl.BlockSpec((1, tm, d), lambda bb, i: (bb, i, 0))
    full = lambda a: pl.BlockSpec(a.shape, lambda bb, i: (0,) * a.ndim)
    if isinstance(od, tuple):
        a, cbd, sbd, fw, fb, fnet_scale = od
        part = lambda p: pl.BlockSpec((1, 1, tm, GROUP_W), lambda bb, i: (bb, p, i, 0))
        od_args = (a, a, cbd, sbd, fw, fb)
        od_specs = [part(0), part(1), full(cbd), full(sbd), full(fw), full(fb)]
    else:
        fnet_scale, od_args, od_specs = None, (od,), [grp]
    return pl.pallas_call(
        functools.partial(_merge_kernel, alpha=alpha, fnet_scale=fnet_scale),
        grid=(b, l // tm),
        in_specs=[grp, grp, grp, *od_specs, tok, pl.BlockSpec((1, 1, d), lambda bb, i: (bb, 0, 0)),
                  full(gn), full(w_out), full(b_out), full(ln_g), full(ln_b)],
        out_specs=tok,
        out_shape=jax.ShapeDtypeStruct((b, l, d), F32),
        compiler_params=_cp(("parallel", "parallel"), VMEM_LIMIT),
        name="merge",
    )(oa, ob, oc, *od_args, x, g1, gn, w_out, b_out, ln_g, ln_b)


def _ffn_kernel(x_ref, xp_ref, xn_ref, sh_ref, sc_ref, g_ref, wup_ref, vec_ref, wd_ref, bd_ref,
                lng_ref, lnb_ref, o_ref, h_ref, acc_ref, u0_ref, u1_ref, *, alpha, tm, n_chunks, n_tiles):
    i = pl.program_id(1)
    sh, sc = sh_ref[0], sc_ref[0]
    cw = FFN_CHUNK

    def hmod(v):
        return (_layer_norm(v) * (1.0 + sc) + sh).astype(BF16)

    h_ref[0:HALO] = hmod(xp_ref[0])
    h_ref[HALO:HALO + tm] = hmod(x_ref[0])
    h_ref[HALO + tm:] = hmod(xn_ref[0])
    acc_ref[...] = jnp.zeros_like(acc_ref)
    row8 = lax.broadcasted_iota(jnp.int32, (8, 1), 0)
    pad_top = (row8 == 7) & (i == 0)
    pad_bot = (row8 == 0) & (i == n_tiles - 1)

    def up(c, dst):
        dst[...] = _dot(h_ref[...], wup_ref[c])

    def down(c, src):
        vec = vec_ref[c]
        bias, w0, w1, w2 = vec[0:1], vec[1:2], vec[2:3], vec[3:4]
        src[HALO - 8:HALO] = jnp.where(pad_top, -bias, src[HALO - 8:HALO])
        src[HALO + tm:HALO + tm + 8] = jnp.where(pad_bot, -bias, src[HALO + tm:HALO + tm + 8])
        conv = (src[HALO - 1:HALO - 1 + tm] * w0 + src[HALO:HALO + tm] * w1 + src[HALO + 1:HALO + 1 + tm] * w2
                + (vec[4:5] + bias * (w0 + w1 + w2)))
        a, g = conv[:, :cw], conv[:, cw:]
        act = (a * jax.nn.sigmoid(a) * g).astype(BF16)
        acc_ref[...] += _dot(act, wd_ref[c])

    up(0, u0_ref)

    def body(p, carry):
        c = 2 * p
        up(c + 1, u1_ref)
        down(c, u0_ref)
        up(c + 2, u0_ref)
        down(c + 1, u1_ref)
        return carry

    n_pairs = (n_chunks - 1) // 2
    lax.fori_loop(0, n_pairs, body, 0)
    c = 2 * n_pairs
    if c + 1 < n_chunks:
        up(c + 1, u1_ref)
    down(c, u0_ref)
    if c + 1 < n_chunks:
        down(c + 1, u1_ref)
    r = alpha * x_ref[0] + g_ref[0] * (acc_ref[...] + bd_ref[...])
    o_ref[0] = _layer_norm(r) * lng_ref[...] + lnb_ref[...]


def _ffn(x, sh, sc, g2, wup, vec, wd, b_down, ln_g, ln_b, alpha):
    b, l, d = x.shape
    tm = min(l, 512)
    n_chunks = wd.shape[0]
    main, prev, nxt = _halo_specs(tm, l, d)
    mod = pl.BlockSpec((1, 1, d), lambda bb, i: (bb, 0, 0))
    full = lambda a: pl.BlockSpec(a.shape, lambda bb, i: (0,) * a.ndim)
    return pl.pallas_call(
        functools.partial(_ffn_kernel, alpha=alpha, tm=tm, n_chunks=n_chunks, n_tiles=l // tm),
        grid=(b, l // tm),
        in_specs=[main, prev, nxt, mod, mod, mod, full(wup), full(vec), full(wd),
                  full(b_down), full(ln_g), full(ln_b)],
        out_specs=main,
        out_shape=jax.ShapeDtypeStruct((b, l, d), F32),
        scratch_shapes=[pltpu.VMEM((tm + 2 * HALO, d), BF16), pltpu.VMEM((tm, d), F32),
                        pltpu.VMEM((tm + 2 * HALO, 2 * FFN_CHUNK), F32),
                        pltpu.VMEM((tm + 2 * HALO, 2 * FFN_CHUNK), F32)],
        compiler_params=_cp(("parallel", "parallel"), VMEM_LIMIT),
        name="ffn",
    )(x, x, x, sh, sc, g2, wup, vec, wd, b_down, ln_g, ln_b)


def _cos_sin(idx, n):
    ang = (idx % n).astype(F32) * (2.0 * math.pi / n)
    return jnp.cos(ang), jnp.sin(ang)


def _cplx_rows(re, im):
    return jnp.concatenate([re, im], axis=-2)


def _cplx_block(re, im):
    return jnp.concatenate([jnp.concatenate([re, -im], axis=-1), jnp.concatenate([im, re], axis=-1)], axis=-2)


def _ar(n):
    return jnp.arange(n, dtype=jnp.int32)


def _hyena_tables(l):
    n, n2 = 2 * l, HYENA_N2
    n1 = n // n2
    c, s = _cos_sin(_ar(n1)[:, None] * _ar(n1)[None, :], n1)
    f_full = _cplx_rows(c, -s).astype(BF16)
    g = (jnp.concatenate([c, -s], axis=1)[:n1 // 2] / n).astype(BF16)
    k1, k2, m2 = _ar(n1)[:, None, None], _ar(n2)[None, :, None], _ar(n2)[None, None, :]
    c, s = _cos_sin(m2 * k1 + n1 * m2 * k2, n)
    mf = _cplx_block(c, -s).astype(BF16)
    return dict(f_half=_lead_kron(f_full[:, :n1 // 2]), f_full=_lead_kron(f_full), g=_lead_kron(g),
                mf=mf, mi=jnp.swapaxes(mf, 1, 2))


def _fnet_tables(l):
    n2 = FNET_N2
    n1 = l // n2
    nblk = n1 // 8
    blk, j = _ar(nblk)[:, None, None, None], _ar(8)[None, :, None, None]
    k2, m2 = _ar(n2)[None, None, :, None], _ar(n2)[None, None, None, :]
    c, s = _cos_sin(n1 * m2 * k2 + (8 * blk + j) * k2, l)
    eye = jnp.eye(8, dtype=F32)
    emb = lambda t: jnp.einsum('bjkn,ji->bjkni', t, eye).reshape(nblk, 8 * n2, n2 * 8)
    first = jnp.concatenate([emb(c), emb(-s)], axis=1).astype(BF16)
    c, s = _cos_sin(_ar(n1)[:, None] * _ar(n1)[None, :], n1)
    second = _cplx_block(c, -s).astype(BF16)
    return dict(first=first, second=_lead_kron(second))


def _dense_tables(l):
    n = 2 * l
    c, s = _cos_sin(_ar(n)[:, None] * _ar(n)[None, :], n)
    ff = _cplx_rows(c, -s).astype(BF16)
    gd = (jnp.concatenate([c, -s], axis=1)[:l] / n).astype(BF16)
    cl, sl = _cos_sin(_ar(l)[:, None] * _ar(l)[None, :], l)
    return dict(fd=ff[:, :l], ff=ff, gd=gd, cl=cl.astype(BF16), sl=sl.astype(BF16))


def _rope_tables(l):
    rows = l // GRID_W
    row = jnp.broadcast_to(jnp.arange(rows)[:, None], (rows, GRID_W)).reshape(-1).astype(F32)
    col = jnp.broadcast_to(jnp.arange(GRID_W)[None, :], (rows, GRID_W)).reshape(-1).astype(F32)
    half = HEAD_DIM // 2
    inv = ROPE_THETA ** (-jnp.arange(0, half, 2, dtype=F32) / half)
    ar, ac = row[:, None] * inv, col[:, None] * inv
    cos = jnp.concatenate([jnp.cos(ar), jnp.cos(ar), jnp.cos(ac), jnp.cos(ac)], axis=1)
    sin = jnp.concatenate([-jnp.sin(ar), jnp.sin(ar), -jnp.sin(ac), jnp.sin(ac)], axis=1)
    return jnp.tile(cos, (1, 2)), jnp.tile(sin, (1, 2))


def kernel(x, c, ctx, c_ctx, w_ada, b_ada, w_in, sink_a, q_norm_g, k_norm_g, hy_conv_w, hy_conv_b,
           hy_f_w1, hy_f_b1, hy_f_freq, hy_f_w2, hy_f_b2, hy_f_w3, hy_skip, fnet_w, fnet_b,
           out_norm_g, w_out, b_out, ln1_g, ln1_b, ffn_w_up, ffn_b_up, ffn_conv_w, ffn_conv_b,
           ffn_w_down, ffn_b_down, ln2_g, ln2_b):
    bsz, l, d = x.shape
    lc = ctx.shape[1]
    depth = w_ada.shape[0]
    d_ff = ffn_w_down.shape[1]
    alpha = (2 * depth) ** 0.25
    n2h, n2f = HYENA_N2, FNET_N2
    n1h, n1f = 2 * l // n2h, l // n2f

    ht, ft, dt = _hyena_tables(l), _fnet_tables(l), _dense_tables(lc)
    cos_l, sin_l = _rope_tables(l)
    cos_c, sin_c = jnp.ones((lc, 128), F32), jnp.zeros((lc, 128), F32)
    jj = _ar(GROUP_W)
    same = (jj[:, None] // FNET_GROUP_DIM) == (jj[None, :] // FNET_GROUP_DIM)
    bd_mean = (same.astype(F32) / HEAD_DIM).astype(BF16)
    cg, sg = _cos_sin(jj[:, None] * jj[None, :], FNET_GROUP_DIM)
    cbd, sbd = jnp.where(same, cg, 0.0).astype(BF16), jnp.where(same, sg, 0.0).astype(BF16)

    n_rows = -(-(bsz + 1) // 8) * 8
    cc = jnp.zeros((n_rows, d), F32).at[:bsz].set(c).at[bsz].set(c_ctx)
    mod = _ada(cc, w_ada, b_ada)

    k_lat = _hy_filter(l, hy_f_w1, hy_f_b1, hy_f_freq, hy_f_w2, hy_f_b2, hy_f_w3)
    k_ctx = _hy_filter(lc, hy_f_w1, hy_f_b1, hy_f_freq, hy_f_w2, hy_f_b2, hy_f_w3)
    a_k = _lead_mm(ht['f_full'], k_lat.reshape(depth, n1h, n2h, GROUP_W))
    k_spec = _hy_spectrum(a_k.reshape(depth, 2, n1h, n2h, GROUP_W), ht['mf'])

    n_chunks = -(-d_ff // FFN_CHUNK)
    pad = n_chunks * FFN_CHUNK - d_ff

    def chunked(v):
        v = v.reshape(v.shape[:-1] + (2, d_ff))
        v = jnp.pad(v, [(0, 0)] * (v.ndim - 1) + [(0, pad)])
        v = v.reshape(v.shape[:-2] + (2, n_chunks, FFN_CHUNK))
        v = jnp.swapaxes(v, -3, -2).reshape(v.shape[:-3] + (n_chunks, 2 * FFN_CHUNK))
        return jnp.moveaxis(v, -2, 0)

    for layer in range(depth):
        m6 = mod[layer].reshape(n_rows, 6, d)
        lat = [m6[:bsz, i][:, None, :] for i in range(6)]
        con = [jnp.broadcast_to(m6[bsz, i][None, None, :], (bsz, 1, d)) for i in range(6)]
        w_in_l = w_in[layer].astype(BF16)
        qg = jnp.tile(q_norm_g[layer], N_HEADS).reshape(1, -1)
        kg = jnp.tile(k_norm_g[layer], N_KV).reshape(1, -1)
        w_out_l = w_out[layer].astype(BF16)
        gn, b_out_l = out_norm_g[layer].reshape(1, d), b_out[layer].reshape(1, d)
        l1g, l1b = ln1_g[layer].reshape(1, d), ln1_b[layer].reshape(1, d)
        l2g, l2b = ln2_g[layer].reshape(1, d), ln2_b[layer].reshape(1, d)
        fw, fb = fnet_w[layer].astype(BF16), fnet_b[layer].reshape(1, GROUP_W)
        skip = hy_skip[layer].reshape(1, GROUP_W)
        wup = chunked(ffn_w_up[layer]).astype(BF16)
        vec = chunked(jnp.concatenate([ffn_b_up[layer][None], ffn_conv_w[layer], ffn_conv_b[layer][None],
                                       jnp.zeros((3, 2 * d_ff), F32)], axis=0))
        wd = jnp.pad(ffn_w_down[layer], ((0, pad), (0, 0))).reshape(n_chunks, FFN_CHUNK, d).astype(BF16)
        bdn = ffn_b_down[layer].reshape(1, d)
        last = layer == depth - 1

        cqa, cka, cva, cqb, ckb, cvb, czh, czf = _proj(ctx, con[0], con[1], w_in_l, cos_c, sin_c,
                                                       bd_mean, qg, kg)
        qa, ka, va, qb, kb, vb, zh, zf = _proj(x, lat[0], lat[1], w_in_l, cos_l, sin_l, bd_mean, qg, kg)
        oa = _attention(qa, ka, va, sink_a[layer], banded=True, kc=cka, vc=cva)
        ob = _attention(qb, kb, vb, None, banded=False, kc=ckb, vc=cvb)

        vx, x0 = _hy_gate(zh, hy_conv_w[layer], hy_conv_b[layer])
        vx4, x04 = (a.reshape(bsz, n1h // 2, n2h, GROUP_W) for a in (vx, x0))
        a1 = _lead_mm(ht['f_half'], vx4)
        bo = _hy_mid(a1.reshape(bsz, 2, n1h, n2h, GROUP_W), ht['mf'], ht['mi'], k_spec, layer)
        oc = _hy_out(ht['g'], bo.reshape(bsz, 2 * n1h, n2h, GROUP_W), vx4, x04, skip).reshape(bsz, l, GROUP_W)

        f1 = _fn_first(zf.reshape(bsz, n2f, n1f, GROUP_W), ft['first'])
        f2 = _lead_mm(ft['second'], f1.reshape(bsz, 2 * n1f, n2f, GROUP_W))
        od = (f2.reshape(bsz, 2, l, GROUP_W), cbd, sbd, fw, fb, (l * FNET_GROUP_DIM) ** -0.5)

        x = _merge(oa, ob, oc, od, x, lat[2], gn, w_out_l, b_out_l, l1g, l1b, alpha)
        x = _ffn(x, lat[3], lat[4], lat[5], wup, vec, wd, bdn, l2g, l2b, alpha)

        if not last:
            oac = _attention(cqa, cka, cva, sink_a[layer], banded=False)
            obc = _attention(cqb, ckb, cvb, None, banded=False)
            cvx, cx0 = _hy_gate(czh, hy_conv_w[layer], hy_conv_b[layer])
            occ = _hy_small(cvx, cx0, k_ctx, layer, dt['fd'], dt['ff'], dt['gd'], skip)
            odc = _fn_small(czf, dt['cl'], dt['sl'], cbd, sbd, fw, fb, (lc * FNET_GROUP_DIM) ** -0.5)
            ctx = _merge(oac, obc, occ, odc, ctx, con[2], gn, w_out_l, b_out_l, l1g, l1b, alpha)
            ctx = _ffn(ctx, con[3], con[4], con[5], wup, vec, wd, bdn, l2g, l2b, alpha)
    return x
```

```python
import functools
import math

import jax
import jax.numpy as jnp
from jax import lax
from jax.experimental import pallas as pl
from jax.experimental.pallas import tpu as pltpu

F32 = jnp.float32
BF16 = jnp.bfloat16

HEAD_DIM = 64
GROUP_W = 256
N_HEADS = 4
N_KV = 2
WINDOW = 128
GRID_W = 64
ROPE_THETA = 10000.0
FNET_GROUP_DIM = 64
HYENA_BANDS = 8
HYENA_FAST_DECAY = 0.3
HYENA_SLOW_DECAY = 1.5
HYENA_TARGET = 1e-2
LN_EPS = 1e-6
NEG_INF = -1e30
HYENA_N2 = 256
FNET_N2 = 128
FFN_CHUNK = 256
HALO = 16
VMEM_LIMIT = 56 * 1024 * 1024


def _cp(sem, vmem=None):
    return pltpu.CompilerParams(dimension_semantics=sem, vmem_limit_bytes=vmem)


def _layer_norm(x):
    mu = jnp.mean(x, axis=-1, keepdims=True)
    xc = x - mu
    var = jnp.mean(xc * xc, axis=-1, keepdims=True)
    return xc * lax.rsqrt(var + LN_EPS)


def _dot(a, b):
    return jnp.dot(a, b, preferred_element_type=F32)


def _ada_kernel(c_ref, w_ref, b_ref, o_ref):
    c = c_ref[...]
    s = (c * jax.nn.sigmoid(c)).astype(BF16)
    o_ref[0] = _dot(s, w_ref[0].astype(BF16)) + b_ref[0]


def _ada(cc, w_ada, b_ada):
    depth, d, n = w_ada.shape
    r = cc.shape[0]
    tn = 1536
    return pl.pallas_call(
        _ada_kernel,
        grid=(depth, n // tn),
        in_specs=[pl.BlockSpec((r, d), lambda l, j: (0, 0)),
                  pl.BlockSpec((1, d, tn), lambda l, j: (l, 0, j)),
                  pl.BlockSpec((1, 1, tn), lambda l, j: (l, 0, j))],
        out_specs=pl.BlockSpec((1, r, tn), lambda l, j: (l, 0, j)),
        out_shape=jax.ShapeDtypeStruct((depth, r, n), F32),
        compiler_params=_cp(("parallel", "parallel"), VMEM_LIMIT),
        name="ada",
    )(cc, w_ada, b_ada.reshape(depth, 1, n))


def _swap_halves(z):
    w = z.shape[1]
    lane = lax.broadcasted_iota(jnp.int32, z.shape, 1)
    return jnp.where(lane % 32 < 16, pltpu.roll(z, w - 16, 1), pltpu.roll(z, 16, 1))


def _proj_kernel(x_ref, sh_ref, sc_ref, w_ref, cos_ref, sin_ref, bd_ref, qg_ref, kg_ref,
                 qa_ref, ka_ref, va_ref, qb_ref, kb_ref, vb_ref, zh_ref, zf_ref, *, n_parts):
    tm = x_ref.shape[1]
    rows = tm // n_parts
    scale = HEAD_DIM ** -0.5
    for part in range(n_parts):
        rs = slice(part * rows, (part + 1) * rows)
        h = (_layer_norm(x_ref[0, rs]) * (1.0 + sc_ref[0]) + sh_ref[0]).astype(BF16)
        cos1, sin1 = cos_ref[rs], sin_ref[rs]
        cos2 = jnp.concatenate([cos1, cos1], axis=1)
        sin2 = jnp.concatenate([sin1, sin1], axis=1)

        def proj(lo, width):
            return _dot(h, w_ref[:, lo:lo + width])

        def rope(z):
            if z.shape[1] == 128:
                return z * cos1 + _swap_halves(z) * sin1
            return z * cos2 + _swap_halves(z) * sin2

        def head_norm(z, g):
            w = z.shape[1]
            ms = _dot((z * z).astype(BF16), bd_ref[:w, :w])
            return z * lax.rsqrt(ms + LN_EPS) * g

        def put_heads(ref, z):
            for hh in range(z.shape[1] // HEAD_DIM):
                ref[0, hh, rs] = z[:, hh * HEAD_DIM:(hh + 1) * HEAD_DIM].astype(BF16)

        def put_values(ref, z):
            lane = lax.broadcasted_iota(jnp.int32, (z.shape[0], HEAD_DIM), 1)
            ones_col = jnp.where(lane == 0, 1.0, 0.0)
            for hh in range(z.shape[1] // HEAD_DIM):
                ref[0, hh, rs] = jnp.concatenate([z[:, hh * HEAD_DIM:(hh + 1) * HEAD_DIM], ones_col],
                                                 axis=1).astype(BF16)

        put_heads(qa_ref, rope(proj(0, 256)) * scale)
        kv = proj(256, 256)
        put_heads(ka_ref, rope(kv[:, :128]))
        put_values(va_ref, kv[:, 128:])
        put_heads(qb_ref, rope(head_norm(proj(512, 256), qg_ref[...])) * scale)
        kv = proj(768, 256)
        put_heads(kb_ref, rope(head_norm(kv[:, :128], kg_ref[...])))
        put_values(vb_ref, kv[:, 128:])
        zh_ref[0, rs] = proj(1024, 768).astype(BF16)
        zf_ref[0, rs] = proj(1792, 256).astype(BF16)


def _proj(x, sh, sc, w_in, cos, sin, bd, qg, kg):
    b, l, d = x.shape
    tm = min(l, 1024)
    heads = lambda n, w=HEAD_DIM: jax.ShapeDtypeStruct((b, n, l, w), BF16)
    hspec = lambda n, w=HEAD_DIM: pl.BlockSpec((1, n, tm, w), lambda t, bb: (bb, 0, t, 0))
    tok = lambda w: pl.BlockSpec((1, tm, w), lambda t, bb: (bb, t, 0))
    vec = lambda w: pl.BlockSpec((1, 1, w), lambda t, bb: (bb, 0, 0))
    full = lambda a: pl.BlockSpec(a.shape, lambda t, bb: (0,) * a.ndim)
    tab = pl.BlockSpec((tm, 128), lambda t, bb: (t, 0))
    return pl.pallas_call(
        functools.partial(_proj_kernel, n_parts=1),
        grid=(l // tm, b),
        in_specs=[tok(d), vec(d), vec(d), full(w_in), tab, tab, full(bd), full(qg), full(kg)],
        out_specs=[hspec(4), hspec(2), hspec(2, 128), hspec(4), hspec(2), hspec(2, 128), tok(768), tok(256)],
        out_shape=[heads(4), heads(2), heads(2, 128), heads(4), heads(2), heads(2, 128),
                   jax.ShapeDtypeStruct((b, l, 768), BF16), jax.ShapeDtypeStruct((b, l, 256), BF16)],
        compiler_params=_cp(("parallel", "parallel"), VMEM_LIMIT),
        name="proj",
    )(x, sh, sc, w_in, cos, sin, bd, qg, kg)


_NT = (((1,), (1,)), ((), ()))


def _row_max(*parts):
    tiles = [p[:, i:i + 128] for p in parts for i in range(0, p.shape[1], 128)]
    m = tiles[0]
    for t in tiles[1:]:
        m = jnp.maximum(m, t)
    return jnp.max(m, axis=-1, keepdims=True)


def _sink_rows(sink_ref, j, tq):
    row = lax.broadcasted_iota(jnp.int32, (2 * tq, 1), 0)
    return jnp.where(row < tq, sink_ref[2 * j], sink_ref[2 * j + 1])


def _attn_full_kernel(*refs, has_sink, has_ctx, tq, tk, n_t, n_chunks):
    if has_ctx:
        sink_ref, q_ref, k_ref, v_ref, kc_ref, vc_ref, o_ref, m_ref, acc_ref, s0_ref, s1_ref = refs
        l_ctx = kc_ref.shape[2]
    else:
        sink_ref, q_ref, k_ref, v_ref, o_ref, m_ref, acc_ref, s0_ref, s1_ref = refs
    bufs = (s0_ref, s1_ref)
    j = pl.program_id(1)
    lane = lax.broadcasted_iota(jnp.int32, acc_ref.shape[1:], 1)
    for tile in range(n_t):
        if has_sink:
            m_ref[tile] = _sink_rows(sink_ref, j, tq)
            acc_ref[tile] = jnp.where(lane == HEAD_DIM, 1.0, 0.0)
        else:
            m_ref[tile] = jnp.full(m_ref.shape[1:], NEG_INF, F32)
            acc_ref[tile] = jnp.zeros(acc_ref.shape[1:], F32)

    def chunk(ref, t):
        c = t // n_t
        off = c * tk if isinstance(c, int) else pl.multiple_of(c * tk, tk)
        return ref[0, 0, pl.ds(off, tk), :]

    def scores(t, tile, dst, ctx=False):
        q = q_ref[0, :, tile * tq:(tile + 1) * tq, :].reshape(2 * tq, HEAD_DIM)
        if ctx:
            dst[:, :l_ctx] = lax.dot_general(q, kc_ref[0, 0], _NT, preferred_element_type=F32)
        else:
            dst[...] = lax.dot_general(q, chunk(k_ref, t), _NT, preferred_element_type=F32)

    def update(t, tile, src, ctx=False):
        s = src[:, :l_ctx] if ctx else src[...]
        v = vc_ref[0, 0] if ctx else chunk(v_ref, t)
        m_prev = m_ref[tile]
        m_new = jnp.maximum(m_prev, _row_max(s))
        p = jnp.exp((s - m_new).astype(BF16))
        acc_ref[tile] = jnp.exp(m_prev - m_new) * acc_ref[tile] + _dot(p, v)
        m_ref[tile] = m_new

    total = n_t * n_chunks
    per_iter = max(n_t, 2)
    scores(0, 0, s0_ref)

    def body(i, carry):
        t = per_iter * i
        for r in range(per_iter):
            scores(t + r + 1, (r + 1) % n_t, bufs[(r + 1) % 2])
            update(t + r, r % n_t, bufs[r % 2])
        return carry

    n_iter = (total - 1) // per_iter
    lax.fori_loop(0, n_iter, body, 0, unroll=4 // per_iter)
    first = per_iter * n_iter
    items = [(t, t % n_t, False) for t in range(first, total)]
    if has_ctx:
        items += [(None, tile, True) for tile in range(n_t)]
    for idx, (t, tile, ctx) in enumerate(items):
        if idx + 1 < len(items):
            nt, ntile, nctx = items[idx + 1]
            scores(nt, ntile, bufs[(first + idx + 1) % 2], nctx)
        update(t, tile, bufs[(first + idx) % 2], ctx)
    for tile in range(n_t):
        acc = acc_ref[tile]
        o = acc[:, :HEAD_DIM] * (1.0 / acc[:, HEAD_DIM:HEAD_DIM + 1])
        o_ref[0, tile * tq:(tile + 1) * tq, :] = jnp.concatenate([o[:tq], o[tq:]], axis=1).astype(BF16)


def _band_bias(tq):
    span = tq + 2 * WINDOW
    r = (_ar(2 * tq) % tq)[None, :, None]
    col = _ar(span)[None, None, :]
    shift = (_ar(3) * WINDOW)[:, None, None]
    return jnp.where(jnp.abs(col - shift - r) <= WINDOW, 0.0, NEG_INF).astype(F32)


def _attn_band_kernel(sink_ref, q_ref, k_ref, v_ref, kc_ref, vc_ref, bias_ref, o_ref, *,
                      tq, n_t, n_tiles, l_lat):
    j, g = pl.program_id(1), pl.program_id(2)
    span = tq + 2 * WINDOW
    sink = _sink_rows(sink_ref, j, tq)
    kc, vc = kc_ref[0, 0], vc_ref[0, 0]
    for tile in range(n_t):
        qi = g * n_t + tile
        start = pl.multiple_of(jnp.clip(qi * tq - WINDOW, 0, l_lat - span), WINDOW)
        kind = jnp.where(qi == 0, 0, jnp.where(qi == n_tiles - 1, 2, 1))
        q = q_ref[0, :, tile * tq:(tile + 1) * tq, :].reshape(2 * tq, HEAD_DIM)
        s_lat = (lax.dot_general(q, k_ref[0, 0, pl.ds(start, span), :], _NT, preferred_element_type=F32)
                 + bias_ref[kind])
        s_ctx = lax.dot_general(q, kc, _NT, preferred_element_type=F32)
        m = jnp.maximum(sink, _row_max(s_lat, s_ctx))
        acc = (_dot(jnp.exp((s_lat - m).astype(BF16)), v_ref[0, 0, pl.ds(start, span), :])
               + _dot(jnp.exp((s_ctx - m).astype(BF16)), vc))
        o = acc[:, :HEAD_DIM] * (1.0 / (acc[:, HEAD_DIM:HEAD_DIM + 1] + jnp.exp(sink - m)))
        o_ref[0, tile * tq:(tile + 1) * tq, :] = jnp.concatenate([o[:tq], o[tq:]], axis=1).astype(BF16)


def _attention(q, k, v, sink, *, banded, kc=None, vc=None):
    b, _, lq, _ = q.shape
    lk = k.shape[2]
    tq = 256
    has_sink = sink is not None
    if not has_sink:
        sink = jnp.zeros((N_HEADS,), F32)
    smem = pl.BlockSpec(memory_space=pltpu.SMEM)
    whole = lambda a: pl.BlockSpec((1, 1) + a.shape[2:], lambda bb, j, g: (bb, j, 0, 0))
    out_shape = jax.ShapeDtypeStruct((b, lq, GROUP_W), BF16)
    if banded:
        assert lq == lk and lk >= tq + 2 * WINDOW and tq % WINDOW == 0
        n_tiles = lq // tq
        n_t = next(t for t in (4, 2, 1) if n_tiles % t == 0)
        bias = _band_bias(tq)
        return pl.pallas_call(
            functools.partial(_attn_band_kernel, tq=tq, n_t=n_t, n_tiles=n_tiles, l_lat=lk),
            grid=(b, N_KV, n_tiles // n_t),
            in_specs=[smem,
                      pl.BlockSpec((1, 2, n_t * tq, HEAD_DIM), lambda bb, j, g: (bb, j, g, 0)),
                      whole(k), whole(v), whole(kc), whole(vc),
                      pl.BlockSpec(bias.shape, lambda bb, j, g: (0, 0, 0))],
            out_specs=pl.BlockSpec((1, n_t * tq, 128), lambda bb, j, g: (bb, g, j)),
            out_shape=out_shape,
            compiler_params=_cp(("parallel", "parallel", "parallel"), VMEM_LIMIT),
            name="attn_band",
        )(sink, q, k, v, kc, vc, bias)
    has_ctx = kc is not None
    tk = next(t for t in (1024, 768, 512, 256) if lk % t == 0)
    assert not has_ctx or kc.shape[2] <= tk
    n_t = 2 if lq % (2 * tq) == 0 else 1
    kern = functools.partial(_attn_full_kernel, has_sink=has_sink, has_ctx=has_ctx, tq=tq, tk=tk, n_t=n_t,
                             n_chunks=lk // tk)
    extra = (kc, vc) if has_ctx else ()
    return pl.pallas_call(
        kern,
        grid=(b, N_KV, lq // (n_t * tq)),
        in_specs=[smem,
                  pl.BlockSpec((1, 2, n_t * tq, HEAD_DIM), lambda bb, j, g: (bb, j, g, 0)),
                  whole(k), whole(v), *[whole(a) for a in extra]],
        out_specs=pl.BlockSpec((1, n_t * tq, 128), lambda bb, j, g: (bb, g, j)),
        out_shape=out_shape,
        scratch_shapes=[pltpu.VMEM((n_t, 2 * tq, 1), F32), pltpu.VMEM((n_t, 2 * tq, 2 * HEAD_DIM), F32),
                        pltpu.VMEM((2 * tq, tk), F32), pltpu.VMEM((2 * tq, tk), F32)],
        compiler_params=_cp(("parallel", "parallel", "parallel"), VMEM_LIMIT),
        name="attn_full",
    )(sink, q, k, v, *extra)


def _shift_rows(z, prev_row, next_row):
    n = z.shape[0]
    row = lax.broadcasted_iota(jnp.int32, (n, 1), 0)
    dn = jnp.where(row == 0, prev_row, pltpu.roll(z, 1, 0))
    up = jnp.where(row == n - 1, next_row, pltpu.roll(z, n - 1, 0))
    return dn, up


def _hy_gate_kernel(z_ref, zp_ref, zn_ref, w_ref, b_ref, vx_ref, x0_ref, *, n_tiles):
    i = pl.program_id(1)
    z = z_ref[0].astype(F32)
    prev_row = zp_ref[0].astype(F32)[HALO - 1:HALO] * (i > 0).astype(F32)
    next_row = zn_ref[0].astype(F32)[0:1] * (i < n_tiles - 1).astype(F32)
    dn, up = _shift_rows(z, prev_row, next_row)
    w = w_ref[...]
    u = dn * w[0:1] + z * w[1:2] + up * w[2:3] + b_ref[...]
    x0_ref[0] = u[:, :GROUP_W].astype(BF16)
    vx_ref[0] = (u[:, 2 * GROUP_W:] * u[:, GROUP_W:2 * GROUP_W]).astype(BF16)


def _halo_specs(tl, l, w):
    nb = l // HALO
    per = tl // HALO
    return (pl.BlockSpec((1, tl, w), lambda bb, i: (bb, i, 0)),
            pl.BlockSpec((1, HALO, w), lambda bb, i: (bb, jnp.maximum(i * per - 1, 0), 0)),
            pl.BlockSpec((1, HALO, w), lambda bb, i: (bb, jnp.minimum((i + 1) * per, nb - 1), 0)))


def _hy_gate(zh, conv_w, conv_b):
    b, l, w = zh.shape
    tl = min(l, 512)
    out = jax.ShapeDtypeStruct((b, l, GROUP_W), BF16)
    ospec = pl.BlockSpec((1, tl, GROUP_W), lambda bb, i: (bb, i, 0))
    return pl.pallas_call(
        functools.partial(_hy_gate_kernel, n_tiles=l // tl),
        grid=(b, l // tl),
        in_specs=[*_halo_specs(tl, l, w),
                  pl.BlockSpec((3, w), lambda bb, i: (0, 0)),
                  pl.BlockSpec((1, w), lambda bb, i: (0, 0))],
        out_specs=[ospec, ospec],
        out_shape=[out, out],
        compiler_params=_cp(("parallel", "parallel")),
        name="hy_gate",
    )(zh, zh, zh, conv_w, conv_b.reshape(1, w))


def _hy_filter_kernel(fc_ref, w1_ref, b1_ref, fr_ref, w2_ref, b2_ref, w3_ref, dl_ref, k_ref, *, l, tr):
    base = pl.program_id(1) * tr

    def position(m):
        return jnp.where(m < l, m, 2 * l - m).astype(F32)

    m_row = base + lax.broadcasted_iota(jnp.int32, (1, tr), 1)
    t_row = position(m_row)
    feat_id = lax.broadcasted_iota(jnp.int32, (fc_ref.shape[0], 1), 0)
    phase = fc_ref[...] * t_row
    feat = jnp.where(feat_id == 0, t_row / max(l - 1, 1),
                     jnp.where(feat_id <= HYENA_BANDS, jnp.cos(phase),
                               jnp.where(feat_id <= 2 * HYENA_BANDS, -jnp.sin(phase), 0.0)))
    hp = functools.partial(jnp.dot, precision=lax.Precision.HIGHEST, preferred_element_type=F32)
    fr = fr_ref[0]
    h = jnp.sin(fr * (hp(w1_ref[0], feat) + b1_ref[0]))
    h = jnp.sin(fr * (hp(w2_ref[0], h) + b2_ref[0]))
    h = hp(h.T, w3_ref[0])
    m = base + lax.broadcasted_iota(jnp.int32, (tr, 1), 0)
    dec = jnp.exp(-(position(m) / max(l - 1, 1)) * dl_ref[...])
    hf, hb = h[:, :GROUP_W] * dec, h[:, GROUP_W:] * dec
    k = jnp.where(m < l, hf, hb)
    k = jnp.where(m == 0, hf + hb, k)
    k_ref[0] = jnp.where(m == l, 0.0, k).astype(BF16)


def _hy_filter(l, w1, b1, fr, w2, b2, w3):
    depth, emb, hid = w1.shape
    tr = min(2 * l, 2048)
    n_feat = -(-emb // 8) * 8
    bands = jnp.linspace(1e-4, HYENA_BANDS - 1, HYENA_BANDS, dtype=F32) * (2.0 * math.pi / l)
    fc = jnp.zeros((n_feat, 1), F32).at[1:1 + HYENA_BANDS, 0].set(bands).at[1 + HYENA_BANDS:emb, 0].set(bands)
    w1t = jnp.zeros((depth, hid, n_feat), F32).at[:, :, :emb].set(jnp.swapaxes(w1, 1, 2))
    w2t = jnp.swapaxes(w2, 1, 2)
    min_decay = math.log(HYENA_TARGET) / HYENA_SLOW_DECAY
    max_decay = math.log(HYENA_TARGET) / HYENA_FAST_DECAY
    deltas = jnp.abs(jnp.linspace(min_decay, max_decay, GROUP_W, dtype=F32)).reshape(1, GROUP_W)
    lay = lambda a: pl.BlockSpec((1,) + a.shape[1:], lambda d, i: (d,) + (0,) * (a.ndim - 1))
    fix = lambda a: pl.BlockSpec(a.shape, lambda d, i: (0,) * a.ndim)
    b1, fr, b2 = (a.reshape(depth, hid, 1) for a in (b1, fr, b2))
    return pl.pallas_call(
        functools.partial(_hy_filter_kernel, l=l, tr=tr),
        grid=(depth, 2 * l // tr),
        in_specs=[fix(fc), lay(w1t), lay(b1), lay(fr), lay(w2t), lay(b2), lay(w3), fix(deltas)],
        out_specs=pl.BlockSpec((1, tr, GROUP_W), lambda d, i: (d, i, 0)),
        out_shape=jax.ShapeDtypeStruct((depth, 2 * l, GROUP_W), BF16),
        compiler_params=_cp(("parallel", "parallel")),
        name="hy_filter",
    )(fc, w1t, b1, fr, w2t, b2, w3, deltas)


ROW_TILE = 8


def _lead_kron(f):
    return jnp.kron(f.astype(F32), jnp.eye(ROW_TILE, dtype=F32)).astype(BF16)


def _lead_rows(n):
    return next(r for r in (4 * ROW_TILE, 2 * ROW_TILE, ROW_TILE) if n % r == 0)


def _lead_mm_kernel(f_ref, x_ref, o_ref):
    k, rows, c = x_ref.shape[1:]
    for j in range(rows // ROW_TILE):
        rs = slice(j * ROW_TILE, (j + 1) * ROW_TILE)
        y = _dot(f_ref[...], x_ref[0, :, rs, :].reshape(k * ROW_TILE, c))
        o_ref[0, :, rs, :] = y.reshape(-1, ROW_TILE, c).astype(o_ref.dtype)


def _lead_mm(fk, x):
    b, k, n, c = x.shape
    m = fk.shape[0] // ROW_TILE
    rows = _lead_rows(n)
    return pl.pallas_call(
        _lead_mm_kernel,
        grid=(b, n // rows),
        in_specs=[pl.BlockSpec(fk.shape, lambda bb, i: (0, 0)),
                  pl.BlockSpec((1, k, rows, c), lambda bb, i: (bb, 0, i, 0))],
        out_specs=pl.BlockSpec((1, m, rows, c), lambda bb, i: (bb, 0, i, 0)),
        out_shape=jax.ShapeDtypeStruct((b, m, n, c), BF16),
        compiler_params=_cp(("parallel", "parallel"), VMEM_LIMIT),
        name="lead_mm",
    )(fk, x)


def _hy_spec_kernel(a_ref, mf_ref, k_ref):
    for d in range(a_ref.shape[0]):
        x = a_ref[d, :, 0].reshape(2 * a_ref.shape[3], GROUP_W)
        k_ref[d, 0] = _dot(mf_ref[0], x)


def _hy_mid_kernel(a_ref, mf_ref, mi_ref, k_ref, o_ref):
    n2 = a_ref.shape[3]
    k = k_ref[0, 0]
    kr, ki = k[:n2], k[n2:]
    for bb in range(a_ref.shape[0]):
        x = a_ref[bb, :, 0].reshape(2 * n2, GROUP_W)
        y = _dot(mf_ref[0], x)
        yr, yi = y[:n2], y[n2:]
        z = jnp.concatenate([yr * kr - yi * ki, yr * ki + yi * kr], axis=0).astype(BF16)
        o_ref[bb, :, 0] = _dot(mi_ref[0], z).astype(BF16).reshape(2, n2, GROUP_W)


def _hy_spectrum(a, mf):
    depth, _, n1, n2, w = a.shape
    return pl.pallas_call(
        _hy_spec_kernel,
        grid=(n1,),
        in_specs=[pl.BlockSpec((depth, 2, 1, n2, w), lambda k1: (0, 0, k1, 0, 0)),
                  pl.BlockSpec((1, 2 * n2, 2 * n2), lambda k1: (k1, 0, 0))],
        out_specs=pl.BlockSpec((depth, 1, 2 * n2, w), lambda k1: (0, k1, 0, 0)),
        out_shape=jax.ShapeDtypeStruct((depth, n1, 2 * n2, w), F32),
        compiler_params=_cp(("parallel",)),
        name="hy_spectrum",
    )(a, mf)


def _hy_mid(a, mf, mi, kspec, layer):
    b, _, n1, n2, w = a.shape
    blk = pl.BlockSpec((b, 2, 1, n2, w), lambda k1: (0, 0, k1, 0, 0))
    mat = pl.BlockSpec((1, 2 * n2, 2 * n2), lambda k1: (k1, 0, 0))
    return pl.pallas_call(
        _hy_mid_kernel,
        grid=(n1,),
        in_specs=[blk, mat, mat,
                  pl.BlockSpec((1, 1, 2 * n2, w), lambda k1: (layer, k1, 0, 0))],
        out_specs=blk,
        out_shape=jax.ShapeDtypeStruct(a.shape, BF16),
        compiler_params=_cp(("parallel",)),
        name="hy_mid",
    )(a, mf, mi, kspec)


def _hy_out_kernel(g_ref, b_ref, vx_ref, x0_ref, skip_ref, o_ref):
    k, rows, c = b_ref.shape[1:]
    for j in range(rows // ROW_TILE):
        rs = slice(j * ROW_TILE, (j + 1) * ROW_TILE)
        y = _dot(g_ref[...], b_ref[0, :, rs, :].reshape(k * ROW_TILE, c)).reshape(-1, ROW_TILE, c)
        vx = vx_ref[0, :, rs, :].astype(F32)
        o_ref[0, :, rs, :] = ((y + skip_ref[...] * vx) * x0_ref[0, :, rs, :].astype(F32)).astype(BF16)


def _hy_out(gk, bo, vx, x0, skip):
    b, k, n, c = bo.shape
    m = gk.shape[0] // ROW_TILE
    rows = _lead_rows(n)
    sig = pl.BlockSpec((1, m, rows, c), lambda bb, i: (bb, 0, i, 0))
    return pl.pallas_call(
        _hy_out_kernel,
        grid=(b, n // rows),
        in_specs=[pl.BlockSpec(gk.shape, lambda bb, i: (0, 0)),
                  pl.BlockSpec((1, k, rows, c), lambda bb, i: (bb, 0, i, 0)),
                  sig, sig, pl.BlockSpec((1, c), lambda bb, i: (0, 0))],
        out_specs=sig,
        out_shape=jax.ShapeDtypeStruct((b, m, n, c), BF16),
        compiler_params=_cp(("parallel", "parallel"), VMEM_LIMIT),
        name="hy_out",
    )(gk, bo, vx, x0, skip)


def _hy_small_kernel(vx_ref, x0_ref, k_ref, fd_ref, ff_ref, gd_ref, skip_ref, o_ref):
    nf = fd_ref.shape[0] // 2
    vx = vx_ref[0]
    u = _dot(fd_ref[...], vx)
    k = _dot(ff_ref[...], k_ref[0])
    ur, ui, kr, ki = u[:nf], u[nf:], k[:nf], k[nf:]
    z = jnp.concatenate([ur * kr - ui * ki, ur * ki + ui * kr], axis=0).astype(BF16)
    y = _dot(gd_ref[...], z)
    o_ref[0] = ((y + skip_ref[...] * vx.astype(F32)) * x0_ref[0].astype(F32)).astype(BF16)


def _hy_small(vx, x0, kfilt, layer, fd, ff, gd, skip):
    b, l, w = vx.shape
    sig = pl.BlockSpec((1, l, w), lambda bb: (bb, 0, 0))
    full = lambda a: pl.BlockSpec(a.shape, lambda bb: (0,) * a.ndim)
    return pl.pallas_call(
        _hy_small_kernel,
        grid=(b,),
        in_specs=[sig, sig, pl.BlockSpec((1, 2 * l, w), lambda bb: (layer, 0, 0)),
                  full(fd), full(ff), full(gd), full(skip)],
        out_specs=sig,
        out_shape=jax.ShapeDtypeStruct((b, l, w), BF16),
        compiler_params=_cp(("parallel",)),
        name="hy_small",
    )(vx, x0, kfilt, fd, ff, gd, skip)


def _fn_first_kernel(z_ref, m_ref, o_ref):
    n2 = z_ref.shape[1]
    x = z_ref[0].reshape(n2 * 8, GROUP_W)
    o_ref[0] = _dot(m_ref[0], x).astype(BF16).reshape(2, 8, n2, GROUP_W)


def _fn_first(zf, mats):
    b, n2, n1, w = zf.shape
    return pl.pallas_call(
        _fn_first_kernel,
        grid=(n1 // 8, b),
        in_specs=[pl.BlockSpec((1, n2, 8, w), lambda i, bb: (bb, 0, i, 0)),
                  pl.BlockSpec((1, 16 * n2, 8 * n2), lambda i, bb: (i, 0, 0))],
        out_specs=pl.BlockSpec((1, 2, 8, n2, w), lambda i, bb: (bb, 0, i, 0, 0)),
        out_shape=jax.ShapeDtypeStruct((b, 2, n1, n2, w), BF16),
        compiler_params=_cp(("parallel", "parallel"), VMEM_LIMIT),
        name="fn_first",
    )(zf, mats)


def _fn_small_kernel(u_ref, cl_ref, sl_ref, c_ref, s_ref, w_ref, b_ref, o_ref, *, scale):
    u = u_ref[0]
    uc = _dot(u, c_ref[...]).astype(BF16)
    us = _dot(u, s_ref[...]).astype(BF16)
    f = (_dot(cl_ref[...], uc) - _dot(sl_ref[...], us)) * scale
    o_ref[0] = (_dot(f.astype(BF16), w_ref[...]) + b_ref[...]).astype(BF16)


def _fn_small(zf, cl, sl, cbd, sbd, w, bias, scale):
    b, l, c = zf.shape
    sig = pl.BlockSpec((1, l, c), lambda bb: (bb, 0, 0))
    full = lambda z: pl.BlockSpec(z.shape, lambda bb: (0,) * z.ndim)
    return pl.pallas_call(
        functools.partial(_fn_small_kernel, scale=scale),
        grid=(b,),
        in_specs=[sig, full(cl), full(sl), full(cbd), full(sbd), full(w), full(bias)],
        out_specs=sig,
        out_shape=jax.ShapeDtypeStruct((b, l, c), BF16),
        compiler_params=_cp(("parallel",)),
        name="fn_small",
    )(zf, cl, sl, cbd, sbd, w, bias)


def _merge_kernel(*refs, alpha, fnet_scale):
    if fnet_scale is None:
        oa_ref, ob_ref, oc_ref, od_ref = refs[:4]
        od = od_ref[0].astype(F32)
        rest = refs[4:]
    else:
        oa_ref, ob_ref, oc_ref, ar_ref, ai_ref, c_ref, s_ref, fw_ref, fb_ref = refs[:9]
        t = (_dot(ar_ref[0, 0], c_ref[...]) + _dot(ai_ref[0, 0], s_ref[...])) * fnet_scale
        od = _dot(t.astype(BF16), fw_ref[...]) + fb_ref[...]
        rest = refs[9:]
    x_ref, g1_ref, gn_ref, w_ref, b_ref, lng_ref, lnb_ref, o_ref = rest
    parts = []
    for idx, v in enumerate((oa_ref[0].astype(F32), ob_ref[0].astype(F32), oc_ref[0].astype(F32), od)):
        ms = jnp.mean(v * v, axis=-1, keepdims=True)
        gain = gn_ref[:, idx * GROUP_W:(idx + 1) * GROUP_W]
        parts.append((v * lax.rsqrt(ms + LN_EPS) * gain).astype(BF16))
    y = _dot(jnp.concatenate(parts, axis=1), w_ref[...]) + b_ref[...]
    r = alpha * x_ref[0] + g1_ref[0] * y
    o_ref[0] = _layer_norm(r) * lng_ref[...] + lnb_ref[...]


def _merge(oa, ob, oc, od, x, g1, gn, w_out, b_out, ln_g, ln_b, alpha):
    b, l, d = x.shape
    tm = min(l, 1024)
    grp = pl.BlockSpec((1, tm, GROUP_W), lambda bb, i: (bb, i, 0))
    tok = pl.BlockSpec((1, tm, d), lambda bb, i: (bb, i, 0))
    full = lambda a: pl.BlockSpec(a.shape, lambda bb, i: (0,) * a.ndim)
    if isinstance(od, tuple):
        a, cbd, sbd, fw, fb, fnet_scale = od
        part = lambda p: pl.BlockSpec((1, 1, tm, GROUP_W), lambda bb, i: (bb, p, i, 0))
        od_args = (a, a, cbd, sbd, fw, fb)
        od_specs = [part(0), part(1), full(cbd), full(sbd), full(fw), full(fb)]
    else:
        fnet_scale, od_args, od_specs = None, (od,), [grp]
    return pl.pallas_call(
        functools.partial(_merge_kernel, alpha=alpha, fnet_scale=fnet_scale),
        grid=(b, l // tm),
        in_specs=[grp, grp, grp, *od_specs, tok, pl.BlockSpec((1, 1, d), lambda bb, i: (bb, 0, 0)),
                  full(gn), full(w_out), full(b_out), full(ln_g), full(ln_b)],
        out_specs=tok,
        out_shape=jax.ShapeDtypeStruct((b, l, d), F32),
        compiler_params=_cp(("parallel", "parallel"), VMEM_LIMIT),
        name="merge",
    )(oa, ob, oc, *od_args, x, g1, gn, w_out, b_out, ln_g, ln_b)


def _ffn_kernel(x_ref, xp_ref, xn_ref, sh_ref, sc_ref, g_ref, wup_ref, vec_ref, wd_ref, bd_ref,
                lng_ref, lnb_ref, o_ref, h_ref, acc_ref, u0_ref, u1_ref, *, alpha, tm, n_chunks, n_tiles):
    i = pl.program_id(1)
    sh, sc = sh_ref[0], sc_ref[0]
    cw = FFN_CHUNK

    def hmod(v):
        return (_layer_norm(v) * (1.0 + sc) + sh).astype(BF16)

    h_ref[0:HALO] = hmod(xp_ref[0])
    h_ref[HALO:HALO + tm] = hmod(x_ref[0])
    h_ref[HALO + tm:] = hmod(xn_ref[0])
    acc_ref[...] = jnp.zeros_like(acc_ref)
    row8 = lax.broadcasted_iota(jnp.int32, (8, 1), 0)
    pad_top = (row8 == 7) & (i == 0)
    pad_bot = (row8 == 0) & (i == n_tiles - 1)

    def up(c, dst):
        dst[...] = _dot(h_ref[...], wup_ref[c])

    def down(c, src):
        vec = vec_ref[c]
        bias, w0, w1, w2 = vec[0:1], vec[1:2], vec[2:3], vec[3:4]
        src[HALO - 8:HALO] = jnp.where(pad_top, -bias, src[HALO - 8:HALO])
        src[HALO + tm:HALO + tm + 8] = jnp.where(pad_bot, -bias, src[HALO + tm:HALO + tm + 8])
        conv = (src[HALO - 1:HALO - 1 + tm] * w0 + src[HALO:HALO + tm] * w1 + src[HALO + 1:HALO + 1 + tm] * w2
                + (vec[4:5] + bias * (w0 + w1 + w2)))
        a, g = conv[:, :cw], conv[:, cw:]
        act = (a * jax.nn.sigmoid(a) * g).astype(BF16)
        acc_ref[...] += _dot(act, wd_ref[c])

    up(0, u0_ref)

    def body(p, carry):
        c = 2 * p
        up(c + 1, u1_ref)
        down(c, u0_ref)
        up(c + 2, u0_ref)
        down(c + 1, u1_ref)
        return carry

    n_pairs = (n_chunks - 1) // 2
    lax.fori_loop(0, n_pairs, body, 0)
    c = 2 * n_pairs
    if c + 1 < n_chunks:
        up(c + 1, u1_ref)
    down(c, u0_ref)
    if c + 1 < n_chunks:
        down(c + 1, u1_ref)
    r = alpha * x_ref[0] + g_ref[0] * (acc_ref[...] + bd_ref[...])
    o_ref[0] = _layer_norm(r) * lng_ref[...] + lnb_ref[...]


def _ffn(x, sh, sc, g2, wup, vec, wd, b_down, ln_g, ln_b, alpha):
    b, l, d = x.shape
    tm = min(l, 1024)
    n_chunks = wd.shape[0]
    main, prev, nxt = _halo_specs(tm, l, d)
    mod = pl.BlockSpec((1, 1, d), lambda bb, i: (bb, 0, 0))
    full = lambda a: pl.BlockSpec(a.shape, lambda bb, i: (0,) * a.ndim, pipeline_mode=pl.Buffered(1))
    return pl.pallas_call(
        functools.partial(_ffn_kernel, alpha=alpha, tm=tm, n_chunks=n_chunks, n_tiles=l // tm),
        grid=(b, l // tm),
        in_specs=[main, prev, nxt, mod, mod, mod, full(wup), full(vec), full(wd),
                  full(b_down), full(ln_g), full(ln_b)],
        out_specs=main,
        out_shape=jax.ShapeDtypeStruct((b, l, d), F32),
        scratch_shapes=[pltpu.VMEM((tm + 2 * HALO, d), BF16), pltpu.VMEM((tm, d), F32),
                        pltpu.VMEM((tm + 2 * HALO, 2 * FFN_CHUNK), F32),
                        pltpu.VMEM((tm + 2 * HALO, 2 * FFN_CHUNK), F32)],
        compiler_params=_cp(("parallel", "parallel"), VMEM_LIMIT),
        name="ffn",
    )(x, x, x, sh, sc, g2, wup, vec, wd, b_down, ln_g, ln_b)


def _cos_sin(idx, n):
    ang = (idx % n).astype(F32) * (2.0 * math.pi / n)
    return jnp.cos(ang), jnp.sin(ang)


def _cplx_rows(re, im):
    return jnp.concatenate([re, im], axis=-2)


def _cplx_block(re, im):
    return jnp.concatenate([jnp.concatenate([re, -im], axis=-1), jnp.concatenate([im, re], axis=-1)], axis=-2)


def _ar(n):
    return jnp.arange(n, dtype=jnp.int32)


def _hyena_tables(l):
    n, n2 = 2 * l, HYENA_N2
    n1 = n // n2
    c, s = _cos_sin(_ar(n1)[:, None] * _ar(n1)[None, :], n1)
    f_full = _cplx_rows(c, -s).astype(BF16)
    g = (jnp.concatenate([c, -s], axis=1)[:n1 // 2] / n).astype(BF16)
    k1, k2, m2 = _ar(n1)[:, None, None], _ar(n2)[None, :, None], _ar(n2)[None, None, :]
    c, s = _cos_sin(m2 * k1 + n1 * m2 * k2, n)
    mf = _cplx_block(c, -s).astype(BF16)
    return dict(f_half=_lead_kron(f_full[:, :n1 // 2]), f_full=_lead_kron(f_full), g=_lead_kron(g),
                mf=mf, mi=jnp.swapaxes(mf, 1, 2))


def _fnet_tables(l):
    n2 = FNET_N2
    n1 = l // n2
    nblk = n1 // 8
    blk, j = _ar(nblk)[:, None, None, None], _ar(8)[None, :, None, None]
    k2, m2 = _ar(n2)[None, None, :, None], _ar(n2)[None, None, None, :]
    c, s = _cos_sin(n1 * m2 * k2 + (8 * blk + j) * k2, l)
    eye = jnp.eye(8, dtype=F32)
    emb = lambda t: jnp.einsum('bjkn,ji->bjkni', t, eye).reshape(nblk, 8 * n2, n2 * 8)
    first = jnp.concatenate([emb(c), emb(-s)], axis=1).astype(BF16)
    c, s = _cos_sin(_ar(n1)[:, None] * _ar(n1)[None, :], n1)
    second = _cplx_block(c, -s).astype(BF16)
    return dict(first=first, second=_lead_kron(second))


def _dense_tables(l):
    n = 2 * l
    c, s = _cos_sin(_ar(n)[:, None] * _ar(n)[None, :], n)
    ff = _cplx_rows(c, -s).astype(BF16)
    gd = (jnp.concatenate([c, -s], axis=1)[:l] / n).astype(BF16)
    cl, sl = _cos_sin(_ar(l)[:, None] * _ar(l)[None, :], l)
    return dict(fd=ff[:, :l], ff=ff, gd=gd, cl=cl.astype(BF16), sl=sl.astype(BF16))


def _rope_tables(l):
    rows = l // GRID_W
    row = jnp.broadcast_to(jnp.arange(rows)[:, None], (rows, GRID_W)).reshape(-1).astype(F32)
    col = jnp.broadcast_to(jnp.arange(GRID_W)[None, :], (rows, GRID_W)).reshape(-1).astype(F32)
    half = HEAD_DIM // 2
    inv = ROPE_THETA ** (-jnp.arange(0, half, 2, dtype=F32) / half)
    ar, ac = row[:, None] * inv, col[:, None] * inv
    cos = jnp.concatenate([jnp.cos(ar), jnp.cos(ar), jnp.cos(ac), jnp.cos(ac)], axis=1)
    sin = jnp.concatenate([-jnp.sin(ar), jnp.sin(ar), -jnp.sin(ac), jnp.sin(ac)], axis=1)
    return jnp.tile(cos, (1, 2)), jnp.tile(sin, (1, 2))


def kernel(x, c, ctx, c_ctx, w_ada, b_ada, w_in, sink_a, q_norm_g, k_norm_g, hy_conv_w, hy_conv_b,
           hy_f_w1, hy_f_b1, hy_f_freq, hy_f_w2, hy_f_b2, hy_f_w3, hy_skip, fnet_w, fnet_b,
           out_norm_g, w_out, b_out, ln1_g, ln1_b, ffn_w_up, ffn_b_up, ffn_conv_w, ffn_conv_b,
           ffn_w_down, ffn_b_down, ln2_g, ln2_b):
    bsz, l, d = x.shape
    lc = ctx.shape[1]
    depth = w_ada.shape[0]
    d_ff = ffn_w_down.shape[1]
    alpha = (2 * depth) ** 0.25
    n2h, n2f = HYENA_N2, FNET_N2
    n1h, n1f = 2 * l // n2h, l // n2f

    ht, ft, dt = _hyena_tables(l), _fnet_tables(l), _dense_tables(lc)
    cos_l, sin_l = _rope_tables(l)
    cos_c, sin_c = jnp.ones((lc, 128), F32), jnp.zeros((lc, 128), F32)
    jj = _ar(GROUP_W)
    same = (jj[:, None] // FNET_GROUP_DIM) == (jj[None, :] // FNET_GROUP_DIM)
    bd_mean = (same.astype(F32) / HEAD_DIM).astype(BF16)
    cg, sg = _cos_sin(jj[:, None] * jj[None, :], FNET_GROUP_DIM)
    cbd, sbd = jnp.where(same, cg, 0.0).astype(BF16), jnp.where(same, sg, 0.0).astype(BF16)

    n_rows = -(-(bsz + 1) // 8) * 8
    cc = jnp.zeros((n_rows, d), F32).at[:bsz].set(c).at[bsz].set(c_ctx)
    mod = _ada(cc, w_ada, b_ada)

    k_lat = _hy_filter(l, hy_f_w1, hy_f_b1, hy_f_freq, hy_f_w2, hy_f_b2, hy_f_w3)
    k_ctx = _hy_filter(lc, hy_f_w1, hy_f_b1, hy_f_freq, hy_f_w2, hy_f_b2, hy_f_w3)
    a_k = _lead_mm(ht['f_full'], k_lat.reshape(depth, n1h, n2h, GROUP_W))
    k_spec = _hy_spectrum(a_k.reshape(depth, 2, n1h, n2h, GROUP_W), ht['mf'])

    n_chunks = -(-d_ff // FFN_CHUNK)
    pad = n_chunks * FFN_CHUNK - d_ff

    def chunked(v):
        v = v.reshape(v.shape[:-1] + (2, d_ff))
        v = jnp.pad(v, [(0, 0)] * (v.ndim - 1) + [(0, pad)])
        v = v.reshape(v.shape[:-2] + (2, n_chunks, FFN_CHUNK))
        v = jnp.swapaxes(v, -3, -2).reshape(v.shape[:-3] + (n_chunks, 2 * FFN_CHUNK))
        return jnp.moveaxis(v, -2, 0)

    for layer in range(depth):
        m6 = mod[layer].reshape(n_rows, 6, d)
        lat = [m6[:bsz, i][:, None, :] for i in range(6)]
        con = [jnp.broadcast_to(m6[bsz, i][None, None, :], (bsz, 1, d)) for i in range(6)]
        w_in_l = w_in[layer].astype(BF16)
        qg = jnp.tile(q_norm_g[layer], N_HEADS).reshape(1, -1)
        kg = jnp.tile(k_norm_g[layer], N_KV).reshape(1, -1)
        w_out_l = w_out[layer].astype(BF16)
        gn, b_out_l = out_norm_g[layer].reshape(1, d), b_out[layer].reshape(1, d)
        l1g, l1b = ln1_g[layer].reshape(1, d), ln1_b[layer].reshape(1, d)
        l2g, l2b = ln2_g[layer].reshape(1, d), ln2_b[layer].reshape(1, d)
        fw, fb = fnet_w[layer].astype(BF16), fnet_b[layer].reshape(1, GROUP_W)
        skip = hy_skip[layer].reshape(1, GROUP_W)
        wup = chunked(ffn_w_up[layer]).astype(BF16)
        vec = chunked(jnp.concatenate([ffn_b_up[layer][None], ffn_conv_w[layer], ffn_conv_b[layer][None],
                                       jnp.zeros((3, 2 * d_ff), F32)], axis=0))
        wd = jnp.pad(ffn_w_down[layer], ((0, pad), (0, 0))).reshape(n_chunks, FFN_CHUNK, d).astype(BF16)
        bdn = ffn_b_down[layer].reshape(1, d)
        last = layer == depth - 1

        cqa, cka, cva, cqb, ckb, cvb, czh, czf = _proj(ctx, con[0], con[1], w_in_l, cos_c, sin_c,
                                                       bd_mean, qg, kg)
        qa, ka, va, qb, kb, vb, zh, zf = _proj(x, lat[0], lat[1], w_in_l, cos_l, sin_l, bd_mean, qg, kg)
        oa = _attention(qa, ka, va, sink_a[layer], banded=True, kc=cka, vc=cva)
        ob = _attention(qb, kb, vb, None, banded=False, kc=ckb, vc=cvb)

        vx, x0 = _hy_gate(zh, hy_conv_w[layer], hy_conv_b[layer])
        vx4, x04 = (a.reshape(bsz, n1h // 2, n2h, GROUP_W) for a in (vx, x0))
        a1 = _lead_mm(ht['f_half'], vx4)
        bo = _hy_mid(a1.reshape(bsz, 2, n1h, n2h, GROUP_W), ht['mf'], ht['mi'], k_spec, layer)
        oc = _hy_out(ht['g'], bo.reshape(bsz, 2 * n1h, n2h, GROUP_W), vx4, x04, skip).reshape(bsz, l, GROUP_W)

        f1 = _fn_first(zf.reshape(bsz, n2f, n1f, GROUP_W), ft['first'])
        f2 = _lead_mm(ft['second'], f1.reshape(bsz, 2 * n1f, n2f, GROUP_W))
        od = (f2.reshape(bsz, 2, l, GROUP_W), cbd, sbd, fw, fb, (l * FNET_GROUP_DIM) ** -0.5)

        x = _merge(oa, ob, oc, od, x, lat[2], gn, w_out_l, b_out_l, l1g, l1b, alpha)
        x = _ffn(x, lat[3], lat[4], lat[5], wup, vec, wd, bdn, l2g, l2b, alpha)

        if not last:
            oac = _attention(cqa, cka, cva, sink_a[layer], banded=False)
            obc = _attention(cqb, ckb, cvb, None, banded=False)
            cvx, cx0 = _hy_gate(czh, hy_conv_w[layer], hy_conv_b[layer])
            occ = _hy_small(cvx, cx0, k_ctx, layer, dt['fd'], dt['ff'], dt['gd'], skip)
            odc = _fn_small(czf, dt['cl'], dt['sl'], cbd, sbd, fw, fb, (lc * FNET_GROUP_DIM) ** -0.5)
            ctx = _merge(oac, obc, occ, odc, ctx, con[2], gn, w_out_l, b_out_l, l1g, l1b, alpha)
            ctx = _ffn(ctx, con[3], con[4], con[5], wup, vec, wd, bdn, l2g, l2b, alpha)
    return x
```

```python
import functools
import math

import jax
import jax.numpy as jnp
from jax import lax
from jax.experimental import pallas as pl
from jax.experimental.pallas import tpu as pltpu

F32 = jnp.float32
BF16 = jnp.bfloat16

HEAD_DIM = 64
GROUP_W = 256
N_HEADS = 4
N_KV = 2
WINDOW = 128
GRID_W = 64
ROPE_THETA = 10000.0
FNET_GROUP_DIM = 64
HYENA_BANDS = 8
HYENA_FAST_DECAY = 0.3
HYENA_SLOW_DECAY = 1.5
HYENA_TARGET = 1e-2
LN_EPS = 1e-6
NEG_INF = -1e30
HYENA_N2 = 256
FNET_N2 = 128
FFN_CHUNK = 256
HALO = 16
VMEM_LIMIT = 56 * 1024 * 1024


def _cp(sem, vmem=None):
    return pltpu.CompilerParams(dimension_semantics=sem, vmem_limit_bytes=vmem)


def _layer_norm(x):
    mu = jnp.mean(x, axis=-1, keepdims=True)
    xc = x - mu
    var = jnp.mean(xc * xc, axis=-1, keepdims=True)
    return xc * lax.rsqrt(var + LN_EPS)


def _dot(a, b):
    return jnp.dot(a, b, preferred_element_type=F32)


def _ada_kernel(c_ref, w_ref, b_ref, o_ref):
    c = c_ref[...]
    s = (c * jax.nn.sigmoid(c)).astype(BF16)
    o_ref[0] = _dot(s, w_ref[0].astype(BF16)) + b_ref[0]


def _ada(cc, w_ada, b_ada):
    depth, d, n = w_ada.shape
    r = cc.shape[0]
    tn = 1536
    return pl.pallas_call(
        _ada_kernel,
        grid=(depth, n // tn),
        in_specs=[pl.BlockSpec((r, d), lambda l, j: (0, 0)),
                  pl.BlockSpec((1, d, tn), lambda l, j: (l, 0, j)),
                  pl.BlockSpec((1, 1, tn), lambda l, j: (l, 0, j))],
        out_specs=pl.BlockSpec((1, r, tn), lambda l, j: (l, 0, j)),
        out_shape=jax.ShapeDtypeStruct((depth, r, n), F32),
        compiler_params=_cp(("parallel", "parallel"), VMEM_LIMIT),
        name="ada",
    )(cc, w_ada, b_ada.reshape(depth, 1, n))


def _swap_halves(z):
    w = z.shape[1]
    lane = lax.broadcasted_iota(jnp.int32, z.shape, 1)
    return jnp.where(lane % 32 < 16, pltpu.roll(z, w - 16, 1), pltpu.roll(z, 16, 1))


def _proj_kernel(x_ref, sh_ref, sc_ref, w_ref, cos_ref, sin_ref, bd_ref, qg_ref, kg_ref,
                 qa_ref, ka_ref, va_ref, qb_ref, kb_ref, vb_ref, zh_ref, zf_ref, *, n_parts):
    tm = x_ref.shape[1]
    rows = tm // n_parts
    scale = HEAD_DIM ** -0.5
    for part in range(n_parts):
        rs = slice(part * rows, (part + 1) * rows)
        h = (_layer_norm(x_ref[0, rs]) * (1.0 + sc_ref[0]) + sh_ref[0]).astype(BF16)
        cos1, sin1 = cos_ref[rs], sin_ref[rs]
        cos2 = jnp.concatenate([cos1, cos1], axis=1)
        sin2 = jnp.concatenate([sin1, sin1], axis=1)

        def proj(lo, width):
            return _dot(h, w_ref[:, lo:lo + width])

        def rope(z):
            if z.shape[1] == 128:
                return z * cos1 + _swap_halves(z) * sin1
            return z * cos2 + _swap_halves(z) * sin2

        def head_norm(z, g):
            w = z.shape[1]
            ms = _dot((z * z).astype(BF16), bd_ref[:w, :w])
            return z * lax.rsqrt(ms + LN_EPS) * g

        def put_heads(ref, z):
            for hh in range(z.shape[1] // HEAD_DIM):
                ref[0, hh, rs] = z[:, hh * HEAD_DIM:(hh + 1) * HEAD_DIM].astype(BF16)

        def put_values(ref, z):
            lane = lax.broadcasted_iota(jnp.int32, (z.shape[0], HEAD_DIM), 1)
            ones_col = jnp.where(lane == 0, 1.0, 0.0)
            for hh in range(z.shape[1] // HEAD_DIM):
                ref[0, hh, rs] = jnp.concatenate([z[:, hh * HEAD_DIM:(hh + 1) * HEAD_DIM], ones_col],
                                                 axis=1).astype(BF16)

        put_heads(qa_ref, rope(proj(0, 256)) * scale)
        kv = proj(256, 256)
        put_heads(ka_ref, rope(kv[:, :128]))
        put_values(va_ref, kv[:, 128:])
        put_heads(qb_ref, rope(head_norm(proj(512, 256), qg_ref[...])) * scale)
        kv = proj(768, 256)
        put_heads(kb_ref, rope(head_norm(kv[:, :128], kg_ref[...])))
        put_values(vb_ref, kv[:, 128:])
        zh_ref[0, rs] = proj(1024, 768).astype(BF16)
        zf_ref[0, rs] = proj(1792, 256).astype(BF16)


def _proj(x, sh, sc, w_in, cos, sin, bd, qg, kg):
    b, l, d = x.shape
    tm = min(l, 1024)
    heads = lambda n, w=HEAD_DIM: jax.ShapeDtypeStruct((b, n, l, w), BF16)
    hspec = lambda n, w=HEAD_DIM: pl.BlockSpec((1, n, tm, w), lambda t, bb: (bb, 0, t, 0))
    tok = lambda w: pl.BlockSpec((1, tm, w), lambda t, bb: (bb, t, 0))
    vec = lambda w: pl.BlockSpec((1, 1, w), lambda t, bb: (bb, 0, 0))
    full = lambda a: pl.BlockSpec(a.shape, lambda t, bb: (0,) * a.ndim)
    tab = pl.BlockSpec((tm, 128), lambda t, bb: (t, 0))
    return pl.pallas_call(
        functools.partial(_proj_kernel, n_parts=1),
        grid=(l // tm, b),
        in_specs=[tok(d), vec(d), vec(d), full(w_in), tab, tab, full(bd), full(qg), full(kg)],
        out_specs=[hspec(4), hspec(2), hspec(2, 128), hspec(4), hspec(2), hspec(2, 128), tok(768), tok(256)],
        out_shape=[heads(4), heads(2), heads(2, 128), heads(4), heads(2), heads(2, 128),
                   jax.ShapeDtypeStruct((b, l, 768), BF16), jax.ShapeDtypeStruct((b, l, 256), BF16)],
        compiler_params=_cp(("parallel", "parallel"), VMEM_LIMIT),
        name="proj",
    )(x, sh, sc, w_in, cos, sin, bd, qg, kg)


_NT = (((1,), (1,)), ((), ()))


def _row_max(*parts):
    tiles = [p[:, i:i + 128] for p in parts for i in range(0, p.shape[1], 128)]
    m = tiles[0]
    for t in tiles[1:]:
        m = jnp.maximum(m, t)
    return jnp.max(m, axis=-1, keepdims=True)


def _sink_rows(sink_ref, j, tq):
    row = lax.broadcasted_iota(jnp.int32, (2 * tq, 1), 0)
    return jnp.where(row < tq, sink_ref[2 * j], sink_ref[2 * j + 1])


def _attn_full_kernel(*refs, has_sink, has_ctx, tq, tk, n_t, n_chunks):
    if has_ctx:
        sink_ref, q_ref, k_ref, v_ref, kc_ref, vc_ref, o_ref, m_ref, acc_ref, s0_ref, s1_ref = refs
        l_ctx = kc_ref.shape[2]
    else:
        sink_ref, q_ref, k_ref, v_ref, o_ref, m_ref, acc_ref, s0_ref, s1_ref = refs
    bufs = (s0_ref, s1_ref)
    j = pl.program_id(1)
    lane = lax.broadcasted_iota(jnp.int32, acc_ref.shape[1:], 1)
    for tile in range(n_t):
        if has_sink:
            m_ref[tile] = _sink_rows(sink_ref, j, tq)
            acc_ref[tile] = jnp.where(lane == HEAD_DIM, 1.0, 0.0)
        else:
            m_ref[tile] = jnp.full(m_ref.shape[1:], NEG_INF, F32)
            acc_ref[tile] = jnp.zeros(acc_ref.shape[1:], F32)

    def chunk(ref, t):
        c = t // n_t
        off = c * tk if isinstance(c, int) else pl.multiple_of(c * tk, tk)
        return ref[0, 0, pl.ds(off, tk), :]

    def scores(t, tile, dst, ctx=False):
        q = q_ref[0, :, tile * tq:(tile + 1) * tq, :].reshape(2 * tq, HEAD_DIM)
        if ctx:
            dst[:, :l_ctx] = lax.dot_general(q, kc_ref[0, 0], _NT, preferred_element_type=F32)
        else:
            dst[...] = lax.dot_general(q, chunk(k_ref, t), _NT, preferred_element_type=F32)

    def update(t, tile, src, ctx=False):
        s = src[:, :l_ctx] if ctx else src[...]
        v = vc_ref[0, 0] if ctx else chunk(v_ref, t)
        m_prev = m_ref[tile]
        m_new = jnp.maximum(m_prev, _row_max(s))
        p = jnp.exp((s - m_new).astype(BF16))
        acc_ref[tile] = jnp.exp(m_prev - m_new) * acc_ref[tile] + _dot(p, v)
        m_ref[tile] = m_new

    total = n_t * n_chunks
    per_iter = max(n_t, 2)
    scores(0, 0, s0_ref)

    def body(i, carry):
        t = per_iter * i
        for r in range(per_iter):
            scores(t + r + 1, (r + 1) % n_t, bufs[(r + 1) % 2])
            update(t + r, r % n_t, bufs[r % 2])
        return carry

    n_iter = (total - 1) // per_iter
    lax.fori_loop(0, n_iter, body, 0, unroll=4 // per_iter)
    first = per_iter * n_iter
    items = [(t, t % n_t, False) for t in range(first, total)]
    if has_ctx:
        items += [(None, tile, True) for tile in range(n_t)]
    for idx, (t, tile, ctx) in enumerate(items):
        if idx + 1 < len(items):
            nt, ntile, nctx = items[idx + 1]
            scores(nt, ntile, bufs[(first + idx + 1) % 2], nctx)
        update(t, tile, bufs[(first + idx) % 2], ctx)
    for tile in range(n_t):
        acc = acc_ref[tile]
        o = acc[:, :HEAD_DIM] * (1.0 / acc[:, HEAD_DIM:HEAD_DIM + 1])
        o_ref[0, tile * tq:(tile + 1) * tq, :] = jnp.concatenate([o[:tq], o[tq:]], axis=1).astype(BF16)


def _band_bias(tq):
    span = tq + 2 * WINDOW
    r = (_ar(2 * tq) % tq)[None, :, None]
    col = _ar(span)[None, None, :]
    shift = (_ar(3) * WINDOW)[:, None, None]
    return jnp.where(jnp.abs(col - shift - r) <= WINDOW, 0.0, NEG_INF).astype(F32)


def _attn_band_kernel(sink_ref, q_ref, k_ref, v_ref, kc_ref, vc_ref, bias_ref, o_ref, *,
                      tq, n_t, n_tiles, l_lat):
    j, g = pl.program_id(1), pl.program_id(2)
    span = tq + 2 * WINDOW
    sink = _sink_rows(sink_ref, j, tq)
    kc, vc = kc_ref[0, 0], vc_ref[0, 0]
    for tile in range(n_t):
        qi = g * n_t + tile
        start = pl.multiple_of(jnp.clip(qi * tq - WINDOW, 0, l_lat - span), WINDOW)
        kind = jnp.where(qi == 0, 0, jnp.where(qi == n_tiles - 1, 2, 1))
        q = q_ref[0, :, tile * tq:(tile + 1) * tq, :].reshape(2 * tq, HEAD_DIM)
        s_lat = (lax.dot_general(q, k_ref[0, 0, pl.ds(start, span), :], _NT, preferred_element_type=F32)
                 + bias_ref[kind])
        s_ctx = lax.dot_general(q, kc, _NT, preferred_element_type=F32)
        m = jnp.maximum(sink, _row_max(s_lat, s_ctx))
        acc = (_dot(jnp.exp((s_lat - m).astype(BF16)), v_ref[0, 0, pl.ds(start, span), :])
               + _dot(jnp.exp((s_ctx - m).astype(BF16)), vc))
        o = acc[:, :HEAD_DIM] * (1.0 / (acc[:, HEAD_DIM:HEAD_DIM + 1] + jnp.exp(sink - m)))
        o_ref[0, tile * tq:(tile + 1) * tq, :] = jnp.concatenate([o[:tq], o[tq:]], axis=1).astype(BF16)


def _attention(q, k, v, sink, *, banded, kc=None, vc=None):
    b, _, lq, _ = q.shape
    lk = k.shape[2]
    tq = 256
    has_sink = sink is not None
    if not has_sink:
        sink = jnp.zeros((N_HEADS,), F32)
    smem = pl.BlockSpec(memory_space=pltpu.SMEM)
    whole = lambda a: pl.BlockSpec((1, 1) + a.shape[2:], lambda bb, j, g: (bb, j, 0, 0))
    out_shape = jax.ShapeDtypeStruct((b, lq, GROUP_W), BF16)
    if banded:
        assert lq == lk and lk >= tq + 2 * WINDOW and tq % WINDOW == 0
        n_tiles = lq // tq
        n_t = next(t for t in (4, 2, 1) if n_tiles % t == 0)
        bias = _band_bias(tq)
        return pl.pallas_call(
            functools.partial(_attn_band_kernel, tq=tq, n_t=n_t, n_tiles=n_tiles, l_lat=lk),
            grid=(b, N_KV, n_tiles // n_t),
            in_specs=[smem,
                      pl.BlockSpec((1, 2, n_t * tq, HEAD_DIM), lambda bb, j, g: (bb, j, g, 0)),
                      whole(k), whole(v), whole(kc), whole(vc),
                      pl.BlockSpec(bias.shape, lambda bb, j, g: (0, 0, 0))],
            out_specs=pl.BlockSpec((1, n_t * tq, 128), lambda bb, j, g: (bb, g, j)),
            out_shape=out_shape,
            compiler_params=_cp(("parallel", "parallel", "parallel"), VMEM_LIMIT),
            name="attn_band",
        )(sink, q, k, v, kc, vc, bias)
    has_ctx = kc is not None
    tk = next(t for t in (4096, 2048, 1024, 768, 512, 256) if lk % t == 0)
    assert not has_ctx or kc.shape[2] <= tk
    n_t = 2 if lq % (2 * tq) == 0 else 1
    kern = functools.partial(_attn_full_kernel, has_sink=has_sink, has_ctx=has_ctx, tq=tq, tk=tk, n_t=n_t,
                             n_chunks=lk // tk)
    extra = (kc, vc) if has_ctx else ()
    return pl.pallas_call(
        kern,
        grid=(b, N_KV, lq // (n_t * tq)),
        in_specs=[smem,
                  pl.BlockSpec((1, 2, n_t * tq, HEAD_DIM), lambda bb, j, g: (bb, j, g, 0)),
                  whole(k), whole(v), *[whole(a) for a in extra]],
        out_specs=pl.BlockSpec((1, n_t * tq, 128), lambda bb, j, g: (bb, g, j)),
        out_shape=out_shape,
        scratch_shapes=[pltpu.VMEM((n_t, 2 * tq, 1), F32), pltpu.VMEM((n_t, 2 * tq, 2 * HEAD_DIM), F32),
                        pltpu.VMEM((2 * tq, tk), F32), pltpu.VMEM((2 * tq, tk), F32)],
        compiler_params=_cp(("parallel", "parallel", "parallel"), VMEM_LIMIT),
        name="attn_full",
    )(sink, q, k, v, *extra)


def _shift_rows(z, prev_row, next_row):
    n = z.shape[0]
    row = lax.broadcasted_iota(jnp.int32, (n, 1), 0)
    dn = jnp.where(row == 0, prev_row, pltpu.roll(z, 1, 0))
    up = jnp.where(row == n - 1, next_row, pltpu.roll(z, n - 1, 0))
    return dn, up


def _hy_gate_kernel(z_ref, zp_ref, zn_ref, w_ref, b_ref, vx_ref, x0_ref, *, n_tiles):
    i = pl.program_id(1)
    z = z_ref[0].astype(F32)
    prev_row = zp_ref[0].astype(F32)[HALO - 1:HALO] * (i > 0).astype(F32)
    next_row = zn_ref[0].astype(F32)[0:1] * (i < n_tiles - 1).astype(F32)
    dn, up = _shift_rows(z, prev_row, next_row)
    w = w_ref[...]
    u = dn * w[0:1] + z * w[1:2] + up * w[2:3] + b_ref[...]
    x0_ref[0] = u[:, :GROUP_W].astype(BF16)
    vx_ref[0] = (u[:, 2 * GROUP_W:] * u[:, GROUP_W:2 * GROUP_W]).astype(BF16)


def _halo_specs(tl, l, w):
    nb = l // HALO
    per = tl // HALO
    return (pl.BlockSpec((1, tl, w), lambda bb, i: (bb, i, 0)),
            pl.BlockSpec((1, HALO, w), lambda bb, i: (bb, jnp.maximum(i * per - 1, 0), 0)),
            pl.BlockSpec((1, HALO, w), lambda bb, i: (bb, jnp.minimum((i + 1) * per, nb - 1), 0)))


def _hy_gate(zh, conv_w, conv_b):
    b, l, w = zh.shape
    tl = min(l, 512)
    out = jax.ShapeDtypeStruct((b, l, GROUP_W), BF16)
    ospec = pl.BlockSpec((1, tl, GROUP_W), lambda bb, i: (bb, i, 0))
    return pl.pallas_call(
        functools.partial(_hy_gate_kernel, n_tiles=l // tl),
        grid=(b, l // tl),
        in_specs=[*_halo_specs(tl, l, w),
                  pl.BlockSpec((3, w), lambda bb, i: (0, 0)),
                  pl.BlockSpec((1, w), lambda bb, i: (0, 0))],
        out_specs=[ospec, ospec],
        out_shape=[out, out],
        compiler_params=_cp(("parallel", "parallel")),
        name="hy_gate",
    )(zh, zh, zh, conv_w, conv_b.reshape(1, w))


def _hy_filter_kernel(fc_ref, w1_ref, b1_ref, fr_ref, w2_ref, b2_ref, w3_ref, dl_ref, k_ref, *, l, tr):
    base = pl.program_id(1) * tr

    def position(m):
        return jnp.where(m < l, m, 2 * l - m).astype(F32)

    m_row = base + lax.broadcasted_iota(jnp.int32, (1, tr), 1)
    t_row = position(m_row)
    feat_id = lax.broadcasted_iota(jnp.int32, (fc_ref.shape[0], 1), 0)
    phase = fc_ref[...] * t_row
    feat = jnp.where(feat_id == 0, t_row / max(l - 1, 1),
                     jnp.where(feat_id <= HYENA_BANDS, jnp.cos(phase),
                               jnp.where(feat_id <= 2 * HYENA_BANDS, -jnp.sin(phase), 0.0)))
    hp = functools.partial(jnp.dot, precision=lax.Precision.HIGHEST, preferred_element_type=F32)
    fr = fr_ref[0]
    h = jnp.sin(fr * (hp(w1_ref[0], feat) + b1_ref[0]))
    h = jnp.sin(fr * (hp(w2_ref[0], h) + b2_ref[0]))
    h = hp(h.T, w3_ref[0])
    m = base + lax.broadcasted_iota(jnp.int32, (tr, 1), 0)
    dec = jnp.exp(-(position(m) / max(l - 1, 1)) * dl_ref[...])
    hf, hb = h[:, :GROUP_W] * dec, h[:, GROUP_W:] * dec
    k = jnp.where(m < l, hf, hb)
    k = jnp.where(m == 0, hf + hb, k)
    k_ref[0] = jnp.where(m == l, 0.0, k).astype(BF16)


def _hy_filter(l, w1, b1, fr, w2, b2, w3):
    depth, emb, hid = w1.shape
    tr = min(2 * l, 2048)
    n_feat = -(-emb // 8) * 8
    bands = jnp.linspace(1e-4, HYENA_BANDS - 1, HYENA_BANDS, dtype=F32) * (2.0 * math.pi / l)
    fc = jnp.zeros((n_feat, 1), F32).at[1:1 + HYENA_BANDS, 0].set(bands).at[1 + HYENA_BANDS:emb, 0].set(bands)
    w1t = jnp.zeros((depth, hid, n_feat), F32).at[:, :, :emb].set(jnp.swapaxes(w1, 1, 2))
    w2t = jnp.swapaxes(w2, 1, 2)
    min_decay = math.log(HYENA_TARGET) / HYENA_SLOW_DECAY
    max_decay = math.log(HYENA_TARGET) / HYENA_FAST_DECAY
    deltas = jnp.abs(jnp.linspace(min_decay, max_decay, GROUP_W, dtype=F32)).reshape(1, GROUP_W)
    lay = lambda a: pl.BlockSpec((1,) + a.shape[1:], lambda d, i: (d,) + (0,) * (a.ndim - 1))
    fix = lambda a: pl.BlockSpec(a.shape, lambda d, i: (0,) * a.ndim)
    b1, fr, b2 = (a.reshape(depth, hid, 1) for a in (b1, fr, b2))
    return pl.pallas_call(
        functools.partial(_hy_filter_kernel, l=l, tr=tr),
        grid=(depth, 2 * l // tr),
        in_specs=[fix(fc), lay(w1t), lay(b1), lay(fr), lay(w2t), lay(b2), lay(w3), fix(deltas)],
        out_specs=pl.BlockSpec((1, tr, GROUP_W), lambda d, i: (d, i, 0)),
        out_shape=jax.ShapeDtypeStruct((depth, 2 * l, GROUP_W), BF16),
        compiler_params=_cp(("parallel", "parallel")),
        name="hy_filter",
    )(fc, w1t, b1, fr, w2t, b2, w3, deltas)


ROW_TILE = 8


def _lead_kron(f):
    return jnp.kron(f.astype(F32), jnp.eye(ROW_TILE, dtype=F32)).astype(BF16)


def _lead_rows(n):
    return next(r for r in (4 * ROW_TILE, 2 * ROW_TILE, ROW_TILE) if n % r == 0)


def _lead_mm_kernel(f_ref, x_ref, o_ref):
    k, rows, c = x_ref.shape[1:]
    for j in range(rows // ROW_TILE):
        rs = slice(j * ROW_TILE, (j + 1) * ROW_TILE)
        y = _dot(f_ref[...], x_ref[0, :, rs, :].reshape(k * ROW_TILE, c))
        o_ref[0, :, rs, :] = y.reshape(-1, ROW_TILE, c).astype(o_ref.dtype)


def _lead_mm(fk, x):
    b, k, n, c = x.shape
    m = fk.shape[0] // ROW_TILE
    rows = _lead_rows(n)
    return pl.pallas_call(
        _lead_mm_kernel,
        grid=(b, n // rows),
        in_specs=[pl.BlockSpec(fk.shape, lambda bb, i: (0, 0)),
                  pl.BlockSpec((1, k, rows, c), lambda bb, i: (bb, 0, i, 0))],
        out_specs=pl.BlockSpec((1, m, rows, c), lambda bb, i: (bb, 0, i, 0)),
        out_shape=jax.ShapeDtypeStruct((b, m, n, c), BF16),
        compiler_params=_cp(("parallel", "parallel"), VMEM_LIMIT),
        name="lead_mm",
    )(fk, x)


def _hy_spec_kernel(a_ref, mf_ref, k_ref):
    for d in range(a_ref.shape[0]):
        x = a_ref[d, :, 0].reshape(2 * a_ref.shape[3], GROUP_W)
        k_ref[d, 0] = _dot(mf_ref[0], x)


def _hy_mid_kernel(a_ref, mf_ref, mi_ref, k_ref, o_ref):
    n2 = a_ref.shape[3]
    k = k_ref[0, 0]
    kr, ki = k[:n2], k[n2:]
    for bb in range(a_ref.shape[0]):
        x = a_ref[bb, :, 0].reshape(2 * n2, GROUP_W)
        y = _dot(mf_ref[0], x)
        yr, yi = y[:n2], y[n2:]
        z = jnp.concatenate([yr * kr - yi * ki, yr * ki + yi * kr], axis=0).astype(BF16)
        o_ref[bb, :, 0] = _dot(mi_ref[0], z).astype(BF16).reshape(2, n2, GROUP_W)


def _hy_spectrum(a, mf):
    depth, _, n1, n2, w = a.shape
    return pl.pallas_call(
        _hy_spec_kernel,
        grid=(n1,),
        in_specs=[pl.BlockSpec((depth, 2, 1, n2, w), lambda k1: (0, 0, k1, 0, 0)),
                  pl.BlockSpec((1, 2 * n2, 2 * n2), lambda k1: (k1, 0, 0))],
        out_specs=pl.BlockSpec((depth, 1, 2 * n2, w), lambda k1: (0, k1, 0, 0)),
        out_shape=jax.ShapeDtypeStruct((depth, n1, 2 * n2, w), F32),
        compiler_params=_cp(("parallel",)),
        name="hy_spectrum",
    )(a, mf)


def _hy_mid(a, mf, mi, kspec, layer):
    b, _, n1, n2, w = a.shape
    blk = pl.BlockSpec((b, 2, 1, n2, w), lambda k1: (0, 0, k1, 0, 0))
    mat = pl.BlockSpec((1, 2 * n2, 2 * n2), lambda k1: (k1, 0, 0))
    return pl.pallas_call(
        _hy_mid_kernel,
        grid=(n1,),
        in_specs=[blk, mat, mat,
                  pl.BlockSpec((1, 1, 2 * n2, w), lambda k1: (layer, k1, 0, 0))],
        out_specs=blk,
        out_shape=jax.ShapeDtypeStruct(a.shape, BF16),
        compiler_params=_cp(("parallel",)),
        name="hy_mid",
    )(a, mf, mi, kspec)


def _hy_out_kernel(g_ref, b_ref, vx_ref, x0_ref, skip_ref, o_ref):
    k, rows, c = b_ref.shape[1:]
    for j in range(rows // ROW_TILE):
        rs = slice(j * ROW_TILE, (j + 1) * ROW_TILE)
        y = _dot(g_ref[...], b_ref[0, :, rs, :].reshape(k * ROW_TILE, c)).reshape(-1, ROW_TILE, c)
        vx = vx_ref[0, :, rs, :].astype(F32)
        o_ref[0, :, rs, :] = ((y + skip_ref[...] * vx) * x0_ref[0, :, rs, :].astype(F32)).astype(BF16)


def _hy_out(gk, bo, vx, x0, skip):
    b, k, n, c = bo.shape
    m = gk.shape[0] // ROW_TILE
    rows = _lead_rows(n)
    sig = pl.BlockSpec((1, m, rows, c), lambda bb, i: (bb, 0, i, 0))
    return pl.pallas_call(
        _hy_out_kernel,
        grid=(b, n // rows),
        in_specs=[pl.BlockSpec(gk.shape, lambda bb, i: (0, 0)),
                  pl.BlockSpec((1, k, rows, c), lambda bb, i: (bb, 0, i, 0)),
                  sig, sig, pl.BlockSpec((1, c), lambda bb, i: (0, 0))],
        out_specs=sig,
        out_shape=jax.ShapeDtypeStruct((b, m, n, c), BF16),
        compiler_params=_cp(("parallel", "parallel"), VMEM_LIMIT),
        name="hy_out",
    )(gk, bo, vx, x0, skip)


def _hy_small_kernel(vx_ref, x0_ref, k_ref, fd_ref, ff_ref, gd_ref, skip_ref, o_ref):
    nf = fd_ref.shape[0] // 2
    vx = vx_ref[0]
    u = _dot(fd_ref[...], vx)
    k = _dot(ff_ref[...], k_ref[0])
    ur, ui, kr, ki = u[:nf], u[nf:], k[:nf], k[nf:]
    z = jnp.concatenate([ur * kr - ui * ki, ur * ki + ui * kr], axis=0).astype(BF16)
    y = _dot(gd_ref[...], z)
    o_ref[0] = ((y + skip_ref[...] * vx.astype(F32)) * x0_ref[0].astype(F32)).astype(BF16)


def _hy_small(vx, x0, kfilt, layer, fd, ff, gd, skip):
    b, l, w = vx.shape
    sig = pl.BlockSpec((1, l, w), lambda bb: (bb, 0, 0))
    full = lambda a: pl.BlockSpec(a.shape, lambda bb: (0,) * a.ndim)
    return pl.pallas_call(
        _hy_small_kernel,
        grid=(b,),
        in_specs=[sig, sig, pl.BlockSpec((1, 2 * l, w), lambda bb: (layer, 0, 0)),
                  full(fd), full(ff), full(gd), full(skip)],
        out_specs=sig,
        out_shape=jax.ShapeDtypeStruct((b, l, w), BF16),
        compiler_params=_cp(("parallel",)),
        name="hy_small",
    )(vx, x0, kfilt, fd, ff, gd, skip)


def _fn_first_kernel(z_ref, m_ref, o_ref):
    n2 = z_ref.shape[1]
    x = z_ref[0].reshape(n2 * 8, GROUP_W)
    o_ref[0] = _dot(m_ref[0], x).astype(BF16).reshape(2, 8, n2, GROUP_W)


def _fn_first(zf, mats):
    b, n2, n1, w = zf.shape
    return pl.pallas_call(
        _fn_first_kernel,
        grid=(n1 // 8, b),
        in_specs=[pl.BlockSpec((1, n2, 8, w), lambda i, bb: (bb, 0, i, 0)),
                  pl.BlockSpec((1, 16 * n2, 8 * n2), lambda i, bb: (i, 0, 0))],
        out_specs=pl.BlockSpec((1, 2, 8, n2, w), lambda i, bb: (bb, 0, i, 0, 0)),
        out_shape=jax.ShapeDtypeStruct((b, 2, n1, n2, w), BF16),
        compiler_params=_cp(("parallel", "parallel"), VMEM_LIMIT),
        name="fn_first",
    )(zf, mats)


def _fn_small_kernel(u_ref, cl_ref, sl_ref, c_ref, s_ref, w_ref, b_ref, o_ref, *, scale):
    u = u_ref[0]
    uc = _dot(u, c_ref[...]).astype(BF16)
    us = _dot(u, s_ref[...]).astype(BF16)
    f = (_dot(cl_ref[...], uc) - _dot(sl_ref[...], us)) * scale
    o_ref[0] = (_dot(f.astype(BF16), w_ref[...]) + b_ref[...]).astype(BF16)


def _fn_small(zf, cl, sl, cbd, sbd, w, bias, scale):
    b, l, c = zf.shape
    sig = pl.BlockSpec((1, l, c), lambda bb: (bb, 0, 0))
    full = lambda z: pl.BlockSpec(z.shape, lambda bb: (0,) * z.ndim)
    return pl.pallas_call(
        functools.partial(_fn_small_kernel, scale=scale),
        grid=(b,),
        in_specs=[sig, full(cl), full(sl), full(cbd), full(sbd), full(w), full(bias)],
        out_specs=sig,
        out_shape=jax.ShapeDtypeStruct((b, l, c), BF16),
        compiler_params=_cp(("parallel",)),
        name="fn_small",
    )(zf, cl, sl, cbd, sbd, w, bias)


def _merge_kernel(*refs, alpha, fnet_scale):
    if fnet_scale is None:
        oa_ref, ob_ref, oc_ref, od_ref = refs[:4]
        od = od_ref[0].astype(F32)
        rest = refs[4:]
    else:
        oa_ref, ob_ref, oc_ref, ar_ref, ai_ref, c_ref, s_ref, fw_ref, fb_ref = refs[:9]
        t = (_dot(ar_ref[0, 0], c_ref[...]) + _dot(ai_ref[0, 0], s_ref[...])) * fnet_scale
        od = _dot(t.astype(BF16), fw_ref[...]) + fb_ref[...]
        rest = refs[9:]
    x_ref, g1_ref, gn_ref, w_ref, b_ref, lng_ref, lnb_ref, o_ref = rest
    parts = []
    for idx, v in enumerate((oa_ref[0].astype(F32), ob_ref[0].astype(F32), oc_ref[0].astype(F32), od)):
        ms = jnp.mean(v * v, axis=-1, keepdims=True)
        gain = gn_ref[:, idx * GROUP_W:(idx + 1) * GROUP_W]
        parts.append((v * lax.rsqrt(ms + LN_EPS) * gain).astype(BF16))
    y = _dot(jnp.concatenate(parts, axis=1), w_ref[...]) + b_ref[...]
    r = alpha * x_ref[0] + g1_ref[0] * y
    o_ref[0] = _layer_norm(r) * lng_ref[...] + lnb_ref[...]


def _merge(oa, ob, oc, od, x, g1, gn, w_out, b_out, ln_g, ln_b, alpha):
    b, l, d = x.shape
    tm = min(l, 1024)
    grp = pl.BlockSpec((1, tm, GROUP_W), lambda bb, i: (bb, i, 0))
    tok = pl.BlockSpec((1, tm, d), lambda bb, i: (bb, i, 0))
    full = lambda a: pl.BlockSpec(a.shape, lambda bb, i: (0,) * a.ndim)
    if isinstance(od, tuple):
        a, cbd, sbd, fw, fb, fnet_scale = od
        part = lambda p: pl.BlockSpec((1, 1, tm, GROUP_W), lambda bb, i: (bb, p, i, 0))
        od_args = (a, a, cbd, sbd, fw, fb)
        od_specs = [part(0), part(1), full(cbd), full(sbd), full(fw), full(fb)]
    else:
        fnet_scale, od_args, od_specs = None, (od,), [grp]
    return pl.pallas_call(
        functools.partial(_merge_kernel, alpha=alpha, fnet_scale=fnet_scale),
        grid=(b, l // tm),
        in_specs=[grp, grp, grp, *od_specs, tok, pl.BlockSpec((1, 1, d), lambda bb, i: (bb, 0, 0)),
                  full(gn), full(w_out), full(b_out), full(ln_g), full(ln_b)],
        out_specs=tok,
        out_shape=jax.ShapeDtypeStruct((b, l, d), F32),
        compiler_params=_cp(("parallel", "parallel"), VMEM_LIMIT),
        name="merge",
    )(oa, ob, oc, *od_args, x, g1, gn, w_out, b_out, ln_g, ln_b)


def _ffn_kernel(x_ref, xp_ref, xn_ref, sh_ref, sc_ref, g_ref, wup_ref, vec_ref, wd_ref, bd_ref,
                lng_ref, lnb_ref, o_ref, h_ref, acc_ref, u0_ref, u1_ref, *, alpha, tm, n_chunks, n_tiles):
    i = pl.program_id(1)
    sh, sc = sh_ref[0], sc_ref[0]
    cw = FFN_CHUNK

    def hmod(v):
        return (_layer_norm(v) * (1.0 + sc) + sh).astype(BF16)

    h_ref[0:HALO] = hmod(xp_ref[0])
    h_ref[HALO:HALO + tm] = hmod(x_ref[0])
    h_ref[HALO + tm:] = hmod(xn_ref[0])
    acc_ref[...] = jnp.zeros_like(acc_ref)
    row8 = lax.broadcasted_iota(jnp.int32, (8, 1), 0)
    pad_top = (row8 == 7) & (i == 0)
    pad_bot = (row8 == 0) & (i == n_tiles - 1)

    def up(c, dst):
        dst[...] = _dot(h_ref[...], wup_ref[c])

    def down(c, src):
        vec = vec_ref[c]
        bias, w0, w1, w2 = vec[0:1], vec[1:2], vec[2:3], vec[3:4]
        src[HALO - 8:HALO] = jnp.where(pad_top, -bias, src[HALO - 8:HALO])
        src[HALO + tm:HALO + tm + 8] = jnp.where(pad_bot, -bias, src[HALO + tm:HALO + tm + 8])
        conv = (src[HALO - 1:HALO - 1 + tm] * w0 + src[HALO:HALO + tm] * w1 + src[HALO + 1:HALO + 1 + tm] * w2
                + (vec[4:5] + bias * (w0 + w1 + w2)))
        a, g = conv[:, :cw], conv[:, cw:]
        act = (a * jax.nn.sigmoid(a) * g).astype(BF16)
        acc_ref[...] += _dot(act, wd_ref[c])

    up(0, u0_ref)

    def body(p, carry):
        c = 2 * p
        up(c + 1, u1_ref)
        down(c, u0_ref)
        up(c + 2, u0_ref)
        down(c + 1, u1_ref)
        return carry

    n_pairs = (n_chunks - 1) // 2
    lax.fori_loop(0, n_pairs, body, 0)
    c = 2 * n_pairs
    if c + 1 < n_chunks:
        up(c + 1, u1_ref)
    down(c, u0_ref)
    if c + 1 < n_chunks:
        down(c + 1, u1_ref)
    r = alpha * x_ref[0] + g_ref[0] * (acc_ref[...] + bd_ref[...])
    o_ref[0] = _layer_norm(r) * lng_ref[...] + lnb_ref[...]


def _ffn(x, sh, sc, g2, wup, vec, wd, b_down, ln_g, ln_b, alpha):
    b, l, d = x.shape
    tm = min(l, 1024)
    n_chunks = wd.shape[0]
    main, prev, nxt = _halo_specs(tm, l, d)
    mod = pl.BlockSpec((1, 1, d), lambda bb, i: (bb, 0, 0))
    full = lambda a: pl.BlockSpec(a.shape, lambda bb, i: (0,) * a.ndim, pipeline_mode=pl.Buffered(1))
    return pl.pallas_call(
        functools.partial(_ffn_kernel, alpha=alpha, tm=tm, n_chunks=n_chunks, n_tiles=l // tm),
        grid=(b, l // tm),
        in_specs=[main, prev, nxt, mod, mod, mod, full(wup), full(vec), full(wd),
                  full(b_down), full(ln_g), full(ln_b)],
        out_specs=main,
        out_shape=jax.ShapeDtypeStruct((b, l, d), F32),
        scratch_shapes=[pltpu.VMEM((tm + 2 * HALO, d), BF16), pltpu.VMEM((tm, d), F32),
                        pltpu.VMEM((tm + 2 * HALO, 2 * FFN_CHUNK), F32),
                        pltpu.VMEM((tm + 2 * HALO, 2 * FFN_CHUNK), F32)],
        compiler_params=_cp(("parallel", "parallel"), VMEM_LIMIT),
        name="ffn",
    )(x, x, x, sh, sc, g2, wup, vec, wd, b_down, ln_g, ln_b)


def _cos_sin(idx, n):
    ang = (idx % n).astype(F32) * (2.0 * math.pi / n)
    return jnp.cos(ang), jnp.sin(ang)


def _cplx_rows(re, im):
    return jnp.concatenate([re, im], axis=-2)


def _cplx_block(re, im):
    return jnp.concatenate([jnp.concatenate([re, -im], axis=-1), jnp.concatenate([im, re], axis=-1)], axis=-2)


def _ar(n):
    return jnp.arange(n, dtype=jnp.int32)


def _hyena_tables(l):
    n, n2 = 2 * l, HYENA_N2
    n1 = n // n2
    c, s = _cos_sin(_ar(n1)[:, None] * _ar(n1)[None, :], n1)
    f_full = _cplx_rows(c, -s).astype(BF16)
    g = (jnp.concatenate([c, -s], axis=1)[:n1 // 2] / n).astype(BF16)
    k1, k2, m2 = _ar(n1)[:, None, None], _ar(n2)[None, :, None], _ar(n2)[None, None, :]
    c, s = _cos_sin(m2 * k1 + n1 * m2 * k2, n)
    mf = _cplx_block(c, -s).astype(BF16)
    return dict(f_half=_lead_kron(f_full[:, :n1 // 2]), f_full=_lead_kron(f_full), g=_lead_kron(g),
                mf=mf, mi=jnp.swapaxes(mf, 1, 2))


def _fnet_tables(l):
    n2 = FNET_N2
    n1 = l // n2
    nblk = n1 // 8
    blk, j = _ar(nblk)[:, None, None, None], _ar(8)[None, :, None, None]
    k2, m2 = _ar(n2)[None, None, :, None], _ar(n2)[None, None, None, :]
    c, s = _cos_sin(n1 * m2 * k2 + (8 * blk + j) * k2, l)
    eye = jnp.eye(8, dtype=F32)
    emb = lambda t: jnp.einsum('bjkn,ji->bjkni', t, eye).reshape(nblk, 8 * n2, n2 * 8)
    first = jnp.concatenate([emb(c), emb(-s)], axis=1).astype(BF16)
    c, s = _cos_sin(_ar(n1)[:, None] * _ar(n1)[None, :], n1)
    second = _cplx_block(c, -s).astype(BF16)
    return dict(first=first, second=_lead_kron(second))


def _dense_tables(l):
    n = 2 * l
    c, s = _cos_sin(_ar(n)[:, None] * _ar(n)[None, :], n)
    ff = _cplx_rows(c, -s).astype(BF16)
    gd = (jnp.concatenate([c, -s], axis=1)[:l] / n).astype(BF16)
    cl, sl = _cos_sin(_ar(l)[:, None] * _ar(l)[None, :], l)
    return dict(fd=ff[:, :l], ff=ff, gd=gd, cl=cl.astype(BF16), sl=sl.astype(BF16))


def _rope_tables(l):
    rows = l // GRID_W
    row = jnp.broadcast_to(jnp.arange(rows)[:, None], (rows, GRID_W)).reshape(-1).astype(F32)
    col = jnp.broadcast_to(jnp.arange(GRID_W)[None, :], (rows, GRID_W)).reshape(-1).astype(F32)
    half = HEAD_DIM // 2
    inv = ROPE_THETA ** (-jnp.arange(0, half, 2, dtype=F32) / half)
    ar, ac = row[:, None] * inv, col[:, None] * inv
    cos = jnp.concatenate([jnp.cos(ar), jnp.cos(ar), jnp.cos(ac), jnp.cos(ac)], axis=1)
    sin = jnp.concatenate([-jnp.sin(ar), jnp.sin(ar), -jnp.sin(ac), jnp.sin(ac)], axis=1)
    return jnp.tile(cos, (1, 2)), jnp.tile(sin, (1, 2))


def kernel(x, c, ctx, c_ctx, w_ada, b_ada, w_in, sink_a, q_norm_g, k_norm_g, hy_conv_w, hy_conv_b,
           hy_f_w1, hy_f_b1, hy_f_freq, hy_f_w2, hy_f_b2, hy_f_w3, hy_skip, fnet_w, fnet_b,
           out_norm_g, w_out, b_out, ln1_g, ln1_b, ffn_w_up, ffn_b_up, ffn_conv_w, ffn_conv_b,
           ffn_w_down, ffn_b_down, ln2_g, ln2_b):
    bsz, l, d = x.shape
    lc = ctx.shape[1]
    depth = w_ada.shape[0]
    d_ff = ffn_w_down.shape[1]
    alpha = (2 * depth) ** 0.25
    n2h, n2f = HYENA_N2, FNET_N2
    n1h, n1f = 2 * l // n2h, l // n2f

    ht, ft, dt = _hyena_tables(l), _fnet_tables(l), _dense_tables(lc)
    cos_l, sin_l = _rope_tables(l)
    cos_c, sin_c = jnp.ones((lc, 128), F32), jnp.zeros((lc, 128), F32)
    jj = _ar(GROUP_W)
    same = (jj[:, None] // FNET_GROUP_DIM) == (jj[None, :] // FNET_GROUP_DIM)
    bd_mean = (same.astype(F32) / HEAD_DIM).astype(BF16)
    cg, sg = _cos_sin(jj[:, None] * jj[None, :], FNET_GROUP_DIM)
    cbd, sbd = jnp.where(same, cg, 0.0).astype(BF16), jnp.where(same, sg, 0.0).astype(BF16)

    n_rows = -(-(bsz + 1) // 8) * 8
    cc = jnp.zeros((n_rows, d), F32).at[:bsz].set(c).at[bsz].set(c_ctx)
    mod = _ada(cc, w_ada, b_ada)

    k_lat = _hy_filter(l, hy_f_w1, hy_f_b1, hy_f_freq, hy_f_w2, hy_f_b2, hy_f_w3)
    k_ctx = _hy_filter(lc, hy_f_w1, hy_f_b1, hy_f_freq, hy_f_w2, hy_f_b2, hy_f_w3)
    a_k = _lead_mm(ht['f_full'], k_lat.reshape(depth, n1h, n2h, GROUP_W))
    k_spec = _hy_spectrum(a_k.reshape(depth, 2, n1h, n2h, GROUP_W), ht['mf'])

    n_chunks = -(-d_ff // FFN_CHUNK)
    pad = n_chunks * FFN_CHUNK - d_ff

    def chunked(v):
        v = v.reshape(v.shape[:-1] + (2, d_ff))
        v = jnp.pad(v, [(0, 0)] * (v.ndim - 1) + [(0, pad)])
        v = v.reshape(v.shape[:-2] + (2, n_chunks, FFN_CHUNK))
        v = jnp.swapaxes(v, -3, -2).reshape(v.shape[:-3] + (n_chunks, 2 * FFN_CHUNK))
        return jnp.moveaxis(v, -2, 0)

    for layer in range(depth):
        m6 = mod[layer].reshape(n_rows, 6, d)
        lat = [m6[:bsz, i][:, None, :] for i in range(6)]
        con = [jnp.broadcast_to(m6[bsz, i][None, None, :], (bsz, 1, d)) for i in range(6)]
        w_in_l = w_in[layer].astype(BF16)
        qg = jnp.tile(q_norm_g[layer], N_HEADS).reshape(1, -1)
        kg = jnp.tile(k_norm_g[layer], N_KV).reshape(1, -1)
        w_out_l = w_out[layer].astype(BF16)
        gn, b_out_l = out_norm_g[layer].reshape(1, d), b_out[layer].reshape(1, d)
        l1g, l1b = ln1_g[layer].reshape(1, d), ln1_b[layer].reshape(1, d)
        l2g, l2b = ln2_g[layer].reshape(1, d), ln2_b[layer].reshape(1, d)
        fw, fb = fnet_w[layer].astype(BF16), fnet_b[layer].reshape(1, GROUP_W)
        skip = hy_skip[layer].reshape(1, GROUP_W)
        wup = chunked(ffn_w_up[layer]).astype(BF16)
        vec = chunked(jnp.concatenate([ffn_b_up[layer][None], ffn_conv_w[layer], ffn_conv_b[layer][None],
                                       jnp.zeros((3, 2 * d_ff), F32)], axis=0))
        wd = jnp.pad(ffn_w_down[layer], ((0, pad), (0, 0))).reshape(n_chunks, FFN_CHUNK, d).astype(BF16)
        bdn = ffn_b_down[layer].reshape(1, d)
        last = layer == depth - 1

        cqa, cka, cva, cqb, ckb, cvb, czh, czf = _proj(ctx, con[0], con[1], w_in_l, cos_c, sin_c,
                                                       bd_mean, qg, kg)
        qa, ka, va, qb, kb, vb, zh, zf = _proj(x, lat[0], lat[1], w_in_l, cos_l, sin_l, bd_mean, qg, kg)
        oa = _attention(qa, ka, va, sink_a[layer], banded=True, kc=cka, vc=cva)
        ob = _attention(qb, kb, vb, None, banded=False, kc=ckb, vc=cvb)

        vx, x0 = _hy_gate(zh, hy_conv_w[layer], hy_conv_b[layer])
        vx4, x04 = (a.reshape(bsz, n1h // 2, n2h, GROUP_W) for a in (vx, x0))
        a1 = _lead_mm(ht['f_half'], vx4)
        bo = _hy_mid(a1.reshape(bsz, 2, n1h, n2h, GROUP_W), ht['mf'], ht['mi'], k_spec, layer)
        oc = _hy_out(ht['g'], bo.reshape(bsz, 2 * n1h, n2h, GROUP_W), vx4, x04, skip).reshape(bsz, l, GROUP_W)

        f1 = _fn_first(zf.reshape(bsz, n2f, n1f, GROUP_W), ft['first'])
        f2 = _lead_mm(ft['second'], f1.reshape(bsz, 2 * n1f, n2f, GROUP_W))
        od = (f2.reshape(bsz, 2, l, GROUP_W), cbd, sbd, fw, fb, (l * FNET_GROUP_DIM) ** -0.5)

        x = _merge(oa, ob, oc, od, x, lat[2], gn, w_out_l, b_out_l, l1g, l1b, alpha)
        x = _ffn(x, lat[3], lat[4], lat[5], wup, vec, wd, bdn, l2g, l2b, alpha)

        if not last:
            oac = _attention(cqa, cka, cva, sink_a[layer], banded=False)
            obc = _attention(cqb, ckb, cvb, None, banded=False)
            cvx, cx0 = _hy_gate(czh, hy_conv_w[layer], hy_conv_b[layer])
            occ = _hy_small(cvx, cx0, k_ctx, layer, dt['fd'], dt['ff'], dt['gd'], skip)
            odc = _fn_small(czf, dt['cl'], dt['sl'], cbd, sbd, fw, fb, (lc * FNET_GROUP_DIM) ** -0.5)
            ctx = _merge(oac, obc, occ, odc, ctx, con[2], gn, w_out_l, b_out_l, l1g, l1b, alpha)
            ctx = _ffn(ctx, con[3], con[4], con[5], wup, vec, wd, bdn, l2g, l2b, alpha)
    return x
```

```python
import functools
import math

import jax
import jax.numpy as jnp
from jax import lax
from jax.experimental import pallas as pl
from jax.experimental.pallas import tpu as pltpu

F32 = jnp.float32
BF16 = jnp.bfloat16

HEAD_DIM = 64
GROUP_W = 256
N_HEADS = 4
N_KV = 2
WINDOW = 128
GRID_W = 64
ROPE_THETA = 10000.0
FNET_GROUP_DIM = 64
HYENA_BANDS = 8
HYENA_FAST_DECAY = 0.3
HYENA_SLOW_DECAY = 1.5
HYENA_TARGET = 1e-2
LN_EPS = 1e-6
NEG_INF = -1e30
HYENA_N2 = 256
FNET_N2 = 128
FFN_CHUNK = 256
HALO = 16
VMEM_LIMIT = 56 * 1024 * 1024


def _cp(sem, vmem=None):
    return pltpu.CompilerParams(dimension_semantics=sem, vmem_limit_bytes=vmem)


def _layer_norm(x):
    mu = jnp.mean(x, axis=-1, keepdims=True)
    xc = x - mu
    var = jnp.mean(xc * xc, axis=-1, keepdims=True)
    return xc * lax.rsqrt(var + LN_EPS)


def _dot(a, b):
    return jnp.dot(a, b, preferred_element_type=F32)


def _ada_kernel(c_ref, w_ref, b_ref, o_ref):
    c = c_ref[...]
    s = (c * jax.nn.sigmoid(c)).astype(BF16)
    o_ref[0] = _dot(s, w_ref[0].astype(BF16)) + b_ref[0]


def _ada(cc, w_ada, b_ada):
    depth, d, n = w_ada.shape
    r = cc.shape[0]
    tn = 1536
    return pl.pallas_call(
        _ada_kernel,
        grid=(depth, n // tn),
        in_specs=[pl.BlockSpec((r, d), lambda l, j: (0, 0)),
                  pl.BlockSpec((1, d, tn), lambda l, j: (l, 0, j)),
                  pl.BlockSpec((1, 1, tn), lambda l, j: (l, 0, j))],
        out_specs=pl.BlockSpec((1, r, tn), lambda l, j: (l, 0, j)),
        out_shape=jax.ShapeDtypeStruct((depth, r, n), F32),
        compiler_params=_cp(("parallel", "parallel"), VMEM_LIMIT),
        name="ada",
    )(cc, w_ada, b_ada.reshape(depth, 1, n))


def _swap_halves(z):
    w = z.shape[1]
    lane = lax.broadcasted_iota(jnp.int32, z.shape, 1)
    return jnp.where(lane % 32 < 16, pltpu.roll(z, w - 16, 1), pltpu.roll(z, 16, 1))


def _proj_kernel(x_ref, sh_ref, sc_ref, w_ref, cos_ref, sin_ref, bd_ref, qg_ref, kg_ref,
                 qa_ref, ka_ref, va_ref, qb_ref, kb_ref, vb_ref, zh_ref, zf_ref, *, n_parts):
    tm = x_ref.shape[1]
    rows = tm // n_parts
    scale = HEAD_DIM ** -0.5
    for part in range(n_parts):
        rs = slice(part * rows, (part + 1) * rows)
        h = (_layer_norm(x_ref[0, rs]) * (1.0 + sc_ref[0]) + sh_ref[0]).astype(BF16)
        cos1, sin1 = cos_ref[rs], sin_ref[rs]
        cos2 = jnp.concatenate([cos1, cos1], axis=1)
        sin2 = jnp.concatenate([sin1, sin1], axis=1)

        def proj(lo, width):
            return _dot(h, w_ref[:, lo:lo + width])

        def rope(z):
            if z.shape[1] == 128:
                return z * cos1 + _swap_halves(z) * sin1
            return z * cos2 + _swap_halves(z) * sin2

        def head_norm(z, g):
            w = z.shape[1]
            ms = _dot((z * z).astype(BF16), bd_ref[:w, :w])
            return z * lax.rsqrt(ms + LN_EPS) * g

        def put_heads(ref, z):
            for hh in range(z.shape[1] // HEAD_DIM):
                ref[0, hh, rs] = z[:, hh * HEAD_DIM:(hh + 1) * HEAD_DIM].astype(BF16)

        def put_values(ref, z):
            lane = lax.broadcasted_iota(jnp.int32, (z.shape[0], HEAD_DIM), 1)
            ones_col = jnp.where(lane == 0, 1.0, 0.0)
            for hh in range(z.shape[1] // HEAD_DIM):
                ref[0, hh, rs] = jnp.concatenate([z[:, hh * HEAD_DIM:(hh + 1) * HEAD_DIM], ones_col],
                                                 axis=1).astype(BF16)

        put_heads(qa_ref, rope(proj(0, 256)) * scale)
        kv = proj(256, 256)
        put_heads(ka_ref, rope(kv[:, :128]))
        put_values(va_ref, kv[:, 128:])
        put_heads(qb_ref, rope(head_norm(proj(512, 256), qg_ref[...])) * scale)
        kv = proj(768, 256)
        put_heads(kb_ref, rope(head_norm(kv[:, :128], kg_ref[...])))
        put_values(vb_ref, kv[:, 128:])
        zh_ref[0, rs] = proj(1024, 768).astype(BF16)
        zf_ref[0, rs] = proj(1792, 256).astype(BF16)


def _proj(x, sh, sc, w_in, cos, sin, bd, qg, kg):
    b, l, d = x.shape
    tm = min(l, 1024)
    heads = lambda n, w=HEAD_DIM: jax.ShapeDtypeStruct((b, n, l, w), BF16)
    hspec = lambda n, w=HEAD_DIM: pl.BlockSpec((1, n, tm, w), lambda t, bb: (bb, 0, t, 0))
    tok = lambda w: pl.BlockSpec((1, tm, w), lambda t, bb: (bb, t, 0))
    vec = lambda w: pl.BlockSpec((1, 1, w), lambda t, bb: (bb, 0, 0))
    full = lambda a: pl.BlockSpec(a.shape, lambda t, bb: (0,) * a.ndim)
    tab = pl.BlockSpec((tm, 128), lambda t, bb: (t, 0))
    return pl.pallas_call(
        functools.partial(_proj_kernel, n_parts=1),
        grid=(l // tm, b),
        in_specs=[tok(d), vec(d), vec(d), full(w_in), tab, tab, full(bd), full(qg), full(kg)],
        out_specs=[hspec(4), hspec(2), hspec(2, 128), hspec(4), hspec(2), hspec(2, 128), tok(768), tok(256)],
        out_shape=[heads(4), heads(2), heads(2, 128), heads(4), heads(2), heads(2, 128),
                   jax.ShapeDtypeStruct((b, l, 768), BF16), jax.ShapeDtypeStruct((b, l, 256), BF16)],
        compiler_params=_cp(("parallel", "parallel"), VMEM_LIMIT),
        name="proj",
    )(x, sh, sc, w_in, cos, sin, bd, qg, kg)


_NT = (((1,), (1,)), ((), ()))


def _row_max(*parts):
    tiles = [p[:, i:i + 128] for p in parts for i in range(0, p.shape[1], 128)]
    m = tiles[0]
    for t in tiles[1:]:
        m = jnp.maximum(m, t)
    return jnp.max(m, axis=-1, keepdims=True)


def _sink_rows(sink_ref, j, tq):
    row = lax.broadcasted_iota(jnp.int32, (2 * tq, 1), 0)
    return jnp.where(row < tq, sink_ref[2 * j], sink_ref[2 * j + 1])


def _attn_full_kernel(*refs, has_sink, has_ctx, tq, tk, n_t, n_chunks):
    if has_ctx:
        sink_ref, q_ref, k_ref, v_ref, kc_ref, vc_ref, o_ref, m_ref, acc_ref, s0_ref, s1_ref = refs
        l_ctx = kc_ref.shape[2]
    else:
        sink_ref, q_ref, k_ref, v_ref, o_ref, m_ref, acc_ref, s0_ref, s1_ref = refs
    bufs = (s0_ref, s1_ref)
    j = pl.program_id(1)
    lane = lax.broadcasted_iota(jnp.int32, acc_ref.shape[1:], 1)
    for tile in range(n_t):
        if has_sink:
            m_ref[tile] = _sink_rows(sink_ref, j, tq)
            acc_ref[tile] = jnp.where(lane == HEAD_DIM, 1.0, 0.0)
        else:
            m_ref[tile] = jnp.full(m_ref.shape[1:], NEG_INF, F32)
            acc_ref[tile] = jnp.zeros(acc_ref.shape[1:], F32)

    def chunk(ref, t):
        c = t // n_t
        off = c * tk if isinstance(c, int) else pl.multiple_of(c * tk, tk)
        return ref[0, 0, pl.ds(off, tk), :]

    def scores(t, tile, dst, ctx=False):
        q = q_ref[0, :, tile * tq:(tile + 1) * tq, :].reshape(2 * tq, HEAD_DIM)
        if ctx:
            dst[:, :l_ctx] = lax.dot_general(q, kc_ref[0, 0], _NT, preferred_element_type=F32)
        else:
            dst[...] = lax.dot_general(q, chunk(k_ref, t), _NT, preferred_element_type=F32)

    def update(t, tile, src, ctx=False):
        s = src[:, :l_ctx] if ctx else src[...]
        v = vc_ref[0, 0] if ctx else chunk(v_ref, t)
        m_prev = m_ref[tile]
        m_new = jnp.maximum(m_prev, _row_max(s))
        p = jnp.exp((s - m_new).astype(BF16))
        acc_ref[tile] = jnp.exp(m_prev - m_new) * acc_ref[tile] + _dot(p, v)
        m_ref[tile] = m_new

    total = n_t * n_chunks
    per_iter = max(n_t, 2)
    scores(0, 0, s0_ref)

    def body(i, carry):
        t = per_iter * i
        for r in range(per_iter):
            scores(t + r + 1, (r + 1) % n_t, bufs[(r + 1) % 2])
            update(t + r, r % n_t, bufs[r % 2])
        return carry

    n_iter = (total - 1) // per_iter
    lax.fori_loop(0, n_iter, body, 0, unroll=4 // per_iter)
    first = per_iter * n_iter
    items = [(t, t % n_t, False) for t in range(first, total)]
    if has_ctx:
        items += [(None, tile, True) for tile in range(n_t)]
    for idx, (t, tile, ctx) in enumerate(items):
        if idx + 1 < len(items):
            nt, ntile, nctx = items[idx + 1]
            scores(nt, ntile, bufs[(first + idx + 1) % 2], nctx)
        update(t, tile, bufs[(first + idx) % 2], ctx)
    for tile in range(n_t):
        acc = acc_ref[tile]
        o = acc[:, :HEAD_DIM] * (1.0 / acc[:, HEAD_DIM:HEAD_DIM + 1])
        o_ref[0, tile * tq:(tile + 1) * tq, :] = jnp.concatenate([o[:tq], o[tq:]], axis=1).astype(BF16)


def _band_bias(tq):
    span = tq + 2 * WINDOW
    r = (_ar(2 * tq) % tq)[None, :, None]
    col = _ar(span)[None, None, :]
    shift = (_ar(3) * WINDOW)[:, None, None]
    return jnp.where(jnp.abs(col - shift - r) <= WINDOW, 0.0, NEG_INF).astype(F32)


def _attn_band_kernel(sink_ref, q_ref, k_ref, v_ref, kc_ref, vc_ref, bias_ref, o_ref, *,
                      tq, n_t, n_tiles, l_lat):
    j, g = pl.program_id(1), pl.program_id(2)
    span = tq + 2 * WINDOW
    sink = _sink_rows(sink_ref, j, tq)
    kc, vc = kc_ref[0, 0], vc_ref[0, 0]
    for tile in range(n_t):
        qi = g * n_t + tile
        start = pl.multiple_of(jnp.clip(qi * tq - WINDOW, 0, l_lat - span), WINDOW)
        kind = jnp.where(qi == 0, 0, jnp.where(qi == n_tiles - 1, 2, 1))
        q = q_ref[0, :, tile * tq:(tile + 1) * tq, :].reshape(2 * tq, HEAD_DIM)
        s_lat = (lax.dot_general(q, k_ref[0, 0, pl.ds(start, span), :], _NT, preferred_element_type=F32)
                 + bias_ref[kind])
        s_ctx = lax.dot_general(q, kc, _NT, preferred_element_type=F32)
        m = jnp.maximum(sink, _row_max(s_lat, s_ctx))
        acc = (_dot(jnp.exp((s_lat - m).astype(BF16)), v_ref[0, 0, pl.ds(start, span), :])
               + _dot(jnp.exp((s_ctx - m).astype(BF16)), vc))
        o = acc[:, :HEAD_DIM] * (1.0 / (acc[:, HEAD_DIM:HEAD_DIM + 1] + jnp.exp(sink - m)))
        o_ref[0, tile * tq:(tile + 1) * tq, :] = jnp.concatenate([o[:tq], o[tq:]], axis=1).astype(BF16)


def _attention(q, k, v, sink, *, banded, kc=None, vc=None):
    b, _, lq, _ = q.shape
    lk = k.shape[2]
    tq = 256
    has_sink = sink is not None
    if not has_sink:
        sink = jnp.zeros((N_HEADS,), F32)
    smem = pl.BlockSpec(memory_space=pltpu.SMEM)
    whole = lambda a: pl.BlockSpec((1, 1) + a.shape[2:], lambda bb, j, g: (bb, j, 0, 0))
    out_shape = jax.ShapeDtypeStruct((b, lq, GROUP_W), BF16)
    if banded:
        assert lq == lk and lk >= tq + 2 * WINDOW and tq % WINDOW == 0
        n_tiles = lq // tq
        n_t = next(t for t in (4, 2, 1) if n_tiles % t == 0)
        bias = _band_bias(tq)
        return pl.pallas_call(
            functools.partial(_attn_band_kernel, tq=tq, n_t=n_t, n_tiles=n_tiles, l_lat=lk),
            grid=(b, N_KV, n_tiles // n_t),
            in_specs=[smem,
                      pl.BlockSpec((1, 2, n_t * tq, HEAD_DIM), lambda bb, j, g: (bb, j, g, 0)),
                      whole(k), whole(v), whole(kc), whole(vc),
                      pl.BlockSpec(bias.shape, lambda bb, j, g: (0, 0, 0))],
            out_specs=pl.BlockSpec((1, n_t * tq, 128), lambda bb, j, g: (bb, g, j)),
            out_shape=out_shape,
            compiler_params=_cp(("parallel", "parallel", "parallel"), VMEM_LIMIT),
            name="attn_band",
        )(sink, q, k, v, kc, vc, bias)
    has_ctx = kc is not None
    tk = next(t for t in (4096, 2048, 1024, 768, 512, 256) if lk % t == 0)
    assert not has_ctx or kc.shape[2] <= tk
    n_t = 2 if lq % (2 * tq) == 0 else 1
    kern = functools.partial(_attn_full_kernel, has_sink=has_sink, has_ctx=has_ctx, tq=tq, tk=tk, n_t=n_t,
                             n_chunks=lk // tk)
    extra = (kc, vc) if has_ctx else ()
    return pl.pallas_call(
        kern,
        grid=(b, N_KV, lq // (n_t * tq)),
        in_specs=[smem,
                  pl.BlockSpec((1, 2, n_t * tq, HEAD_DIM), lambda bb, j, g: (bb, j, g, 0)),
                  whole(k), whole(v), *[whole(a) for a in extra]],
        out_specs=pl.BlockSpec((1, n_t * tq, 128), lambda bb, j, g: (bb, g, j)),
        out_shape=out_shape,
        scratch_shapes=[pltpu.VMEM((n_t, 2 * tq, 1), F32), pltpu.VMEM((n_t, 2 * tq, 2 * HEAD_DIM), F32),
                        pltpu.VMEM((2 * tq, tk), F32), pltpu.VMEM((2 * tq, tk), F32)],
        compiler_params=_cp(("parallel", "parallel", "parallel"), VMEM_LIMIT),
        name="attn_full",
    )(sink, q, k, v, *extra)


def _shift_rows(z, prev_row, next_row):
    n = z.shape[0]
    row = lax.broadcasted_iota(jnp.int32, (n, 1), 0)
    dn = jnp.where(row == 0, prev_row, pltpu.roll(z, 1, 0))
    up = jnp.where(row == n - 1, next_row, pltpu.roll(z, n - 1, 0))
    return dn, up


def _hy_gate_kernel(z_ref, zp_ref, zn_ref, w_ref, b_ref, vx_ref, x0_ref, *, n_tiles):
    i = pl.program_id(1)
    z = z_ref[0].astype(F32)
    prev_row = zp_ref[0].astype(F32)[HALO - 1:HALO] * (i > 0).astype(F32)
    next_row = zn_ref[0].astype(F32)[0:1] * (i < n_tiles - 1).astype(F32)
    dn, up = _shift_rows(z, prev_row, next_row)
    w = w_ref[...]
    u = dn * w[0:1] + z * w[1:2] + up * w[2:3] + b_ref[...]
    x0_ref[0] = u[:, :GROUP_W].astype(BF16)
    vx_ref[0] = (u[:, 2 * GROUP_W:] * u[:, GROUP_W:2 * GROUP_W]).astype(BF16)


def _halo_specs(tl, l, w):
    nb = l // HALO
    per = tl // HALO
    return (pl.BlockSpec((1, tl, w), lambda bb, i: (bb, i, 0)),
            pl.BlockSpec((1, HALO, w), lambda bb, i: (bb, jnp.maximum(i * per - 1, 0), 0)),
            pl.BlockSpec((1, HALO, w), lambda bb, i: (bb, jnp.minimum((i + 1) * per, nb - 1), 0)))


def _hy_gate(zh, conv_w, conv_b):
    b, l, w = zh.shape
    tl = min(l, 512)
    out = jax.ShapeDtypeStruct((b, l, GROUP_W), BF16)
    ospec = pl.BlockSpec((1, tl, GROUP_W), lambda bb, i: (bb, i, 0))
    return pl.pallas_call(
        functools.partial(_hy_gate_kernel, n_tiles=l // tl),
        grid=(b, l // tl),
        in_specs=[*_halo_specs(tl, l, w),
                  pl.BlockSpec((3, w), lambda bb, i: (0, 0)),
                  pl.BlockSpec((1, w), lambda bb, i: (0, 0))],
        out_specs=[ospec, ospec],
        out_shape=[out, out],
        compiler_params=_cp(("parallel", "parallel")),
        name="hy_gate",
    )(zh, zh, zh, conv_w, conv_b.reshape(1, w))


def _hy_filter_kernel(fc_ref, w1_ref, b1_ref, fr_ref, w2_ref, b2_ref, w3_ref, dl_ref, k_ref, *, l, tr):
    base = pl.program_id(1) * tr

    def position(m):
        return jnp.where(m < l, m, 2 * l - m).astype(F32)

    m_row = base + lax.broadcasted_iota(jnp.int32, (1, tr), 1)
    t_row = position(m_row)
    feat_id = lax.broadcasted_iota(jnp.int32, (fc_ref.shape[0], 1), 0)
    phase = fc_ref[...] * t_row
    feat = jnp.where(feat_id == 0, t_row / max(l - 1, 1),
                     jnp.where(feat_id <= HYENA_BANDS, jnp.cos(phase),
                               jnp.where(feat_id <= 2 * HYENA_BANDS, -jnp.sin(phase), 0.0)))
    hp = functools.partial(jnp.dot, precision=lax.Precision.HIGHEST, preferred_element_type=F32)
    fr = fr_ref[0]
    h = jnp.sin(fr * (hp(w1_ref[0], feat) + b1_ref[0]))
    h = jnp.sin(fr * (hp(w2_ref[0], h) + b2_ref[0]))
    h = hp(h.T, w3_ref[0])
    m = base + lax.broadcasted_iota(jnp.int32, (tr, 1), 0)
    dec = jnp.exp(-(position(m) / max(l - 1, 1)) * dl_ref[...])
    hf, hb = h[:, :GROUP_W] * dec, h[:, GROUP_W:] * dec
    k = jnp.where(m < l, hf, hb)
    k = jnp.where(m == 0, hf + hb, k)
    k_ref[0] = jnp.where(m == l, 0.0, k).astype(BF16)


def _hy_filter(l, w1, b1, fr, w2, b2, w3):
    depth, emb, hid = w1.shape
    tr = min(2 * l, 2048)
    n_feat = -(-emb // 8) * 8
    bands = jnp.linspace(1e-4, HYENA_BANDS - 1, HYENA_BANDS, dtype=F32) * (2.0 * math.pi / l)
    fc = jnp.zeros((n_feat, 1), F32).at[1:1 + HYENA_BANDS, 0].set(bands).at[1 + HYENA_BANDS:emb, 0].set(bands)
    w1t = jnp.zeros((depth, hid, n_feat), F32).at[:, :, :emb].set(jnp.swapaxes(w1, 1, 2))
    w2t = jnp.swapaxes(w2, 1, 2)
    min_decay = math.log(HYENA_TARGET) / HYENA_SLOW_DECAY
    max_decay = math.log(HYENA_TARGET) / HYENA_FAST_DECAY
    deltas = jnp.abs(jnp.linspace(min_decay, max_decay, GROUP_W, dtype=F32)).reshape(1, GROUP_W)
    lay = lambda a: pl.BlockSpec((1,) + a.shape[1:], lambda d, i: (d,) + (0,) * (a.ndim - 1))
    fix = lambda a: pl.BlockSpec(a.shape, lambda d, i: (0,) * a.ndim)
    b1, fr, b2 = (a.reshape(depth, hid, 1) for a in (b1, fr, b2))
    return pl.pallas_call(
        functools.partial(_hy_filter_kernel, l=l, tr=tr),
        grid=(depth, 2 * l // tr),
        in_specs=[fix(fc), lay(w1t), lay(b1), lay(fr), lay(w2t), lay(b2), lay(w3), fix(deltas)],
        out_specs=pl.BlockSpec((1, tr, GROUP_W), lambda d, i: (d, i, 0)),
        out_shape=jax.ShapeDtypeStruct((depth, 2 * l, GROUP_W), BF16),
        compiler_params=_cp(("parallel", "parallel")),
        name="hy_filter",
    )(fc, w1t, b1, fr, w2t, b2, w3, deltas)


ROW_TILE = 8


def _lead_kron(f):
    return jnp.kron(f.astype(F32), jnp.eye(ROW_TILE, dtype=F32)).astype(BF16)


def _lead_rows(n):
    return next(r for r in (4 * ROW_TILE, 2 * ROW_TILE, ROW_TILE) if n % r == 0)


def _lead_mm_kernel(f_ref, x_ref, o_ref):
    k, rows, c = x_ref.shape[1:]
    for j in range(rows // ROW_TILE):
        rs = slice(j * ROW_TILE, (j + 1) * ROW_TILE)
        y = _dot(f_ref[...], x_ref[0, :, rs, :].reshape(k * ROW_TILE, c))
        o_ref[0, :, rs, :] = y.reshape(-1, ROW_TILE, c).astype(o_ref.dtype)


def _lead_mm(fk, x):
    b, k, n, c = x.shape
    m = fk.shape[0] // ROW_TILE
    rows = _lead_rows(n)
    return pl.pallas_call(
        _lead_mm_kernel,
        grid=(b, n // rows),
        in_specs=[pl.BlockSpec(fk.shape, lambda bb, i: (0, 0)),
                  pl.BlockSpec((1, k, rows, c), lambda bb, i: (bb, 0, i, 0))],
        out_specs=pl.BlockSpec((1, m, rows, c), lambda bb, i: (bb, 0, i, 0)),
        out_shape=jax.ShapeDtypeStruct((b, m, n, c), BF16),
        compiler_params=_cp(("parallel", "parallel"), VMEM_LIMIT),
        name="lead_mm",
    )(fk, x)


def _hy_spec_kernel(a_ref, mf_ref, k_ref):
    for d in range(a_ref.shape[0]):
        x = a_ref[d, :, 0].reshape(2 * a_ref.shape[3], GROUP_W)
        k_ref[d, 0] = _dot(mf_ref[0], x)


def _hy_mid_kernel(a_ref, mf_ref, mi_ref, k_ref, o_ref):
    n2 = a_ref.shape[3]
    k = k_ref[0, 0]
    kr, ki = k[:n2], k[n2:]
    for bb in range(a_ref.shape[0]):
        x = a_ref[bb, :, 0].reshape(2 * n2, GROUP_W)
        y = _dot(mf_ref[0], x)
        yr, yi = y[:n2], y[n2:]
        z = jnp.concatenate([yr * kr - yi * ki, yr * ki + yi * kr], axis=0).astype(BF16)
        o_ref[bb, :, 0] = _dot(mi_ref[0], z).astype(BF16).reshape(2, n2, GROUP_W)


def _hy_spectrum(a, mf):
    depth, _, n1, n2, w = a.shape
    return pl.pallas_call(
        _hy_spec_kernel,
        grid=(n1,),
        in_specs=[pl.BlockSpec((depth, 2, 1, n2, w), lambda k1: (0, 0, k1, 0, 0)),
                  pl.BlockSpec((1, 2 * n2, 2 * n2), lambda k1: (k1, 0, 0))],
        out_specs=pl.BlockSpec((depth, 1, 2 * n2, w), lambda k1: (0, k1, 0, 0)),
        out_shape=jax.ShapeDtypeStruct((depth, n1, 2 * n2, w), F32),
        compiler_params=_cp(("parallel",)),
        name="hy_spectrum",
    )(a, mf)


def _hy_mid(a, mf, mi, kspec, layer):
    b, _, n1, n2, w = a.shape
    blk = pl.BlockSpec((b, 2, 1, n2, w), lambda k1: (0, 0, k1, 0, 0))
    mat = pl.BlockSpec((1, 2 * n2, 2 * n2), lambda k1: (k1, 0, 0))
    return pl.pallas_call(
        _hy_mid_kernel,
        grid=(n1,),
        in_specs=[blk, mat, mat,
                  pl.BlockSpec((1, 1, 2 * n2, w), lambda k1: (layer, k1, 0, 0))],
        out_specs=blk,
        out_shape=jax.ShapeDtypeStruct(a.shape, BF16),
        compiler_params=_cp(("parallel",)),
        name="hy_mid",
    )(a, mf, mi, kspec)


def _hy_out_kernel(g_ref, b_ref, vx_ref, x0_ref, skip_ref, o_ref):
    k, rows, c = b_ref.shape[1:]
    for j in range(rows // ROW_TILE):
        rs = slice(j * ROW_TILE, (j + 1) * ROW_TILE)
        y = _dot(g_ref[...], b_ref[0, :, rs, :].reshape(k * ROW_TILE, c)).reshape(-1, ROW_TILE, c)
        vx = vx_ref[0, :, rs, :].astype(F32)
        o_ref[0, :, rs, :] = ((y + skip_ref[...] * vx) * x0_ref[0, :, rs, :].astype(F32)).astype(BF16)


def _hy_out(gk, bo, vx, x0, skip):
    b, k, n, c = bo.shape
    m = gk.shape[0] // ROW_TILE
    rows = _lead_rows(n)
    sig = pl.BlockSpec((1, m, rows, c), lambda bb, i: (bb, 0, i, 0))
    return pl.pallas_call(
        _hy_out_kernel,
        grid=(b, n // rows),
        in_specs=[pl.BlockSpec(gk.shape, lambda bb, i: (0, 0)),
                  pl.BlockSpec((1, k, rows, c), lambda bb, i: (bb, 0, i, 0)),
                  sig, sig, pl.BlockSpec((1, c), lambda bb, i: (0, 0))],
        out_specs=sig,
        out_shape=jax.ShapeDtypeStruct((b, m, n, c), BF16),
        compiler_params=_cp(("parallel", "parallel"), VMEM_LIMIT),
        name="hy_out",
    )(gk, bo, vx, x0, skip)


def _hy_small_kernel(vx_ref, x0_ref, k_ref, fd_ref, ff_ref, gd_ref, skip_ref, o_ref):
    nf = fd_ref.shape[0] // 2
    vx = vx_ref[0]
    u = _dot(fd_ref[...], vx)
    k = _dot(ff_ref[...], k_ref[0])
    ur, ui, kr, ki = u[:nf], u[nf:], k[:nf], k[nf:]
    z = jnp.concatenate([ur * kr - ui * ki, ur * ki + ui * kr], axis=0).astype(BF16)
    y = _dot(gd_ref[...], z)
    o_ref[0] = ((y + skip_ref[...] * vx.astype(F32)) * x0_ref[0].astype(F32)).astype(BF16)


def _hy_small(vx, x0, kfilt, layer, fd, ff, gd, skip):
    b, l, w = vx.shape
    sig = pl.BlockSpec((1, l, w), lambda bb: (bb, 0, 0))
    full = lambda a: pl.BlockSpec(a.shape, lambda bb: (0,) * a.ndim)
    return pl.pallas_call(
        _hy_small_kernel,
        grid=(b,),
        in_specs=[sig, sig, pl.BlockSpec((1, 2 * l, w), lambda bb: (layer, 0, 0)),
                  full(fd), full(ff), full(gd), full(skip)],
        out_specs=sig,
        out_shape=jax.ShapeDtypeStruct((b, l, w), BF16),
        compiler_params=_cp(("parallel",)),
        name="hy_small",
    )(vx, x0, kfilt, fd, ff, gd, skip)


def _fn_first_kernel(z_ref, m_ref, o_ref):
    n2 = z_ref.shape[1]
    x = z_ref[0].reshape(n2 * 8, GROUP_W)
    o_ref[0] = _dot(m_ref[0], x).astype(BF16).reshape(2, 8, n2, GROUP_W)


def _fn_first(zf, mats):
    b, n2, n1, w = zf.shape
    return pl.pallas_call(
        _fn_first_kernel,
        grid=(n1 // 8, b),
        in_specs=[pl.BlockSpec((1, n2, 8, w), lambda i, bb: (bb, 0, i, 0)),
                  pl.BlockSpec((1, 16 * n2, 8 * n2), lambda i, bb: (i, 0, 0))],
        out_specs=pl.BlockSpec((1, 2, 8, n2, w), lambda i, bb: (bb, 0, i, 0, 0)),
        out_shape=jax.ShapeDtypeStruct((b, 2, n1, n2, w), BF16),
        compiler_params=_cp(("parallel", "parallel"), VMEM_LIMIT),
        name="fn_first",
    )(zf, mats)


def _fn_small_kernel(u_ref, cl_ref, sl_ref, c_ref, s_ref, w_ref, b_ref, o_ref, *, scale):
    u = u_ref[0]
    uc = _dot(u, c_ref[...]).astype(BF16)
    us = _dot(u, s_ref[...]).astype(BF16)
    f = (_dot(cl_ref[...], uc) - _dot(sl_ref[...], us)) * scale
    o_ref[0] = (_dot(f.astype(BF16), w_ref[...]) + b_ref[...]).astype(BF16)


def _fn_small(zf, cl, sl, cbd, sbd, w, bias, scale):
    b, l, c = zf.shape
    sig = pl.BlockSpec((1, l, c), lambda bb: (bb, 0, 0))
    full = lambda z: pl.BlockSpec(z.shape, lambda bb: (0,) * z.ndim)
    return pl.pallas_call(
        functools.partial(_fn_small_kernel, scale=scale),
        grid=(b,),
        in_specs=[sig, full(cl), full(sl), full(cbd), full(sbd), full(w), full(bias)],
        out_specs=sig,
        out_shape=jax.ShapeDtypeStruct((b, l, c), BF16),
        compiler_params=_cp(("parallel",)),
        name="fn_small",
    )(zf, cl, sl, cbd, sbd, w, bias)


def _merge_kernel(*refs, alpha, fnet_scale):
    if fnet_scale is None:
        oa_ref, ob_ref, oc_ref, od_ref = refs[:4]
        od = od_ref[0].astype(F32)
        rest = refs[4:]
    else:
        oa_ref, ob_ref, oc_ref, ar_ref, ai_ref, c_ref, s_ref, fw_ref, fb_ref = refs[:9]
        t = (_dot(ar_ref[0, 0], c_ref[...]) + _dot(ai_ref[0, 0], s_ref[...])) * fnet_scale
        od = _dot(t.astype(BF16), fw_ref[...]) + fb_ref[...]
        rest = refs[9:]
    x_ref, g1_ref, gn_ref, w_ref, b_ref, lng_ref, lnb_ref, o_ref = rest
    parts = []
    for idx, v in enumerate((oa_ref[0].astype(F32), ob_ref[0].astype(F32), oc_ref[0].astype(F32), od)):
        ms = jnp.mean(v * v, axis=-1, keepdims=True)
        gain = gn_ref[:, idx * GROUP_W:(idx + 1) * GROUP_W]
        parts.append((v * lax.rsqrt(ms + LN_EPS) * gain).astype(BF16))
    y = _dot(jnp.concatenate(parts, axis=1), w_ref[...]) + b_ref[...]
    r = alpha * x_ref[0] + g1_ref[0] * y
    o_ref[0] = _layer_norm(r) * lng_ref[...] + lnb_ref[...]


def _merge(oa, ob, oc, od, x, g1, gn, w_out, b_out, ln_g, ln_b, alpha):
    b, l, d = x.shape
    tm = min(l, 1024)
    grp = pl.BlockSpec((1, tm, GROUP_W), lambda bb, i: (bb, i, 0))
    tok = pl.BlockSpec((1, tm, d), lambda bb, i: (bb, i, 0))
    full = lambda a: pl.BlockSpec(a.shape, lambda bb, i: (0,) * a.ndim)
    if isinstance(od, tuple):
        a, cbd, sbd, fw, fb, fnet_scale = od
        part = lambda p: pl.BlockSpec((1, 1, tm, GROUP_W), lambda bb, i: (bb, p, i, 0))
        od_args = (a, a, cbd, sbd, fw, fb)
        od_specs = [part(0), part(1), full(cbd), full(sbd), full(fw), full(fb)]
    else:
        fnet_scale, od_args, od_specs = None, (od,), [grp]
    return pl.pallas_call(
        functools.partial(_merge_kernel, alpha=alpha, fnet_scale=fnet_scale),
        grid=(b, l // tm),
        in_specs=[grp, grp, grp, *od_specs, tok, pl.BlockSpec((1, 1, d), lambda bb, i: (bb, 0, 0)),
                  full(gn), full(w_out), full(b_out), full(ln_g), full(ln_b)],
        out_specs=tok,
        out_shape=jax.ShapeDtypeStruct((b, l, d), F32),
        compiler_params=_cp(("parallel", "parallel"), VMEM_LIMIT),
        name="merge",
    )(oa, ob, oc, *od_args, x, g1, gn, w_out, b_out, ln_g, ln_b)


def _ffn_kernel(x_ref, xp_ref, xn_ref, sh_ref, sc_ref, g_ref, wup_ref, vec_ref, wd_ref, bd_ref,
                lng_ref, lnb_ref, o_ref, h_ref, acc_ref, u0_ref, u1_ref, *, alpha, tm, n_chunks, n_tiles):
    i = pl.program_id(1)
    sh, sc = sh_ref[0], sc_ref[0]
    cw = FFN_CHUNK

    def hmod(v):
        return (_layer_norm(v) * (1.0 + sc) + sh).astype(BF16)

    h_ref[0:HALO] = hmod(xp_ref[0])
    h_ref[HALO:HALO + tm] = hmod(x_ref[0])
    h_ref[HALO + tm:] = hmod(xn_ref[0])
    acc_ref[...] = jnp.zeros_like(acc_ref)
    row8 = lax.broadcasted_iota(jnp.int32, (8, 1), 0)
    pad_top = (row8 == 7) & (i == 0)
    pad_bot = (row8 == 0) & (i == n_tiles - 1)

    def up(c, dst):
        dst[...] = _dot(h_ref[...], wup_ref[c])

    def down(c, src):
        vec = vec_ref[c]
        bias, w0, w1, w2 = vec[0:1], vec[1:2], vec[2:3], vec[3:4]
        src[HALO - 8:HALO] = jnp.where(pad_top, -bias, src[HALO - 8:HALO])
        src[HALO + tm:HALO + tm + 8] = jnp.where(pad_bot, -bias, src[HALO + tm:HALO + tm + 8])
        conv = (src[HALO - 1:HALO - 1 + tm] * w0 + src[HALO:HALO + tm] * w1 + src[HALO + 1:HALO + 1 + tm] * w2
                + (vec[4:5] + bias * (w0 + w1 + w2)))
        a, g = conv[:, :cw], conv[:, cw:]
        act = (a * jax.nn.sigmoid(a) * g).astype(BF16)
        acc_ref[...] += _dot(act, wd_ref[c])

    up(0, u0_ref)

    def body(p, carry):
        c = 2 * p
        up(c + 1, u1_ref)
        down(c, u0_ref)
        up(c + 2, u0_ref)
        down(c + 1, u1_ref)
        return carry

    n_pairs = (n_chunks - 1) // 2
    lax.fori_loop(0, n_pairs, body, 0)
    c = 2 * n_pairs
    if c + 1 < n_chunks:
        up(c + 1, u1_ref)
    down(c, u0_ref)
    if c + 1 < n_chunks:
        down(c + 1, u1_ref)
    r = alpha * x_ref[0] + g_ref[0] * (acc_ref[...] + bd_ref[...])
    o_ref[0] = _layer_norm(r) * lng_ref[...] + lnb_ref[...]


def _ffn(x, sh, sc, g2, wup, vec, wd, b_down, ln_g, ln_b, alpha):
    b, l, d = x.shape
    tm = min(l, 1024)
    n_chunks = wd.shape[0]
    main, prev, nxt = _halo_specs(tm, l, d)
    mod = pl.BlockSpec((1, 1, d), lambda bb, i: (bb, 0, 0))
    full = lambda a: pl.BlockSpec(a.shape, lambda bb, i: (0,) * a.ndim, pipeline_mode=pl.Buffered(1))
    return pl.pallas_call(
        functools.partial(_ffn_kernel, alpha=alpha, tm=tm, n_chunks=n_chunks, n_tiles=l // tm),
        grid=(b, l // tm),
        in_specs=[main, prev, nxt, mod, mod, mod, full(wup), full(vec), full(wd),
                  full(b_down), full(ln_g), full(ln_b)],
        out_specs=main,
        out_shape=jax.ShapeDtypeStruct((b, l, d), F32),
        scratch_shapes=[pltpu.VMEM((tm + 2 * HALO, d), BF16), pltpu.VMEM((tm, d), F32),
                        pltpu.VMEM((tm + 2 * HALO, 2 * FFN_CHUNK), F32),
                        pltpu.VMEM((tm + 2 * HALO, 2 * FFN_CHUNK), F32)],
        compiler_params=_cp(("parallel", "parallel"), VMEM_LIMIT),
        name="ffn",
    )(x, x, x, sh, sc, g2, wup, vec, wd, b_down, ln_g, ln_b)


def _cos_sin(idx, n):
    ang = (idx % n).astype(F32) * (2.0 * math.pi / n)
    return jnp.cos(ang), jnp.sin(ang)


def _cplx_rows(re, im):
    return jnp.concatenate([re, im], axis=-2)


def _cplx_block(re, im):
    return jnp.concatenate([jnp.concatenate([re, -im], axis=-1), jnp.concatenate([im, re], axis=-1)], axis=-2)


def _ar(n):
    return jnp.arange(n, dtype=jnp.int32)


def _hyena_tables(l):
    n, n2 = 2 * l, HYENA_N2
    n1 = n // n2
    nk = n1 // 2 + 1
    c, s = _cos_sin(_ar(nk)[:, None] * _ar(n1)[None, :], n1)
    f_full = _cplx_rows(c, -s).astype(BF16)
    wgt = jnp.where((_ar(nk) == 0) | (_ar(nk) == n1 // 2), 1.0, 2.0)[:, None] / n
    g = jnp.concatenate([(c * wgt).T, (-s * wgt).T], axis=1)[:n1 // 2].astype(BF16)
    k1, k2, m2 = _ar(nk)[:, None, None], _ar(n2)[None, :, None], _ar(n2)[None, None, :]
    c, s = _cos_sin(m2 * k1 + n1 * m2 * k2, n)
    mf = _cplx_block(c, -s).astype(BF16)
    return dict(f_half=_lead_kron(f_full[:, :n1 // 2]), f_full=_lead_kron(f_full), g=_lead_kron(g),
                mf=mf, mi=jnp.swapaxes(mf, 1, 2))


def _fnet_tables(l):
    n2 = FNET_N2
    n1 = l // n2
    nblk = n1 // 8
    blk, j = _ar(nblk)[:, None, None, None], _ar(8)[None, :, None, None]
    k2, m2 = _ar(n2)[None, None, :, None], _ar(n2)[None, None, None, :]
    c, s = _cos_sin(n1 * m2 * k2 + (8 * blk + j) * k2, l)
    eye = jnp.eye(8, dtype=F32)
    emb = lambda t: jnp.einsum('bjkn,ji->bjkni', t, eye).reshape(nblk, 8 * n2, n2 * 8)
    first = jnp.concatenate([emb(c), emb(-s)], axis=1).astype(BF16)
    c, s = _cos_sin(_ar(n1)[:, None] * _ar(n1)[None, :], n1)
    second = _cplx_block(c, -s).astype(BF16)
    return dict(first=first, second=_lead_kron(second))


def _dense_tables(l):
    n = 2 * l
    c, s = _cos_sin(_ar(n)[:, None] * _ar(n)[None, :], n)
    ff = _cplx_rows(c, -s).astype(BF16)
    gd = (jnp.concatenate([c, -s], axis=1)[:l] / n).astype(BF16)
    cl, sl = _cos_sin(_ar(l)[:, None] * _ar(l)[None, :], l)
    return dict(fd=ff[:, :l], ff=ff, gd=gd, cl=cl.astype(BF16), sl=sl.astype(BF16))


def _rope_tables(l):
    rows = l // GRID_W
    row = jnp.broadcast_to(jnp.arange(rows)[:, None], (rows, GRID_W)).reshape(-1).astype(F32)
    col = jnp.broadcast_to(jnp.arange(GRID_W)[None, :], (rows, GRID_W)).reshape(-1).astype(F32)
    half = HEAD_DIM // 2
    inv = ROPE_THETA ** (-jnp.arange(0, half, 2, dtype=F32) / half)
    ar, ac = row[:, None] * inv, col[:, None] * inv
    cos = jnp.concatenate([jnp.cos(ar), jnp.cos(ar), jnp.cos(ac), jnp.cos(ac)], axis=1)
    sin = jnp.concatenate([-jnp.sin(ar), jnp.sin(ar), -jnp.sin(ac), jnp.sin(ac)], axis=1)
    return jnp.tile(cos, (1, 2)), jnp.tile(sin, (1, 2))


def kernel(x, c, ctx, c_ctx, w_ada, b_ada, w_in, sink_a, q_norm_g, k_norm_g, hy_conv_w, hy_conv_b,
           hy_f_w1, hy_f_b1, hy_f_freq, hy_f_w2, hy_f_b2, hy_f_w3, hy_skip, fnet_w, fnet_b,
           out_norm_g, w_out, b_out, ln1_g, ln1_b, ffn_w_up, ffn_b_up, ffn_conv_w, ffn_conv_b,
           ffn_w_down, ffn_b_down, ln2_g, ln2_b):
    bsz, l, d = x.shape
    lc = ctx.shape[1]
    depth = w_ada.shape[0]
    d_ff = ffn_w_down.shape[1]
    alpha = (2 * depth) ** 0.25
    n2h, n2f = HYENA_N2, FNET_N2
    n1h, n1f = 2 * l // n2h, l // n2f
    n1k = n1h // 2 + 1

    ht, ft, dt = _hyena_tables(l), _fnet_tables(l), _dense_tables(lc)
    cos_l, sin_l = _rope_tables(l)
    cos_c, sin_c = jnp.ones((lc, 128), F32), jnp.zeros((lc, 128), F32)
    jj = _ar(GROUP_W)
    same = (jj[:, None] // FNET_GROUP_DIM) == (jj[None, :] // FNET_GROUP_DIM)
    bd_mean = (same.astype(F32) / HEAD_DIM).astype(BF16)
    cg, sg = _cos_sin(jj[:, None] * jj[None, :], FNET_GROUP_DIM)
    cbd, sbd = jnp.where(same, cg, 0.0).astype(BF16), jnp.where(same, sg, 0.0).astype(BF16)

    n_rows = -(-(bsz + 1) // 8) * 8
    cc = jnp.zeros((n_rows, d), F32).at[:bsz].set(c).at[bsz].set(c_ctx)
    mod = _ada(cc, w_ada, b_ada)

    k_lat = _hy_filter(l, hy_f_w1, hy_f_b1, hy_f_freq, hy_f_w2, hy_f_b2, hy_f_w3)
    k_ctx = _hy_filter(lc, hy_f_w1, hy_f_b1, hy_f_freq, hy_f_w2, hy_f_b2, hy_f_w3)
    a_k = _lead_mm(ht['f_full'], k_lat.reshape(depth, n1h, n2h, GROUP_W))
    k_spec = _hy_spectrum(a_k.reshape(depth, 2, n1k, n2h, GROUP_W), ht['mf'])

    n_chunks = -(-d_ff // FFN_CHUNK)
    pad = n_chunks * FFN_CHUNK - d_ff

    def chunked(v):
        v = v.reshape(v.shape[:-1] + (2, d_ff))
        v = jnp.pad(v, [(0, 0)] * (v.ndim - 1) + [(0, pad)])
        v = v.reshape(v.shape[:-2] + (2, n_chunks, FFN_CHUNK))
        v = jnp.swapaxes(v, -3, -2).reshape(v.shape[:-3] + (n_chunks, 2 * FFN_CHUNK))
        return jnp.moveaxis(v, -2, 0)

    for layer in range(depth):
        m6 = mod[layer].reshape(n_rows, 6, d)
        lat = [m6[:bsz, i][:, None, :] for i in range(6)]
        con = [jnp.broadcast_to(m6[bsz, i][None, None, :], (bsz, 1, d)) for i in range(6)]
        w_in_l = w_in[layer].astype(BF16)
        qg = jnp.tile(q_norm_g[layer], N_HEADS).reshape(1, -1)
        kg = jnp.tile(k_norm_g[layer], N_KV).reshape(1, -1)
        w_out_l = w_out[layer].astype(BF16)
        gn, b_out_l = out_norm_g[layer].reshape(1, d), b_out[layer].reshape(1, d)
        l1g, l1b = ln1_g[layer].reshape(1, d), ln1_b[layer].reshape(1, d)
        l2g, l2b = ln2_g[layer].reshape(1, d), ln2_b[layer].reshape(1, d)
        fw, fb = fnet_w[layer].astype(BF16), fnet_b[layer].reshape(1, GROUP_W)
        skip = hy_skip[layer].reshape(1, GROUP_W)
        wup = chunked(ffn_w_up[layer]).astype(BF16)
        vec = chunked(jnp.concatenate([ffn_b_up[layer][None], ffn_conv_w[layer], ffn_conv_b[layer][None],
                                       jnp.zeros((3, 2 * d_ff), F32)], axis=0))
        wd = jnp.pad(ffn_w_down[layer], ((0, pad), (0, 0))).reshape(n_chunks, FFN_CHUNK, d).astype(BF16)
        bdn = ffn_b_down[layer].reshape(1, d)
        last = layer == depth - 1

        cqa, cka, cva, cqb, ckb, cvb, czh, czf = _proj(ctx, con[0], con[1], w_in_l, cos_c, sin_c,
                                                       bd_mean, qg, kg)
        qa, ka, va, qb, kb, vb, zh, zf = _proj(x, lat[0], lat[1], w_in_l, cos_l, sin_l, bd_mean, qg, kg)
        oa = _attention(qa, ka, va, sink_a[layer], banded=True, kc=cka, vc=cva)
        ob = _attention(qb, kb, vb, None, banded=False, kc=ckb, vc=cvb)

        vx, x0 = _hy_gate(zh, hy_conv_w[layer], hy_conv_b[layer])
        vx4, x04 = (a.reshape(bsz, n1h // 2, n2h, GROUP_W) for a in (vx, x0))
        a1 = _lead_mm(ht['f_half'], vx4)
        bo = _hy_mid(a1.reshape(bsz, 2, n1k, n2h, GROUP_W), ht['mf'], ht['mi'], k_spec, layer)
        oc = _hy_out(ht['g'], bo.reshape(bsz, 2 * n1k, n2h, GROUP_W), vx4, x04, skip).reshape(bsz, l, GROUP_W)

        f1 = _fn_first(zf.reshape(bsz, n2f, n1f, GROUP_W), ft['first'])
        f2 = _lead_mm(ft['second'], f1.reshape(bsz, 2 * n1f, n2f, GROUP_W))
        od = (f2.reshape(bsz, 2, l, GROUP_W), cbd, sbd, fw, fb, (l * FNET_GROUP_DIM) ** -0.5)

        x = _merge(oa, ob, oc, od, x, lat[2], gn, w_out_l, b_out_l, l1g, l1b, alpha)
        x = _ffn(x, lat[3], lat[4], lat[5], wup, vec, wd, bdn, l2g, l2b, alpha)

        if not last:
            oac = _attention(cqa, cka, cva, sink_a[layer], banded=False)
            obc = _attention(cqb, ckb, cvb, None, banded=False)
            cvx, cx0 = _hy_gate(czh, hy_conv_w[layer], hy_conv_b[layer])
            occ = _hy_small(cvx, cx0, k_ctx, layer, dt['fd'], dt['ff'], dt['gd'], skip)
            odc = _fn_small(czf, dt['cl'], dt['sl'], cbd, sbd, fw, fb, (lc * FNET_GROUP_DIM) ** -0.5)
            ctx = _merge(oac, obc, occ, odc, ctx, con[2], gn, w_out_l, b_out_l, l1g, l1b, alpha)
            ctx = _ffn(ctx, con[3], con[4], con[5], wup, vec, wd, bdn, l2g, l2b, alpha)
    return x
```

```python
import functools
import math

import jax
import jax.numpy as jnp
from jax import lax
from jax.experimental import pallas as pl
from jax.experimental.pallas import tpu as pltpu

F32 = jnp.float32
BF16 = jnp.bfloat16

HEAD_DIM = 64
GROUP_W = 256
N_HEADS = 4
N_KV = 2
WINDOW = 128
GRID_W = 64
ROPE_THETA = 10000.0
FNET_GROUP_DIM = 64
HYENA_BANDS = 8
HYENA_FAST_DECAY = 0.3
HYENA_SLOW_DECAY = 1.5
HYENA_TARGET = 1e-2
LN_EPS = 1e-6
NEG_INF = -1e30
HYENA_N2 = 256
FNET_N2 = 128
FFN_CHUNK = 256
HALO = 16
VMEM_LIMIT = 56 * 1024 * 1024


def _cp(sem, vmem=None):
    return pltpu.CompilerParams(dimension_semantics=sem, vmem_limit_bytes=vmem)


def _layer_norm(x):
    mu = jnp.mean(x, axis=-1, keepdims=True)
    xc = x - mu
    var = jnp.mean(xc * xc, axis=-1, keepdims=True)
    return xc * lax.rsqrt(var + LN_EPS)


def _dot(a, b):
    return jnp.dot(a, b, preferred_element_type=F32)


def _ada_kernel(c_ref, w_ref, b_ref, o_ref):
    c = c_ref[...]
    s = (c * jax.nn.sigmoid(c)).astype(BF16)
    o_ref[0] = _dot(s, w_ref[0].astype(BF16)) + b_ref[0]


def _ada(cc, w_ada, b_ada):
    depth, d, n = w_ada.shape
    r = cc.shape[0]
    tn = 1536
    return pl.pallas_call(
        _ada_kernel,
        grid=(depth, n // tn),
        in_specs=[pl.BlockSpec((r, d), lambda l, j: (0, 0)),
                  pl.BlockSpec((1, d, tn), lambda l, j: (l, 0, j)),
                  pl.BlockSpec((1, 1, tn), lambda l, j: (l, 0, j))],
        out_specs=pl.BlockSpec((1, r, tn), lambda l, j: (l, 0, j)),
        out_shape=jax.ShapeDtypeStruct((depth, r, n), F32),
        compiler_params=_cp(("parallel", "parallel"), VMEM_LIMIT),
        name="ada",
    )(cc, w_ada, b_ada.reshape(depth, 1, n))


def _swap_halves(z):
    w = z.shape[1]
    lane = lax.broadcasted_iota(jnp.int32, z.shape, 1)
    return jnp.where(lane % 32 < 16, pltpu.roll(z, w - 16, 1), pltpu.roll(z, 16, 1))


def _proj_kernel(x_ref, sh_ref, sc_ref, w_ref, cos_ref, sin_ref, bd_ref, qg_ref, kg_ref,
                 qa_ref, ka_ref, va_ref, qb_ref, kb_ref, vb_ref, zh_ref, zf_ref):
    scale = HEAD_DIM ** -0.5
    h = (_layer_norm(x_ref[0]) * (1.0 + sc_ref[0]) + sh_ref[0]).astype(BF16)
    cos1, sin1 = cos_ref[...], sin_ref[...]
    cos2 = jnp.concatenate([cos1, cos1], axis=1)
    sin2 = jnp.concatenate([sin1, sin1], axis=1)

    def proj(lo, width):
        return _dot(h, w_ref[:, lo:lo + width])

    def rope(z):
        if z.shape[1] == 128:
            return z * cos1 + _swap_halves(z) * sin1
        return z * cos2 + _swap_halves(z) * sin2

    def head_norm(z, g):
        w = z.shape[1]
        ms = _dot((z * z).astype(BF16), bd_ref[:w, :w])
        return z * lax.rsqrt(ms + LN_EPS) * g

    def put_heads(ref, z):
        for hh in range(z.shape[1] // HEAD_DIM):
            ref[0, hh] = z[:, hh * HEAD_DIM:(hh + 1) * HEAD_DIM].astype(BF16)

    def put_values(ref, z):
        lane = lax.broadcasted_iota(jnp.int32, (z.shape[0], HEAD_DIM), 1)
        ones_col = jnp.where(lane == 0, 1.0, 0.0)
        for hh in range(z.shape[1] // HEAD_DIM):
            ref[0, hh] = jnp.concatenate([z[:, hh * HEAD_DIM:(hh + 1) * HEAD_DIM], ones_col],
                                         axis=1).astype(BF16)

    put_heads(qa_ref, rope(proj(0, 256)) * scale)
    kv = proj(256, 256)
    put_heads(ka_ref, rope(kv[:, :128]))
    put_values(va_ref, kv[:, 128:])
    put_heads(qb_ref, rope(head_norm(proj(512, 256), qg_ref[...])) * scale)
    kv = proj(768, 256)
    put_heads(kb_ref, rope(head_norm(kv[:, :128], kg_ref[...])))
    put_values(vb_ref, kv[:, 128:])
    zh_ref[0] = proj(1024, 768).astype(BF16)
    zf_ref[0] = proj(1792, 256).astype(BF16)


def _proj(x, sh, sc, w_in, cos, sin, bd, qg, kg):
    b, l, d = x.shape
    tm = min(l, 1024)
    heads = lambda n, w=HEAD_DIM: jax.ShapeDtypeStruct((b, n, l, w), BF16)
    hspec = lambda n, w=HEAD_DIM: pl.BlockSpec((1, n, tm, w), lambda t, bb: (bb, 0, t, 0))
    tok = lambda w: pl.BlockSpec((1, tm, w), lambda t, bb: (bb, t, 0))
    vec = lambda w: pl.BlockSpec((1, 1, w), lambda t, bb: (bb, 0, 0))
    full = lambda a: pl.BlockSpec(a.shape, lambda t, bb: (0,) * a.ndim)
    tab = pl.BlockSpec((tm, 128), lambda t, bb: (t, 0))
    return pl.pallas_call(
        _proj_kernel,
        grid=(l // tm, b),
        in_specs=[tok(d), vec(d), vec(d), full(w_in), tab, tab, full(bd), full(qg), full(kg)],
        out_specs=[hspec(4), hspec(2), hspec(2, 128), hspec(4), hspec(2), hspec(2, 128), tok(768), tok(256)],
        out_shape=[heads(4), heads(2), heads(2, 128), heads(4), heads(2), heads(2, 128),
                   jax.ShapeDtypeStruct((b, l, 768), BF16), jax.ShapeDtypeStruct((b, l, 256), BF16)],
        compiler_params=_cp(("parallel", "parallel"), VMEM_LIMIT),
        name="proj",
    )(x, sh, sc, w_in, cos, sin, bd, qg, kg)


_NT = (((1,), (1,)), ((), ()))


def _row_max(*parts):
    tiles = [p[:, i:i + 128] for p in parts for i in range(0, p.shape[1], 128)]
    m = tiles[0]
    for t in tiles[1:]:
        m = jnp.maximum(m, t)
    return jnp.max(m, axis=-1, keepdims=True)


def _sink_rows(sink_ref, j, tq):
    row = lax.broadcasted_iota(jnp.int32, (2 * tq, 1), 0)
    return jnp.where(row < tq, sink_ref[2 * j], sink_ref[2 * j + 1])


def _attn_full_kernel(*refs, has_sink, has_ctx, tq, tk, n_t, n_chunks):
    if has_ctx:
        sink_ref, q_ref, k_ref, v_ref, kc_ref, vc_ref, o_ref, m_ref, acc_ref, s0_ref, s1_ref = refs
        l_ctx = kc_ref.shape[2]
    else:
        sink_ref, q_ref, k_ref, v_ref, o_ref, m_ref, acc_ref, s0_ref, s1_ref = refs
    bufs = (s0_ref, s1_ref)
    j = pl.program_id(1)
    lane = lax.broadcasted_iota(jnp.int32, acc_ref.shape[1:], 1)
    for tile in range(n_t):
        if has_sink:
            m_ref[tile] = _sink_rows(sink_ref, j, tq)
            acc_ref[tile] = jnp.where(lane == HEAD_DIM, 1.0, 0.0)
        else:
            m_ref[tile] = jnp.full(m_ref.shape[1:], NEG_INF, F32)
            acc_ref[tile] = jnp.zeros(acc_ref.shape[1:], F32)

    def chunk(ref, t):
        c = t // n_t
        off = c * tk if isinstance(c, int) else pl.multiple_of(c * tk, tk)
        return ref[0, 0, pl.ds(off, tk), :]

    def scores(t, tile, dst, ctx=False):
        q = q_ref[0, :, tile * tq:(tile + 1) * tq, :].reshape(2 * tq, HEAD_DIM)
        if ctx:
            dst[:, :l_ctx] = lax.dot_general(q, kc_ref[0, 0], _NT, preferred_element_type=F32)
        else:
            dst[...] = lax.dot_general(q, chunk(k_ref, t), _NT, preferred_element_type=F32)

    def update(t, tile, src, ctx=False):
        s = src[:, :l_ctx] if ctx else src[...]
        v = vc_ref[0, 0] if ctx else chunk(v_ref, t)
        m_prev = m_ref[tile]
        m_new = jnp.maximum(m_prev, _row_max(s))
        p = jnp.exp((s - m_new).astype(BF16))
        acc_ref[tile] = jnp.exp(m_prev - m_new) * acc_ref[tile] + _dot(p, v)
        m_ref[tile] = m_new

    total = n_t * n_chunks
    per_iter = max(n_t, 2)
    scores(0, 0, s0_ref)

    def body(i, carry):
        t = per_iter * i
        for r in range(per_iter):
            scores(t + r + 1, (r + 1) % n_t, bufs[(r + 1) % 2])
            update(t + r, r % n_t, bufs[r % 2])
        return carry

    n_iter = (total - 1) // per_iter
    lax.fori_loop(0, n_iter, body, 0, unroll=4 // per_iter)
    first = per_iter * n_iter
    items = [(t, t % n_t, False) for t in range(first, total)]
    if has_ctx:
        items += [(None, tile, True) for tile in range(n_t)]
    for idx, (t, tile, ctx) in enumerate(items):
        if idx + 1 < len(items):
            nt, ntile, nctx = items[idx + 1]
            scores(nt, ntile, bufs[(first + idx + 1) % 2], nctx)
        update(t, tile, bufs[(first + idx) % 2], ctx)
    for tile in range(n_t):
        acc = acc_ref[tile]
        o = acc[:, :HEAD_DIM] * (1.0 / acc[:, HEAD_DIM:HEAD_DIM + 1])
        o_ref[0, tile * tq:(tile + 1) * tq, :] = jnp.concatenate([o[:tq], o[tq:]], axis=1).astype(BF16)


def _band_bias(tq):
    span = tq + 2 * WINDOW
    r = (_ar(2 * tq) % tq)[None, :, None]
    col = _ar(span)[None, None, :]
    shift = (_ar(3) * WINDOW)[:, None, None]
    return jnp.where(jnp.abs(col - shift - r) <= WINDOW, 0.0, NEG_INF).astype(F32)


def _attn_band_kernel(sink_ref, q_ref, k_ref, v_ref, kc_ref, vc_ref, bias_ref, o_ref, *,
                      tq, n_t, n_tiles, l_lat):
    j, g = pl.program_id(1), pl.program_id(2)
    span = tq + 2 * WINDOW
    sink = _sink_rows(sink_ref, j, tq)
    kc, vc = kc_ref[0, 0], vc_ref[0, 0]

    def scores(tile):
        qi = g * n_t + tile
        start = pl.multiple_of(jnp.clip(qi * tq - WINDOW, 0, l_lat - span), WINDOW)
        kind = jnp.where(qi == 0, 0, jnp.where(qi == n_tiles - 1, 2, 1))
        q = q_ref[0, :, tile * tq:(tile + 1) * tq, :].reshape(2 * tq, HEAD_DIM)
        s_lat = (lax.dot_general(q, k_ref[0, 0, pl.ds(start, span), :], _NT, preferred_element_type=F32)
                 + bias_ref[kind])
        return start, s_lat, lax.dot_general(q, kc, _NT, preferred_element_type=F32)

    nxt = scores(0)
    for tile in range(n_t):
        start, s_lat, s_ctx = nxt
        if tile + 1 < n_t:
            nxt = scores(tile + 1)
        m = jnp.maximum(sink, _row_max(s_lat, s_ctx))
        acc = (_dot(jnp.exp((s_lat - m).astype(BF16)), v_ref[0, 0, pl.ds(start, span), :])
               + _dot(jnp.exp((s_ctx - m).astype(BF16)), vc))
        o = acc[:, :HEAD_DIM] * (1.0 / (acc[:, HEAD_DIM:HEAD_DIM + 1] + jnp.exp(sink - m)))
        o_ref[0, tile * tq:(tile + 1) * tq, :] = jnp.concatenate([o[:tq], o[tq:]], axis=1).astype(BF16)


def _attention(q, k, v, sink, *, banded, kc=None, vc=None):
    b, _, lq, _ = q.shape
    lk = k.shape[2]
    tq = 256
    has_sink = sink is not None
    if not has_sink:
        sink = jnp.zeros((N_HEADS,), F32)
    smem = pl.BlockSpec(memory_space=pltpu.SMEM)
    whole = lambda a: pl.BlockSpec((1, 1) + a.shape[2:], lambda bb, j, g: (bb, j, 0, 0))
    out_shape = jax.ShapeDtypeStruct((b, lq, GROUP_W), BF16)
    if banded:
        assert lq == lk and lk >= tq + 2 * WINDOW and tq % WINDOW == 0
        n_tiles = lq // tq
        n_t = next(t for t in (4, 2, 1) if n_tiles % t == 0)
        bias = _band_bias(tq)
        return pl.pallas_call(
            functools.partial(_attn_band_kernel, tq=tq, n_t=n_t, n_tiles=n_tiles, l_lat=lk),
            grid=(b, N_KV, n_tiles // n_t),
            in_specs=[smem,
                      pl.BlockSpec((1, 2, n_t * tq, HEAD_DIM), lambda bb, j, g: (bb, j, g, 0)),
                      whole(k), whole(v), whole(kc), whole(vc),
                      pl.BlockSpec(bias.shape, lambda bb, j, g: (0, 0, 0))],
            out_specs=pl.BlockSpec((1, n_t * tq, 128), lambda bb, j, g: (bb, g, j)),
            out_shape=out_shape,
            compiler_params=_cp(("parallel", "parallel", "parallel"), VMEM_LIMIT),
            name="attn_band",
        )(sink, q, k, v, kc, vc, bias)
    has_ctx = kc is not None
    tk = next(t for t in (4096, 2048, 1024, 768, 512, 256) if lk % t == 0)
    assert not has_ctx or kc.shape[2] <= tk
    n_t = 2 if lq % (2 * tq) == 0 else 1
    kern = functools.partial(_attn_full_kernel, has_sink=has_sink, has_ctx=has_ctx, tq=tq, tk=tk, n_t=n_t,
                             n_chunks=lk // tk)
    extra = (kc, vc) if has_ctx else ()
    return pl.pallas_call(
        kern,
        grid=(b, N_KV, lq // (n_t * tq)),
        in_specs=[smem,
                  pl.BlockSpec((1, 2, n_t * tq, HEAD_DIM), lambda bb, j, g: (bb, j, g, 0)),
                  whole(k), whole(v), *[whole(a) for a in extra]],
        out_specs=pl.BlockSpec((1, n_t * tq, 128), lambda bb, j, g: (bb, g, j)),
        out_shape=out_shape,
        scratch_shapes=[pltpu.VMEM((n_t, 2 * tq, 1), F32), pltpu.VMEM((n_t, 2 * tq, 2 * HEAD_DIM), F32),
                        pltpu.VMEM((2 * tq, tk), F32), pltpu.VMEM((2 * tq, tk), F32)],
        compiler_params=_cp(("parallel", "parallel", "parallel"), VMEM_LIMIT),
        name="attn_full",
    )(sink, q, k, v, *extra)


def _shift_rows(z, prev_row, next_row):
    n = z.shape[0]
    row = lax.broadcasted_iota(jnp.int32, (n, 1), 0)
    dn = jnp.where(row == 0, prev_row, pltpu.roll(z, 1, 0))
    up = jnp.where(row == n - 1, next_row, pltpu.roll(z, n - 1, 0))
    return dn, up


def _hy_gate_kernel(z_ref, zp_ref, zn_ref, w_ref, b_ref, vx_ref, x0_ref, *, n_tiles):
    i = pl.program_id(1)
    z = z_ref[0].astype(F32)
    prev_row = zp_ref[0].astype(F32)[HALO - 1:HALO] * (i > 0).astype(F32)
    next_row = zn_ref[0].astype(F32)[0:1] * (i < n_tiles - 1).astype(F32)
    dn, up = _shift_rows(z, prev_row, next_row)
    w = w_ref[...]
    u = dn * w[0:1] + z * w[1:2] + up * w[2:3] + b_ref[...]
    x0_ref[0] = u[:, :GROUP_W].astype(BF16)
    vx_ref[0] = (u[:, 2 * GROUP_W:] * u[:, GROUP_W:2 * GROUP_W]).astype(BF16)


def _halo_specs(tl, l, w):
    nb = l // HALO
    per = tl // HALO
    return (pl.BlockSpec((1, tl, w), lambda bb, i: (bb, i, 0)),
            pl.BlockSpec((1, HALO, w), lambda bb, i: (bb, jnp.maximum(i * per - 1, 0), 0)),
            pl.BlockSpec((1, HALO, w), lambda bb, i: (bb, jnp.minimum((i + 1) * per, nb - 1), 0)))


def _hy_gate(zh, conv_w, conv_b):
    b, l, w = zh.shape
    tl = min(l, 512)
    out = jax.ShapeDtypeStruct((b, l, GROUP_W), BF16)
    ospec = pl.BlockSpec((1, tl, GROUP_W), lambda bb, i: (bb, i, 0))
    return pl.pallas_call(
        functools.partial(_hy_gate_kernel, n_tiles=l // tl),
        grid=(b, l // tl),
        in_specs=[*_halo_specs(tl, l, w),
                  pl.BlockSpec((3, w), lambda bb, i: (0, 0)),
                  pl.BlockSpec((1, w), lambda bb, i: (0, 0))],
        out_specs=[ospec, ospec],
        out_shape=[out, out],
        compiler_params=_cp(("parallel", "parallel")),
        name="hy_gate",
    )(zh, zh, zh, conv_w, conv_b.reshape(1, w))


def _hy_filter_kernel(fc_ref, w1_ref, b1_ref, fr_ref, w2_ref, b2_ref, w3_ref, dl_ref, k_ref, *, l, tr):
    base = pl.program_id(1) * tr

    def position(m):
        return jnp.where(m < l, m, 2 * l - m).astype(F32)

    m_row = base + lax.broadcasted_iota(jnp.int32, (1, tr), 1)
    t_row = position(m_row)
    feat_id = lax.broadcasted_iota(jnp.int32, (fc_ref.shape[0], 1), 0)
    phase = fc_ref[...] * t_row
    feat = jnp.where(feat_id == 0, t_row / max(l - 1, 1),
                     jnp.where(feat_id <= HYENA_BANDS, jnp.cos(phase),
                               jnp.where(feat_id <= 2 * HYENA_BANDS, -jnp.sin(phase), 0.0)))
    hp = functools.partial(jnp.dot, precision=lax.Precision.HIGHEST, preferred_element_type=F32)
    fr = fr_ref[0]
    h = jnp.sin(fr * (hp(w1_ref[0], feat) + b1_ref[0]))
    h = jnp.sin(fr * (hp(w2_ref[0], h) + b2_ref[0]))
    h = hp(h.T, w3_ref[0])
    m = base + lax.broadcasted_iota(jnp.int32, (tr, 1), 0)
    dec = jnp.exp(-(position(m) / max(l - 1, 1)) * dl_ref[...])
    hf, hb = h[:, :GROUP_W] * dec, h[:, GROUP_W:] * dec
    k = jnp.where(m < l, hf, hb)
    k = jnp.where(m == 0, hf + hb, k)
    k_ref[0] = jnp.where(m == l, 0.0, k).astype(BF16)


def _hy_filter(l, w1, b1, fr, w2, b2, w3):
    depth, emb, hid = w1.shape
    tr = min(2 * l, 2048)
    n_feat = -(-emb // 8) * 8
    bands = jnp.linspace(1e-4, HYENA_BANDS - 1, HYENA_BANDS, dtype=F32) * (2.0 * math.pi / l)
    fc = jnp.zeros((n_feat, 1), F32).at[1:1 + HYENA_BANDS, 0].set(bands).at[1 + HYENA_BANDS:emb, 0].set(bands)
    w1t = jnp.zeros((depth, hid, n_feat), F32).at[:, :, :emb].set(jnp.swapaxes(w1, 1, 2))
    w2t = jnp.swapaxes(w2, 1, 2)
    min_decay = math.log(HYENA_TARGET) / HYENA_SLOW_DECAY
    max_decay = math.log(HYENA_TARGET) / HYENA_FAST_DECAY
    deltas = jnp.abs(jnp.linspace(min_decay, max_decay, GROUP_W, dtype=F32)).reshape(1, GROUP_W)
    lay = lambda a: pl.BlockSpec((1,) + a.shape[1:], lambda d, i: (d,) + (0,) * (a.ndim - 1))
    fix = lambda a: pl.BlockSpec(a.shape, lambda d, i: (0,) * a.ndim)
    b1, fr, b2 = (a.reshape(depth, hid, 1) for a in (b1, fr, b2))
    return pl.pallas_call(
        functools.partial(_hy_filter_kernel, l=l, tr=tr),
        grid=(depth, 2 * l // tr),
        in_specs=[fix(fc), lay(w1t), lay(b1), lay(fr), lay(w2t), lay(b2), lay(w3), fix(deltas)],
        out_specs=pl.BlockSpec((1, tr, GROUP_W), lambda d, i: (d, i, 0)),
        out_shape=jax.ShapeDtypeStruct((depth, 2 * l, GROUP_W), BF16),
        compiler_params=_cp(("parallel", "parallel")),
        name="hy_filter",
    )(fc, w1t, b1, fr, w2t, b2, w3, deltas)


ROW_TILE = 8


def _lead_kron(f):
    return jnp.kron(f.astype(F32), jnp.eye(ROW_TILE, dtype=F32)).astype(BF16)


def _lead_rows(n):
    return next(r for r in (4 * ROW_TILE, 2 * ROW_TILE, ROW_TILE) if n % r == 0)


def _lead_mm_kernel(f_ref, x_ref, o_ref):
    k, rows, c = x_ref.shape[1:]
    for j in range(rows // ROW_TILE):
        rs = slice(j * ROW_TILE, (j + 1) * ROW_TILE)
        y = _dot(f_ref[...], x_ref[0, :, rs, :].reshape(k * ROW_TILE, c))
        o_ref[0, :, rs, :] = y.reshape(-1, ROW_TILE, c).astype(o_ref.dtype)


def _lead_mm(fk, x):
    b, k, n, c = x.shape
    m = fk.shape[0] // ROW_TILE
    rows = _lead_rows(n)
    return pl.pallas_call(
        _lead_mm_kernel,
        grid=(b, n // rows),
        in_specs=[pl.BlockSpec(fk.shape, lambda bb, i: (0, 0)),
                  pl.BlockSpec((1, k, rows, c), lambda bb, i: (bb, 0, i, 0))],
        out_specs=pl.BlockSpec((1, m, rows, c), lambda bb, i: (bb, 0, i, 0)),
        out_shape=jax.ShapeDtypeStruct((b, m, n, c), BF16),
        compiler_params=_cp(("parallel", "parallel"), VMEM_LIMIT),
        name="lead_mm",
    )(fk, x)


def _hy_spec_kernel(a_ref, mf_ref, k_ref):
    for d in range(a_ref.shape[0]):
        x = a_ref[d, :, 0].reshape(2 * a_ref.shape[3], GROUP_W)
        k_ref[d, 0] = _dot(mf_ref[0], x)


def _hy_mid_kernel(a_ref, mf_ref, mi_ref, k_ref, o_ref):
    n2 = a_ref.shape[3]
    k = k_ref[0, 0]
    kr, ki = k[:n2], k[n2:]
    for bb in range(a_ref.shape[0]):
        x = a_ref[bb, :, 0].reshape(2 * n2, GROUP_W)
        y = _dot(mf_ref[0], x)
        yr, yi = y[:n2], y[n2:]
        z = jnp.concatenate([yr * kr - yi * ki, yr * ki + yi * kr], axis=0).astype(BF16)
        o_ref[bb, :, 0] = _dot(mi_ref[0], z).astype(BF16).reshape(2, n2, GROUP_W)


def _hy_spectrum(a, mf):
    depth, _, n1, n2, w = a.shape
    return pl.pallas_call(
        _hy_spec_kernel,
        grid=(n1,),
        in_specs=[pl.BlockSpec((depth, 2, 1, n2, w), lambda k1: (0, 0, k1, 0, 0)),
                  pl.BlockSpec((1, 2 * n2, 2 * n2), lambda k1: (k1, 0, 0))],
        out_specs=pl.BlockSpec((depth, 1, 2 * n2, w), lambda k1: (0, k1, 0, 0)),
        out_shape=jax.ShapeDtypeStruct((depth, n1, 2 * n2, w), F32),
        compiler_params=_cp(("parallel",)),
        name="hy_spectrum",
    )(a, mf)


def _hy_mid(a, mf, mi, kspec, layer):
    b, _, n1, n2, w = a.shape
    blk = pl.BlockSpec((b, 2, 1, n2, w), lambda k1: (0, 0, k1, 0, 0))
    mat = pl.BlockSpec((1, 2 * n2, 2 * n2), lambda k1: (k1, 0, 0))
    return pl.pallas_call(
        _hy_mid_kernel,
        grid=(n1,),
        in_specs=[blk, mat, mat,
                  pl.BlockSpec((1, 1, 2 * n2, w), lambda k1: (layer, k1, 0, 0))],
        out_specs=blk,
        out_shape=jax.ShapeDtypeStruct(a.shape, BF16),
        compiler_params=_cp(("parallel",)),
        name="hy_mid",
    )(a, mf, mi, kspec)


def _hy_out_kernel(g_ref, b_ref, vx_ref, x0_ref, skip_ref, o_ref):
    k, rows, c = b_ref.shape[1:]
    for j in range(rows // ROW_TILE):
        rs = slice(j * ROW_TILE, (j + 1) * ROW_TILE)
        y = _dot(g_ref[...], b_ref[0, :, rs, :].reshape(k * ROW_TILE, c)).reshape(-1, ROW_TILE, c)
        vx = vx_ref[0, :, rs, :].astype(F32)
        o_ref[0, :, rs, :] = ((y + skip_ref[...] * vx) * x0_ref[0, :, rs, :].astype(F32)).astype(BF16)


def _hy_out(gk, bo, vx, x0, skip):
    b, k, n, c = bo.shape
    m = gk.shape[0] // ROW_TILE
    rows = _lead_rows(n)
    sig = pl.BlockSpec((1, m, rows, c), lambda bb, i: (bb, 0, i, 0))
    return pl.pallas_call(
        _hy_out_kernel,
        grid=(b, n // rows),
        in_specs=[pl.BlockSpec(gk.shape, lambda bb, i: (0, 0)),
                  pl.BlockSpec((1, k, rows, c), lambda bb, i: (bb, 0, i, 0)),
                  sig, sig, pl.BlockSpec((1, c), lambda bb, i: (0, 0))],
        out_specs=sig,
        out_shape=jax.ShapeDtypeStruct((b, m, n, c), BF16),
        compiler_params=_cp(("parallel", "parallel"), VMEM_LIMIT),
        name="hy_out",
    )(gk, bo, vx, x0, skip)


def _hy_small_kernel(vx_ref, x0_ref, k_ref, fd_ref, ff_ref, gd_ref, skip_ref, o_ref):
    nf = fd_ref.shape[0] // 2
    vx = vx_ref[0]
    u = _dot(fd_ref[...], vx)
    k = _dot(ff_ref[...], k_ref[0])
    ur, ui, kr, ki = u[:nf], u[nf:], k[:nf], k[nf:]
    z = jnp.concatenate([ur * kr - ui * ki, ur * ki + ui * kr], axis=0).astype(BF16)
    y = _dot(gd_ref[...], z)
    o_ref[0] = ((y + skip_ref[...] * vx.astype(F32)) * x0_ref[0].astype(F32)).astype(BF16)


def _hy_small(vx, x0, kfilt, layer, fd, ff, gd, skip):
    b, l, w = vx.shape
    sig = pl.BlockSpec((1, l, w), lambda bb: (bb, 0, 0))
    full = lambda a: pl.BlockSpec(a.shape, lambda bb: (0,) * a.ndim)
    return pl.pallas_call(
        _hy_small_kernel,
        grid=(b,),
        in_specs=[sig, sig, pl.BlockSpec((1, 2 * l, w), lambda bb: (layer, 0, 0)),
                  full(fd), full(ff), full(gd), full(skip)],
        out_specs=sig,
        out_shape=jax.ShapeDtypeStruct((b, l, w), BF16),
        compiler_params=_cp(("parallel",)),
        name="hy_small",
    )(vx, x0, kfilt, fd, ff, gd, skip)


def _fn_first_kernel(z_ref, m_ref, o_ref):
    n2 = z_ref.shape[1]
    x = z_ref[0].reshape(n2 * 8, GROUP_W)
    o_ref[0] = _dot(m_ref[0], x).astype(BF16).reshape(2, 8, n2, GROUP_W)


def _fn_first(zf, mats):
    b, n2, n1, w = zf.shape
    return pl.pallas_call(
        _fn_first_kernel,
        grid=(n1 // 8, b),
        in_specs=[pl.BlockSpec((1, n2, 8, w), lambda i, bb: (bb, 0, i, 0)),
                  pl.BlockSpec((1, 16 * n2, 8 * n2), lambda i, bb: (i, 0, 0))],
        out_specs=pl.BlockSpec((1, 2, 8, n2, w), lambda i, bb: (bb, 0, i, 0, 0)),
        out_shape=jax.ShapeDtypeStruct((b, 2, n1, n2, w), BF16),
        compiler_params=_cp(("parallel", "parallel"), VMEM_LIMIT),
        name="fn_first",
    )(zf, mats)


def _fn_small_kernel(u_ref, cl_ref, sl_ref, c_ref, s_ref, w_ref, b_ref, o_ref, *, scale):
    u = u_ref[0]
    uc = _dot(u, c_ref[...]).astype(BF16)
    us = _dot(u, s_ref[...]).astype(BF16)
    f = (_dot(cl_ref[...], uc) - _dot(sl_ref[...], us)) * scale
    o_ref[0] = (_dot(f.astype(BF16), w_ref[...]) + b_ref[...]).astype(BF16)


def _fn_small(zf, cl, sl, cbd, sbd, w, bias, scale):
    b, l, c = zf.shape
    sig = pl.BlockSpec((1, l, c), lambda bb: (bb, 0, 0))
    full = lambda z: pl.BlockSpec(z.shape, lambda bb: (0,) * z.ndim)
    return pl.pallas_call(
        functools.partial(_fn_small_kernel, scale=scale),
        grid=(b,),
        in_specs=[sig, full(cl), full(sl), full(cbd), full(sbd), full(w), full(bias)],
        out_specs=sig,
        out_shape=jax.ShapeDtypeStruct((b, l, c), BF16),
        compiler_params=_cp(("parallel",)),
        name="fn_small",
    )(zf, cl, sl, cbd, sbd, w, bias)


def _merge_kernel(*refs, alpha, fnet_scale):
    if fnet_scale is None:
        oa_ref, ob_ref, oc_ref, od_ref = refs[:4]
        od = od_ref[0].astype(F32)
        rest = refs[4:]
    else:
        oa_ref, ob_ref, oc_ref, ar_ref, ai_ref, c_ref, s_ref, fw_ref, fb_ref = refs[:9]
        t = (_dot(ar_ref[0, 0], c_ref[...]) + _dot(ai_ref[0, 0], s_ref[...])) * fnet_scale
        od = _dot(t.astype(BF16), fw_ref[...]) + fb_ref[...]
        rest = refs[9:]
    x_ref, g1_ref, gn_ref, w_ref, b_ref, lng_ref, lnb_ref, o_ref = rest
    parts = []
    for idx, v in enumerate((oa_ref[0].astype(F32), ob_ref[0].astype(F32), oc_ref[0].astype(F32), od)):
        ms = jnp.mean(v * v, axis=-1, keepdims=True)
        gain = gn_ref[:, idx * GROUP_W:(idx + 1) * GROUP_W]
        parts.append((v * lax.rsqrt(ms + LN_EPS) * gain).astype(BF16))
    y = _dot(jnp.concatenate(parts, axis=1), w_ref[...]) + b_ref[...]
    r = alpha * x_ref[0] + g1_ref[0] * y
    o_ref[0] = _layer_norm(r) * lng_ref[...] + lnb_ref[...]


def _merge(oa, ob, oc, od, x, g1, gn, w_out, b_out, ln_g, ln_b, alpha):
    b, l, d = x.shape
    tm = min(l, 1024)
    grp = pl.BlockSpec((1, tm, GROUP_W), lambda bb, i: (bb, i, 0))
    tok = pl.BlockSpec((1, tm, d), lambda bb, i: (bb, i, 0))
    full = lambda a: pl.BlockSpec(a.shape, lambda bb, i: (0,) * a.ndim)
    if isinstance(od, tuple):
        a, cbd, sbd, fw, fb, fnet_scale = od
        part = lambda p: pl.BlockSpec((1, 1, tm, GROUP_W), lambda bb, i: (bb, p, i, 0))
        od_args = (a, a, cbd, sbd, fw, fb)
        od_specs = [part(0), part(1), full(cbd), full(sbd), full(fw), full(fb)]
    else:
        fnet_scale, od_args, od_specs = None, (od,), [grp]
    return pl.pallas_call(
        functools.partial(_merge_kernel, alpha=alpha, fnet_scale=fnet_scale),
        grid=(b, l // tm),
        in_specs=[grp, grp, grp, *od_specs, tok, pl.BlockSpec((1, 1, d), lambda bb, i: (bb, 0, 0)),
                  full(gn), full(w_out), full(b_out), full(ln_g), full(ln_b)],
        out_specs=tok,
        out_shape=jax.ShapeDtypeStruct((b, l, d), F32),
        compiler_params=_cp(("parallel", "parallel"), VMEM_LIMIT),
        name="merge",
    )(oa, ob, oc, *od_args, x, g1, gn, w_out, b_out, ln_g, ln_b)


def _ffn_kernel(x_ref, xp_ref, xn_ref, sh_ref, sc_ref, g_ref, wup_ref, vec_ref, wd_ref, bd_ref,
                lng_ref, lnb_ref, o_ref, h_ref, acc_ref, u0_ref, u1_ref, *, alpha, tm, n_chunks, n_tiles):
    i = pl.program_id(1)
    sh, sc = sh_ref[0], sc_ref[0]
    cw = FFN_CHUNK

    def hmod(v):
        return (_layer_norm(v) * (1.0 + sc) + sh).astype(BF16)

    h_ref[0:HALO] = hmod(xp_ref[0])
    h_ref[HALO:HALO + tm] = hmod(x_ref[0])
    h_ref[HALO + tm:] = hmod(xn_ref[0])
    acc_ref[...] = jnp.zeros_like(acc_ref)
    row8 = lax.broadcasted_iota(jnp.int32, (8, 1), 0)
    pad_top = (row8 == 7) & (i == 0)
    pad_bot = (row8 == 0) & (i == n_tiles - 1)

    def up(c, dst):
        dst[...] = _dot(h_ref[...], wup_ref[c])

    def down(c, src):
        vec = vec_ref[c]
        bias, w0, w1, w2 = vec[0:1], vec[1:2], vec[2:3], vec[3:4]
        src[HALO - 8:HALO] = jnp.where(pad_top, -bias, src[HALO - 8:HALO])
        src[HALO + tm:HALO + tm + 8] = jnp.where(pad_bot, -bias, src[HALO + tm:HALO + tm + 8])
        conv = (src[HALO - 1:HALO - 1 + tm] * w0 + src[HALO:HALO + tm] * w1 + src[HALO + 1:HALO + 1 + tm] * w2
                + (vec[4:5] + bias * (w0 + w1 + w2)))
        a, g = conv[:, :cw], conv[:, cw:]
        act = (a * jax.nn.sigmoid(a) * g).astype(BF16)
        acc_ref[...] += _dot(act, wd_ref[c])

    up(0, u0_ref)

    def body(p, carry):
        c = 2 * p
        up(c + 1, u1_ref)
        down(c, u0_ref)
        up(c + 2, u0_ref)
        down(c + 1, u1_ref)
        return carry

    n_pairs = (n_chunks - 1) // 2
    lax.fori_loop(0, n_pairs, body, 0)
    c = 2 * n_pairs
    if c + 1 < n_chunks:
        up(c + 1, u1_ref)
    down(c, u0_ref)
    if c + 1 < n_chunks:
        down(c + 1, u1_ref)
    r = alpha * x_ref[0] + g_ref[0] * (acc_ref[...] + bd_ref[...])
    o_ref[0] = _layer_norm(r) * lng_ref[...] + lnb_ref[...]


def _ffn(x, sh, sc, g2, wup, vec, wd, b_down, ln_g, ln_b, alpha):
    b, l, d = x.shape
    tm = min(l, 1024)
    n_chunks = wd.shape[0]
    main, prev, nxt = _halo_specs(tm, l, d)
    mod = pl.BlockSpec((1, 1, d), lambda bb, i: (bb, 0, 0))
    full = lambda a: pl.BlockSpec(a.shape, lambda bb, i: (0,) * a.ndim, pipeline_mode=pl.Buffered(1))
    return pl.pallas_call(
        functools.partial(_ffn_kernel, alpha=alpha, tm=tm, n_chunks=n_chunks, n_tiles=l // tm),
        grid=(b, l // tm),
        in_specs=[main, prev, nxt, mod, mod, mod, full(wup), full(vec), full(wd),
                  full(b_down), full(ln_g), full(ln_b)],
        out_specs=main,
        out_shape=jax.ShapeDtypeStruct((b, l, d), F32),
        scratch_shapes=[pltpu.VMEM((tm + 2 * HALO, d), BF16), pltpu.VMEM((tm, d), F32),
                        pltpu.VMEM((tm + 2 * HALO, 2 * FFN_CHUNK), F32),
                        pltpu.VMEM((tm + 2 * HALO, 2 * FFN_CHUNK), F32)],
        compiler_params=_cp(("parallel", "parallel"), VMEM_LIMIT),
        name="ffn",
    )(x, x, x, sh, sc, g2, wup, vec, wd, b_down, ln_g, ln_b)


def _cos_sin(idx, n):
    ang = (idx % n).astype(F32) * (2.0 * math.pi / n)
    return jnp.cos(ang), jnp.sin(ang)


def _cplx_rows(re, im):
    return jnp.concatenate([re, im], axis=-2)


def _cplx_block(re, im):
    return jnp.concatenate([jnp.concatenate([re, -im], axis=-1), jnp.concatenate([im, re], axis=-1)], axis=-2)


def _ar(n):
    return jnp.arange(n, dtype=jnp.int32)


def _hyena_tables(l):
    n, n2 = 2 * l, HYENA_N2
    n1 = n // n2
    nk = n1 // 2 + 1
    c, s = _cos_sin(_ar(nk)[:, None] * _ar(n1)[None, :], n1)
    f_full = _cplx_rows(c, -s).astype(BF16)
    wgt = jnp.where((_ar(nk) == 0) | (_ar(nk) == n1 // 2), 1.0, 2.0)[:, None] / n
    g = jnp.concatenate([(c * wgt).T, (-s * wgt).T], axis=1)[:n1 // 2].astype(BF16)
    k1, k2, m2 = _ar(nk)[:, None, None], _ar(n2)[None, :, None], _ar(n2)[None, None, :]
    c, s = _cos_sin(m2 * k1 + n1 * m2 * k2, n)
    mf = _cplx_block(c, -s).astype(BF16)
    return dict(f_half=_lead_kron(f_full[:, :n1 // 2]), f_full=_lead_kron(f_full), g=_lead_kron(g),
                mf=mf, mi=jnp.swapaxes(mf, 1, 2))


def _fnet_tables(l):
    n2 = FNET_N2
    n1 = l // n2
    nblk = n1 // 8
    blk, j = _ar(nblk)[:, None, None, None], _ar(8)[None, :, None, None]
    k2, m2 = _ar(n2)[None, None, :, None], _ar(n2)[None, None, None, :]
    c, s = _cos_sin(n1 * m2 * k2 + (8 * blk + j) * k2, l)
    eye = jnp.eye(8, dtype=F32)
    emb = lambda t: jnp.einsum('bjkn,ji->bjkni', t, eye).reshape(nblk, 8 * n2, n2 * 8)
    first = jnp.concatenate([emb(c), emb(-s)], axis=1).astype(BF16)
    c, s = _cos_sin(_ar(n1)[:, None] * _ar(n1)[None, :], n1)
    second = _cplx_block(c, -s).astype(BF16)
    return dict(first=first, second=_lead_kron(second))


def _dense_tables(l):
    n = 2 * l
    c, s = _cos_sin(_ar(n)[:, None] * _ar(n)[None, :], n)
    ff = _cplx_rows(c, -s).astype(BF16)
    gd = (jnp.concatenate([c, -s], axis=1)[:l] / n).astype(BF16)
    cl, sl = _cos_sin(_ar(l)[:, None] * _ar(l)[None, :], l)
    return dict(fd=ff[:, :l], ff=ff, gd=gd, cl=cl.astype(BF16), sl=sl.astype(BF16))


def _rope_tables(l):
    rows = l // GRID_W
    row = jnp.broadcast_to(jnp.arange(rows)[:, None], (rows, GRID_W)).reshape(-1).astype(F32)
    col = jnp.broadcast_to(jnp.arange(GRID_W)[None, :], (rows, GRID_W)).reshape(-1).astype(F32)
    half = HEAD_DIM // 2
    inv = ROPE_THETA ** (-jnp.arange(0, half, 2, dtype=F32) / half)
    ar, ac = row[:, None] * inv, col[:, None] * inv
    cos = jnp.concatenate([jnp.cos(ar), jnp.cos(ar), jnp.cos(ac), jnp.cos(ac)], axis=1)
    sin = jnp.concatenate([-jnp.sin(ar), jnp.sin(ar), -jnp.sin(ac), jnp.sin(ac)], axis=1)
    return jnp.tile(cos, (1, 2)), jnp.tile(sin, (1, 2))


def kernel(x, c, ctx, c_ctx, w_ada, b_ada, w_in, sink_a, q_norm_g, k_norm_g, hy_conv_w, hy_conv_b,
           hy_f_w1, hy_f_b1, hy_f_freq, hy_f_w2, hy_f_b2, hy_f_w3, hy_skip, fnet_w, fnet_b,
           out_norm_g, w_out, b_out, ln1_g, ln1_b, ffn_w_up, ffn_b_up, ffn_conv_w, ffn_conv_b,
           ffn_w_down, ffn_b_down, ln2_g, ln2_b):
    bsz, l, d = x.shape
    lc = ctx.shape[1]
    depth = w_ada.shape[0]
    d_ff = ffn_w_down.shape[1]
    alpha = (2 * depth) ** 0.25
    n2h, n2f = HYENA_N2, FNET_N2
    n1h, n1f = 2 * l // n2h, l // n2f
    n1k = n1h // 2 + 1

    ht, ft, dt = _hyena_tables(l), _fnet_tables(l), _dense_tables(lc)
    cos_l, sin_l = _rope_tables(l)
    cos_c, sin_c = jnp.ones((lc, 128), F32), jnp.zeros((lc, 128), F32)
    jj = _ar(GROUP_W)
    same = (jj[:, None] // FNET_GROUP_DIM) == (jj[None, :] // FNET_GROUP_DIM)
    bd_mean = (same.astype(F32) / HEAD_DIM).astype(BF16)
    cg, sg = _cos_sin(jj[:, None] * jj[None, :], FNET_GROUP_DIM)
    cbd, sbd = jnp.where(same, cg, 0.0).astype(BF16), jnp.where(same, sg, 0.0).astype(BF16)

    n_rows = -(-(bsz + 1) // 8) * 8
    cc = jnp.zeros((n_rows, d), F32).at[:bsz].set(c).at[bsz].set(c_ctx)
    mod = _ada(cc, w_ada, b_ada)

    k_lat = _hy_filter(l, hy_f_w1, hy_f_b1, hy_f_freq, hy_f_w2, hy_f_b2, hy_f_w3)
    k_ctx = _hy_filter(lc, hy_f_w1, hy_f_b1, hy_f_freq, hy_f_w2, hy_f_b2, hy_f_w3)
    a_k = _lead_mm(ht['f_full'], k_lat.reshape(depth, n1h, n2h, GROUP_W))
    k_spec = _hy_spectrum(a_k.reshape(depth, 2, n1k, n2h, GROUP_W), ht['mf'])

    n_chunks = -(-d_ff // FFN_CHUNK)
    pad = n_chunks * FFN_CHUNK - d_ff

    def chunked(v):
        v = v.reshape(v.shape[:-1] + (2, d_ff))
        v = jnp.pad(v, [(0, 0)] * (v.ndim - 1) + [(0, pad)])
        v = v.reshape(v.shape[:-2] + (2, n_chunks, FFN_CHUNK))
        v = jnp.swapaxes(v, -3, -2).reshape(v.shape[:-3] + (n_chunks, 2 * FFN_CHUNK))
        return jnp.moveaxis(v, -2, 0)

    for layer in range(depth):
        m6 = mod[layer].reshape(n_rows, 6, d)
        lat = [m6[:bsz, i][:, None, :] for i in range(6)]
        con = [jnp.broadcast_to(m6[bsz, i][None, None, :], (bsz, 1, d)) for i in range(6)]
        w_in_l = w_in[layer].astype(BF16)
        qg = jnp.tile(q_norm_g[layer], N_HEADS).reshape(1, -1)
        kg = jnp.tile(k_norm_g[layer], N_KV).reshape(1, -1)
        w_out_l = w_out[layer].astype(BF16)
        gn, b_out_l = out_norm_g[layer].reshape(1, d), b_out[layer].reshape(1, d)
        l1g, l1b = ln1_g[layer].reshape(1, d), ln1_b[layer].reshape(1, d)
        l2g, l2b = ln2_g[layer].reshape(1, d), ln2_b[layer].reshape(1, d)
        fw, fb = fnet_w[layer].astype(BF16), fnet_b[layer].reshape(1, GROUP_W)
        skip = hy_skip[layer].reshape(1, GROUP_W)
        wup = chunked(ffn_w_up[layer]).astype(BF16)
        vec = chunked(jnp.concatenate([ffn_b_up[layer][None], ffn_conv_w[layer], ffn_conv_b[layer][None],
                                       jnp.zeros((3, 2 * d_ff), F32)], axis=0))
        wd = jnp.pad(ffn_w_down[layer], ((0, pad), (0, 0))).reshape(n_chunks, FFN_CHUNK, d).astype(BF16)
        bdn = ffn_b_down[layer].reshape(1, d)
        last = layer == depth - 1

        cqa, cka, cva, cqb, ckb, cvb, czh, czf = _proj(ctx, con[0], con[1], w_in_l, cos_c, sin_c,
                                                       bd_mean, qg, kg)
        qa, ka, va, qb, kb, vb, zh, zf = _proj(x, lat[0], lat[1], w_in_l, cos_l, sin_l, bd_mean, qg, kg)
        oa = _attention(qa, ka, va, sink_a[layer], banded=True, kc=cka, vc=cva)
        ob = _attention(qb, kb, vb, None, banded=False, kc=ckb, vc=cvb)

        vx, x0 = _hy_gate(zh, hy_conv_w[layer], hy_conv_b[layer])
        vx4, x04 = (a.reshape(bsz, n1h // 2, n2h, GROUP_W) for a in (vx, x0))
        a1 = _lead_mm(ht['f_half'], vx4)
        bo = _hy_mid(a1.reshape(bsz, 2, n1k, n2h, GROUP_W), ht['mf'], ht['mi'], k_spec, layer)
        oc = _hy_out(ht['g'], bo.reshape(bsz, 2 * n1k, n2h, GROUP_W), vx4, x04, skip).reshape(bsz, l, GROUP_W)

        f1 = _fn_first(zf.reshape(bsz, n2f, n1f, GROUP_W), ft['first'])
        f2 = _lead_mm(ft['second'], f1.reshape(bsz, 2 * n1f, n2f, GROUP_W))
        od = (f2.reshape(bsz, 2, l, GROUP_W), cbd, sbd, fw, fb, (l * FNET_GROUP_DIM) ** -0.5)

        x = _merge(oa, ob, oc, od, x, lat[2], gn, w_out_l, b_out_l, l1g, l1b, alpha)
        x = _ffn(x, lat[3], lat[4], lat[5], wup, vec, wd, bdn, l2g, l2b, alpha)

        if not last:
            oac = _attention(cqa, cka, cva, sink_a[layer], banded=False)
            obc = _attention(cqb, ckb, cvb, None, banded=False)
            cvx, cx0 = _hy_gate(czh, hy_conv_w[layer], hy_conv_b[layer])
            occ = _hy_small(cvx, cx0, k_ctx, layer, dt['fd'], dt['ff'], dt['gd'], skip)
            odc = _fn_small(czf, dt['cl'], dt['sl'], cbd, sbd, fw, fb, (lc * FNET_GROUP_DIM) ** -0.5)
            ctx = _merge(oac, obc, occ, odc, ctx, con[2], gn, w_out_l, b_out_l, l1g, l1b, alpha)
            ctx = _ffn(ctx, con[3], con[4], con[5], wup, vec, wd, bdn, l2g, l2b, alpha)
    return x
```

```python
import functools
import math

import jax
import jax.numpy as jnp
from jax import lax
from jax.experimental import pallas as pl
from jax.experimental.pallas import tpu as pltpu

F32 = jnp.float32
BF16 = jnp.bfloat16

HEAD_DIM = 64
GROUP_W = 256
N_HEADS = 4
N_KV = 2
WINDOW = 128
GRID_W = 64
ROPE_THETA = 10000.0
FNET_GROUP_DIM = 64
HYENA_BANDS = 8
HYENA_FAST_DECAY = 0.3
HYENA_SLOW_DECAY = 1.5
HYENA_TARGET = 1e-2
LN_EPS = 1e-6
NEG_INF = -1e30
HYENA_N2 = 256
FNET_N2 = 128
FFN_CHUNK = 256
HALO = 16
VMEM_LIMIT = 56 * 1024 * 1024


def _cp(sem, vmem=None):
    return pltpu.CompilerParams(dimension_semantics=sem, vmem_limit_bytes=vmem)


def _layer_norm(x):
    mu = jnp.mean(x, axis=-1, keepdims=True)
    xc = x - mu
    var = jnp.mean(xc * xc, axis=-1, keepdims=True)
    return xc * lax.rsqrt(var + LN_EPS)


def _dot(a, b):
    return jnp.dot(a, b, preferred_element_type=F32)


def _ada_kernel(c_ref, w_ref, b_ref, o_ref):
    c = c_ref[...]
    s = (c * jax.nn.sigmoid(c)).astype(BF16)
    o_ref[0] = _dot(s, w_ref[0].astype(BF16)) + b_ref[0]


def _ada(cc, w_ada, b_ada):
    depth, d, n = w_ada.shape
    r = cc.shape[0]
    tn = 1536
    return pl.pallas_call(
        _ada_kernel,
        grid=(depth, n // tn),
        in_specs=[pl.BlockSpec((r, d), lambda l, j: (0, 0)),
                  pl.BlockSpec((1, d, tn), lambda l, j: (l, 0, j)),
                  pl.BlockSpec((1, 1, tn), lambda l, j: (l, 0, j))],
        out_specs=pl.BlockSpec((1, r, tn), lambda l, j: (l, 0, j)),
        out_shape=jax.ShapeDtypeStruct((depth, r, n), F32),
        compiler_params=_cp(("parallel", "parallel"), VMEM_LIMIT),
        name="ada",
    )(cc, w_ada, b_ada.reshape(depth, 1, n))


def _swap_halves(z):
    w = z.shape[1]
    lane = lax.broadcasted_iota(jnp.int32, z.shape, 1)
    return jnp.where(lane % 32 < 16, pltpu.roll(z, w - 16, 1), pltpu.roll(z, 16, 1))


def _proj_kernel(x_ref, sh_ref, sc_ref, w_ref, cos_ref, sin_ref, bd_ref, qg_ref, kg_ref,
                 qa_ref, ka_ref, va_ref, qb_ref, kb_ref, vb_ref, zh_ref, zf_ref):
    scale = HEAD_DIM ** -0.5
    h = (_layer_norm(x_ref[0]) * (1.0 + sc_ref[0]) + sh_ref[0]).astype(BF16)
    cos1, sin1 = cos_ref[...], sin_ref[...]
    cos2 = jnp.concatenate([cos1, cos1], axis=1)
    sin2 = jnp.concatenate([sin1, sin1], axis=1)

    def proj(lo, width):
        return _dot(h, w_ref[:, lo:lo + width])

    def rope(z):
        if z.shape[1] == 128:
            return z * cos1 + _swap_halves(z) * sin1
        return z * cos2 + _swap_halves(z) * sin2

    def head_norm(z, g):
        w = z.shape[1]
        ms = _dot((z * z).astype(BF16), bd_ref[:w, :w])
        return z * lax.rsqrt(ms + LN_EPS) * g

    def put_heads(ref, z):
        for hh in range(z.shape[1] // HEAD_DIM):
            ref[0, hh] = z[:, hh * HEAD_DIM:(hh + 1) * HEAD_DIM].astype(BF16)

    def put_values(ref, z):
        lane = lax.broadcasted_iota(jnp.int32, (z.shape[0], HEAD_DIM), 1)
        ones_col = jnp.where(lane == 0, 1.0, 0.0)
        for hh in range(z.shape[1] // HEAD_DIM):
            ref[0, hh] = jnp.concatenate([z[:, hh * HEAD_DIM:(hh + 1) * HEAD_DIM], ones_col],
                                         axis=1).astype(BF16)

    put_heads(qa_ref, rope(proj(0, 256)) * scale)
    kv = proj(256, 256)
    put_heads(ka_ref, rope(kv[:, :128]))
    put_values(va_ref, kv[:, 128:])
    put_heads(qb_ref, rope(head_norm(proj(512, 256), qg_ref[...])) * scale)
    kv = proj(768, 256)
    put_heads(kb_ref, rope(head_norm(kv[:, :128], kg_ref[...])))
    put_values(vb_ref, kv[:, 128:])
    zh_ref[0] = proj(1024, 768).astype(BF16)
    zf_ref[0] = proj(1792, 256).astype(BF16)


def _proj(x, sh, sc, w_in, cos, sin, bd, qg, kg):
    b, l, d = x.shape
    tm = min(l, 1024)
    heads = lambda n, w=HEAD_DIM: jax.ShapeDtypeStruct((b, n, l, w), BF16)
    hspec = lambda n, w=HEAD_DIM: pl.BlockSpec((1, n, tm, w), lambda t, bb: (bb, 0, t, 0))
    tok = lambda w: pl.BlockSpec((1, tm, w), lambda t, bb: (bb, t, 0))
    vec = lambda w: pl.BlockSpec((1, 1, w), lambda t, bb: (bb, 0, 0))
    full = lambda a: pl.BlockSpec(a.shape, lambda t, bb: (0,) * a.ndim)
    tab = pl.BlockSpec((tm, 128), lambda t, bb: (t, 0))
    return pl.pallas_call(
        _proj_kernel,
        grid=(l // tm, b),
        in_specs=[tok(d), vec(d), vec(d), full(w_in), tab, tab, full(bd), full(qg), full(kg)],
        out_specs=[hspec(4), hspec(2), hspec(2, 128), hspec(4), hspec(2), hspec(2, 128), tok(768), tok(256)],
        out_shape=[heads(4), heads(2), heads(2, 128), heads(4), heads(2), heads(2, 128),
                   jax.ShapeDtypeStruct((b, l, 768), BF16), jax.ShapeDtypeStruct((b, l, 256), BF16)],
        compiler_params=_cp(("parallel", "parallel"), VMEM_LIMIT),
        name="proj",
    )(x, sh, sc, w_in, cos, sin, bd, qg, kg)


_NT = (((1,), (1,)), ((), ()))


def _row_max(*parts):
    tiles = [p[:, i:i + 128] for p in parts for i in range(0, p.shape[1], 128)]
    m = tiles[0]
    for t in tiles[1:]:
        m = jnp.maximum(m, t)
    return jnp.max(m, axis=-1, keepdims=True)


def _sink_rows(sink_ref, j, tq):
    row = lax.broadcasted_iota(jnp.int32, (2 * tq, 1), 0)
    return jnp.where(row < tq, sink_ref[2 * j], sink_ref[2 * j + 1])


def _attn_full_kernel(*refs, has_sink, has_ctx, tq, tk, n_t, n_chunks):
    if has_ctx:
        sink_ref, q_ref, k_ref, v_ref, kc_ref, vc_ref, o_ref, m_ref, acc_ref, s0_ref, s1_ref = refs
        l_ctx = kc_ref.shape[2]
    else:
        sink_ref, q_ref, k_ref, v_ref, o_ref, m_ref, acc_ref, s0_ref, s1_ref = refs
    bufs = (s0_ref, s1_ref)
    j = pl.program_id(1)
    lane = lax.broadcasted_iota(jnp.int32, acc_ref.shape[1:], 1)
    for tile in range(n_t):
        if has_sink:
            m_ref[tile] = _sink_rows(sink_ref, j, tq)
            acc_ref[tile] = jnp.where(lane == HEAD_DIM, 1.0, 0.0)
        else:
            m_ref[tile] = jnp.full(m_ref.shape[1:], NEG_INF, F32)
            acc_ref[tile] = jnp.zeros(acc_ref.shape[1:], F32)

    def chunk(ref, t):
        c = t // n_t
        off = c * tk if isinstance(c, int) else pl.multiple_of(c * tk, tk)
        return ref[0, 0, pl.ds(off, tk), :]

    def scores(t, tile, dst, ctx=False):
        q = q_ref[0, :, tile * tq:(tile + 1) * tq, :].reshape(2 * tq, HEAD_DIM)
        if ctx:
            dst[:, :l_ctx] = lax.dot_general(q, kc_ref[0, 0], _NT, preferred_element_type=F32)
        else:
            dst[...] = lax.dot_general(q, chunk(k_ref, t), _NT, preferred_element_type=F32)

    def update(t, tile, src, ctx=False):
        s = src[:, :l_ctx] if ctx else src[...]
        v = vc_ref[0, 0] if ctx else chunk(v_ref, t)
        m_prev = m_ref[tile]
        m_new = jnp.maximum(m_prev, _row_max(s))
        p = jnp.exp((s - m_new).astype(BF16))
        acc_ref[tile] = jnp.exp(m_prev - m_new) * acc_ref[tile] + _dot(p, v)
        m_ref[tile] = m_new

    total = n_t * n_chunks
    per_iter = max(n_t, 2)
    scores(0, 0, s0_ref)

    def body(i, carry):
        t = per_iter * i
        for r in range(per_iter):
            scores(t + r + 1, (r + 1) % n_t, bufs[(r + 1) % 2])
            update(t + r, r % n_t, bufs[r % 2])
        return carry

    n_iter = (total - 1) // per_iter
    lax.fori_loop(0, n_iter, body, 0, unroll=4 // per_iter)
    first = per_iter * n_iter
    items = [(t, t % n_t, False) for t in range(first, total)]
    if has_ctx:
        items += [(None, tile, True) for tile in range(n_t)]
    for idx, (t, tile, ctx) in enumerate(items):
        if idx + 1 < len(items):
            nt, ntile, nctx = items[idx + 1]
            scores(nt, ntile, bufs[(first + idx + 1) % 2], nctx)
        update(t, tile, bufs[(first + idx) % 2], ctx)
    for tile in range(n_t):
        acc = acc_ref[tile]
        o = acc[:, :HEAD_DIM] * (1.0 / acc[:, HEAD_DIM:HEAD_DIM + 1])
        o_ref[0, tile * tq:(tile + 1) * tq, :] = jnp.concatenate([o[:tq], o[tq:]], axis=1).astype(BF16)


def _band_bias(tq):
    span = tq + 2 * WINDOW
    r = (_ar(2 * tq) % tq)[None, :, None]
    col = _ar(span)[None, None, :]
    shift = (_ar(3) * WINDOW)[:, None, None]
    return jnp.where(jnp.abs(col - shift - r) <= WINDOW, 0.0, NEG_INF).astype(F32)


def _attn_band_kernel(sink_ref, q_ref, k_ref, v_ref, kc_ref, vc_ref, bias_ref, o_ref, *,
                      tq, n_t, n_tiles, l_lat):
    j, g = pl.program_id(1), pl.program_id(2)
    span = tq + 2 * WINDOW
    sink = _sink_rows(sink_ref, j, tq)
    kc, vc = kc_ref[0, 0], vc_ref[0, 0]

    def scores(tile):
        qi = g * n_t + tile
        start = pl.multiple_of(jnp.clip(qi * tq - WINDOW, 0, l_lat - span), WINDOW)
        kind = jnp.where(qi == 0, 0, jnp.where(qi == n_tiles - 1, 2, 1))
        q = q_ref[0, :, tile * tq:(tile + 1) * tq, :].reshape(2 * tq, HEAD_DIM)
        s_lat = (lax.dot_general(q, k_ref[0, 0, pl.ds(start, span), :], _NT, preferred_element_type=F32)
                 + bias_ref[kind])
        return start, s_lat, lax.dot_general(q, kc, _NT, preferred_element_type=F32)

    nxt = scores(0)
    for tile in range(n_t):
        start, s_lat, s_ctx = nxt
        if tile + 1 < n_t:
            nxt = scores(tile + 1)
        m = jnp.maximum(sink, _row_max(s_lat, s_ctx))
        acc = (_dot(jnp.exp((s_lat - m).astype(BF16)), v_ref[0, 0, pl.ds(start, span), :])
               + _dot(jnp.exp((s_ctx - m).astype(BF16)), vc))
        o = acc[:, :HEAD_DIM] * (1.0 / (acc[:, HEAD_DIM:HEAD_DIM + 1] + jnp.exp(sink - m)))
        o_ref[0, tile * tq:(tile + 1) * tq, :] = jnp.concatenate([o[:tq], o[tq:]], axis=1).astype(BF16)


def _attention(q, k, v, sink, *, banded, kc=None, vc=None):
    b, _, lq, _ = q.shape
    lk = k.shape[2]
    tq = 256
    has_sink = sink is not None
    if not has_sink:
        sink = jnp.zeros((N_HEADS,), F32)
    smem = pl.BlockSpec(memory_space=pltpu.SMEM)
    whole = lambda a: pl.BlockSpec((1, 1) + a.shape[2:], lambda bb, j, g: (bb, j, 0, 0))
    out_shape = jax.ShapeDtypeStruct((b, lq, GROUP_W), BF16)
    if banded:
        assert lq == lk and lk >= tq + 2 * WINDOW and tq % WINDOW == 0
        n_tiles = lq // tq
        n_t = next(t for t in (4, 2, 1) if n_tiles % t == 0)
        bias = _band_bias(tq)
        return pl.pallas_call(
            functools.partial(_attn_band_kernel, tq=tq, n_t=n_t, n_tiles=n_tiles, l_lat=lk),
            grid=(b, N_KV, n_tiles // n_t),
            in_specs=[smem,
                      pl.BlockSpec((1, 2, n_t * tq, HEAD_DIM), lambda bb, j, g: (bb, j, g, 0)),
                      whole(k), whole(v), whole(kc), whole(vc),
                      pl.BlockSpec(bias.shape, lambda bb, j, g: (0, 0, 0))],
            out_specs=pl.BlockSpec((1, n_t * tq, 128), lambda bb, j, g: (bb, g, j)),
            out_shape=out_shape,
            compiler_params=_cp(("parallel", "parallel", "parallel"), VMEM_LIMIT),
            name="attn_band",
        )(sink, q, k, v, kc, vc, bias)
    has_ctx = kc is not None
    tk = next(t for t in (4096, 2048, 1024, 768, 512, 256) if lk % t == 0)
    assert not has_ctx or kc.shape[2] <= tk
    n_t = 2 if lq % (2 * tq) == 0 else 1
    kern = functools.partial(_attn_full_kernel, has_sink=has_sink, has_ctx=has_ctx, tq=tq, tk=tk, n_t=n_t,
                             n_chunks=lk // tk)
    extra = (kc, vc) if has_ctx else ()
    return pl.pallas_call(
        kern,
        grid=(b, N_KV, lq // (n_t * tq)),
        in_specs=[smem,
                  pl.BlockSpec((1, 2, n_t * tq, HEAD_DIM), lambda bb, j, g: (bb, j, g, 0)),
                  whole(k), whole(v), *[whole(a) for a in extra]],
        out_specs=pl.BlockSpec((1, n_t * tq, 128), lambda bb, j, g: (bb, g, j)),
        out_shape=out_shape,
        scratch_shapes=[pltpu.VMEM((n_t, 2 * tq, 1), F32), pltpu.VMEM((n_t, 2 * tq, 2 * HEAD_DIM), F32),
                        pltpu.VMEM((2 * tq, tk), F32), pltpu.VMEM((2 * tq, tk), F32)],
        compiler_params=_cp(("parallel", "parallel", "parallel"), VMEM_LIMIT),
        name="attn_full",
    )(sink, q, k, v, *extra)


def _shift_rows(z, prev_row, next_row):
    n = z.shape[0]
    row = lax.broadcasted_iota(jnp.int32, (n, 1), 0)
    dn = jnp.where(row == 0, prev_row, pltpu.roll(z, 1, 0))
    up = jnp.where(row == n - 1, next_row, pltpu.roll(z, n - 1, 0))
    return dn, up


def _hy_gate_kernel(z_ref, zp_ref, zn_ref, w_ref, b_ref, vx_ref, x0_ref, *, n_tiles):
    i = pl.program_id(1)
    z = z_ref[0].astype(F32)
    prev_row = zp_ref[0].astype(F32)[HALO - 1:HALO] * (i > 0).astype(F32)
    next_row = zn_ref[0].astype(F32)[0:1] * (i < n_tiles - 1).astype(F32)
    dn, up = _shift_rows(z, prev_row, next_row)
    w = w_ref[...]
    u = dn * w[0:1] + z * w[1:2] + up * w[2:3] + b_ref[...]
    x0_ref[0] = u[:, :GROUP_W].astype(BF16)
    vx_ref[0] = (u[:, 2 * GROUP_W:] * u[:, GROUP_W:2 * GROUP_W]).astype(BF16)


def _halo_specs(tl, l, w):
    nb = l // HALO
    per = tl // HALO
    return (pl.BlockSpec((1, tl, w), lambda bb, i: (bb, i, 0)),
            pl.BlockSpec((1, HALO, w), lambda bb, i: (bb, jnp.maximum(i * per - 1, 0), 0)),
            pl.BlockSpec((1, HALO, w), lambda bb, i: (bb, jnp.minimum((i + 1) * per, nb - 1), 0)))


def _hy_gate(zh, conv_w, conv_b):
    b, l, w = zh.shape
    tl = min(l, 2048)
    out = jax.ShapeDtypeStruct((b, l, GROUP_W), BF16)
    ospec = pl.BlockSpec((1, tl, GROUP_W), lambda bb, i: (bb, i, 0))
    return pl.pallas_call(
        functools.partial(_hy_gate_kernel, n_tiles=l // tl),
        grid=(b, l // tl),
        in_specs=[*_halo_specs(tl, l, w),
                  pl.BlockSpec((3, w), lambda bb, i: (0, 0)),
                  pl.BlockSpec((1, w), lambda bb, i: (0, 0))],
        out_specs=[ospec, ospec],
        out_shape=[out, out],
        compiler_params=_cp(("parallel", "parallel")),
        name="hy_gate",
    )(zh, zh, zh, conv_w, conv_b.reshape(1, w))


def _hy_filter_kernel(fc_ref, w1_ref, b1_ref, fr_ref, w2_ref, b2_ref, w3_ref, dl_ref, k_ref, *, l, tr):
    base = pl.program_id(1) * tr

    def position(m):
        return jnp.where(m < l, m, 2 * l - m).astype(F32)

    m_row = base + lax.broadcasted_iota(jnp.int32, (1, tr), 1)
    t_row = position(m_row)
    feat_id = lax.broadcasted_iota(jnp.int32, (fc_ref.shape[0], 1), 0)
    phase = fc_ref[...] * t_row
    feat = jnp.where(feat_id == 0, t_row / max(l - 1, 1),
                     jnp.where(feat_id <= HYENA_BANDS, jnp.cos(phase),
                               jnp.where(feat_id <= 2 * HYENA_BANDS, -jnp.sin(phase), 0.0)))
    hp = functools.partial(jnp.dot, precision=lax.Precision.HIGHEST, preferred_element_type=F32)
    fr = fr_ref[0]
    h = jnp.sin(fr * (hp(w1_ref[0], feat) + b1_ref[0]))
    h = jnp.sin(fr * (hp(w2_ref[0], h) + b2_ref[0]))
    h = hp(h.T, w3_ref[0])
    m = base + lax.broadcasted_iota(jnp.int32, (tr, 1), 0)
    dec = jnp.exp(-(position(m) / max(l - 1, 1)) * dl_ref[...])
    hf, hb = h[:, :GROUP_W] * dec, h[:, GROUP_W:] * dec
    k = jnp.where(m < l, hf, hb)
    k = jnp.where(m == 0, hf + hb, k)
    k_ref[0] = jnp.where(m == l, 0.0, k).astype(BF16)


def _hy_filter(l, w1, b1, fr, w2, b2, w3):
    depth, emb, hid = w1.shape
    tr = min(2 * l, 2048)
    n_feat = -(-emb // 8) * 8
    bands = jnp.linspace(1e-4, HYENA_BANDS - 1, HYENA_BANDS, dtype=F32) * (2.0 * math.pi / l)
    fc = jnp.zeros((n_feat, 1), F32).at[1:1 + HYENA_BANDS, 0].set(bands).at[1 + HYENA_BANDS:emb, 0].set(bands)
    w1t = jnp.zeros((depth, hid, n_feat), F32).at[:, :, :emb].set(jnp.swapaxes(w1, 1, 2))
    w2t = jnp.swapaxes(w2, 1, 2)
    min_decay = math.log(HYENA_TARGET) / HYENA_SLOW_DECAY
    max_decay = math.log(HYENA_TARGET) / HYENA_FAST_DECAY
    deltas = jnp.abs(jnp.linspace(min_decay, max_decay, GROUP_W, dtype=F32)).reshape(1, GROUP_W)
    lay = lambda a: pl.BlockSpec((1,) + a.shape[1:], lambda d, i: (d,) + (0,) * (a.ndim - 1))
    fix = lambda a: pl.BlockSpec(a.shape, lambda d, i: (0,) * a.ndim)
    b1, fr, b2 = (a.reshape(depth, hid, 1) for a in (b1, fr, b2))
    return pl.pallas_call(
        functools.partial(_hy_filter_kernel, l=l, tr=tr),
        grid=(depth, 2 * l // tr),
        in_specs=[fix(fc), lay(w1t), lay(b1), lay(fr), lay(w2t), lay(b2), lay(w3), fix(deltas)],
        out_specs=pl.BlockSpec((1, tr, GROUP_W), lambda d, i: (d, i, 0)),
        out_shape=jax.ShapeDtypeStruct((depth, 2 * l, GROUP_W), BF16),
        compiler_params=_cp(("parallel", "parallel")),
        name="hy_filter",
    )(fc, w1t, b1, fr, w2t, b2, w3, deltas)


ROW_TILE = 8


def _lead_kron(f):
    return jnp.kron(f.astype(F32), jnp.eye(ROW_TILE, dtype=F32)).astype(BF16)


def _lead_rows(n):
    return next(r for r in (4 * ROW_TILE, 2 * ROW_TILE, ROW_TILE) if n % r == 0)


def _lead_mm_kernel(f_ref, x_ref, o_ref):
    k, rows, c = x_ref.shape[1:]
    for j in range(rows // ROW_TILE):
        rs = slice(j * ROW_TILE, (j + 1) * ROW_TILE)
        y = _dot(f_ref[...], x_ref[0, :, rs, :].reshape(k * ROW_TILE, c))
        o_ref[0, :, rs, :] = y.reshape(-1, ROW_TILE, c).astype(o_ref.dtype)


def _lead_mm(fk, x):
    b, k, n, c = x.shape
    m = fk.shape[0] // ROW_TILE
    rows = _lead_rows(n)
    return pl.pallas_call(
        _lead_mm_kernel,
        grid=(b, n // rows),
        in_specs=[pl.BlockSpec(fk.shape, lambda bb, i: (0, 0)),
                  pl.BlockSpec((1, k, rows, c), lambda bb, i: (bb, 0, i, 0))],
        out_specs=pl.BlockSpec((1, m, rows, c), lambda bb, i: (bb, 0, i, 0)),
        out_shape=jax.ShapeDtypeStruct((b, m, n, c), BF16),
        compiler_params=_cp(("parallel", "parallel"), VMEM_LIMIT),
        name="lead_mm",
    )(fk, x)


def _hy_spec_kernel(a_ref, mf_ref, k_ref):
    for d in range(a_ref.shape[0]):
        x = a_ref[d, :, 0].reshape(2 * a_ref.shape[3], GROUP_W)
        k_ref[d, 0] = _dot(mf_ref[0], x)


def _hy_mid_kernel(a_ref, mf_ref, mi_ref, k_ref, o_ref):
    n2 = a_ref.shape[3]
    k = k_ref[0, 0]
    kr, ki = k[:n2], k[n2:]
    for bb in range(a_ref.shape[0]):
        x = a_ref[bb, :, 0].reshape(2 * n2, GROUP_W)
        y = _dot(mf_ref[0], x)
        yr, yi = y[:n2], y[n2:]
        z = jnp.concatenate([yr * kr - yi * ki, yr * ki + yi * kr], axis=0).astype(BF16)
        o_ref[bb, :, 0] = _dot(mi_ref[0], z).astype(BF16).reshape(2, n2, GROUP_W)


def _hy_spectrum(a, mf):
    depth, _, n1, n2, w = a.shape
    return pl.pallas_call(
        _hy_spec_kernel,
        grid=(n1,),
        in_specs=[pl.BlockSpec((depth, 2, 1, n2, w), lambda k1: (0, 0, k1, 0, 0)),
                  pl.BlockSpec((1, 2 * n2, 2 * n2), lambda k1: (k1, 0, 0))],
        out_specs=pl.BlockSpec((depth, 1, 2 * n2, w), lambda k1: (0, k1, 0, 0)),
        out_shape=jax.ShapeDtypeStruct((depth, n1, 2 * n2, w), F32),
        compiler_params=_cp(("parallel",)),
        name="hy_spectrum",
    )(a, mf)


def _hy_mid(a, mf, mi, kspec, layer):
    b, _, n1, n2, w = a.shape
    blk = pl.BlockSpec((b, 2, 1, n2, w), lambda k1: (0, 0, k1, 0, 0))
    mat = pl.BlockSpec((1, 2 * n2, 2 * n2), lambda k1: (k1, 0, 0))
    return pl.pallas_call(
        _hy_mid_kernel,
        grid=(n1,),
        in_specs=[blk, mat, mat,
                  pl.BlockSpec((1, 1, 2 * n2, w), lambda k1: (layer, k1, 0, 0))],
        out_specs=blk,
        out_shape=jax.ShapeDtypeStruct(a.shape, BF16),
        compiler_params=_cp(("parallel",)),
        name="hy_mid",
    )(a, mf, mi, kspec)


def _hy_out_kernel(g_ref, b_ref, vx_ref, x0_ref, skip_ref, o_ref):
    k, rows, c = b_ref.shape[1:]
    for j in range(rows // ROW_TILE):
        rs = slice(j * ROW_TILE, (j + 1) * ROW_TILE)
        y = _dot(g_ref[...], b_ref[0, :, rs, :].reshape(k * ROW_TILE, c)).reshape(-1, ROW_TILE, c)
        vx = vx_ref[0, :, rs, :].astype(F32)
        o_ref[0, :, rs, :] = ((y + skip_ref[...] * vx) * x0_ref[0, :, rs, :].astype(F32)).astype(BF16)


def _hy_out(gk, bo, vx, x0, skip):
    b, k, n, c = bo.shape
    m = gk.shape[0] // ROW_TILE
    rows = _lead_rows(n)
    sig = pl.BlockSpec((1, m, rows, c), lambda bb, i: (bb, 0, i, 0))
    return pl.pallas_call(
        _hy_out_kernel,
        grid=(b, n // rows),
        in_specs=[pl.BlockSpec(gk.shape, lambda bb, i: (0, 0)),
                  pl.BlockSpec((1, k, rows, c), lambda bb, i: (bb, 0, i, 0)),
                  sig, sig, pl.BlockSpec((1, c), lambda bb, i: (0, 0))],
        out_specs=sig,
        out_shape=jax.ShapeDtypeStruct((b, m, n, c), BF16),
        compiler_params=_cp(("parallel", "parallel"), VMEM_LIMIT),
        name="hy_out",
    )(gk, bo, vx, x0, skip)


def _hy_small_kernel(vx_ref, x0_ref, k_ref, fd_ref, ff_ref, gd_ref, skip_ref, o_ref):
    nf = fd_ref.shape[0] // 2
    vx = vx_ref[0]
    u = _dot(fd_ref[...], vx)
    k = _dot(ff_ref[...], k_ref[0])
    ur, ui, kr, ki = u[:nf], u[nf:], k[:nf], k[nf:]
    z = jnp.concatenate([ur * kr - ui * ki, ur * ki + ui * kr], axis=0).astype(BF16)
    y = _dot(gd_ref[...], z)
    o_ref[0] = ((y + skip_ref[...] * vx.astype(F32)) * x0_ref[0].astype(F32)).astype(BF16)


def _hy_small(vx, x0, kfilt, layer, fd, ff, gd, skip):
    b, l, w = vx.shape
    sig = pl.BlockSpec((1, l, w), lambda bb: (bb, 0, 0))
    full = lambda a: pl.BlockSpec(a.shape, lambda bb: (0,) * a.ndim)
    return pl.pallas_call(
        _hy_small_kernel,
        grid=(b,),
        in_specs=[sig, sig, pl.BlockSpec((1, 2 * l, w), lambda bb: (layer, 0, 0)),
                  full(fd), full(ff), full(gd), full(skip)],
        out_specs=sig,
        out_shape=jax.ShapeDtypeStruct((b, l, w), BF16),
        compiler_params=_cp(("parallel",)),
        name="hy_small",
    )(vx, x0, kfilt, fd, ff, gd, skip)


def _fn_first_kernel(z_ref, m_ref, o_ref):
    n2 = z_ref.shape[1]
    x = z_ref[0].reshape(n2 * 8, GROUP_W)
    o_ref[0] = _dot(m_ref[0], x).astype(BF16).reshape(2, 8, n2, GROUP_W)


def _fn_first(zf, mats):
    b, n2, n1, w = zf.shape
    return pl.pallas_call(
        _fn_first_kernel,
        grid=(n1 // 8, b),
        in_specs=[pl.BlockSpec((1, n2, 8, w), lambda i, bb: (bb, 0, i, 0)),
                  pl.BlockSpec((1, 16 * n2, 8 * n2), lambda i, bb: (i, 0, 0))],
        out_specs=pl.BlockSpec((1, 2, 8, n2, w), lambda i, bb: (bb, 0, i, 0, 0)),
        out_shape=jax.ShapeDtypeStruct((b, 2, n1, n2, w), BF16),
        compiler_params=_cp(("parallel", "parallel"), VMEM_LIMIT),
        name="fn_first",
    )(zf, mats)


def _fn_small_kernel(u_ref, cl_ref, sl_ref, c_ref, s_ref, w_ref, b_ref, o_ref, *, scale):
    u = u_ref[0]
    uc = _dot(u, c_ref[...]).astype(BF16)
    us = _dot(u, s_ref[...]).astype(BF16)
    f = (_dot(cl_ref[...], uc) - _dot(sl_ref[...], us)) * scale
    o_ref[0] = (_dot(f.astype(BF16), w_ref[...]) + b_ref[...]).astype(BF16)


def _fn_small(zf, cl, sl, cbd, sbd, w, bias, scale):
    b, l, c = zf.shape
    sig = pl.BlockSpec((1, l, c), lambda bb: (bb, 0, 0))
    full = lambda z: pl.BlockSpec(z.shape, lambda bb: (0,) * z.ndim)
    return pl.pallas_call(
        functools.partial(_fn_small_kernel, scale=scale),
        grid=(b,),
        in_specs=[sig, full(cl), full(sl), full(cbd), full(sbd), full(w), full(bias)],
        out_specs=sig,
        out_shape=jax.ShapeDtypeStruct((b, l, c), BF16),
        compiler_params=_cp(("parallel",)),
        name="fn_small",
    )(zf, cl, sl, cbd, sbd, w, bias)


def _merge_kernel(*refs, alpha, fnet_scale):
    if fnet_scale is None:
        oa_ref, ob_ref, oc_ref, od_ref = refs[:4]
        od = od_ref[0].astype(F32)
        rest = refs[4:]
    else:
        oa_ref, ob_ref, oc_ref, ar_ref, ai_ref, c_ref, s_ref, fw_ref, fb_ref = refs[:9]
        t = (_dot(ar_ref[0, 0], c_ref[...]) + _dot(ai_ref[0, 0], s_ref[...])) * fnet_scale
        od = _dot(t.astype(BF16), fw_ref[...]) + fb_ref[...]
        rest = refs[9:]
    x_ref, g1_ref, gn_ref, w_ref, b_ref, lng_ref, lnb_ref, o_ref = rest
    parts = []
    for idx, v in enumerate((oa_ref[0].astype(F32), ob_ref[0].astype(F32), oc_ref[0].astype(F32), od)):
        ms = jnp.mean(v * v, axis=-1, keepdims=True)
        gain = gn_ref[:, idx * GROUP_W:(idx + 1) * GROUP_W]
        parts.append((v * lax.rsqrt(ms + LN_EPS) * gain).astype(BF16))
    y = _dot(jnp.concatenate(parts, axis=1), w_ref[...]) + b_ref[...]
    r = alpha * x_ref[0] + g1_ref[0] * y
    o_ref[0] = _layer_norm(r) * lng_ref[...] + lnb_ref[...]


def _merge(oa, ob, oc, od, x, g1, gn, w_out, b_out, ln_g, ln_b, alpha):
    b, l, d = x.shape
    tm = min(l, 1024)
    grp = pl.BlockSpec((1, tm, GROUP_W), lambda bb, i: (bb, i, 0))
    tok = pl.BlockSpec((1, tm, d), lambda bb, i: (bb, i, 0))
    full = lambda a: pl.BlockSpec(a.shape, lambda bb, i: (0,) * a.ndim)
    if isinstance(od, tuple):
        a, cbd, sbd, fw, fb, fnet_scale = od
        part = lambda p: pl.BlockSpec((1, 1, tm, GROUP_W), lambda bb, i: (bb, p, i, 0))
        od_args = (a, a, cbd, sbd, fw, fb)
        od_specs = [part(0), part(1), full(cbd), full(sbd), full(fw), full(fb)]
    else:
        fnet_scale, od_args, od_specs = None, (od,), [grp]
    return pl.pallas_call(
        functools.partial(_merge_kernel, alpha=alpha, fnet_scale=fnet_scale),
        grid=(b, l // tm),
        in_specs=[grp, grp, grp, *od_specs, tok, pl.BlockSpec((1, 1, d), lambda bb, i: (bb, 0, 0)),
                  full(gn), full(w_out), full(b_out), full(ln_g), full(ln_b)],
        out_specs=tok,
        out_shape=jax.ShapeDtypeStruct((b, l, d), F32),
        compiler_params=_cp(("parallel", "parallel"), VMEM_LIMIT),
        name="merge",
    )(oa, ob, oc, *od_args, x, g1, gn, w_out, b_out, ln_g, ln_b)


def _ffn_kernel(x_ref, xp_ref, xn_ref, sh_ref, sc_ref, g_ref, wup_ref, vec_ref, wd_ref, bd_ref,
                lng_ref, lnb_ref, o_ref, h_ref, acc_ref, u0_ref, u1_ref, *, alpha, tm, n_chunks, n_tiles):
    i = pl.program_id(1)
    sh, sc = sh_ref[0], sc_ref[0]
    cw = FFN_CHUNK

    def hmod(v):
        return (_layer_norm(v) * (1.0 + sc) + sh).astype(BF16)

    h_ref[0:HALO] = hmod(xp_ref[0])
    h_ref[HALO:HALO + tm] = hmod(x_ref[0])
    h_ref[HALO + tm:] = hmod(xn_ref[0])
    acc_ref[...] = jnp.zeros_like(acc_ref)
    row8 = lax.broadcasted_iota(jnp.int32, (8, 1), 0)
    pad_top = (row8 == 7) & (i == 0)
    pad_bot = (row8 == 0) & (i == n_tiles - 1)

    def up(c, dst):
        dst[...] = _dot(h_ref[...], wup_ref[c])

    def down(c, src):
        vec = vec_ref[c]
        bias, w0, w1, w2 = vec[0:1], vec[1:2], vec[2:3], vec[3:4]
        src[HALO - 8:HALO] = jnp.where(pad_top, -bias, src[HALO - 8:HALO])
        src[HALO + tm:HALO + tm + 8] = jnp.where(pad_bot, -bias, src[HALO + tm:HALO + tm + 8])
        conv = (src[HALO - 1:HALO - 1 + tm] * w0 + src[HALO:HALO + tm] * w1 + src[HALO + 1:HALO + 1 + tm] * w2
                + (vec[4:5] + bias * (w0 + w1 + w2)))
        a, g = conv[:, :cw], conv[:, cw:]
        act = (a * jax.nn.sigmoid(a) * g).astype(BF16)
        acc_ref[...] += _dot(act, wd_ref[c])

    up(0, u0_ref)

    def body(p, carry):
        c = 2 * p
        up(c + 1, u1_ref)
        down(c, u0_ref)
        up(c + 2, u0_ref)
        down(c + 1, u1_ref)
        return carry

    n_pairs = (n_chunks - 1) // 2
    lax.fori_loop(0, n_pairs, body, 0)
    c = 2 * n_pairs
    if c + 1 < n_chunks:
        up(c + 1, u1_ref)
    down(c, u0_ref)
    if c + 1 < n_chunks:
        down(c + 1, u1_ref)
    r = alpha * x_ref[0] + g_ref[0] * (acc_ref[...] + bd_ref[...])
    o_ref[0] = _layer_norm(r) * lng_ref[...] + lnb_ref[...]


def _ffn(x, sh, sc, g2, wup, vec, wd, b_down, ln_g, ln_b, alpha):
    b, l, d = x.shape
    tm = min(l, 1024)
    n_chunks = wd.shape[0]
    main, prev, nxt = _halo_specs(tm, l, d)
    mod = pl.BlockSpec((1, 1, d), lambda bb, i: (bb, 0, 0))
    full = lambda a: pl.BlockSpec(a.shape, lambda bb, i: (0,) * a.ndim, pipeline_mode=pl.Buffered(1))
    return pl.pallas_call(
        functools.partial(_ffn_kernel, alpha=alpha, tm=tm, n_chunks=n_chunks, n_tiles=l // tm),
        grid=(b, l // tm),
        in_specs=[main, prev, nxt, mod, mod, mod, full(wup), full(vec), full(wd),
                  full(b_down), full(ln_g), full(ln_b)],
        out_specs=main,
        out_shape=jax.ShapeDtypeStruct((b, l, d), F32),
        scratch_shapes=[pltpu.VMEM((tm + 2 * HALO, d), BF16), pltpu.VMEM((tm, d), F32),
                        pltpu.VMEM((tm + 2 * HALO, 2 * FFN_CHUNK), F32),
                        pltpu.VMEM((tm + 2 * HALO, 2 * FFN_CHUNK), F32)],
        compiler_params=_cp(("parallel", "parallel"), VMEM_LIMIT),
        name="ffn",
    )(x, x, x, sh, sc, g2, wup, vec, wd, b_down, ln_g, ln_b)


def _cos_sin(idx, n):
    ang = (idx % n).astype(F32) * (2.0 * math.pi / n)
    return jnp.cos(ang), jnp.sin(ang)


def _cplx_rows(re, im):
    return jnp.concatenate([re, im], axis=-2)


def _cplx_block(re, im):
    return jnp.concatenate([jnp.concatenate([re, -im], axis=-1), jnp.concatenate([im, re], axis=-1)], axis=-2)


def _ar(n):
    return jnp.arange(n, dtype=jnp.int32)


def _hyena_tables(l):
    n, n2 = 2 * l, HYENA_N2
    n1 = n // n2
    nk = n1 // 2 + 1
    c, s = _cos_sin(_ar(nk)[:, None] * _ar(n1)[None, :], n1)
    f_full = _cplx_rows(c, -s).astype(BF16)
    wgt = jnp.where((_ar(nk) == 0) | (_ar(nk) == n1 // 2), 1.0, 2.0)[:, None] / n
    g = jnp.concatenate([(c * wgt).T, (-s * wgt).T], axis=1)[:n1 // 2].astype(BF16)
    k1, k2, m2 = _ar(nk)[:, None, None], _ar(n2)[None, :, None], _ar(n2)[None, None, :]
    c, s = _cos_sin(m2 * k1 + n1 * m2 * k2, n)
    mf = _cplx_block(c, -s).astype(BF16)
    return dict(f_half=_lead_kron(f_full[:, :n1 // 2]), f_full=_lead_kron(f_full), g=_lead_kron(g),
                mf=mf, mi=jnp.swapaxes(mf, 1, 2))


def _fnet_tables(l):
    n2 = FNET_N2
    n1 = l // n2
    nblk = n1 // 8
    blk, j = _ar(nblk)[:, None, None, None], _ar(8)[None, :, None, None]
    k2, m2 = _ar(n2)[None, None, :, None], _ar(n2)[None, None, None, :]
    c, s = _cos_sin(n1 * m2 * k2 + (8 * blk + j) * k2, l)
    eye = jnp.eye(8, dtype=F32)
    emb = lambda t: jnp.einsum('bjkn,ji->bjkni', t, eye).reshape(nblk, 8 * n2, n2 * 8)
    first = jnp.concatenate([emb(c), emb(-s)], axis=1).astype(BF16)
    c, s = _cos_sin(_ar(n1)[:, None] * _ar(n1)[None, :], n1)
    second = _cplx_block(c, -s).astype(BF16)
    return dict(first=first, second=_lead_kron(second))


def _dense_tables(l):
    n = 2 * l
    c, s = _cos_sin(_ar(n)[:, None] * _ar(n)[None, :], n)
    ff = _cplx_rows(c, -s).astype(BF16)
    gd = (jnp.concatenate([c, -s], axis=1)[:l] / n).astype(BF16)
    cl, sl = _cos_sin(_ar(l)[:, None] * _ar(l)[None, :], l)
    return dict(fd=ff[:, :l], ff=ff, gd=gd, cl=cl.astype(BF16), sl=sl.astype(BF16))


def _rope_tables(l):
    rows = l // GRID_W
    row = jnp.broadcast_to(jnp.arange(rows)[:, None], (rows, GRID_W)).reshape(-1).astype(F32)
    col = jnp.broadcast_to(jnp.arange(GRID_W)[None, :], (rows, GRID_W)).reshape(-1).astype(F32)
    half = HEAD_DIM // 2
    inv = ROPE_THETA ** (-jnp.arange(0, half, 2, dtype=F32) / half)
    ar, ac = row[:, None] * inv, col[:, None] * inv
    cos = jnp.concatenate([jnp.cos(ar), jnp.cos(ar), jnp.cos(ac), jnp.cos(ac)], axis=1)
    sin = jnp.concatenate([-jnp.sin(ar), jnp.sin(ar), -jnp.sin(ac), jnp.sin(ac)], axis=1)
    return jnp.tile(cos, (1, 2)), jnp.tile(sin, (1, 2))


def kernel(x, c, ctx, c_ctx, w_ada, b_ada, w_in, sink_a, q_norm_g, k_norm_g, hy_conv_w, hy_conv_b,
           hy_f_w1, hy_f_b1, hy_f_freq, hy_f_w2, hy_f_b2, hy_f_w3, hy_skip, fnet_w, fnet_b,
           out_norm_g, w_out, b_out, ln1_g, ln1_b, ffn_w_up, ffn_b_up, ffn_conv_w, ffn_conv_b,
           ffn_w_down, ffn_b_down, ln2_g, ln2_b):
    bsz, l, d = x.shape
    lc = ctx.shape[1]
    depth = w_ada.shape[0]
    d_ff = ffn_w_down.shape[1]
    alpha = (2 * depth) ** 0.25
    n2h, n2f = HYENA_N2, FNET_N2
    n1h, n1f = 2 * l // n2h, l // n2f
    n1k = n1h // 2 + 1

    ht, ft, dt = _hyena_tables(l), _fnet_tables(l), _dense_tables(lc)
    cos_l, sin_l = _rope_tables(l)
    cos_c, sin_c = jnp.ones((lc, 128), F32), jnp.zeros((lc, 128), F32)
    jj = _ar(GROUP_W)
    same = (jj[:, None] // FNET_GROUP_DIM) == (jj[None, :] // FNET_GROUP_DIM)
    bd_mean = (same.astype(F32) / HEAD_DIM).astype(BF16)
    cg, sg = _cos_sin(jj[:, None] * jj[None, :], FNET_GROUP_DIM)
    cbd, sbd = jnp.where(same, cg, 0.0).astype(BF16), jnp.where(same, sg, 0.0).astype(BF16)

    n_rows = -(-(bsz + 1) // 8) * 8
    cc = jnp.zeros((n_rows, d), F32).at[:bsz].set(c).at[bsz].set(c_ctx)
    mod = _ada(cc, w_ada, b_ada)

    k_lat = _hy_filter(l, hy_f_w1, hy_f_b1, hy_f_freq, hy_f_w2, hy_f_b2, hy_f_w3)
    k_ctx = _hy_filter(lc, hy_f_w1, hy_f_b1, hy_f_freq, hy_f_w2, hy_f_b2, hy_f_w3)
    a_k = _lead_mm(ht['f_full'], k_lat.reshape(depth, n1h, n2h, GROUP_W))
    k_spec = _hy_spectrum(a_k.reshape(depth, 2, n1k, n2h, GROUP_W), ht['mf'])

    n_chunks = -(-d_ff // FFN_CHUNK)
    pad = n_chunks * FFN_CHUNK - d_ff

    def chunked(v):
        v = v.reshape(v.shape[:-1] + (2, d_ff))
        v = jnp.pad(v, [(0, 0)] * (v.ndim - 1) + [(0, pad)])
        v = v.reshape(v.shape[:-2] + (2, n_chunks, FFN_CHUNK))
        v = jnp.swapaxes(v, -3, -2).reshape(v.shape[:-3] + (n_chunks, 2 * FFN_CHUNK))
        return jnp.moveaxis(v, -2, 0)

    for layer in range(depth):
        m6 = mod[layer].reshape(n_rows, 6, d)
        lat = [m6[:bsz, i][:, None, :] for i in range(6)]
        con = [jnp.broadcast_to(m6[bsz, i][None, None, :], (bsz, 1, d)) for i in range(6)]
        w_in_l = w_in[layer].astype(BF16)
        qg = jnp.tile(q_norm_g[layer], N_HEADS).reshape(1, -1)
        kg = jnp.tile(k_norm_g[layer], N_KV).reshape(1, -1)
        w_out_l = w_out[layer].astype(BF16)
        gn, b_out_l = out_norm_g[layer].reshape(1, d), b_out[layer].reshape(1, d)
        l1g, l1b = ln1_g[layer].reshape(1, d), ln1_b[layer].reshape(1, d)
        l2g, l2b = ln2_g[layer].reshape(1, d), ln2_b[layer].reshape(1, d)
        fw, fb = fnet_w[layer].astype(BF16), fnet_b[layer].reshape(1, GROUP_W)
        skip = hy_skip[layer].reshape(1, GROUP_W)
        wup = chunked(ffn_w_up[layer]).astype(BF16)
        vec = chunked(jnp.concatenate([ffn_b_up[layer][None], ffn_conv_w[layer], ffn_conv_b[layer][None],
                                       jnp.zeros((3, 2 * d_ff), F32)], axis=0))
        wd = jnp.pad(ffn_w_down[layer], ((0, pad), (0, 0))).reshape(n_chunks, FFN_CHUNK, d).astype(BF16)
        bdn = ffn_b_down[layer].reshape(1, d)
        last = layer == depth - 1

        cqa, cka, cva, cqb, ckb, cvb, czh, czf = _proj(ctx, con[0], con[1], w_in_l, cos_c, sin_c,
                                                       bd_mean, qg, kg)
        qa, ka, va, qb, kb, vb, zh, zf = _proj(x, lat[0], lat[1], w_in_l, cos_l, sin_l, bd_mean, qg, kg)
        oa = _attention(qa, ka, va, sink_a[layer], banded=True, kc=cka, vc=cva)
        ob = _attention(qb, kb, vb, None, banded=False, kc=ckb, vc=cvb)

        vx, x0 = _hy_gate(zh, hy_conv_w[layer], hy_conv_b[layer])
        vx4, x04 = (a.reshape(bsz, n1h // 2, n2h, GROUP_W) for a in (vx, x0))
        a1 = _lead_mm(ht['f_half'], vx4)
        bo = _hy_mid(a1.reshape(bsz, 2, n1k, n2h, GROUP_W), ht['mf'], ht['mi'], k_spec, layer)
        oc = _hy_out(ht['g'], bo.reshape(bsz, 2 * n1k, n2h, GROUP_W), vx4, x04, skip).reshape(bsz, l, GROUP_W)

        f1 = _fn_first(zf.reshape(bsz, n2f, n1f, GROUP_W), ft['first'])
        f2 = _lead_mm(ft['second'], f1.reshape(bsz, 2 * n1f, n2f, GROUP_W))
        od = (f2.reshape(bsz, 2, l, GROUP_W), cbd, sbd, fw, fb, (l * FNET_GROUP_DIM) ** -0.5)

        x = _merge(oa, ob, oc, od, x, lat[2], gn, w_out_l, b_out_l, l1g, l1b, alpha)
        x = _ffn(x, lat[3], lat[4], lat[5], wup, vec, wd, bdn, l2g, l2b, alpha)

        if not last:
            oac = _attention(cqa, cka, cva, sink_a[layer], banded=False)
            obc = _attention(cqb, ckb, cvb, None, banded=False)
            cvx, cx0 = _hy_gate(czh, hy_conv_w[layer], hy_conv_b[layer])
            occ = _hy_small(cvx, cx0, k_ctx, layer, dt['fd'], dt['ff'], dt['gd'], skip)
            odc = _fn_small(czf, dt['cl'], dt['sl'], cbd, sbd, fw, fb, (lc * FNET_GROUP_DIM) ** -0.5)
            ctx = _merge(oac, obc, occ, odc, ctx, con[2], gn, w_out_l, b_out_l, l1g, l1b, alpha)
            ctx = _ffn(ctx, con[3], con[4], con[5], wup, vec, wd, bdn, l2g, l2b, alpha)
    return x
```

```python
import functools
import math

import jax
import jax.numpy as jnp
from jax import lax
from jax.experimental import pallas as pl
from jax.experimental.pallas import tpu as pltpu

F32 = jnp.float32
BF16 = jnp.bfloat16

HEAD_DIM = 64
GROUP_W = 256
N_HEADS = 4
N_KV = 2
WINDOW = 128
GRID_W = 64
ROPE_THETA = 10000.0
FNET_GROUP_DIM = 64
HYENA_BANDS = 8
HYENA_FAST_DECAY = 0.3
HYENA_SLOW_DECAY = 1.5
HYENA_TARGET = 1e-2
LN_EPS = 1e-6
NEG_INF = -1e30
HYENA_N2 = 256
FNET_N2 = 128
FFN_CHUNK = 256
HALO = 16
VMEM_LIMIT = 56 * 1024 * 1024


def _cp(sem, vmem=None):
    return pltpu.CompilerParams(dimension_semantics=sem, vmem_limit_bytes=vmem)


def _layer_norm(x):
    mu = jnp.mean(x, axis=-1, keepdims=True)
    xc = x - mu
    var = jnp.mean(xc * xc, axis=-1, keepdims=True)
    return xc * lax.rsqrt(var + LN_EPS)


def _dot(a, b):
    return jnp.dot(a, b, preferred_element_type=F32)


def _ada_kernel(c_ref, w_ref, b_ref, o_ref):
    c = c_ref[...]
    s = (c * jax.nn.sigmoid(c)).astype(BF16)
    o_ref[0] = _dot(s, w_ref[0].astype(BF16)) + b_ref[0]


def _ada(cc, w_ada, b_ada):
    depth, d, n = w_ada.shape
    r = cc.shape[0]
    tn = 1536
    return pl.pallas_call(
        _ada_kernel,
        grid=(depth, n // tn),
        in_specs=[pl.BlockSpec((r, d), lambda l, j: (0, 0)),
                  pl.BlockSpec((1, d, tn), lambda l, j: (l, 0, j)),
                  pl.BlockSpec((1, 1, tn), lambda l, j: (l, 0, j))],
        out_specs=pl.BlockSpec((1, r, tn), lambda l, j: (l, 0, j)),
        out_shape=jax.ShapeDtypeStruct((depth, r, n), F32),
        compiler_params=_cp(("parallel", "parallel"), VMEM_LIMIT),
        name="ada",
    )(cc, w_ada, b_ada.reshape(depth, 1, n))


def _swap_halves(z):
    w = z.shape[1]
    lane = lax.broadcasted_iota(jnp.int32, z.shape, 1)
    return jnp.where(lane % 32 < 16, pltpu.roll(z, w - 16, 1), pltpu.roll(z, 16, 1))


def _proj_kernel(x_ref, sh_ref, sc_ref, w_ref, cos_ref, sin_ref, bd_ref, qg_ref, kg_ref,
                 qa_ref, ka_ref, va_ref, qb_ref, kb_ref, vb_ref, zh_ref, zf_ref):
    scale = HEAD_DIM ** -0.5
    h = (_layer_norm(x_ref[0]) * (1.0 + sc_ref[0]) + sh_ref[0]).astype(BF16)
    cos1, sin1 = cos_ref[...], sin_ref[...]
    cos2 = jnp.concatenate([cos1, cos1], axis=1)
    sin2 = jnp.concatenate([sin1, sin1], axis=1)

    def proj(lo, width):
        return _dot(h, w_ref[:, lo:lo + width])

    def rope(z):
        if z.shape[1] == 128:
            return z * cos1 + _swap_halves(z) * sin1
        return z * cos2 + _swap_halves(z) * sin2

    def head_norm(z, g):
        w = z.shape[1]
        ms = _dot((z * z).astype(BF16), bd_ref[:w, :w])
        return z * lax.rsqrt(ms + LN_EPS) * g

    def put_heads(ref, z):
        for hh in range(z.shape[1] // HEAD_DIM):
            ref[0, hh] = z[:, hh * HEAD_DIM:(hh + 1) * HEAD_DIM].astype(BF16)

    def put_values(ref, z):
        lane = lax.broadcasted_iota(jnp.int32, (z.shape[0], HEAD_DIM), 1)
        ones_col = jnp.where(lane == 0, 1.0, 0.0)
        for hh in range(z.shape[1] // HEAD_DIM):
            ref[0, hh] = jnp.concatenate([z[:, hh * HEAD_DIM:(hh + 1) * HEAD_DIM], ones_col],
                                         axis=1).astype(BF16)

    put_heads(qa_ref, rope(proj(0, 256)) * scale)
    kv = proj(256, 256)
    put_heads(ka_ref, rope(kv[:, :128]))
    put_values(va_ref, kv[:, 128:])
    put_heads(qb_ref, rope(head_norm(proj(512, 256), qg_ref[...])) * scale)
    kv = proj(768, 256)
    put_heads(kb_ref, rope(head_norm(kv[:, :128], kg_ref[...])))
    put_values(vb_ref, kv[:, 128:])
    zh_ref[0] = proj(1024, 768).astype(BF16)
    zf_ref[0] = proj(1792, 256).astype(BF16)


def _proj(x, sh, sc, w_in, cos, sin, bd, qg, kg):
    b, l, d = x.shape
    tm = min(l, 1024)
    heads = lambda n, w=HEAD_DIM: jax.ShapeDtypeStruct((b, n, l, w), BF16)
    hspec = lambda n, w=HEAD_DIM: pl.BlockSpec((1, n, tm, w), lambda t, bb: (bb, 0, t, 0))
    tok = lambda w: pl.BlockSpec((1, tm, w), lambda t, bb: (bb, t, 0))
    vec = lambda w: pl.BlockSpec((1, 1, w), lambda t, bb: (bb, 0, 0))
    full = lambda a: pl.BlockSpec(a.shape, lambda t, bb: (0,) * a.ndim)
    tab = pl.BlockSpec((tm, 128), lambda t, bb: (t, 0))
    return pl.pallas_call(
        _proj_kernel,
        grid=(l // tm, b),
        in_specs=[tok(d), vec(d), vec(d), full(w_in), tab, tab, full(bd), full(qg), full(kg)],
        out_specs=[hspec(4), hspec(2), hspec(2, 128), hspec(4), hspec(2), hspec(2, 128), tok(768), tok(256)],
        out_shape=[heads(4), heads(2), heads(2, 128), heads(4), heads(2), heads(2, 128),
                   jax.ShapeDtypeStruct((b, l, 768), BF16), jax.ShapeDtypeStruct((b, l, 256), BF16)],
        compiler_params=_cp(("parallel", "parallel"), VMEM_LIMIT),
        name="proj",
    )(x, sh, sc, w_in, cos, sin, bd, qg, kg)


_NT = (((1,), (1,)), ((), ()))


def _row_max(*parts):
    tiles = [p[:, i:i + 128] for p in parts for i in range(0, p.shape[1], 128)]
    m = tiles[0]
    for t in tiles[1:]:
        m = jnp.maximum(m, t)
    return jnp.max(m, axis=-1, keepdims=True)


def _sink_rows(sink_ref, j, tq):
    row = lax.broadcasted_iota(jnp.int32, (2 * tq, 1), 0)
    return jnp.where(row < tq, sink_ref[2 * j], sink_ref[2 * j + 1])


def _attn_full_kernel(*refs, has_sink, has_ctx, tq, tk, n_t, n_chunks):
    if has_ctx:
        sink_ref, q_ref, k_ref, v_ref, kc_ref, vc_ref, o_ref, m_ref, acc_ref, s0_ref, s1_ref = refs
        l_ctx = kc_ref.shape[2]
    else:
        sink_ref, q_ref, k_ref, v_ref, o_ref, m_ref, acc_ref, s0_ref, s1_ref = refs
    bufs = (s0_ref, s1_ref)
    j = pl.program_id(1)
    lane = lax.broadcasted_iota(jnp.int32, acc_ref.shape[1:], 1)
    for tile in range(n_t):
        if has_sink:
            m_ref[tile] = _sink_rows(sink_ref, j, tq)
            acc_ref[tile] = jnp.where(lane == HEAD_DIM, 1.0, 0.0)
        else:
            m_ref[tile] = jnp.full(m_ref.shape[1:], NEG_INF, F32)
            acc_ref[tile] = jnp.zeros(acc_ref.shape[1:], F32)

    def chunk(ref, t):
        c = t // n_t
        off = c * tk if isinstance(c, int) else pl.multiple_of(c * tk, tk)
        return ref[0, 0, pl.ds(off, tk), :]

    def scores(t, tile, dst, ctx=False):
        q = q_ref[0, :, tile * tq:(tile + 1) * tq, :].reshape(2 * tq, HEAD_DIM)
        if ctx:
            dst[:, :l_ctx] = lax.dot_general(q, kc_ref[0, 0], _NT, preferred_element_type=F32)
        else:
            dst[...] = lax.dot_general(q, chunk(k_ref, t), _NT, preferred_element_type=F32)

    def update(t, tile, src, ctx=False):
        s = src[:, :l_ctx] if ctx else src[...]
        v = vc_ref[0, 0] if ctx else chunk(v_ref, t)
        m_prev = m_ref[tile]
        m_new = jnp.maximum(m_prev, _row_max(s))
        p = jnp.exp((s - m_new).astype(BF16))
        acc_ref[tile] = jnp.exp(m_prev - m_new) * acc_ref[tile] + _dot(p, v)
        m_ref[tile] = m_new

    total = n_t * n_chunks
    per_iter = max(n_t, 2)
    scores(0, 0, s0_ref)

    def body(i, carry):
        t = per_iter * i
        for r in range(per_iter):
            scores(t + r + 1, (r + 1) % n_t, bufs[(r + 1) % 2])
            update(t + r, r % n_t, bufs[r % 2])
        return carry

    n_iter = (total - 1) // per_iter
    lax.fori_loop(0, n_iter, body, 0, unroll=4 // per_iter)
    first = per_iter * n_iter
    items = [(t, t % n_t, False) for t in range(first, total)]
    if has_ctx:
        items += [(None, tile, True) for tile in range(n_t)]
    for idx, (t, tile, ctx) in enumerate(items):
        if idx + 1 < len(items):
            nt, ntile, nctx = items[idx + 1]
            scores(nt, ntile, bufs[(first + idx + 1) % 2], nctx)
        update(t, tile, bufs[(first + idx) % 2], ctx)
    for tile in range(n_t):
        acc = acc_ref[tile]
        o = acc[:, :HEAD_DIM] * (1.0 / acc[:, HEAD_DIM:HEAD_DIM + 1])
        o_ref[0, tile * tq:(tile + 1) * tq, :] = jnp.concatenate([o[:tq], o[tq:]], axis=1).astype(BF16)


def _band_bias(tq):
    span = tq + 2 * WINDOW
    r = (_ar(2 * tq) % tq)[None, :, None]
    col = _ar(span)[None, None, :]
    shift = (_ar(3) * WINDOW)[:, None, None]
    return jnp.where(jnp.abs(col - shift - r) <= WINDOW, 0.0, NEG_INF).astype(F32)


def _attn_band_kernel(sink_ref, q_ref, k_ref, v_ref, kc_ref, vc_ref, bias_ref, o_ref, *,
                      tq, n_t, n_tiles, l_lat):
    j, g = pl.program_id(1), pl.program_id(2)
    span = tq + 2 * WINDOW
    sink = _sink_rows(sink_ref, j, tq)
    kc, vc = kc_ref[0, 0], vc_ref[0, 0]

    def scores(tile):
        qi = g * n_t + tile
        start = pl.multiple_of(jnp.clip(qi * tq - WINDOW, 0, l_lat - span), WINDOW)
        kind = jnp.where(qi == 0, 0, jnp.where(qi == n_tiles - 1, 2, 1))
        q = q_ref[0, :, tile * tq:(tile + 1) * tq, :].reshape(2 * tq, HEAD_DIM)
        s_lat = (lax.dot_general(q, k_ref[0, 0, pl.ds(start, span), :], _NT, preferred_element_type=F32)
                 + bias_ref[kind])
        return start, s_lat, lax.dot_general(q, kc, _NT, preferred_element_type=F32)

    nxt = scores(0)
    for tile in range(n_t):
        start, s_lat, s_ctx = nxt
        if tile + 1 < n_t:
            nxt = scores(tile + 1)
        m = jnp.maximum(sink, _row_max(s_lat, s_ctx))
        acc = (_dot(jnp.exp((s_lat - m).astype(BF16)), v_ref[0, 0, pl.ds(start, span), :])
               + _dot(jnp.exp((s_ctx - m).astype(BF16)), vc))
        o = acc[:, :HEAD_DIM] * (1.0 / (acc[:, HEAD_DIM:HEAD_DIM + 1] + jnp.exp(sink - m)))
        o_ref[0, tile * tq:(tile + 1) * tq, :] = jnp.concatenate([o[:tq], o[tq:]], axis=1).astype(BF16)


def _attention(q, k, v, sink, *, banded, kc=None, vc=None):
    b, _, lq, _ = q.shape
    lk = k.shape[2]
    tq = 256
    has_sink = sink is not None
    if not has_sink:
        sink = jnp.zeros((N_HEADS,), F32)
    smem = pl.BlockSpec(memory_space=pltpu.SMEM)
    whole = lambda a: pl.BlockSpec((1, 1) + a.shape[2:], lambda bb, j, g: (bb, j, 0, 0))
    out_shape = jax.ShapeDtypeStruct((b, lq, GROUP_W), BF16)
    if banded:
        assert lq == lk and lk >= tq + 2 * WINDOW and tq % WINDOW == 0
        n_tiles = lq // tq
        n_t = next(t for t in (4, 2, 1) if n_tiles % t == 0)
        bias = _band_bias(tq)
        return pl.pallas_call(
            functools.partial(_attn_band_kernel, tq=tq, n_t=n_t, n_tiles=n_tiles, l_lat=lk),
            grid=(b, N_KV, n_tiles // n_t),
            in_specs=[smem,
                      pl.BlockSpec((1, 2, n_t * tq, HEAD_DIM), lambda bb, j, g: (bb, j, g, 0)),
                      whole(k), whole(v), whole(kc), whole(vc),
                      pl.BlockSpec(bias.shape, lambda bb, j, g: (0, 0, 0))],
            out_specs=pl.BlockSpec((1, n_t * tq, 128), lambda bb, j, g: (bb, g, j)),
            out_shape=out_shape,
            compiler_params=_cp(("parallel", "parallel", "parallel"), VMEM_LIMIT),
            name="attn_band",
        )(sink, q, k, v, kc, vc, bias)
    has_ctx = kc is not None
    tk = next(t for t in (4096, 2048, 1024, 768, 512, 256) if lk % t == 0)
    assert not has_ctx or kc.shape[2] <= tk
    n_t = 2 if lq % (2 * tq) == 0 else 1
    kern = functools.partial(_attn_full_kernel, has_sink=has_sink, has_ctx=has_ctx, tq=tq, tk=tk, n_t=n_t,
                             n_chunks=lk // tk)
    extra = (kc, vc) if has_ctx else ()
    return pl.pallas_call(
        kern,
        grid=(b, N_KV, lq // (n_t * tq)),
        in_specs=[smem,
                  pl.BlockSpec((1, 2, n_t * tq, HEAD_DIM), lambda bb, j, g: (bb, j, g, 0)),
                  whole(k), whole(v), *[whole(a) for a in extra]],
        out_specs=pl.BlockSpec((1, n_t * tq, 128), lambda bb, j, g: (bb, g, j)),
        out_shape=out_shape,
        scratch_shapes=[pltpu.VMEM((n_t, 2 * tq, 1), F32), pltpu.VMEM((n_t, 2 * tq, 2 * HEAD_DIM), F32),
                        pltpu.VMEM((2 * tq, tk), F32), pltpu.VMEM((2 * tq, tk), F32)],
        compiler_params=_cp(("parallel", "parallel", "parallel"), VMEM_LIMIT),
        name="attn_full",
    )(sink, q, k, v, *extra)


def _shift_rows(z, prev_row, next_row):
    n = z.shape[0]
    row = lax.broadcasted_iota(jnp.int32, (n, 1), 0)
    dn = jnp.where(row == 0, prev_row, pltpu.roll(z, 1, 0))
    up = jnp.where(row == n - 1, next_row, pltpu.roll(z, n - 1, 0))
    return dn, up


def _hy_gate_kernel(z_ref, zp_ref, zn_ref, w_ref, b_ref, vx_ref, x0_ref, *, n_tiles):
    i = pl.program_id(1)
    z = z_ref[0].astype(F32)
    prev_row = zp_ref[0].astype(F32)[HALO - 1:HALO] * (i > 0).astype(F32)
    next_row = zn_ref[0].astype(F32)[0:1] * (i < n_tiles - 1).astype(F32)
    dn, up = _shift_rows(z, prev_row, next_row)
    w = w_ref[...]
    u = dn * w[0:1] + z * w[1:2] + up * w[2:3] + b_ref[...]
    x0_ref[0] = u[:, :GROUP_W].astype(BF16)
    vx_ref[0] = (u[:, 2 * GROUP_W:] * u[:, GROUP_W:2 * GROUP_W]).astype(BF16)


def _halo_specs(tl, l, w):
    nb = l // HALO
    per = tl // HALO
    return (pl.BlockSpec((1, tl, w), lambda bb, i: (bb, i, 0)),
            pl.BlockSpec((1, HALO, w), lambda bb, i: (bb, jnp.maximum(i * per - 1, 0), 0)),
            pl.BlockSpec((1, HALO, w), lambda bb, i: (bb, jnp.minimum((i + 1) * per, nb - 1), 0)))


def _hy_gate(zh, conv_w, conv_b):
    b, l, w = zh.shape
    tl = min(l, 2048)
    out = jax.ShapeDtypeStruct((b, l, GROUP_W), BF16)
    ospec = pl.BlockSpec((1, tl, GROUP_W), lambda bb, i: (bb, i, 0))
    return pl.pallas_call(
        functools.partial(_hy_gate_kernel, n_tiles=l // tl),
        grid=(b, l // tl),
        in_specs=[*_halo_specs(tl, l, w),
                  pl.BlockSpec((3, w), lambda bb, i: (0, 0)),
                  pl.BlockSpec((1, w), lambda bb, i: (0, 0))],
        out_specs=[ospec, ospec],
        out_shape=[out, out],
        compiler_params=_cp(("parallel", "parallel")),
        name="hy_gate",
    )(zh, zh, zh, conv_w, conv_b.reshape(1, w))


def _hy_filter_kernel(fc_ref, w1_ref, b1_ref, fr_ref, w2_ref, b2_ref, w3_ref, dl_ref, k_ref, *, l, tr):
    base = pl.program_id(1) * tr

    def position(m):
        return jnp.where(m < l, m, 2 * l - m).astype(F32)

    m_row = base + lax.broadcasted_iota(jnp.int32, (1, tr), 1)
    t_row = position(m_row)
    feat_id = lax.broadcasted_iota(jnp.int32, (fc_ref.shape[0], 1), 0)
    phase = fc_ref[...] * t_row
    feat = jnp.where(feat_id == 0, t_row / max(l - 1, 1),
                     jnp.where(feat_id <= HYENA_BANDS, jnp.cos(phase),
                               jnp.where(feat_id <= 2 * HYENA_BANDS, -jnp.sin(phase), 0.0)))
    hp = functools.partial(jnp.dot, precision=lax.Precision.HIGHEST, preferred_element_type=F32)
    fr = fr_ref[0]
    h = jnp.sin(fr * (hp(w1_ref[0], feat) + b1_ref[0]))
    h = jnp.sin(fr * (hp(w2_ref[0], h) + b2_ref[0]))
    h = hp(h.T, w3_ref[0])
    m = base + lax.broadcasted_iota(jnp.int32, (tr, 1), 0)
    dec = jnp.exp(-(position(m) / max(l - 1, 1)) * dl_ref[...])
    hf, hb = h[:, :GROUP_W] * dec, h[:, GROUP_W:] * dec
    k = jnp.where(m < l, hf, hb)
    k = jnp.where(m == 0, hf + hb, k)
    k_ref[0] = jnp.where(m == l, 0.0, k).astype(BF16)


def _hy_filter(l, w1, b1, fr, w2, b2, w3):
    depth, emb, hid = w1.shape
    tr = min(2 * l, 2048)
    n_feat = -(-emb // 8) * 8
    bands = jnp.linspace(1e-4, HYENA_BANDS - 1, HYENA_BANDS, dtype=F32) * (2.0 * math.pi / l)
    fc = jnp.zeros((n_feat, 1), F32).at[1:1 + HYENA_BANDS, 0].set(bands).at[1 + HYENA_BANDS:emb, 0].set(bands)
    w1t = jnp.zeros((depth, hid, n_feat), F32).at[:, :, :emb].set(jnp.swapaxes(w1, 1, 2))
    w2t = jnp.swapaxes(w2, 1, 2)
    min_decay = math.log(HYENA_TARGET) / HYENA_SLOW_DECAY
    max_decay = math.log(HYENA_TARGET) / HYENA_FAST_DECAY
    deltas = jnp.abs(jnp.linspace(min_decay, max_decay, GROUP_W, dtype=F32)).reshape(1, GROUP_W)
    lay = lambda a: pl.BlockSpec((1,) + a.shape[1:], lambda d, i: (d,) + (0,) * (a.ndim - 1))
    fix = lambda a: pl.BlockSpec(a.shape, lambda d, i: (0,) * a.ndim)
    b1, fr, b2 = (a.reshape(depth, hid, 1) for a in (b1, fr, b2))
    return pl.pallas_call(
        functools.partial(_hy_filter_kernel, l=l, tr=tr),
        grid=(depth, 2 * l // tr),
        in_specs=[fix(fc), lay(w1t), lay(b1), lay(fr), lay(w2t), lay(b2), lay(w3), fix(deltas)],
        out_specs=pl.BlockSpec((1, tr, GROUP_W), lambda d, i: (d, i, 0)),
        out_shape=jax.ShapeDtypeStruct((depth, 2 * l, GROUP_W), BF16),
        compiler_params=_cp(("parallel", "parallel")),
        name="hy_filter",
    )(fc, w1t, b1, fr, w2t, b2, w3, deltas)


ROW_TILE = 8


def _lead_kron(f):
    return jnp.kron(f.astype(F32), jnp.eye(ROW_TILE, dtype=F32)).astype(BF16)


def _lead_rows(n):
    return next(r for r in (8 * ROW_TILE, 4 * ROW_TILE, 2 * ROW_TILE, ROW_TILE) if n % r == 0)


def _lead_mm_kernel(f_ref, x_ref, o_ref):
    k, rows, c = x_ref.shape[1:]
    for j in range(rows // ROW_TILE):
        rs = slice(j * ROW_TILE, (j + 1) * ROW_TILE)
        y = _dot(f_ref[...], x_ref[0, :, rs, :].reshape(k * ROW_TILE, c))
        o_ref[0, :, rs, :] = y.reshape(-1, ROW_TILE, c).astype(o_ref.dtype)


def _lead_mm(fk, x):
    b, k, n, c = x.shape
    m = fk.shape[0] // ROW_TILE
    rows = _lead_rows(n)
    return pl.pallas_call(
        _lead_mm_kernel,
        grid=(b, n // rows),
        in_specs=[pl.BlockSpec(fk.shape, lambda bb, i: (0, 0)),
                  pl.BlockSpec((1, k, rows, c), lambda bb, i: (bb, 0, i, 0))],
        out_specs=pl.BlockSpec((1, m, rows, c), lambda bb, i: (bb, 0, i, 0)),
        out_shape=jax.ShapeDtypeStruct((b, m, n, c), BF16),
        compiler_params=_cp(("parallel", "parallel"), VMEM_LIMIT),
        name="lead_mm",
    )(fk, x)


def _hy_spec_kernel(a_ref, mf_ref, k_ref):
    for d in range(a_ref.shape[0]):
        x = a_ref[d, :, 0].reshape(2 * a_ref.shape[3], GROUP_W)
        k_ref[d, 0] = _dot(mf_ref[0], x)


def _hy_mid_kernel(a_ref, mf_ref, mi_ref, k_ref, o_ref):
    n2 = a_ref.shape[3]
    k = k_ref[0, 0]
    kr, ki = k[:n2], k[n2:]
    for bb in range(a_ref.shape[0]):
        x = a_ref[bb, :, 0].reshape(2 * n2, GROUP_W)
        y = _dot(mf_ref[0], x)
        yr, yi = y[:n2], y[n2:]
        z = jnp.concatenate([yr * kr - yi * ki, yr * ki + yi * kr], axis=0).astype(BF16)
        o_ref[bb, :, 0] = _dot(mi_ref[0], z).astype(BF16).reshape(2, n2, GROUP_W)


def _hy_spectrum(a, mf):
    depth, _, n1, n2, w = a.shape
    return pl.pallas_call(
        _hy_spec_kernel,
        grid=(n1,),
        in_specs=[pl.BlockSpec((depth, 2, 1, n2, w), lambda k1: (0, 0, k1, 0, 0)),
                  pl.BlockSpec((1, 2 * n2, 2 * n2), lambda k1: (k1, 0, 0))],
        out_specs=pl.BlockSpec((depth, 1, 2 * n2, w), lambda k1: (0, k1, 0, 0)),
        out_shape=jax.ShapeDtypeStruct((depth, n1, 2 * n2, w), F32),
        compiler_params=_cp(("parallel",)),
        name="hy_spectrum",
    )(a, mf)


def _hy_mid(a, mf, mi, kspec, layer):
    b, _, n1, n2, w = a.shape
    blk = pl.BlockSpec((b, 2, 1, n2, w), lambda k1: (0, 0, k1, 0, 0))
    mat = pl.BlockSpec((1, 2 * n2, 2 * n2), lambda k1: (k1, 0, 0))
    return pl.pallas_call(
        _hy_mid_kernel,
        grid=(n1,),
        in_specs=[blk, mat, mat,
                  pl.BlockSpec((1, 1, 2 * n2, w), lambda k1: (layer, k1, 0, 0))],
        out_specs=blk,
        out_shape=jax.ShapeDtypeStruct(a.shape, BF16),
        compiler_params=_cp(("parallel",)),
        name="hy_mid",
    )(a, mf, mi, kspec)


def _hy_out_kernel(g_ref, b_ref, vx_ref, x0_ref, skip_ref, o_ref):
    k, rows, c = b_ref.shape[1:]
    for j in range(rows // ROW_TILE):
        rs = slice(j * ROW_TILE, (j + 1) * ROW_TILE)
        y = _dot(g_ref[...], b_ref[0, :, rs, :].reshape(k * ROW_TILE, c)).reshape(-1, ROW_TILE, c)
        vx = vx_ref[0, :, rs, :].astype(F32)
        o_ref[0, :, rs, :] = ((y + skip_ref[...] * vx) * x0_ref[0, :, rs, :].astype(F32)).astype(BF16)


def _hy_out(gk, bo, vx, x0, skip):
    b, k, n, c = bo.shape
    m = gk.shape[0] // ROW_TILE
    rows = _lead_rows(n)
    sig = pl.BlockSpec((1, m, rows, c), lambda bb, i: (bb, 0, i, 0))
    return pl.pallas_call(
        _hy_out_kernel,
        grid=(b, n // rows),
        in_specs=[pl.BlockSpec(gk.shape, lambda bb, i: (0, 0)),
                  pl.BlockSpec((1, k, rows, c), lambda bb, i: (bb, 0, i, 0)),
                  sig, sig, pl.BlockSpec((1, c), lambda bb, i: (0, 0))],
        out_specs=sig,
        out_shape=jax.ShapeDtypeStruct((b, m, n, c), BF16),
        compiler_params=_cp(("parallel", "parallel"), VMEM_LIMIT),
        name="hy_out",
    )(gk, bo, vx, x0, skip)


def _hy_small_kernel(vx_ref, x0_ref, k_ref, fd_ref, ff_ref, gd_ref, skip_ref, o_ref):
    nf = fd_ref.shape[0] // 2
    vx = vx_ref[0]
    u = _dot(fd_ref[...], vx)
    k = _dot(ff_ref[...], k_ref[0])
    ur, ui, kr, ki = u[:nf], u[nf:], k[:nf], k[nf:]
    z = jnp.concatenate([ur * kr - ui * ki, ur * ki + ui * kr], axis=0).astype(BF16)
    y = _dot(gd_ref[...], z)
    o_ref[0] = ((y + skip_ref[...] * vx.astype(F32)) * x0_ref[0].astype(F32)).astype(BF16)


def _hy_small(vx, x0, kfilt, layer, fd, ff, gd, skip):
    b, l, w = vx.shape
    sig = pl.BlockSpec((1, l, w), lambda bb: (bb, 0, 0))
    full = lambda a: pl.BlockSpec(a.shape, lambda bb: (0,) * a.ndim)
    return pl.pallas_call(
        _hy_small_kernel,
        grid=(b,),
        in_specs=[sig, sig, pl.BlockSpec((1, 2 * l, w), lambda bb: (layer, 0, 0)),
                  full(fd), full(ff), full(gd), full(skip)],
        out_specs=sig,
        out_shape=jax.ShapeDtypeStruct((b, l, w), BF16),
        compiler_params=_cp(("parallel",)),
        name="hy_small",
    )(vx, x0, kfilt, fd, ff, gd, skip)


def _fn_first_kernel(z_ref, m_ref, o_ref):
    n2 = z_ref.shape[1]
    x = z_ref[0].reshape(n2 * 8, GROUP_W)
    o_ref[0] = _dot(m_ref[0], x).astype(BF16).reshape(2, 8, n2, GROUP_W)


def _fn_first(zf, mats):
    b, n2, n1, w = zf.shape
    return pl.pallas_call(
        _fn_first_kernel,
        grid=(n1 // 8, b),
        in_specs=[pl.BlockSpec((1, n2, 8, w), lambda i, bb: (bb, 0, i, 0)),
                  pl.BlockSpec((1, 16 * n2, 8 * n2), lambda i, bb: (i, 0, 0))],
        out_specs=pl.BlockSpec((1, 2, 8, n2, w), lambda i, bb: (bb, 0, i, 0, 0)),
        out_shape=jax.ShapeDtypeStruct((b, 2, n1, n2, w), BF16),
        compiler_params=_cp(("parallel", "parallel"), VMEM_LIMIT),
        name="fn_first",
    )(zf, mats)


def _fn_small_kernel(u_ref, cl_ref, sl_ref, c_ref, s_ref, w_ref, b_ref, o_ref, *, scale):
    u = u_ref[0]
    uc = _dot(u, c_ref[...]).astype(BF16)
    us = _dot(u, s_ref[...]).astype(BF16)
    f = (_dot(cl_ref[...], uc) - _dot(sl_ref[...], us)) * scale
    o_ref[0] = (_dot(f.astype(BF16), w_ref[...]) + b_ref[...]).astype(BF16)


def _fn_small(zf, cl, sl, cbd, sbd, w, bias, scale):
    b, l, c = zf.shape
    sig = pl.BlockSpec((1, l, c), lambda bb: (bb, 0, 0))
    full = lambda z: pl.BlockSpec(z.shape, lambda bb: (0,) * z.ndim)
    return pl.pallas_call(
        functools.partial(_fn_small_kernel, scale=scale),
        grid=(b,),
        in_specs=[sig, full(cl), full(sl), full(cbd), full(sbd), full(w), full(bias)],
        out_specs=sig,
        out_shape=jax.ShapeDtypeStruct((b, l, c), BF16),
        compiler_params=_cp(("parallel",)),
        name="fn_small",
    )(zf, cl, sl, cbd, sbd, w, bias)


def _merge_kernel(*refs, alpha, fnet_scale):
    if fnet_scale is None:
        oa_ref, ob_ref, oc_ref, od_ref = refs[:4]
        od = od_ref[0].astype(F32)
        rest = refs[4:]
    else:
        oa_ref, ob_ref, oc_ref, ar_ref, ai_ref, c_ref, s_ref, fw_ref, fb_ref = refs[:9]
        t = (_dot(ar_ref[0, 0], c_ref[...]) + _dot(ai_ref[0, 0], s_ref[...])) * fnet_scale
        od = _dot(t.astype(BF16), fw_ref[...]) + fb_ref[...]
        rest = refs[9:]
    x_ref, g1_ref, gn_ref, w_ref, b_ref, lng_ref, lnb_ref, o_ref = rest
    parts = []
    for idx, v in enumerate((oa_ref[0].astype(F32), ob_ref[0].astype(F32), oc_ref[0].astype(F32), od)):
        ms = jnp.mean(v * v, axis=-1, keepdims=True)
        gain = gn_ref[:, idx * GROUP_W:(idx + 1) * GROUP_W]
        parts.append((v * lax.rsqrt(ms + LN_EPS) * gain).astype(BF16))
    y = _dot(jnp.concatenate(parts, axis=1), w_ref[...]) + b_ref[...]
    r = alpha * x_ref[0] + g1_ref[0] * y
    o_ref[0] = _layer_norm(r) * lng_ref[...] + lnb_ref[...]


def _merge(oa, ob, oc, od, x, g1, gn, w_out, b_out, ln_g, ln_b, alpha):
    b, l, d = x.shape
    tm = min(l, 1024)
    grp = pl.BlockSpec((1, tm, GROUP_W), lambda bb, i: (bb, i, 0))
    tok = pl.BlockSpec((1, tm, d), lambda bb, i: (bb, i, 0))
    full = lambda a: pl.BlockSpec(a.shape, lambda bb, i: (0,) * a.ndim)
    if isinstance(od, tuple):
        a, cbd, sbd, fw, fb, fnet_scale = od
        part = lambda p: pl.BlockSpec((1, 1, tm, GROUP_W), lambda bb, i: (bb, p, i, 0))
        od_args = (a, a, cbd, sbd, fw, fb)
        od_specs = [part(0), part(1), full(cbd), full(sbd), full(fw), full(fb)]
    else:
        fnet_scale, od_args, od_specs = None, (od,), [grp]
    return pl.pallas_call(
        functools.partial(_merge_kernel, alpha=alpha, fnet_scale=fnet_scale),
        grid=(b, l // tm),
        in_specs=[grp, grp, grp, *od_specs, tok, pl.BlockSpec((1, 1, d), lambda bb, i: (bb, 0, 0)),
                  full(gn), full(w_out), full(b_out), full(ln_g), full(ln_b)],
        out_specs=tok,
        out_shape=jax.ShapeDtypeStruct((b, l, d), F32),
        compiler_params=_cp(("parallel", "parallel"), VMEM_LIMIT),
        name="merge",
    )(oa, ob, oc, *od_args, x, g1, gn, w_out, b_out, ln_g, ln_b)


def _ffn_kernel(x_ref, xp_ref, xn_ref, sh_ref, sc_ref, g_ref, wup_ref, vec_ref, wd_ref, bd_ref,
                lng_ref, lnb_ref, o_ref, h_ref, acc_ref, u0_ref, u1_ref, *, alpha, tm, n_chunks, n_tiles):
    i = pl.program_id(1)
    sh, sc = sh_ref[0], sc_ref[0]
    cw = FFN_CHUNK

    def hmod(v):
        return (_layer_norm(v) * (1.0 + sc) + sh).astype(BF16)

    h_ref[0:HALO] = hmod(xp_ref[0])
    h_ref[HALO:HALO + tm] = hmod(x_ref[0])
    h_ref[HALO + tm:] = hmod(xn_ref[0])
    acc_ref[...] = jnp.zeros_like(acc_ref)
    row8 = lax.broadcasted_iota(jnp.int32, (8, 1), 0)
    pad_top = (row8 == 7) & (i == 0)
    pad_bot = (row8 == 0) & (i == n_tiles - 1)

    def up(c, dst):
        dst[...] = _dot(h_ref[...], wup_ref[c])

    def down(c, src):
        vec = vec_ref[c]
        bias, w0, w1, w2 = vec[0:1], vec[1:2], vec[2:3], vec[3:4]
        src[HALO - 8:HALO] = jnp.where(pad_top, -bias, src[HALO - 8:HALO])
        src[HALO + tm:HALO + tm + 8] = jnp.where(pad_bot, -bias, src[HALO + tm:HALO + tm + 8])
        conv = (src[HALO - 1:HALO - 1 + tm] * w0 + src[HALO:HALO + tm] * w1 + src[HALO + 1:HALO + 1 + tm] * w2
                + (vec[4:5] + bias * (w0 + w1 + w2)))
        a, g = conv[:, :cw], conv[:, cw:]
        act = (a * jax.nn.sigmoid(a) * g).astype(BF16)
        acc_ref[...] += _dot(act, wd_ref[c])

    up(0, u0_ref)

    def body(p, carry):
        c = 2 * p
        up(c + 1, u1_ref)
        down(c, u0_ref)
        up(c + 2, u0_ref)
        down(c + 1, u1_ref)
        return carry

    n_pairs = (n_chunks - 1) // 2
    lax.fori_loop(0, n_pairs, body, 0)
    c = 2 * n_pairs
    if c + 1 < n_chunks:
        up(c + 1, u1_ref)
    down(c, u0_ref)
    if c + 1 < n_chunks:
        down(c + 1, u1_ref)
    r = alpha * x_ref[0] + g_ref[0] * (acc_ref[...] + bd_ref[...])
    o_ref[0] = _layer_norm(r) * lng_ref[...] + lnb_ref[...]


def _ffn(x, sh, sc, g2, wup, vec, wd, b_down, ln_g, ln_b, alpha):
    b, l, d = x.shape
    tm = min(l, 1024)
    n_chunks = wd.shape[0]
    main, prev, nxt = _halo_specs(tm, l, d)
    mod = pl.BlockSpec((1, 1, d), lambda bb, i: (bb, 0, 0))
    full = lambda a: pl.BlockSpec(a.shape, lambda bb, i: (0,) * a.ndim, pipeline_mode=pl.Buffered(1))
    return pl.pallas_call(
        functools.partial(_ffn_kernel, alpha=alpha, tm=tm, n_chunks=n_chunks, n_tiles=l // tm),
        grid=(b, l // tm),
        in_specs=[main, prev, nxt, mod, mod, mod, full(wup), full(vec), full(wd),
                  full(b_down), full(ln_g), full(ln_b)],
        out_specs=main,
        out_shape=jax.ShapeDtypeStruct((b, l, d), F32),
        scratch_shapes=[pltpu.VMEM((tm + 2 * HALO, d), BF16), pltpu.VMEM((tm, d), F32),
                        pltpu.VMEM((tm + 2 * HALO, 2 * FFN_CHUNK), F32),
                        pltpu.VMEM((tm + 2 * HALO, 2 * FFN_CHUNK), F32)],
        compiler_params=_cp(("parallel", "parallel"), VMEM_LIMIT),
        name="ffn",
    )(x, x, x, sh, sc, g2, wup, vec, wd, b_down, ln_g, ln_b)


def _cos_sin(idx, n):
    ang = (idx % n).astype(F32) * (2.0 * math.pi / n)
    return jnp.cos(ang), jnp.sin(ang)


def _cplx_rows(re, im):
    return jnp.concatenate([re, im], axis=-2)


def _cplx_block(re, im):
    return jnp.concatenate([jnp.concatenate([re, -im], axis=-1), jnp.concatenate([im, re], axis=-1)], axis=-2)


def _ar(n):
    return jnp.arange(n, dtype=jnp.int32)


def _hyena_tables(l):
    n, n2 = 2 * l, HYENA_N2
    n1 = n // n2
    nk = n1 // 2 + 1
    c, s = _cos_sin(_ar(nk)[:, None] * _ar(n1)[None, :], n1)
    f_full = _cplx_rows(c, -s).astype(BF16)
    wgt = jnp.where((_ar(nk) == 0) | (_ar(nk) == n1 // 2), 1.0, 2.0)[:, None] / n
    g = jnp.concatenate([(c * wgt).T, (-s * wgt).T], axis=1)[:n1 // 2].astype(BF16)
    k1, k2, m2 = _ar(nk)[:, None, None], _ar(n2)[None, :, None], _ar(n2)[None, None, :]
    c, s = _cos_sin(m2 * k1 + n1 * m2 * k2, n)
    mf = _cplx_block(c, -s).astype(BF16)
    return dict(f_half=_lead_kron(f_full[:, :n1 // 2]), f_full=_lead_kron(f_full), g=_lead_kron(g),
                mf=mf, mi=jnp.swapaxes(mf, 1, 2))


def _fnet_tables(l):
    n2 = FNET_N2
    n1 = l // n2
    nblk = n1 // 8
    blk, j = _ar(nblk)[:, None, None, None], _ar(8)[None, :, None, None]
    k2, m2 = _ar(n2)[None, None, :, None], _ar(n2)[None, None, None, :]
    c, s = _cos_sin(n1 * m2 * k2 + (8 * blk + j) * k2, l)
    eye = jnp.eye(8, dtype=F32)
    emb = lambda t: jnp.einsum('bjkn,ji->bjkni', t, eye).reshape(nblk, 8 * n2, n2 * 8)
    first = jnp.concatenate([emb(c), emb(-s)], axis=1).astype(BF16)
    c, s = _cos_sin(_ar(n1)[:, None] * _ar(n1)[None, :], n1)
    second = _cplx_block(c, -s).astype(BF16)
    return dict(first=first, second=_lead_kron(second))


def _dense_tables(l):
    n = 2 * l
    c, s = _cos_sin(_ar(n)[:, None] * _ar(n)[None, :], n)
    ff = _cplx_rows(c, -s).astype(BF16)
    gd = (jnp.concatenate([c, -s], axis=1)[:l] / n).astype(BF16)
    cl, sl = _cos_sin(_ar(l)[:, None] * _ar(l)[None, :], l)
    return dict(fd=ff[:, :l], ff=ff, gd=gd, cl=cl.astype(BF16), sl=sl.astype(BF16))


def _rope_tables(l):
    rows = l // GRID_W
    row = jnp.broadcast_to(jnp.arange(rows)[:, None], (rows, GRID_W)).reshape(-1).astype(F32)
    col = jnp.broadcast_to(jnp.arange(GRID_W)[None, :], (rows, GRID_W)).reshape(-1).astype(F32)
    half = HEAD_DIM // 2
    inv = ROPE_THETA ** (-jnp.arange(0, half, 2, dtype=F32) / half)
    ar, ac = row[:, None] * inv, col[:, None] * inv
    cos = jnp.concatenate([jnp.cos(ar), jnp.cos(ar), jnp.cos(ac), jnp.cos(ac)], axis=1)
    sin = jnp.concatenate([-jnp.sin(ar), jnp.sin(ar), -jnp.sin(ac), jnp.sin(ac)], axis=1)
    return jnp.tile(cos, (1, 2)), jnp.tile(sin, (1, 2))


def kernel(x, c, ctx, c_ctx, w_ada, b_ada, w_in, sink_a, q_norm_g, k_norm_g, hy_conv_w, hy_conv_b,
           hy_f_w1, hy_f_b1, hy_f_freq, hy_f_w2, hy_f_b2, hy_f_w3, hy_skip, fnet_w, fnet_b,
           out_norm_g, w_out, b_out, ln1_g, ln1_b, ffn_w_up, ffn_b_up, ffn_conv_w, ffn_conv_b,
           ffn_w_down, ffn_b_down, ln2_g, ln2_b):
    bsz, l, d = x.shape
    lc = ctx.shape[1]
    depth = w_ada.shape[0]
    d_ff = ffn_w_down.shape[1]
    alpha = (2 * depth) ** 0.25
    n2h, n2f = HYENA_N2, FNET_N2
    n1h, n1f = 2 * l // n2h, l // n2f
    n1k = n1h // 2 + 1

    ht, ft, dt = _hyena_tables(l), _fnet_tables(l), _dense_tables(lc)
    cos_l, sin_l = _rope_tables(l)
    cos_c, sin_c = jnp.ones((lc, 128), F32), jnp.zeros((lc, 128), F32)
    jj = _ar(GROUP_W)
    same = (jj[:, None] // FNET_GROUP_DIM) == (jj[None, :] // FNET_GROUP_DIM)
    bd_mean = (same.astype(F32) / HEAD_DIM).astype(BF16)
    cg, sg = _cos_sin(jj[:, None] * jj[None, :], FNET_GROUP_DIM)
    cbd, sbd = jnp.where(same, cg, 0.0).astype(BF16), jnp.where(same, sg, 0.0).astype(BF16)

    n_rows = -(-(bsz + 1) // 8) * 8
    cc = jnp.zeros((n_rows, d), F32).at[:bsz].set(c).at[bsz].set(c_ctx)
    mod = _ada(cc, w_ada, b_ada)

    k_lat = _hy_filter(l, hy_f_w1, hy_f_b1, hy_f_freq, hy_f_w2, hy_f_b2, hy_f_w3)
    k_ctx = _hy_filter(lc, hy_f_w1, hy_f_b1, hy_f_freq, hy_f_w2, hy_f_b2, hy_f_w3)
    a_k = _lead_mm(ht['f_full'], k_lat.reshape(depth, n1h, n2h, GROUP_W))
    k_spec = _hy_spectrum(a_k.reshape(depth, 2, n1k, n2h, GROUP_W), ht['mf'])

    n_chunks = -(-d_ff // FFN_CHUNK)
    pad = n_chunks * FFN_CHUNK - d_ff

    def chunked(v):
        v = v.reshape(v.shape[:-1] + (2, d_ff))
        v = jnp.pad(v, [(0, 0)] * (v.ndim - 1) + [(0, pad)])
        v = v.reshape(v.shape[:-2] + (2, n_chunks, FFN_CHUNK))
        v = jnp.swapaxes(v, -3, -2).reshape(v.shape[:-3] + (n_chunks, 2 * FFN_CHUNK))
        return jnp.moveaxis(v, -2, 0)

    for layer in range(depth):
        m6 = mod[layer].reshape(n_rows, 6, d)
        lat = [m6[:bsz, i][:, None, :] for i in range(6)]
        con = [jnp.broadcast_to(m6[bsz, i][None, None, :], (bsz, 1, d)) for i in range(6)]
        w_in_l = w_in[layer].astype(BF16)
        qg = jnp.tile(q_norm_g[layer], N_HEADS).reshape(1, -1)
        kg = jnp.tile(k_norm_g[layer], N_KV).reshape(1, -1)
        w_out_l = w_out[layer].astype(BF16)
        gn, b_out_l = out_norm_g[layer].reshape(1, d), b_out[layer].reshape(1, d)
        l1g, l1b = ln1_g[layer].reshape(1, d), ln1_b[layer].reshape(1, d)
        l2g, l2b = ln2_g[layer].reshape(1, d), ln2_b[layer].reshape(1, d)
        fw, fb = fnet_w[layer].astype(BF16), fnet_b[layer].reshape(1, GROUP_W)
        skip = hy_skip[layer].reshape(1, GROUP_W)
        wup = chunked(ffn_w_up[layer]).astype(BF16)
        vec = chunked(jnp.concatenate([ffn_b_up[layer][None], ffn_conv_w[layer], ffn_conv_b[layer][None],
                                       jnp.zeros((3, 2 * d_ff), F32)], axis=0))
        wd = jnp.pad(ffn_w_down[layer], ((0, pad), (0, 0))).reshape(n_chunks, FFN_CHUNK, d).astype(BF16)
        bdn = ffn_b_down[layer].reshape(1, d)
        last = layer == depth - 1

        cqa, cka, cva, cqb, ckb, cvb, czh, czf = _proj(ctx, con[0], con[1], w_in_l, cos_c, sin_c,
                                                       bd_mean, qg, kg)
        qa, ka, va, qb, kb, vb, zh, zf = _proj(x, lat[0], lat[1], w_in_l, cos_l, sin_l, bd_mean, qg, kg)
        oa = _attention(qa, ka, va, sink_a[layer], banded=True, kc=cka, vc=cva)
        ob = _attention(qb, kb, vb, None, banded=False, kc=ckb, vc=cvb)

        vx, x0 = _hy_gate(zh, hy_conv_w[layer], hy_conv_b[layer])
        vx4, x04 = (a.reshape(bsz, n1h // 2, n2h, GROUP_W) for a in (vx, x0))
        a1 = _lead_mm(ht['f_half'], vx4)
        bo = _hy_mid(a1.reshape(bsz, 2, n1k, n2h, GROUP_W), ht['mf'], ht['mi'], k_spec, layer)
        oc = _hy_out(ht['g'], bo.reshape(bsz, 2 * n1k, n2h, GROUP_W), vx4, x04, skip).reshape(bsz, l, GROUP_W)

        f1 = _fn_first(zf.reshape(bsz, n2f, n1f, GROUP_W), ft['first'])
        f2 = _lead_mm(ft['second'], f1.reshape(bsz, 2 * n1f, n2f, GROUP_W))
        od = (f2.reshape(bsz, 2, l, GROUP_W), cbd, sbd, fw, fb, (l * FNET_GROUP_DIM) ** -0.5)

        x = _merge(oa, ob, oc, od, x, lat[2], gn, w_out_l, b_out_l, l1g, l1b, alpha)
        x = _ffn(x, lat[3], lat[4], lat[5], wup, vec, wd, bdn, l2g, l2b, alpha)

        if not last:
            oac = _attention(cqa, cka, cva, sink_a[layer], banded=False)
            obc = _attention(cqb, ckb, cvb, None, banded=False)
            cvx, cx0 = _hy_gate(czh, hy_conv_w[layer], hy_conv_b[layer])
            occ = _hy_small(cvx, cx0, k_ctx, layer, dt['fd'], dt['ff'], dt['gd'], skip)
            odc = _fn_small(czf, dt['cl'], dt['sl'], cbd, sbd, fw, fb, (lc * FNET_GROUP_DIM) ** -0.5)
            ctx = _merge(oac, obc, occ, odc, ctx, con[2], gn, w_out_l, b_out_l, l1g, l1b, alpha)
            ctx = _ffn(ctx, con[3], con[4], con[5], wup, vec, wd, bdn, l2g, l2b, alpha)
    return x
```
